```python
import jax, jax.numpy as jnp
from jax import lax
import numpy as np

D_MODEL = 1024
BATCH = 8
SEQ = 2048
DEPTH = 4

N_MIXERS = 3
POOL_WINDOWS = (2, 4, 8, 16)
N_POOL_GROUPS = len(POOL_WINDOWS)
POOL_GROUP_DIM = D_MODEL // N_POOL_GROUPS
CONV_WIDTH = 31
LN_EPS = 1e-5
QK_NOPE_DIM = 128
QK_ROPE_DIM = 64
QK_HEAD_DIM = QK_NOPE_DIM + QK_ROPE_DIM
V_HEAD_DIM = 128
MLA_HEADS = D_MODEL // V_HEAD_DIM
KV_LORA_RANK = D_MODEL // 4
Q_LORA_RANK = 3 * KV_LORA_RANK
ROPE_THETA = 10000.0
Q_BLOCK = 128
D_FF = 7 * D_MODEL // 2
N_EXPERTS = 8
TOP_K = 2
NORM_EPS = 1e-6

N_POOL = (DEPTH + 2) // 3
N_CONV = (DEPTH + 1) // 3
N_MLA = DEPTH // 3
N_DENSE = (DEPTH + 1) // 2
N_MOE = DEPTH // 2

kernel_name = "hybrid_pool_conformer_mla_moe"


def rmsnorm(x, g):
    xf = x.astype(jnp.float32)
    y = xf * lax.rsqrt(jnp.mean(xf * xf, axis=-1, keepdims=True) + NORM_EPS)
    return (y * g.astype(jnp.float32)).astype(x.dtype)


def pool_mixer(h, w, scale):
    B, S, D = h.shape
    hf = h.astype(jnp.float32)
    cs = jnp.pad(jnp.cumsum(hf, axis=1), ((0, 0), (1, 0), (0, 0)))
    t1 = jnp.arange(1, S + 1, dtype=jnp.float32)[None, :, None]
    groups = []
    for g, win in enumerate(POOL_WINDOWS):
        sl = slice(g * POOL_GROUP_DIM, (g + 1) * POOL_GROUP_DIM)
        c = cs[..., sl]
        lower = jnp.pad(c[:, :S - win + 1], ((0, 0), (win - 1, 0), (0, 0)))
        mean = (c[:, 1:] - lower) / jnp.minimum(t1, float(win))
        groups.append(mean - hf[..., sl])
    pooled = jnp.stack(groups, axis=2).astype(h.dtype)
    out = jnp.einsum('bsgc,gcd->bsgd', pooled, w).reshape(B, S, D)
    return out * scale


def conv_module(h, w1, b1, w_dw, b_dw, ln_g, ln_b, w2, b2):
    D = h.shape[-1]
    a = h @ w1 + b1
    u = a[..., :D] * jax.nn.sigmoid(a[..., D:])
    u = lax.conv_general_dilated(
        u, w_dw[:, None, :], window_strides=(1,),
        padding=((CONV_WIDTH - 1, 0),),
        dimension_numbers=('NWC', 'WIO', 'NWC'),
        feature_group_count=D) + b_dw
    uf = u.astype(jnp.float32)
    mu = jnp.mean(uf, axis=-1, keepdims=True)
    var = jnp.mean(jnp.square(uf - mu), axis=-1, keepdims=True)
    un = (uf - mu) * lax.rsqrt(var + LN_EPS) * ln_g.astype(jnp.float32) + ln_b.astype(jnp.float32)
    un = jax.nn.silu(un).astype(h.dtype)
    return un @ w2 + b2


def apply_rope(x, cos, sin):
    half = x.shape[-1] // 2
    x1, x2 = x[..., :half], x[..., half:]
    return jnp.concatenate([x1 * cos - x2 * sin, x2 * cos + x1 * sin], axis=-1)


def mla(h, positions, wq_a, q_norm, wq_b, wkv_a, kv_norm, wkv_b, wo):
    B, S, D = h.shape
    H = MLA_HEADS
    q = (rmsnorm(h @ wq_a, q_norm) @ wq_b).reshape(B, S, H, QK_HEAD_DIM)
    q_nope, q_rope = q[..., :QK_NOPE_DIM], q[..., QK_NOPE_DIM:]
    kv_a = h @ wkv_a
    c_kv = rmsnorm(kv_a[..., :KV_LORA_RANK], kv_norm)
    k_rope = kv_a[..., KV_LORA_RANK:]
    kv = (c_kv @ wkv_b).reshape(B, S, H, QK_NOPE_DIM + V_HEAD_DIM)
    k_nope, v = kv[..., :QK_NOPE_DIM], kv[..., QK_NOPE_DIM:]

    inv_freq = ROPE_THETA ** (-jnp.arange(0, QK_ROPE_DIM, 2, dtype=jnp.float32) / QK_ROPE_DIM)
    ang = positions.astype(jnp.float32)[..., None] * inv_freq
    cos, sin = jnp.cos(ang), jnp.sin(ang)
    q_rope = apply_rope(q_rope.astype(jnp.float32), cos[:, :, None], sin[:, :, None]).astype(h.dtype)
    k_rope = apply_rope(k_rope.astype(jnp.float32), cos, sin).astype(h.dtype)

    nb = S // Q_BLOCK
    k_idx = jnp.arange(S)
    scale = QK_HEAD_DIM ** -0.5

    def to_blocks(t):
        return t.reshape(B, nb, Q_BLOCK, *t.shape[2:]).swapaxes(0, 1)

    def attend(args):
        qn, qr, i = args
        s = (jnp.einsum('bqhd,bkhd->bhqk', qn, k_nope, preferred_element_type=jnp.float32)
             + jnp.einsum('bqhr,bkr->bhqk', qr, k_rope, preferred_element_type=jnp.float32))
        q_idx = i * Q_BLOCK + jnp.arange(Q_BLOCK)
        causal = k_idx[None, :] <= q_idx[:, None]
        s = jnp.where(causal, s * scale, -jnp.inf)
        p = jax.nn.softmax(s, axis=-1).astype(v.dtype)
        return jnp.einsum('bhqk,bkhd->bqhd', p, v)

    o = lax.map(attend, (to_blocks(q_nope), to_blocks(q_rope), jnp.arange(nb)))
    o = o.swapaxes(0, 1).reshape(B, S, H * V_HEAD_DIM)
    return o @ wo


def swiglu(x, wg, wu, wd):
    return (jax.nn.silu(x @ wg) * (x @ wu)) @ wd


def moe_swiglu(h, router, wg, wu, wd):
    B, S, D = h.shape
    xf = h.reshape(-1, D)
    logits = (xf @ router).astype(jnp.float32)
    top_vals, top_idx = lax.top_k(logits, TOP_K)
    gates = jax.nn.softmax(top_vals, axis=-1)
    combine = jnp.einsum('nk,nke->ne', gates,
                         jax.nn.one_hot(top_idx, N_EXPERTS, dtype=jnp.float32)).astype(h.dtype)
    y = jnp.zeros_like(xf)
    for e in range(N_EXPERTS):
        y = y + combine[:, e:e + 1] * swiglu(xf, wg[e], wu[e], wd[e])
    return y.reshape(B, S, D)


def setup_inputs(seed: int = 0) -> dict:
    key = jax.random.key(seed)
    ks = iter(jax.random.split(key, 40))
    f32 = jnp.float32

    def w(shape, fan_in):
        return jax.random.normal(next(ks), shape, f32) * (fan_in ** -0.5)

    def gain(shape):
        return 1.0 + 0.05 * jax.random.normal(next(ks), shape, f32)

    def bias(shape):
        return 0.02 * jax.random.normal(next(ks), shape, f32)

    D, F, E, H = D_MODEL, D_FF, N_EXPERTS, MLA_HEADS
    x = jax.random.normal(next(ks), (BATCH, SEQ, D), f32)
    offsets = jax.random.randint(next(ks), (BATCH, 1), 0, 4096, dtype=jnp.int32)
    positions = offsets + jnp.arange(SEQ, dtype=jnp.int32)[None, :]
    return {
        "x": x,
        "positions": positions,
        "norm_mix": gain((DEPTH, D)),
        "norm_ffn": gain((DEPTH, D)),
        "norm_final": gain((D,)),
        "pool_w": w((N_POOL, N_POOL_GROUPS, POOL_GROUP_DIM, POOL_GROUP_DIM), POOL_GROUP_DIM),
        "pool_scale": 1.0 + 0.1 * jax.random.normal(next(ks), (N_POOL, D), f32),
        "conv_w1": w((N_CONV, D, 2 * D), D),
        "conv_b1": bias((N_CONV, 2 * D)),
        "conv_dw": w((N_CONV, CONV_WIDTH, D), CONV_WIDTH),
        "conv_bdw": bias((N_CONV, D)),
        "conv_ln_g": gain((N_CONV, D)),
        "conv_ln_b": bias((N_CONV, D)),
        "conv_w2": w((N_CONV, D, D), D),
        "conv_b2": bias((N_CONV, D)),
        "mla_wq_a": w((N_MLA, D, Q_LORA_RANK), D),
        "mla_q_norm": gain((N_MLA, Q_LORA_RANK)),
        "mla_wq_b": w((N_MLA, Q_LORA_RANK, H * QK_HEAD_DIM), Q_LORA_RANK),
        "mla_wkv_a": w((N_MLA, D, KV_LORA_RANK + QK_ROPE_DIM), D),
        "mla_kv_norm": gain((N_MLA, KV_LORA_RANK)),
        "mla_wkv_b": w((N_MLA, KV_LORA_RANK, H * (QK_NOPE_DIM + V_HEAD_DIM)), KV_LORA_RANK),
        "mla_wo": w((N_MLA, H * V_HEAD_DIM, D), H * V_HEAD_DIM),
        "ffn_wg": w((N_DENSE, D, F), D),
        "ffn_wu": w((N_DENSE, D, F), D),
        "ffn_wd": w((N_DENSE, F, D), F),
        "moe_router": w((N_MOE, D, E), D),
        "moe_wg": w((N_MOE, E, D, F), D),
        "moe_wu": w((N_MOE, E, D, F), D),
        "moe_wd": w((N_MOE, E, F, D), F),
    }


def reference(x, positions, norm_mix, norm_ffn, norm_final, pool_w, pool_scale,
              conv_w1, conv_b1, conv_dw, conv_bdw, conv_ln_g, conv_ln_b, conv_w2, conv_b2,
              mla_wq_a, mla_q_norm, mla_wq_b, mla_wkv_a, mla_kv_norm, mla_wkv_b, mla_wo,
              ffn_wg, ffn_wu, ffn_wd, moe_router, moe_wg, moe_wu, moe_wd):
    h = x
    for i in range(DEPTH):
        kind, j = i % N_MIXERS, i // N_MIXERS
        u = rmsnorm(h, norm_mix[i])
        if kind == 0:
            m = pool_mixer(u, pool_w[j], pool_scale[j])
        elif kind == 1:
            m = conv_module(u, conv_w1[j], conv_b1[j], conv_dw[j], conv_bdw[j],
                            conv_ln_g[j], conv_ln_b[j], conv_w2[j], conv_b2[j])
        else:
            m = mla(u, positions, mla_wq_a[j], mla_q_norm[j], mla_wq_b[j],
                    mla_wkv_a[j], mla_kv_norm[j], mla_wkv_b[j], mla_wo[j])
        h = h + m
        u = rmsnorm(h, norm_ffn[i])
        f_idx = i // 2
        if i % 2 == 0:
            f = swiglu(u, ffn_wg[f_idx], ffn_wu[f_idx], ffn_wd[f_idx])
        else:
            f = moe_swiglu(u, moe_router[f_idx], moe_wg[f_idx], moe_wu[f_idx], moe_wd[f_idx])
        h = h + f
    return rmsnorm(h, norm_final)
```

```python
import functools

import jax
import jax.numpy as jnp
from jax import lax
from jax.experimental import pallas as pl
from jax.experimental.pallas import tpu as pltpu

F32 = jnp.float32
BF16 = jnp.bfloat16

NORM_EPS = 1e-6
LN_EPS = 1e-5
ROPE_THETA = 10000.0

POOL_WINDOWS = (2, 4, 8, 16)
POOL_HALO = 16
CONV_HALO = 32
LANES = 128
NOPE = 128
ROPE = 64
TOP_K = 2

VMEM_LIMIT = 56 * 1024 * 1024


def _params(*sem):
    return pltpu.CompilerParams(dimension_semantics=sem, vmem_limit_bytes=VMEM_LIMIT)


def _rms(x, g, eps=NORM_EPS):
    return x * lax.rsqrt(jnp.mean(x * x, axis=-1, keepdims=True) + eps) * g


def _dot(a, b):
    return jnp.dot(a, b, preferred_element_type=F32)


def _dot_nt(a, b):
    return lax.dot_general(a, b, (((1,), (1,)), ((), ())), preferred_element_type=F32)


def _pool_kernel(h_ref, g_ref, w_ref, sc_ref, o_ref, ubuf, *, ts, gdim):
    j = pl.program_id(1)
    h = h_ref[...]
    u = _rms(h, g_ref[...])

    @pl.when(j == 0)
    def _():
        ubuf[0:POOL_HALO, :] = jnp.zeros((POOL_HALO, ubuf.shape[1]), F32)

    ubuf[POOL_HALO:POOL_HALO + ts, :] = u
    t1 = lax.broadcasted_iota(jnp.int32, (ts, 1), 0) + (j * ts + 1)
    for g, win in enumerate(POOL_WINDOWS):
        cs = slice(g * gdim, (g + 1) * gdim)
        ug = u[:, cs]
        s = ug
        for k in range(1, win):
            s = s + ubuf[POOL_HALO - k:POOL_HALO - k + ts, cs]
        cnt = jnp.minimum(t1, win).astype(F32)
        pooled = s / cnt - ug
        m = _dot(pooled.astype(BF16), w_ref[g])
        o_ref[:, cs] = h[:, cs] + m * sc_ref[:, cs]
    ubuf[0:POOL_HALO, :] = ubuf[ts:ts + POOL_HALO, :]


def _pool_layer(h, g, w, scale, *, batch, seq):
    n, d = h.shape
    ts = min(512, seq)
    nt = seq // ts
    gdim = d // len(POOL_WINDOWS)
    row = lambda b, j: (b * nt + j, 0)
    const2 = lambda b, j: (0, 0)
    return pl.pallas_call(
        functools.partial(_pool_kernel, ts=ts, gdim=gdim),
        grid=(batch, nt),
        in_specs=[pl.BlockSpec((ts, d), row),
                  pl.BlockSpec((1, d), const2),
                  pl.BlockSpec(w.shape, lambda b, j: (0, 0, 0)),
                  pl.BlockSpec((1, d), const2)],
        out_specs=pl.BlockSpec((ts, d), row),
        out_shape=jax.ShapeDtypeStruct((n, d), F32),
        scratch_shapes=[pltpu.VMEM((ts + POOL_HALO, d), F32)],
        compiler_params=_params("arbitrary", "arbitrary"),
        name="pool_mixer",
    )(h, g, w, scale)


def _swiglu_step(u, wg, wu, wd):
    a = _dot(u, wg)
    b = _dot(u, wu)
    hh = a * jax.nn.sigmoid(a) * b
    return _dot(hh.astype(BF16), wd)


def _ffn_kernel(h_ref, g_ref, wg_ref, wu_ref, wd_ref, o_ref, u_sc, acc_sc):
    j = pl.program_id(1)

    @pl.when(j == 0)
    def _():
        u_sc[...] = _rms(h_ref[...], g_ref[...]).astype(BF16)
        acc_sc[...] = jnp.zeros(acc_sc.shape, F32)

    acc_sc[...] += _swiglu_step(u_sc[...], wg_ref[...], wu_ref[...], wd_ref[...])

    @pl.when(j == pl.num_programs(1) - 1)
    def _():
        o_ref[...] = h_ref[...] + acc_sc[...]


def _ffn_layer(h, g, wg, wu, wd):
    n, d = h.shape
    f = wg.shape[1]
    tm = min(512, n)
    tf = min(512, f)
    return pl.pallas_call(
        _ffn_kernel,
        grid=(n // tm, f // tf),
        in_specs=[pl.BlockSpec((tm, d), lambda i, j: (i, 0)),
                  pl.BlockSpec((1, d), lambda i, j: (0, 0)),
                  pl.BlockSpec((d, tf), lambda i, j: (0, j)),
                  pl.BlockSpec((d, tf), lambda i, j: (0, j)),
                  pl.BlockSpec((tf, d), lambda i, j: (j, 0))],
        out_specs=pl.BlockSpec((tm, d), lambda i, j: (i, 0)),
        out_shape=jax.ShapeDtypeStruct((n, d), F32),
        scratch_shapes=[pltpu.VMEM((tm, d), BF16), pltpu.VMEM((tm, d), F32)],
        compiler_params=_params("arbitrary", "arbitrary"),
        name="dense_swiglu",
    )(h, g, wg, wu, wd)


def _conv_kernel(h_ref, g_ref, w1_ref, b1_ref, dw_ref, bdw_ref, lng_ref, lnb_ref, w2_ref, b2_ref,
                 o_ref, gbuf, cbuf, *, ts, width, rb):
    j = pl.program_id(1)
    d = h_ref.shape[1]
    h = h_ref[...]
    u = _rms(h, g_ref[...]).astype(BF16)
    a = _dot(u, w1_ref[...]) + b1_ref[...]
    glu = a[:, :d] * jax.nn.sigmoid(a[:, d:])

    @pl.when(j == 0)
    def _():
        gbuf[0:CONV_HALO, :] = jnp.zeros((CONV_HALO, d), F32)

    gbuf[CONV_HALO:CONV_HALO + ts, :] = glu
    for cb in range(d // LANES):
        cs = slice(cb * LANES, (cb + 1) * LANES)
        for r0 in range(0, ts, rb):
            acc = jnp.zeros((rb, LANES), F32)
            for k in range(width):
                off = CONV_HALO + r0 + k - (width - 1)
                acc = acc + gbuf[off:off + rb, cs] * dw_ref[k:k + 1, cs]
            cbuf[r0:r0 + rb, cs] = acc + bdw_ref[:, cs]
    gbuf[0:CONV_HALO, :] = gbuf[ts:ts + CONV_HALO, :]

    c = cbuf[...]
    mu = jnp.mean(c, axis=-1, keepdims=True)
    cc = c - mu
    var = jnp.mean(cc * cc, axis=-1, keepdims=True)
    un = cc * lax.rsqrt(var + LN_EPS) * lng_ref[...] + lnb_ref[...]
    un = un * jax.nn.sigmoid(un)
    o_ref[...] = h + _dot(un.astype(BF16), w2_ref[...]) + b2_ref[...]


def _conv_layer(h, g, w1, b1, dw, bdw, lng, lnb, w2, b2, *, batch, seq):
    n, d = h.shape
    width = dw.shape[0]
    assert width - 1 <= CONV_HALO
    ts = min(256, seq)
    rb = min(128, ts)
    nt = seq // ts
    row = lambda b, j: (b * nt + j, 0)
    c2 = lambda b, j: (0, 0)
    return pl.pallas_call(
        functools.partial(_conv_kernel, ts=ts, width=width, rb=rb),
        grid=(batch, nt),
        in_specs=[pl.BlockSpec((ts, d), row),
                  pl.BlockSpec((1, d), c2),
                  pl.BlockSpec(w1.shape, c2),
                  pl.BlockSpec(b1.shape, c2),
                  pl.BlockSpec(dw.shape, c2),
                  pl.BlockSpec((1, d), c2),
                  pl.BlockSpec((1, d), c2),
                  pl.BlockSpec((1, d), c2),
                  pl.BlockSpec(w2.shape, c2),
                  pl.BlockSpec((1, d), c2)],
        out_specs=pl.BlockSpec((ts, d), row),
        out_shape=jax.ShapeDtypeStruct((n, d), F32),
        scratch_shapes=[pltpu.VMEM((ts + CONV_HALO, d), F32), pltpu.VMEM((ts, d), F32)],
        compiler_params=_params("arbitrary", "arbitrary"),
        name="conv_module",
    )(h, g, w1, b1, dw, bdw, lng, lnb, w2, b2)


def _mla_proj_kernel(h_ref, pos_ref, g_ref, wqa_ref, qnorm_ref, wqb_ref, wkva_ref, kvnorm_ref, wkvb_ref,
                     invf_ref, sgn_ref, qn_o, qr_o, kn_o, v_o, kr_o, *, heads, lora):
    hn = heads * NOPE
    u = _rms(h_ref[...], g_ref[...]).astype(BF16)
    qa = _rms(_dot(u, wqa_ref[...]), qnorm_ref[...]).astype(BF16)
    q = _dot(qa, wqb_ref[...])
    kva = _dot(u, wkva_ref[...])
    ckv = _rms(kva[:, :lora], kvnorm_ref[...]).astype(BF16)
    kv = _dot(ckv, wkvb_ref[...])
    ang = pos_ref[...].astype(F32) * invf_ref[...]
    c = jnp.cos(ang)
    s = jnp.sin(ang) * sgn_ref[...]
    kr = kva[:, lora:lora + LANES] * c + kva[:, lora + LANES:lora + 2 * LANES] * s
    c8 = jnp.concatenate([c] * heads, axis=1)
    s8 = jnp.concatenate([s] * heads, axis=1)
    qr = q[:, hn:2 * hn] * c8 + q[:, 2 * hn:3 * hn] * s8
    qn_o[...] = q[:, :hn].astype(BF16)
    qr_o[...] = qr.astype(BF16)
    kn_o[...] = kv[:, :hn].astype(BF16)
    v_o[...] = kv[:, hn:].astype(BF16)
    kr_o[...] = kr.astype(BF16)


def _attn_kernel(qn_ref, qr_ref, kn_ref, kr_ref, v_ref, h_ref, wo_ref, o_ref, m_sc, l_sc, acc_sc,
                 *, heads, tq, tk, scale):
    i = pl.program_id(1)
    j = pl.program_id(2)

    @pl.when(j == 0)
    def _():
        m_sc[...] = jnp.full(m_sc.shape, -jnp.inf, F32)
        l_sc[...] = jnp.zeros(l_sc.shape, F32)
        acc_sc[...] = jnp.zeros(acc_sc.shape, F32)

    @pl.when(j <= i)
    def _():
        q_idx = lax.broadcasted_iota(jnp.int32, (tq, tk), 0) + i * tq
        k_idx = lax.broadcasted_iota(jnp.int32, (tq, tk), 1) + j * tk
        causal = k_idx <= q_idx
        kr = kr_ref[...]
        for hd in range(heads):
            hs = slice(hd * NOPE, (hd + 1) * NOPE)
            s = _dot_nt(qn_ref[:, hs], kn_ref[:, hs]) + _dot_nt(qr_ref[:, hs], kr)
            s = jnp.where(causal, s * scale, -jnp.inf)
            m_prev = m_sc[hd]
            m_new = jnp.maximum(m_prev, jnp.max(s, axis=-1, keepdims=True))
            alpha = jnp.exp(m_prev - m_new)
            p = jnp.exp(s - m_new)
            l_sc[hd] = alpha * l_sc[hd] + jnp.sum(p, axis=-1, keepdims=True)
            acc_sc[:, hs] = alpha * acc_sc[:, hs] + _dot(p.astype(BF16), v_ref[:, hs])
            m_sc[hd] = m_new

    @pl.when(j == i)
    def _():
        parts = [acc_sc[:, hd * NOPE:(hd + 1) * NOPE] / l_sc[hd] for hd in range(heads)]
        o = jnp.concatenate(parts, axis=1).astype(BF16)
        o_ref[...] = h_ref[...] + _dot(o, wo_ref[...])


def _mla_layer(h, positions, g, wqa, qnorm, wqb, wkva, kvnorm, wkvb, wo, invf, sgn, *, batch, seq, heads, lora):
    n, d = h.shape
    hn = heads * NOPE
    tp = min(256, n)
    c2 = lambda i: (0, 0)
    row = lambda i: (i, 0)
    act = jax.ShapeDtypeStruct((n, hn), BF16)
    qn, qr, kn, v, kr = pl.pallas_call(
        functools.partial(_mla_proj_kernel, heads=heads, lora=lora),
        grid=(n // tp,),
        in_specs=[pl.BlockSpec((tp, d), row),
                  pl.BlockSpec((tp, 1), row),
                  pl.BlockSpec((1, d), c2),
                  pl.BlockSpec(wqa.shape, c2),
                  pl.BlockSpec(qnorm.shape, c2),
                  pl.BlockSpec(wqb.shape, c2),
                  pl.BlockSpec(wkva.shape, c2),
                  pl.BlockSpec(kvnorm.shape, c2),
                  pl.BlockSpec(wkvb.shape, c2),
                  pl.BlockSpec((1, LANES), c2),
                  pl.BlockSpec((1, LANES), c2)],
        out_specs=[pl.BlockSpec((tp, hn), row)] * 4 + [pl.BlockSpec((tp, LANES), row)],
        out_shape=[act, act, act, act, jax.ShapeDtypeStruct((n, LANES), BF16)],
        compiler_params=_params("arbitrary"),
        name="mla_proj",
    )(h, positions, g, wqa, qnorm, wqb, wkva, kvnorm, wkvb, invf, sgn)

    tq = tk = min(512, seq)
    nq = seq // tq
    scale = float(NOPE + ROPE) ** -0.5
    qmap = lambda b, i, j: (b * nq + i, 0)
    kmap = lambda b, i, j: (b * nq + jnp.minimum(j, i), 0)
    return pl.pallas_call(
        functools.partial(_attn_kernel, heads=heads, tq=tq, tk=tk, scale=scale),
        grid=(batch, nq, nq),
        in_specs=[pl.BlockSpec((tq, hn), qmap),
                  pl.BlockSpec((tq, hn), qmap),
                  pl.BlockSpec((tk, hn), kmap),
                  pl.BlockSpec((tk, LANES), kmap),
                  pl.BlockSpec((tk, hn), kmap),
                  pl.BlockSpec((tq, d), qmap),
                  pl.BlockSpec(wo.shape, lambda b, i, j: (0, 0))],
        out_specs=pl.BlockSpec((tq, d), qmap),
        out_shape=jax.ShapeDtypeStruct((n, d), F32),
        scratch_shapes=[pltpu.VMEM((heads, tq, 1), F32), pltpu.VMEM((heads, tq, 1), F32),
                        pltpu.VMEM((tq, hn), F32)],
        compiler_params=_params("arbitrary", "arbitrary", "arbitrary"),
        name="mla_attention",
    )(qn, qr, kn, kr, v, h, wo)


def _route_kernel(h_ref, g_ref, r_ref, u_ref, info_ref, cnt_ref, carry, *, tr, n_exp):
    i = pl.program_id(0)

    @pl.when(i == 0)
    def _():
        carry[...] = jnp.zeros(carry.shape, F32)

    u = _rms(h_ref[...], g_ref[...])
    u_ref[...] = u
    logits = jnp.dot(u, r_ref[...], preferred_element_type=F32, precision=lax.Precision.HIGHEST)
    lane = lax.broadcasted_iota(jnp.int32, (tr, LANES), 1)
    lg = jnp.where(lane < n_exp, logits, -jnp.inf)
    m1 = jnp.max(lg, axis=1, keepdims=True)
    i1 = jnp.min(jnp.where(lg == m1, lane, LANES), axis=1, keepdims=True)
    lg2 = jnp.where(lane == i1, -jnp.inf, lg)
    m2 = jnp.max(lg2, axis=1, keepdims=True)
    i2 = jnp.min(jnp.where(lg2 == m2, lane, LANES), axis=1, keepdims=True)
    e = jnp.exp(m2 - m1)
    g1 = 1.0 / (1.0 + e)
    g2 = e / (1.0 + e)
    sel1 = lane == i1
    sel2 = lane == i2
    onehot = jnp.where(sel1 | sel2, 1.0, 0.0)
    rr = lax.broadcasted_iota(jnp.int32, (tr, tr), 0)
    cc = lax.broadcasted_iota(jnp.int32, (tr, tr), 1)
    tri = jnp.where(cc < rr, 1.0, 0.0).astype(BF16)
    before = _dot(tri, onehot.astype(BF16)) + carry[0:1, :]
    rank1 = jnp.sum(jnp.where(sel1, before, 0.0), axis=1, keepdims=True)
    rank2 = jnp.sum(jnp.where(sel2, before, 0.0), axis=1, keepdims=True)
    carry[...] = carry[...] + jnp.sum(onehot, axis=0, keepdims=True)
    cols = (i1.astype(F32), i2.astype(F32), g1, g2, rank1, rank2)
    info = jnp.zeros((tr, LANES), F32)
    for k, col in enumerate(cols):
        info = jnp.where(lane == k, col, info)
    info_ref[...] = info
    cnt_ref[...] = carry[...]


def _route(h, g, router_padded, n_exp):
    n, d = h.shape
    tr = min(512, n)
    return pl.pallas_call(
        functools.partial(_route_kernel, tr=tr, n_exp=n_exp),
        grid=(n // tr,),
        in_specs=[pl.BlockSpec((tr, d), lambda i: (i, 0)),
                  pl.BlockSpec((1, d), lambda i: (0, 0)),
                  pl.BlockSpec((d, LANES), lambda i: (0, 0))],
        out_specs=[pl.BlockSpec((tr, d), lambda i: (i, 0)),
                   pl.BlockSpec((tr, LANES), lambda i: (i, 0)),
                   pl.BlockSpec((8, LANES), lambda i: (0, 0))],
        out_shape=[jax.ShapeDtypeStruct((n, d), F32),
                   jax.ShapeDtypeStruct((n, LANES), F32),
                   jax.ShapeDtypeStruct((8, LANES), F32)],
        scratch_shapes=[pltpu.VMEM((8, LANES), F32)],
        compiler_params=_params("arbitrary"),
        name="moe_route",
    )(h, g, router_padded)


def _row_copy(src_ref, dst_ref, sem, src_row, dst_row):
    return pltpu.make_async_copy(src_ref.at[pl.ds(src_row, 1)], dst_ref.at[pl.ds(dst_row, 1)], sem)


def _gather_rows(idx_ref, k, src_ref, dst_ref, sem, rows):
    def start(r, c):
        _row_copy(src_ref, dst_ref, sem, idx_ref[0, k, r], r).start()
        return c

    lax.fori_loop(0, rows, start, 0, unroll=8)

    def wait(r, c):
        _row_copy(src_ref, dst_ref, sem, 0, r).wait()
        return c

    lax.fori_loop(0, rows, wait, 0, unroll=8)


def _dispatch_kernel(idx_ref, src_ref, o_ref, sem, *, rows):
    _gather_rows(idx_ref, 0, src_ref, o_ref, sem, rows)


def _dispatch(src, inv, tile):
    rows = inv.shape[0]
    d = src.shape[1]
    return pl.pallas_call(
        functools.partial(_dispatch_kernel, rows=tile),
        grid=(rows // tile,),
        in_specs=[pl.BlockSpec((1, 1, tile), lambda i: (i, 0, 0), memory_space=pltpu.SMEM),
                  pl.BlockSpec(memory_space=pl.ANY)],
        out_specs=pl.BlockSpec((tile, d), lambda i: (i, 0)),
        out_shape=jax.ShapeDtypeStruct((rows, d), src.dtype),
        scratch_shapes=[pltpu.SemaphoreType.DMA(())],
        compiler_params=_params("arbitrary"),
        name="moe_dispatch",
    )(inv.reshape(rows // tile, 1, tile), src)


def _moe_kernel(te_ref, nu_ref, x_ref, wg_ref, wu_ref, wd_ref, o_ref, x_sc, acc_sc):
    t = pl.program_id(0)
    j = pl.program_id(1)

    @pl.when(t < nu_ref[0])
    def _():
        @pl.when(j == 0)
        def _():
            x_sc[...] = x_ref[...].astype(BF16)
            acc_sc[...] = jnp.zeros(acc_sc.shape, F32)

        acc_sc[...] += _swiglu_step(x_sc[...], wg_ref[0], wu_ref[0], wd_ref[0])

        @pl.when(j == pl.num_programs(1) - 1)
        def _():
            o_ref[...] = acc_sc[...]

    @pl.when((t >= nu_ref[0]) & (j == 0))
    def _():
        o_ref[...] = jnp.zeros(o_ref.shape, F32)


def _moe_experts(x_sorted, tile_expert, n_used, wg, wu, wd, tm):
    rows, d = x_sorted.shape
    f = wg.shape[2]
    tf = min(512, f)
    nj = f // tf
    tmap = lambda t, j, te, nu: (jnp.minimum(t, nu[0] - 1), 0)
    jsel = lambda t, j, nu: jnp.where(t < nu[0], j, nj - 1)
    grid_spec = pltpu.PrefetchScalarGridSpec(
        num_scalar_prefetch=2,
        grid=(rows // tm, nj),
        in_specs=[pl.BlockSpec((tm, d), tmap),
                  pl.BlockSpec((1, d, tf), lambda t, j, te, nu: (te[t], 0, jsel(t, j, nu))),
                  pl.BlockSpec((1, d, tf), lambda t, j, te, nu: (te[t], 0, jsel(t, j, nu))),
                  pl.BlockSpec((1, tf, d), lambda t, j, te, nu: (te[t], jsel(t, j, nu), 0))],
        out_specs=pl.BlockSpec((tm, d), lambda t, j, te, nu: (t, 0)),
        scratch_shapes=[pltpu.VMEM((tm, d), BF16), pltpu.VMEM((tm, d), F32)],
    )
    return pl.pallas_call(
        _moe_kernel,
        grid_spec=grid_spec,
        out_shape=jax.ShapeDtypeStruct((rows, d), F32),
        compiler_params=_params("arbitrary", "arbitrary"),
        name="moe_experts",
    )(tile_expert, n_used, x_sorted, wg, wu, wd)


def _combine_kernel(pos_ref, h_ref, gate_ref, gf_ref, y_ref, o_ref, y0, y1, sem, *, rows, final_norm):
    _gather_rows(pos_ref, 0, y_ref, y0, sem.at[0], rows)
    _gather_rows(pos_ref, 1, y_ref, y1, sem.at[1], rows)
    gates = gate_ref[...]
    out = h_ref[...] + gates[:, 2:3] * y0[...] + gates[:, 3:4] * y1[...]
    if final_norm:
        out = _rms(out, gf_ref[...])
    o_ref[...] = out


def _combine(h, info, pos, y_sorted, g_final, final_norm):
    n, d = h.shape
    tile = min(256, n)
    pos_t = pos.T.reshape(TOP_K, n // tile, tile).transpose(1, 0, 2)
    return pl.pallas_call(
        functools.partial(_combine_kernel, rows=tile, final_norm=final_norm),
        grid=(n // tile,),
        in_specs=[pl.BlockSpec((1, TOP_K, tile), lambda i: (i, 0, 0), memory_space=pltpu.SMEM),
                  pl.BlockSpec((tile, d), lambda i: (i, 0)),
                  pl.BlockSpec((tile, LANES), lambda i: (i, 0)),
                  pl.BlockSpec((1, d), lambda i: (0, 0)),
                  pl.BlockSpec(memory_space=pl.ANY)],
        out_specs=pl.BlockSpec((tile, d), lambda i: (i, 0)),
        out_shape=jax.ShapeDtypeStruct((n, d), F32),
        scratch_shapes=[pltpu.VMEM((tile, d), F32), pltpu.VMEM((tile, d), F32),
                        pltpu.SemaphoreType.DMA((2,))],
        compiler_params=_params("arbitrary"),
        name="moe_combine",
    )(pos_t, h, info, g_final, y_sorted)


def _moe_layer(h, g, router, wg, wu, wd, g_final, final_norm):
    n, d = h.shape
    n_exp = router.shape[1]
    tm = min(512, n)
    router_padded = jnp.pad(router, ((0, 0), (0, LANES - n_exp)))
    u, info, cnt = _route(h, g, router_padded, n_exp)

    idx = info[:, 0:TOP_K].astype(jnp.int32)
    rank = info[:, 4:4 + TOP_K].astype(jnp.int32)
    counts = cnt[0, :n_exp].astype(jnp.int32)
    padded = (counts + tm - 1) // tm * tm
    ends = jnp.cumsum(padded)
    starts = ends - padded
    pos = jnp.sum(jnp.where(idx[..., None] == jnp.arange(n_exp), starts, 0), axis=-1) + rank
    rows = TOP_K * n + n_exp * tm
    token = jnp.broadcast_to(jnp.arange(n, dtype=jnp.int32)[:, None], (n, TOP_K))
    inv = jnp.zeros((rows,), jnp.int32).at[pos.reshape(-1)].set(token.reshape(-1))
    n_used = (ends[-1] // tm).astype(jnp.int32)
    tile_ids = jnp.minimum(jnp.arange(rows // tm, dtype=jnp.int32), n_used - 1)
    tile_expert = jnp.sum(tile_ids[:, None] * tm >= ends[None, :], axis=1).astype(jnp.int32)

    x_sorted = _dispatch(u, inv, min(256, n))
    y_sorted = _moe_experts(x_sorted, tile_expert, n_used.reshape(1), wg, wu, wd, tm)
    return _combine(h, info, pos, y_sorted, g_final, final_norm)


def _mla_weights(wq_b, wkv_a, wkv_b, heads, lora):
    def padded_rope(w):
        pad = jnp.zeros(w.shape[:-1] + (LANES - ROPE,), w.dtype)
        swapped = jnp.concatenate([w[..., ROPE // 2:], w[..., :ROPE // 2]], axis=-1)
        return jnp.concatenate([w, pad], axis=-1), jnp.concatenate([swapped, pad], axis=-1)

    rq = wq_b.shape[0]
    wq = wq_b.reshape(rq, heads, NOPE + ROPE)
    q_r, q_rs = padded_rope(wq[..., NOPE:])
    wqb = jnp.concatenate([wq[..., :NOPE].reshape(rq, -1), q_r.reshape(rq, -1), q_rs.reshape(rq, -1)], axis=1)
    k_r, k_rs = padded_rope(wkv_a[:, lora:])
    wkva = jnp.concatenate([wkv_a[:, :lora], k_r, k_rs], axis=1)
    wkv = wkv_b.reshape(lora, heads, 2 * NOPE)
    wkvb = jnp.concatenate([wkv[..., :NOPE].reshape(lora, -1), wkv[..., NOPE:].reshape(lora, -1)], axis=1)
    return wqb.astype(BF16), wkva.astype(BF16), wkvb.astype(BF16)


@jax.jit
def _trunk(x, positions, norm_mix, norm_ffn, norm_final, pool_w, pool_scale,
           conv_w1, conv_b1, conv_dw, conv_bdw, conv_ln_g, conv_ln_b, conv_w2, conv_b2,
           mla_wq_a, mla_q_norm, mla_wq_b, mla_wkv_a, mla_kv_norm, mla_wkv_b, mla_wo,
           ffn_wg, ffn_wu, ffn_wd, moe_router, moe_wg, moe_wu, moe_wd):
    batch, seq, d = x.shape
    depth = norm_mix.shape[0]
    lora = mla_kv_norm.shape[1]
    heads = mla_wo.shape[1] // NOPE
    assert mla_wq_b.shape[2] == heads * (NOPE + ROPE) and mla_wkv_a.shape[2] == lora + ROPE
    n = batch * seq
    h = x.reshape(n, d)
    pos = positions.reshape(n, 1)
    row = lambda v: v.reshape(1, -1)

    half = jnp.arange(0, ROPE, 2, dtype=F32) / ROPE
    inv_freq = ROPE_THETA ** (-half)
    invf = row(jnp.concatenate([inv_freq, inv_freq, jnp.zeros((LANES - ROPE,), F32)]))
    sgn = row(jnp.concatenate([-jnp.ones((ROPE // 2,), F32), jnp.ones((ROPE // 2,), F32),
                               jnp.zeros((LANES - ROPE,), F32)]))

    for i in range(depth):
        kind, j = i % 3, i // 3
        g = row(norm_mix[i])
        if kind == 0:
            h = _pool_layer(h, g, pool_w[j].astype(BF16), row(pool_scale[j]), batch=batch, seq=seq)
        elif kind == 1:
            h = _conv_layer(h, g, conv_w1[j].astype(BF16), row(conv_b1[j]), conv_dw[j], row(conv_bdw[j]),
                            row(conv_ln_g[j]), row(conv_ln_b[j]), conv_w2[j].astype(BF16), row(conv_b2[j]),
                            batch=batch, seq=seq)
        else:
            wqb, wkva, wkvb = _mla_weights(mla_wq_b[j], mla_wkv_a[j], mla_wkv_b[j], heads, lora)
            h = _mla_layer(h, pos, g, mla_wq_a[j].astype(BF16), row(mla_q_norm[j]), wqb, wkva,
                           row(mla_kv_norm[j]), wkvb, mla_wo[j].astype(BF16), invf, sgn,
                           batch=batch, seq=seq, heads=heads, lora=lora)
        g = row(norm_ffn[i])
        f_idx = i // 2
        last = i == depth - 1
        if i % 2 == 0:
            h = _ffn_layer(h, g, ffn_wg[f_idx].astype(BF16), ffn_wu[f_idx].astype(BF16),
                           ffn_wd[f_idx].astype(BF16))
            if last:
                h = _final_norm(h, row(norm_final))
        else:
            h = _moe_layer(h, g, moe_router[f_idx], moe_wg[f_idx].astype(BF16), moe_wu[f_idx].astype(BF16),
                           moe_wd[f_idx].astype(BF16), row(norm_final), last)
    return h.reshape(batch, seq, d)


def _final_norm_kernel(h_ref, g_ref, o_ref):
    o_ref[...] = _rms(h_ref[...], g_ref[...])


def _final_norm(h, g):
    n, d = h.shape
    tm = min(512, n)
    return pl.pallas_call(
        _final_norm_kernel,
        grid=(n // tm,),
        in_specs=[pl.BlockSpec((tm, d), lambda i: (i, 0)), pl.BlockSpec((1, d), lambda i: (0, 0))],
        out_specs=pl.BlockSpec((tm, d), lambda i: (i, 0)),
        out_shape=jax.ShapeDtypeStruct((n, d), F32),
        compiler_params=_params("arbitrary"),
        name="final_norm",
    )(h, g)


def kernel(x, positions, norm_mix, norm_ffn, norm_final, pool_w, pool_scale, conv_w1, conv_b1, conv_dw, conv_bdw, conv_ln_g, conv_ln_b, conv_w2, conv_b2, mla_wq_a, mla_q_norm, mla_wq_b, mla_wkv_a, mla_kv_norm, mla_wkv_b, mla_wo, ffn_wg, ffn_wu, ffn_wd, moe_router, moe_wg, moe_wu, moe_wd):
    return _trunk(x, positions, norm_mix, norm_ffn, norm_final, pool_w, pool_scale,
                  conv_w1, conv_b1, conv_dw, conv_bdw, conv_ln_g, conv_ln_b, conv_w2, conv_b2,
                  mla_wq_a, mla_q_norm, mla_wq_b, mla_wkv_a, mla_kv_norm, mla_wkv_b, mla_wo,
                  ffn_wg, ffn_wu, ffn_wd, moe_router, moe_wg, moe_wu, moe_wd)
```

```python
import functools

import jax
import jax.numpy as jnp
from jax import lax
from jax.experimental import pallas as pl
from jax.experimental.pallas import tpu as pltpu

F32 = jnp.float32
BF16 = jnp.bfloat16

NORM_EPS = 1e-6
LN_EPS = 1e-5
ROPE_THETA = 10000.0
LOG2E = 1.4426950408889634

POOL_WINDOWS = (2, 4, 8, 16)
POOL_HALO = 16
CONV_HALO = 32
LANES = 128
NOPE = 128
ROPE = 64
TOP_K = 2

VMEM_LIMIT = 56 * 1024 * 1024


def _params(*sem):
    return pltpu.CompilerParams(dimension_semantics=sem, vmem_limit_bytes=VMEM_LIMIT)


def _rms(x, g, eps=NORM_EPS):
    return x * lax.rsqrt(jnp.mean(x * x, axis=-1, keepdims=True) + eps) * g


def _dot(a, b):
    return jnp.dot(a, b, preferred_element_type=F32)


def _dot_nt(a, b):
    return lax.dot_general(a, b, (((1,), (1,)), ((), ())), preferred_element_type=F32)


def _pool_kernel(h_ref, g_ref, w_ref, sc_ref, o_ref, ubuf, *, ts, gdim):
    j = pl.program_id(1)
    h = h_ref[...]
    u = _rms(h, g_ref[...])

    @pl.when(j == 0)
    def _():
        ubuf[0:POOL_HALO, :] = jnp.zeros((POOL_HALO, ubuf.shape[1]), F32)

    ubuf[POOL_HALO:POOL_HALO + ts, :] = u
    t1 = lax.broadcasted_iota(jnp.int32, (ts, 1), 0) + (j * ts + 1)
    for g, win in enumerate(POOL_WINDOWS):
        cs = slice(g * gdim, (g + 1) * gdim)
        ug = u[:, cs]
        s = ug
        for k in range(1, win):
            s = s + ubuf[POOL_HALO - k:POOL_HALO - k + ts, cs]
        cnt = jnp.minimum(t1, win).astype(F32)
        pooled = s / cnt - ug
        m = _dot(pooled.astype(BF16), w_ref[g])
        o_ref[:, cs] = h[:, cs] + m * sc_ref[:, cs]
    ubuf[0:POOL_HALO, :] = ubuf[ts:ts + POOL_HALO, :]


def _pool_layer(h, g, w, scale, *, batch, seq):
    n, d = h.shape
    ts = min(512, seq)
    nt = seq // ts
    gdim = d // len(POOL_WINDOWS)
    row = lambda b, j: (b * nt + j, 0)
    const2 = lambda b, j: (0, 0)
    return pl.pallas_call(
        functools.partial(_pool_kernel, ts=ts, gdim=gdim),
        grid=(batch, nt),
        in_specs=[pl.BlockSpec((ts, d), row),
                  pl.BlockSpec((1, d), const2),
                  pl.BlockSpec(w.shape, lambda b, j: (0, 0, 0)),
                  pl.BlockSpec((1, d), const2)],
        out_specs=pl.BlockSpec((ts, d), row),
        out_shape=jax.ShapeDtypeStruct((n, d), F32),
        scratch_shapes=[pltpu.VMEM((ts + POOL_HALO, d), F32)],
        compiler_params=_params("arbitrary", "arbitrary"),
        name="pool_mixer",
    )(h, g, w, scale)


def _swiglu_step(u, wg, wu, wd):
    a = _dot(u, wg)
    b = _dot(u, wu)
    hh = a * jax.nn.sigmoid(a) * b
    return _dot(hh.astype(BF16), wd)


def _ffn_kernel(h_ref, g_ref, wg_ref, wu_ref, wd_ref, o_ref, u_sc, acc_sc):
    j = pl.program_id(1)

    @pl.when(j == 0)
    def _():
        u_sc[...] = _rms(h_ref[...], g_ref[...]).astype(BF16)
        acc_sc[...] = jnp.zeros(acc_sc.shape, F32)

    acc_sc[...] += _swiglu_step(u_sc[...], wg_ref[...], wu_ref[...], wd_ref[...])

    @pl.when(j == pl.num_programs(1) - 1)
    def _():
        o_ref[...] = h_ref[...] + acc_sc[...]


def _ffn_layer(h, g, wg, wu, wd):
    n, d = h.shape
    f = wg.shape[1]
    tm = min(512, n)
    tf = min(512, f)
    return pl.pallas_call(
        _ffn_kernel,
        grid=(n // tm, f // tf),
        in_specs=[pl.BlockSpec((tm, d), lambda i, j: (i, 0)),
                  pl.BlockSpec((1, d), lambda i, j: (0, 0)),
                  pl.BlockSpec((d, tf), lambda i, j: (0, j)),
                  pl.BlockSpec((d, tf), lambda i, j: (0, j)),
                  pl.BlockSpec((tf, d), lambda i, j: (j, 0))],
        out_specs=pl.BlockSpec((tm, d), lambda i, j: (i, 0)),
        out_shape=jax.ShapeDtypeStruct((n, d), F32),
        scratch_shapes=[pltpu.VMEM((tm, d), BF16), pltpu.VMEM((tm, d), F32)],
        compiler_params=_params("arbitrary", "arbitrary"),
        name="dense_swiglu",
    )(h, g, wg, wu, wd)


def _conv_kernel(h_ref, g_ref, w1_ref, b1_ref, dw_ref, bdw_ref, lng_ref, lnb_ref, w2_ref, b2_ref,
                 o_ref, gbuf, cbuf, *, ts, width, rb):
    j = pl.program_id(1)
    d = h_ref.shape[1]
    h = h_ref[...]
    u = _rms(h, g_ref[...]).astype(BF16)
    a = _dot(u, w1_ref[...]) + b1_ref[...]
    glu = a[:, :d] * jax.nn.sigmoid(a[:, d:])

    @pl.when(j == 0)
    def _():
        gbuf[0:CONV_HALO, :] = jnp.zeros((CONV_HALO, d), F32)

    gbuf[CONV_HALO:CONV_HALO + ts, :] = glu
    for cb in range(d // LANES):
        cs = slice(cb * LANES, (cb + 1) * LANES)
        for r0 in range(0, ts, rb):
            acc = jnp.zeros((rb, LANES), F32)
            for k in range(width):
                off = CONV_HALO + r0 + k - (width - 1)
                acc = acc + gbuf[off:off + rb, cs] * dw_ref[k:k + 1, cs]
            cbuf[r0:r0 + rb, cs] = acc + bdw_ref[:, cs]
    gbuf[0:CONV_HALO, :] = gbuf[ts:ts + CONV_HALO, :]

    c = cbuf[...]
    mu = jnp.mean(c, axis=-1, keepdims=True)
    cc = c - mu
    var = jnp.mean(cc * cc, axis=-1, keepdims=True)
    un = cc * lax.rsqrt(var + LN_EPS) * lng_ref[...] + lnb_ref[...]
    un = un * jax.nn.sigmoid(un)
    o_ref[...] = h + _dot(un.astype(BF16), w2_ref[...]) + b2_ref[...]


def _conv_layer(h, g, w1, b1, dw, bdw, lng, lnb, w2, b2, *, batch, seq):
    n, d = h.shape
    width = dw.shape[0]
    assert width - 1 <= CONV_HALO
    ts = min(256, seq)
    rb = min(128, ts)
    nt = seq // ts
    row = lambda b, j: (b * nt + j, 0)
    c2 = lambda b, j: (0, 0)
    return pl.pallas_call(
        functools.partial(_conv_kernel, ts=ts, width=width, rb=rb),
        grid=(batch, nt),
        in_specs=[pl.BlockSpec((ts, d), row),
                  pl.BlockSpec((1, d), c2),
                  pl.BlockSpec(w1.shape, c2),
                  pl.BlockSpec(b1.shape, c2),
                  pl.BlockSpec(dw.shape, c2),
                  pl.BlockSpec((1, d), c2),
                  pl.BlockSpec((1, d), c2),
                  pl.BlockSpec((1, d), c2),
                  pl.BlockSpec(w2.shape, c2),
                  pl.BlockSpec((1, d), c2)],
        out_specs=pl.BlockSpec((ts, d), row),
        out_shape=jax.ShapeDtypeStruct((n, d), F32),
        scratch_shapes=[pltpu.VMEM((ts + CONV_HALO, d), F32), pltpu.VMEM((ts, d), F32)],
        compiler_params=_params("arbitrary", "arbitrary"),
        name="conv_module",
    )(h, g, w1, b1, dw, bdw, lng, lnb, w2, b2)


def _mla_proj_kernel(h_ref, pos_ref, g_ref, wqa_ref, qnorm_ref, wqb_ref, wkva_ref, kvnorm_ref, wkvb_ref,
                     invf_ref, sgn_ref, q_o, k_o, v_o, *, heads, lora):
    hn = heads * NOPE
    u = _rms(h_ref[...], g_ref[...]).astype(BF16)
    qa = _rms(_dot(u, wqa_ref[...]), qnorm_ref[...]).astype(BF16)
    q = _dot(qa, wqb_ref[...])
    kva = _dot(u, wkva_ref[...])
    ckv = _rms(kva[:, :lora], kvnorm_ref[...]).astype(BF16)
    kv = _dot(ckv, wkvb_ref[...])
    ang = pos_ref[...].astype(F32) * invf_ref[...]
    c = jnp.cos(ang)
    s = jnp.sin(ang) * sgn_ref[...]
    kr = (kva[:, lora:lora + LANES] * c + kva[:, lora + LANES:lora + 2 * LANES] * s).astype(BF16)
    for hd in range(heads):
        ns = slice(hd * NOPE, (hd + 1) * NOPE)
        qr = q[:, hn + hd * LANES:hn + (hd + 1) * LANES] * c + q[:, 2 * hn + hd * LANES:2 * hn + (hd + 1) * LANES] * s
        q_o[:, 2 * hd * NOPE:(2 * hd + 1) * NOPE] = q[:, ns].astype(BF16)
        q_o[:, (2 * hd + 1) * NOPE:(2 * hd + 2) * NOPE] = qr.astype(BF16)
        k_o[:, 2 * hd * NOPE:(2 * hd + 1) * NOPE] = kv[:, ns].astype(BF16)
        k_o[:, (2 * hd + 1) * NOPE:(2 * hd + 2) * NOPE] = kr
    v_o[...] = kv[:, hn:].astype(BF16)


def _attn_kernel(q_ref, k_ref, v_ref, h_ref, wo_ref, o_ref, m_sc, l_sc, acc_sc, *, heads, tq, tk, scale):
    i = pl.program_id(1)
    j = pl.program_id(2)
    c = scale * LOG2E

    @pl.when(j == 0)
    def _():
        m_sc[...] = jnp.full(m_sc.shape, -jnp.inf, F32)
        l_sc[...] = jnp.zeros(l_sc.shape, F32)
        acc_sc[...] = jnp.zeros(acc_sc.shape, F32)

    def step(diagonal):
        if diagonal:
            causal = (lax.broadcasted_iota(jnp.int32, (tq, tk), 1)
                      <= lax.broadcasted_iota(jnp.int32, (tq, tk), 0))
        for hd in range(heads):
            qs = slice(2 * hd * NOPE, (2 * hd + 2) * NOPE)
            hs = slice(hd * NOPE, (hd + 1) * NOPE)
            s = _dot_nt(q_ref[:, qs], k_ref[:, qs])
            if diagonal:
                s = jnp.where(causal, s, -jnp.inf)
            m_prev = m_sc[hd]
            m_new = jnp.maximum(m_prev, jnp.max(s, axis=-1, keepdims=True))
            alpha = jnp.exp2((m_prev - m_new) * c)
            p = jnp.exp2((s - m_new) * c)
            l_sc[hd] = alpha * l_sc[hd] + jnp.sum(p, axis=-1, keepdims=True)
            acc_sc[:, hs] = alpha * acc_sc[:, hs] + _dot(p.astype(BF16), v_ref[:, hs])
            m_sc[hd] = m_new

    @pl.when(j < i)
    def _():
        step(False)

    @pl.when(j == i)
    def _():
        step(True)
        parts = [acc_sc[:, hd * NOPE:(hd + 1) * NOPE] / l_sc[hd] for hd in range(heads)]
        o = jnp.concatenate(parts, axis=1).astype(BF16)
        o_ref[...] = h_ref[...] + _dot(o, wo_ref[...])


def _mla_layer(h, positions, g, wqa, qnorm, wqb, wkva, kvnorm, wkvb, wo, invf, sgn, *, batch, seq, heads, lora):
    n, d = h.shape
    hn = heads * NOPE
    tp = min(256, n)
    c2 = lambda i: (0, 0)
    row = lambda i: (i, 0)
    q, k, v = pl.pallas_call(
        functools.partial(_mla_proj_kernel, heads=heads, lora=lora),
        grid=(n // tp,),
        in_specs=[pl.BlockSpec((tp, d), row),
                  pl.BlockSpec((tp, 1), row),
                  pl.BlockSpec((1, d), c2),
                  pl.BlockSpec(wqa.shape, c2),
                  pl.BlockSpec(qnorm.shape, c2),
                  pl.BlockSpec(wqb.shape, c2),
                  pl.BlockSpec(wkva.shape, c2),
                  pl.BlockSpec(kvnorm.shape, c2),
                  pl.BlockSpec(wkvb.shape, c2),
                  pl.BlockSpec((1, LANES), c2),
                  pl.BlockSpec((1, LANES), c2)],
        out_specs=[pl.BlockSpec((tp, 2 * hn), row), pl.BlockSpec((tp, 2 * hn), row), pl.BlockSpec((tp, hn), row)],
        out_shape=[jax.ShapeDtypeStruct((n, 2 * hn), BF16), jax.ShapeDtypeStruct((n, 2 * hn), BF16),
                   jax.ShapeDtypeStruct((n, hn), BF16)],
        compiler_params=_params("arbitrary"),
        name="mla_proj",
    )(h, positions, g, wqa, qnorm, wqb, wkva, kvnorm, wkvb, invf, sgn)

    tq = tk = min(512, seq)
    nq = seq // tq
    scale = float(NOPE + ROPE) ** -0.5
    qmap = lambda b, i, j: (b * nq + i, 0)
    kmap = lambda b, i, j: (b * nq + jnp.minimum(j, i), 0)
    return pl.pallas_call(
        functools.partial(_attn_kernel, heads=heads, tq=tq, tk=tk, scale=scale),
        grid=(batch, nq, nq),
        in_specs=[pl.BlockSpec((tq, 2 * hn), qmap),
                  pl.BlockSpec((tk, 2 * hn), kmap),
                  pl.BlockSpec((tk, hn), kmap),
                  pl.BlockSpec((tq, d), qmap),
                  pl.BlockSpec(wo.shape, lambda b, i, j: (0, 0))],
        out_specs=pl.BlockSpec((tq, d), qmap),
        out_shape=jax.ShapeDtypeStruct((n, d), F32),
        scratch_shapes=[pltpu.VMEM((heads, tq, 1), F32), pltpu.VMEM((heads, tq, 1), F32),
                        pltpu.VMEM((tq, hn), F32)],
        compiler_params=_params("arbitrary", "arbitrary", "arbitrary"),
        name="mla_attention",
    )(q, k, v, h, wo)


def _store_row_tiles(ref, value, rows):
    slabs = value.shape[1] // LANES
    for s in range(slabs):
        ref[pl.ds(s, rows, stride=slabs), :] = value[:, s * LANES:(s + 1) * LANES]


def _load_row_tile_slab(ref, s, rows, slabs):
    return ref[pl.ds(s, rows, stride=slabs), :]


def _route_kernel(h_ref, g_ref, r_ref, u_ref, info_ref, cnt_ref, carry, *, tr, n_exp):
    i = pl.program_id(0)

    @pl.when(i == 0)
    def _():
        carry[...] = jnp.zeros(carry.shape, F32)

    u = _rms(h_ref[...], g_ref[...])
    _store_row_tiles(u_ref, u, tr)
    logits = jnp.dot(u, r_ref[...], preferred_element_type=F32, precision=lax.Precision.HIGHEST)
    lane = lax.broadcasted_iota(jnp.int32, (tr, LANES), 1)
    lg = jnp.where(lane < n_exp, logits, -jnp.inf)
    m1 = jnp.max(lg, axis=1, keepdims=True)
    i1 = jnp.min(jnp.where(lg == m1, lane, LANES), axis=1, keepdims=True)
    lg2 = jnp.where(lane == i1, -jnp.inf, lg)
    m2 = jnp.max(lg2, axis=1, keepdims=True)
    i2 = jnp.min(jnp.where(lg2 == m2, lane, LANES), axis=1, keepdims=True)
    e = jnp.exp(m2 - m1)
    g1 = 1.0 / (1.0 + e)
    g2 = e / (1.0 + e)
    sel1 = lane == i1
    sel2 = lane == i2
    onehot = jnp.where(sel1 | sel2, 1.0, 0.0)
    rr = lax.broadcasted_iota(jnp.int32, (tr, tr), 0)
    cc = lax.broadcasted_iota(jnp.int32, (tr, tr), 1)
    tri = jnp.where(cc < rr, 1.0, 0.0).astype(BF16)
    before = _dot(tri, onehot.astype(BF16)) + carry[0:1, :]
    rank1 = jnp.sum(jnp.where(sel1, before, 0.0), axis=1, keepdims=True)
    rank2 = jnp.sum(jnp.where(sel2, before, 0.0), axis=1, keepdims=True)
    carry[...] = carry[...] + jnp.sum(onehot, axis=0, keepdims=True)
    cols = (i1.astype(F32), i2.astype(F32), g1, g2, rank1, rank2)
    info = jnp.zeros((tr, LANES), F32)
    for k, col in enumerate(cols):
        info = jnp.where(lane == k, col, info)
    info_ref[...] = info
    cnt_ref[...] = carry[...]


def _route(h, g, router_padded, n_exp):
    n, d = h.shape
    tr = min(512, n)
    slabs = d // LANES
    return pl.pallas_call(
        functools.partial(_route_kernel, tr=tr, n_exp=n_exp),
        grid=(n // tr,),
        in_specs=[pl.BlockSpec((tr, d), lambda i: (i, 0)),
                  pl.BlockSpec((1, d), lambda i: (0, 0)),
                  pl.BlockSpec((d, LANES), lambda i: (0, 0))],
        out_specs=[pl.BlockSpec((tr * slabs, LANES), lambda i: (i, 0)),
                   pl.BlockSpec((tr, LANES), lambda i: (i, 0)),
                   pl.BlockSpec((8, LANES), lambda i: (0, 0))],
        out_shape=[jax.ShapeDtypeStruct((n * slabs, LANES), F32),
                   jax.ShapeDtypeStruct((n, LANES), F32),
                   jax.ShapeDtypeStruct((8, LANES), F32)],
        scratch_shapes=[pltpu.VMEM((8, LANES), F32)],
        compiler_params=_params("arbitrary"),
        name="moe_route",
    )(h, g, router_padded)


def _gather_row_tiles(idx_ref, k, src_ref, dst_ref, sem, rows, slabs):
    def start(r, c):
        src_row = pl.multiple_of(idx_ref[0, k, r] * slabs, slabs)
        dst_row = pl.multiple_of(r * slabs, slabs)
        pltpu.make_async_copy(src_ref.at[pl.ds(src_row, slabs)], dst_ref.at[pl.ds(dst_row, slabs)], sem).start()
        return c

    lax.fori_loop(0, rows, start, 0, unroll=8)
    pltpu.make_async_copy(src_ref.at[pl.ds(0, rows * slabs)], dst_ref, sem).wait()


def _dispatch_kernel(idx_ref, src_ref, o_ref, sem, *, rows, slabs):
    _gather_row_tiles(idx_ref, 0, src_ref, o_ref, sem, rows, slabs)


def _dispatch(src, inv, tile, slabs):
    rows = inv.shape[0]
    return pl.pallas_call(
        functools.partial(_dispatch_kernel, rows=tile, slabs=slabs),
        grid=(rows // tile,),
        in_specs=[pl.BlockSpec((1, 1, tile), lambda i: (i, 0, 0), memory_space=pltpu.SMEM),
                  pl.BlockSpec(memory_space=pl.ANY)],
        out_specs=pl.BlockSpec((tile * slabs, LANES), lambda i: (i, 0)),
        out_shape=jax.ShapeDtypeStruct((rows * slabs, LANES), src.dtype),
        scratch_shapes=[pltpu.SemaphoreType.DMA(())],
        compiler_params=_params("arbitrary"),
        name="moe_dispatch",
    )(inv.reshape(rows // tile, 1, tile), src)


def _moe_kernel(te_ref, nu_ref, x_ref, wg_ref, wu_ref, wd_ref, o_ref, x_sc, acc_sc, *, tm, slabs):
    t = pl.program_id(0)
    j = pl.program_id(1)

    @pl.when(t < nu_ref[0])
    def _():
        @pl.when(j == 0)
        def _():
            for s in range(slabs):
                x_sc[:, s * LANES:(s + 1) * LANES] = _load_row_tile_slab(x_ref, s, tm, slabs).astype(BF16)
            acc_sc[...] = jnp.zeros(acc_sc.shape, F32)

        acc_sc[...] += _swiglu_step(x_sc[...], wg_ref[0], wu_ref[0], wd_ref[0])

        @pl.when(j == pl.num_programs(1) - 1)
        def _():
            _store_row_tiles(o_ref, acc_sc[...], tm)

    @pl.when((t >= nu_ref[0]) & (j == 0))
    def _():
        o_ref[...] = jnp.zeros(o_ref.shape, F32)


def _moe_experts(x_sorted, tile_expert, n_used, wg, wu, wd, tm, slabs):
    d = slabs * LANES
    rows = x_sorted.shape[0] // slabs
    f = wg.shape[2]
    tf = min(512, f)
    nj = f // tf
    tmap = lambda t, j, te, nu: (jnp.minimum(t, nu[0] - 1), 0)
    jsel = lambda t, j, nu: jnp.where(t < nu[0], j, nj - 1)
    grid_spec = pltpu.PrefetchScalarGridSpec(
        num_scalar_prefetch=2,
        grid=(rows // tm, nj),
        in_specs=[pl.BlockSpec((tm * slabs, LANES), tmap),
                  pl.BlockSpec((1, d, tf), lambda t, j, te, nu: (te[t], 0, jsel(t, j, nu))),
                  pl.BlockSpec((1, d, tf), lambda t, j, te, nu: (te[t], 0, jsel(t, j, nu))),
                  pl.BlockSpec((1, tf, d), lambda t, j, te, nu: (te[t], jsel(t, j, nu), 0))],
        out_specs=pl.BlockSpec((tm * slabs, LANES), lambda t, j, te, nu: (t, 0)),
        scratch_shapes=[pltpu.VMEM((tm, d), BF16), pltpu.VMEM((tm, d), F32)],
    )
    return pl.pallas_call(
        functools.partial(_moe_kernel, tm=tm, slabs=slabs),
        grid_spec=grid_spec,
        out_shape=jax.ShapeDtypeStruct((rows * slabs, LANES), F32),
        compiler_params=_params("arbitrary", "arbitrary"),
        name="moe_experts",
    )(tile_expert, n_used, x_sorted, wg, wu, wd)


def _combine_kernel(pos_ref, h_ref, gate_ref, gf_ref, y_ref, o_ref, y0, y1, sem, *, rows, slabs, final_norm):
    _gather_row_tiles(pos_ref, 0, y_ref, y0, sem.at[0], rows, slabs)
    _gather_row_tiles(pos_ref, 1, y_ref, y1, sem.at[1], rows, slabs)
    gates = gate_ref[...]
    g0 = gates[:, 2:3]
    g1 = gates[:, 3:4]
    ssq = jnp.zeros((rows, 1), F32)
    for s in range(slabs):
        cs = slice(s * LANES, (s + 1) * LANES)
        out = (h_ref[:, cs] + g0 * _load_row_tile_slab(y0, s, rows, slabs)
               + g1 * _load_row_tile_slab(y1, s, rows, slabs))
        o_ref[:, cs] = out
        ssq = ssq + jnp.sum(out * out, axis=-1, keepdims=True)
    if final_norm:
        d = slabs * LANES
        o_ref[...] = o_ref[...] * lax.rsqrt(ssq / d + NORM_EPS) * gf_ref[...]


def _combine(h, info, pos, y_sorted, g_final, final_norm, slabs):
    n, d = h.shape
    tile = min(512, n)
    pos_t = pos.T.reshape(TOP_K, n // tile, tile).transpose(1, 0, 2)
    return pl.pallas_call(
        functools.partial(_combine_kernel, rows=tile, slabs=slabs, final_norm=final_norm),
        grid=(n // tile,),
        in_specs=[pl.BlockSpec((1, TOP_K, tile), lambda i: (i, 0, 0), memory_space=pltpu.SMEM),
                  pl.BlockSpec((tile, d), lambda i: (i, 0)),
                  pl.BlockSpec((tile, LANES), lambda i: (i, 0)),
                  pl.BlockSpec((1, d), lambda i: (0, 0)),
                  pl.BlockSpec(memory_space=pl.ANY)],
        out_specs=pl.BlockSpec((tile, d), lambda i: (i, 0)),
        out_shape=jax.ShapeDtypeStruct((n, d), F32),
        scratch_shapes=[pltpu.VMEM((tile * slabs, LANES), F32), pltpu.VMEM((tile * slabs, LANES), F32),
                        pltpu.SemaphoreType.DMA((2,))],
        compiler_params=_params("arbitrary"),
        name="moe_combine",
    )(pos_t, h, info, g_final, y_sorted)


def _moe_layer(h, g, router, wg, wu, wd, g_final, final_norm):
    n, d = h.shape
    n_exp = router.shape[1]
    tm = min(512, n)
    router_padded = jnp.pad(router, ((0, 0), (0, LANES - n_exp)))
    u, info, cnt = _route(h, g, router_padded, n_exp)

    idx = info[:, 0:TOP_K].astype(jnp.int32)
    rank = info[:, 4:4 + TOP_K].astype(jnp.int32)
    counts = cnt[0, :n_exp].astype(jnp.int32)
    padded = (counts + tm - 1) // tm * tm
    ends = jnp.cumsum(padded)
    starts = ends - padded
    pos = jnp.sum(jnp.where(idx[..., None] == jnp.arange(n_exp), starts, 0), axis=-1) + rank
    rows = TOP_K * n + n_exp * tm
    token = jnp.broadcast_to(jnp.arange(n, dtype=jnp.int32)[:, None], (n, TOP_K))
    inv = jnp.zeros((rows,), jnp.int32).at[pos.reshape(-1)].set(token.reshape(-1))
    n_used = (ends[-1] // tm).astype(jnp.int32)
    tile_ids = jnp.minimum(jnp.arange(rows // tm, dtype=jnp.int32), n_used - 1)
    tile_expert = jnp.sum(tile_ids[:, None] * tm >= ends[None, :], axis=1).astype(jnp.int32)

    slabs = d // LANES
    assert slabs == 8
    x_sorted = _dispatch(u, inv, min(1024, n), slabs)
    y_sorted = _moe_experts(x_sorted, tile_expert, n_used.reshape(1), wg, wu, wd, tm, slabs)
    return _combine(h, info, pos, y_sorted, g_final, final_norm, slabs)


def _mla_weights(wq_b, wkv_a, wkv_b, heads, lora):
    def padded_rope(w):
        pad = jnp.zeros(w.shape[:-1] + (LANES - ROPE,), w.dtype)
        swapped = jnp.concatenate([w[..., ROPE // 2:], w[..., :ROPE // 2]], axis=-1)
        return jnp.concatenate([w, pad], axis=-1), jnp.concatenate([swapped, pad], axis=-1)

    rq = wq_b.shape[0]
    wq = wq_b.reshape(rq, heads, NOPE + ROPE)
    q_r, q_rs = padded_rope(wq[..., NOPE:])
    wqb = jnp.concatenate([wq[..., :NOPE].reshape(rq, -1), q_r.reshape(rq, -1), q_rs.reshape(rq, -1)], axis=1)
    k_r, k_rs = padded_rope(wkv_a[:, lora:])
    wkva = jnp.concatenate([wkv_a[:, :lora], k_r, k_rs], axis=1)
    wkv = wkv_b.reshape(lora, heads, 2 * NOPE)
    wkvb = jnp.concatenate([wkv[..., :NOPE].reshape(lora, -1), wkv[..., NOPE:].reshape(lora, -1)], axis=1)
    return wqb.astype(BF16), wkva.astype(BF16), wkvb.astype(BF16)


def _cast_kernel(x_ref, o_ref):
    o_ref[...] = x_ref[0].astype(BF16)


def _to_bf16(w, layer):
    cols = w.shape[-1]
    w3 = w.reshape(w.shape[0], -1, cols)
    rows = w3.shape[1]
    tr = min(256, rows)
    out = pl.pallas_call(
        _cast_kernel,
        grid=(rows // tr,),
        in_specs=[pl.BlockSpec((1, tr, cols), lambda i: (layer, i, 0))],
        out_specs=pl.BlockSpec((tr, cols), lambda i: (i, 0)),
        out_shape=jax.ShapeDtypeStruct((rows, cols), BF16),
        compiler_params=_params("arbitrary"),
        name="weights_to_bf16",
    )(w3)
    return out.reshape(w.shape[1:])


def _final_norm_kernel(h_ref, g_ref, o_ref):
    o_ref[...] = _rms(h_ref[...], g_ref[...])


def _final_norm(h, g):
    n, d = h.shape
    tm = min(512, n)
    return pl.pallas_call(
        _final_norm_kernel,
        grid=(n // tm,),
        in_specs=[pl.BlockSpec((tm, d), lambda i: (i, 0)), pl.BlockSpec((1, d), lambda i: (0, 0))],
        out_specs=pl.BlockSpec((tm, d), lambda i: (i, 0)),
        out_shape=jax.ShapeDtypeStruct((n, d), F32),
        compiler_params=_params("arbitrary"),
        name="final_norm",
    )(h, g)


@jax.jit
def _trunk(x, positions, norm_mix, norm_ffn, norm_final, pool_w, pool_scale,
           conv_w1, conv_b1, conv_dw, conv_bdw, conv_ln_g, conv_ln_b, conv_w2, conv_b2,
           mla_wq_a, mla_q_norm, mla_wq_b, mla_wkv_a, mla_kv_norm, mla_wkv_b, mla_wo,
           ffn_wg, ffn_wu, ffn_wd, moe_router, moe_wg, moe_wu, moe_wd):
    batch, seq, d = x.shape
    depth = norm_mix.shape[0]
    lora = mla_kv_norm.shape[1]
    heads = mla_wo.shape[1] // NOPE
    assert mla_wq_b.shape[2] == heads * (NOPE + ROPE) and mla_wkv_a.shape[2] == lora + ROPE
    n = batch * seq
    h = x.reshape(n, d)
    pos = positions.reshape(n, 1)
    row = lambda v: v.reshape(1, -1)

    half = jnp.arange(0, ROPE, 2, dtype=F32) / ROPE
    inv_freq = ROPE_THETA ** (-half)
    invf = row(jnp.concatenate([inv_freq, inv_freq, jnp.zeros((LANES - ROPE,), F32)]))
    sgn = row(jnp.concatenate([-jnp.ones((ROPE // 2,), F32), jnp.ones((ROPE // 2,), F32),
                               jnp.zeros((LANES - ROPE,), F32)]))

    for i in range(depth):
        kind, j = i % 3, i // 3
        g = row(norm_mix[i])
        if kind == 0:
            h = _pool_layer(h, g, pool_w[j].astype(BF16), row(pool_scale[j]), batch=batch, seq=seq)
        elif kind == 1:
            h = _conv_layer(h, g, conv_w1[j].astype(BF16), row(conv_b1[j]), conv_dw[j], row(conv_bdw[j]),
                            row(conv_ln_g[j]), row(conv_ln_b[j]), conv_w2[j].astype(BF16), row(conv_b2[j]),
                            batch=batch, seq=seq)
        else:
            wqb, wkva, wkvb = _mla_weights(mla_wq_b[j], mla_wkv_a[j], mla_wkv_b[j], heads, lora)
            h = _mla_layer(h, pos, g, mla_wq_a[j].astype(BF16), row(mla_q_norm[j]), wqb, wkva,
                           row(mla_kv_norm[j]), wkvb, mla_wo[j].astype(BF16), invf, sgn,
                           batch=batch, seq=seq, heads=heads, lora=lora)
        g = row(norm_ffn[i])
        f_idx = i // 2
        last = i == depth - 1
        if i % 2 == 0:
            h = _ffn_layer(h, g, _to_bf16(ffn_wg, f_idx), _to_bf16(ffn_wu, f_idx), _to_bf16(ffn_wd, f_idx))
            if last:
                h = _final_norm(h, row(norm_final))
        else:
            h = _moe_layer(h, g, moe_router[f_idx], _to_bf16(moe_wg, f_idx), _to_bf16(moe_wu, f_idx),
                           _to_bf16(moe_wd, f_idx), row(norm_final), last)
    return h.reshape(batch, seq, d)


def kernel(x, positions, norm_mix, norm_ffn, norm_final, pool_w, pool_scale, conv_w1, conv_b1, conv_dw, conv_bdw, conv_ln_g, conv_ln_b, conv_w2, conv_b2, mla_wq_a, mla_q_norm, mla_wq_b, mla_wkv_a, mla_kv_norm, mla_wkv_b, mla_wo, ffn_wg, ffn_wu, ffn_wd, moe_router, moe_wg, moe_wu, moe_wd):
    return _trunk(x, positions, norm_mix, norm_ffn, norm_final, pool_w, pool_scale,
                  conv_w1, conv_b1, conv_dw, conv_bdw, conv_ln_g, conv_ln_b, conv_w2, conv_b2,
                  mla_wq_a, mla_q_norm, mla_wq_b, mla_wkv_a, mla_kv_norm, mla_wkv_b, mla_wo,
                  ffn_wg, ffn_wu, ffn_wd, moe_router, moe_wg, moe_wu, moe_wd)
```

```python
import functools

import jax
import jax.numpy as jnp
from jax import lax
from jax.experimental import pallas as pl
from jax.experimental.pallas import tpu as pltpu

F32 = jnp.float32
BF16 = jnp.bfloat16

NORM_EPS = 1e-6
LN_EPS = 1e-5
ROPE_THETA = 10000.0
LOG2E = 1.4426950408889634

POOL_WINDOWS = (2, 4, 8, 16)
POOL_HALO = 16
CONV_HALO = 32
LANES = 128
SUBLANES = 8
NOPE = 128
ROPE = 64
TOP_K = 2
GATHER_UNROLL = 8

VMEM_LIMIT = 56 * 1024 * 1024


def _params(*sem):
    return pltpu.CompilerParams(dimension_semantics=sem, vmem_limit_bytes=VMEM_LIMIT)


def _rms(x, g, eps=NORM_EPS):
    return x * lax.rsqrt(jnp.mean(x * x, axis=-1, keepdims=True) + eps) * g


def _dot(a, b):
    return jnp.dot(a, b, preferred_element_type=F32)


def _ff_chunk(f):
    for c in (1792, 1024, 512, 256, 128):
        if f % c == 0:
            return c
    return f


def _pool_kernel(h_ref, g_ref, w_ref, sc_ref, o_ref, ubuf, *, ts, gdim):
    j = pl.program_id(1)
    h = h_ref[...]
    u = _rms(h, g_ref[...])

    @pl.when(j == 0)
    def _():
        ubuf[0:POOL_HALO, :] = jnp.zeros((POOL_HALO, ubuf.shape[1]), F32)

    ubuf[POOL_HALO:POOL_HALO + ts, :] = u
    t1 = lax.broadcasted_iota(jnp.int32, (ts, 1), 0) + (j * ts + 1)
    for g, win in enumerate(POOL_WINDOWS):
        cs = slice(g * gdim, (g + 1) * gdim)
        ug = u[:, cs]
        s = ug
        for k in range(1, win):
            s = s + ubuf[POOL_HALO - k:POOL_HALO - k + ts, cs]
        cnt = jnp.minimum(t1, win).astype(F32)
        pooled = s / cnt - ug
        m = _dot(pooled.astype(BF16), w_ref[g])
        o_ref[:, cs] = h[:, cs] + m * sc_ref[:, cs]
    ubuf[0:POOL_HALO, :] = ubuf[ts:ts + POOL_HALO, :]


def _pool_layer(h, g, w, scale, *, batch, seq):
    n, d = h.shape
    ts = min(512, seq)
    nt = seq // ts
    gdim = d // len(POOL_WINDOWS)
    row = lambda b, j: (b * nt + j, 0)
    const2 = lambda b, j: (0, 0)
    return pl.pallas_call(
        functools.partial(_pool_kernel, ts=ts, gdim=gdim),
        grid=(batch, nt),
        in_specs=[pl.BlockSpec((ts, d), row),
                  pl.BlockSpec((1, d), const2),
                  pl.BlockSpec(w.shape, lambda b, j: (0, 0, 0)),
                  pl.BlockSpec((1, d), const2)],
        out_specs=pl.BlockSpec((ts, d), row),
        out_shape=jax.ShapeDtypeStruct((n, d), F32),
        scratch_shapes=[pltpu.VMEM((ts + POOL_HALO, d), F32)],
        compiler_params=_params("arbitrary", "arbitrary"),
        name="pool_mixer",
    )(h, g, w, scale)


def _swiglu_step(u, wg, wu, wd):
    a = _dot(u, wg)
    b = _dot(u, wu)
    hh = a * jax.nn.sigmoid(a) * b
    return _dot(hh.astype(BF16), wd)


def _ffn_kernel(h_ref, g_ref, wg_ref, wu_ref, wd_ref, o_ref, u_sc, acc_sc):
    j = pl.program_id(1)

    @pl.when(j == 0)
    def _():
        u_sc[...] = _rms(h_ref[...], g_ref[...]).astype(BF16)
        acc_sc[...] = jnp.zeros(acc_sc.shape, F32)

    acc_sc[...] += _swiglu_step(u_sc[...], wg_ref[...], wu_ref[...], wd_ref[...])

    @pl.when(j == pl.num_programs(1) - 1)
    def _():
        o_ref[...] = h_ref[...] + acc_sc[...]


def _ffn_layer(h, g, wg, wu, wd):
    n, d = h.shape
    f = wg.shape[1]
    tm = min(512, n)
    tf = _ff_chunk(f)
    return pl.pallas_call(
        _ffn_kernel,
        grid=(n // tm, f // tf),
        in_specs=[pl.BlockSpec((tm, d), lambda i, j: (i, 0)),
                  pl.BlockSpec((1, d), lambda i, j: (0, 0)),
                  pl.BlockSpec((d, tf), lambda i, j: (0, j)),
                  pl.BlockSpec((d, tf), lambda i, j: (0, j)),
                  pl.BlockSpec((tf, d), lambda i, j: (j, 0))],
        out_specs=pl.BlockSpec((tm, d), lambda i, j: (i, 0)),
        out_shape=jax.ShapeDtypeStruct((n, d), F32),
        scratch_shapes=[pltpu.VMEM((tm, d), BF16), pltpu.VMEM((tm, d), F32)],
        compiler_params=_params("arbitrary", "arbitrary"),
        name="dense_swiglu",
    )(h, g, wg, wu, wd)


def _conv_kernel(h_ref, g_ref, w1_ref, b1_ref, dw_ref, bdw_ref, lng_ref, lnb_ref, w2_ref, b2_ref,
                 o_ref, gbuf, cbuf, *, ts, width, rb):
    j = pl.program_id(1)
    d = h_ref.shape[1]
    h = h_ref[...]
    u = _rms(h, g_ref[...]).astype(BF16)
    a = _dot(u, w1_ref[...]) + b1_ref[...]
    glu = a[:, :d] * jax.nn.sigmoid(a[:, d:])

    keep = CONV_HALO + SUBLANES

    @pl.when(j == 0)
    def _():
        gbuf[:, 0:keep, :] = jnp.zeros((SUBLANES, keep, d), F32)

    for b in range(SUBLANES):
        gbuf[b, CONV_HALO + b:CONV_HALO + b + ts, :] = glu
    for cb in range(d // LANES):
        cs = slice(cb * LANES, (cb + 1) * LANES)
        for r0 in range(0, ts, rb):
            acc = jnp.zeros((rb, LANES), F32)
            for k in range(width):
                a, b = divmod(width - 1 - k, SUBLANES)
                off = CONV_HALO + r0 - SUBLANES * a
                acc = acc + gbuf[b, off:off + rb, cs] * dw_ref[k:k + 1, cs]
            cbuf[r0:r0 + rb, cs] = acc + bdw_ref[:, cs]
    gbuf[:, 0:keep, :] = gbuf[:, ts:ts + keep, :]

    c = cbuf[...]
    mu = jnp.mean(c, axis=-1, keepdims=True)
    cc = c - mu
    var = jnp.mean(cc * cc, axis=-1, keepdims=True)
    un = cc * lax.rsqrt(var + LN_EPS) * lng_ref[...] + lnb_ref[...]
    un = un * jax.nn.sigmoid(un)
    o_ref[...] = h + _dot(un.astype(BF16), w2_ref[...]) + b2_ref[...]


def _conv_layer(h, g, w1, b1, dw, bdw, lng, lnb, w2, b2, *, batch, seq):
    n, d = h.shape
    width = dw.shape[0]
    assert width - 1 <= CONV_HALO
    ts = min(256, seq)
    rb = min(128, ts)
    nt = seq // ts
    row = lambda b, j: (b * nt + j, 0)
    c2 = lambda b, j: (0, 0)
    return pl.pallas_call(
        functools.partial(_conv_kernel, ts=ts, width=width, rb=rb),
        grid=(batch, nt),
        in_specs=[pl.BlockSpec((ts, d), row),
                  pl.BlockSpec((1, d), c2),
                  pl.BlockSpec(w1.shape, c2),
                  pl.BlockSpec(b1.shape, c2),
                  pl.BlockSpec(dw.shape, c2),
                  pl.BlockSpec((1, d), c2),
                  pl.BlockSpec((1, d), c2),
                  pl.BlockSpec((1, d), c2),
                  pl.BlockSpec(w2.shape, c2),
                  pl.BlockSpec((1, d), c2)],
        out_specs=pl.BlockSpec((ts, d), row),
        out_shape=jax.ShapeDtypeStruct((n, d), F32),
        scratch_shapes=[pltpu.VMEM((SUBLANES, ts + CONV_HALO + SUBLANES, d), F32), pltpu.VMEM((ts, d), F32)],
        compiler_params=_params("arbitrary", "arbitrary"),
        name="conv_module",
    )(h, g, w1, b1, dw, bdw, lng, lnb, w2, b2)


def _mla_proj_kernel(h_ref, pos_ref, g_ref, wqa_ref, qnorm_ref, wqb_ref, wkva_ref, kvnorm_ref, wkvb_ref,
                     invf_ref, sgn_ref, q_o, kt_o, v_o, *, heads, lora):
    hn = heads * NOPE
    u = _rms(h_ref[...], g_ref[...]).astype(BF16)
    qa = _rms(_dot(u, wqa_ref[...]), qnorm_ref[...]).astype(BF16)
    q = _dot(qa, wqb_ref[...])
    kva = _dot(u, wkva_ref[...])
    ckv = _rms(kva[:, :lora], kvnorm_ref[...]).astype(BF16)
    kv = _dot(ckv, wkvb_ref[...])
    ang = pos_ref[...].astype(F32) * invf_ref[...]
    c = jnp.cos(ang)
    s = jnp.sin(ang) * sgn_ref[...]
    kr_t = (kva[:, lora:lora + LANES] * c + kva[:, lora + LANES:lora + 2 * LANES] * s).T.astype(BF16)
    for hd in range(heads):
        ns = slice(hd * NOPE, (hd + 1) * NOPE)
        qr = q[:, hn + hd * LANES:hn + (hd + 1) * LANES] * c + q[:, 2 * hn + hd * LANES:2 * hn + (hd + 1) * LANES] * s
        q_o[:, 2 * hd * NOPE:(2 * hd + 1) * NOPE] = q[:, ns].astype(BF16)
        q_o[:, (2 * hd + 1) * NOPE:(2 * hd + 2) * NOPE] = qr.astype(BF16)
        kt_o[2 * hd * NOPE:(2 * hd + 1) * NOPE, :] = kv[:, ns].T.astype(BF16)
        kt_o[(2 * hd + 1) * NOPE:(2 * hd + 2) * NOPE, :] = kr_t
    v_o[...] = kv[:, hn:].astype(BF16)


def _attn_kernel(q_ref, kt_ref, v_ref, h_ref, wo_ref, o_ref, m_sc, l_sc, acc_sc, *, heads, tq, tk, scale):
    i = pl.program_id(1)
    j = pl.program_id(2)
    c = scale * LOG2E

    @pl.when(j == 0)
    def _():
        m_sc[...] = jnp.full(m_sc.shape, -jnp.inf, F32)
        l_sc[...] = jnp.zeros(l_sc.shape, F32)
        acc_sc[...] = jnp.zeros(acc_sc.shape, F32)

    def step(diagonal):
        if diagonal:
            causal = (lax.broadcasted_iota(jnp.int32, (tq, tk), 1)
                      <= lax.broadcasted_iota(jnp.int32, (tq, tk), 0))
        for hd in range(heads):
            qs = slice(2 * hd * NOPE, (2 * hd + 2) * NOPE)
            hs = slice(hd * NOPE, (hd + 1) * NOPE)
            s = _dot(q_ref[:, qs], kt_ref[qs, :])
            if diagonal:
                s = jnp.where(causal, s, -jnp.inf)
            m_prev = m_sc[hd]
            m_new = jnp.maximum(m_prev, jnp.max(s, axis=-1, keepdims=True))
            alpha = jnp.exp2((m_prev - m_new) * c)
            p = jnp.exp2((s - m_new) * c)
            l_sc[hd] = alpha * l_sc[hd] + jnp.sum(p, axis=-1, keepdims=True)
            acc_sc[:, hs] = alpha * acc_sc[:, hs] + _dot(p.astype(BF16), v_ref[:, hs])
            m_sc[hd] = m_new

    @pl.when(j < i)
    def _():
        step(False)

    @pl.when(j == i)
    def _():
        step(True)
        parts = [acc_sc[:, hd * NOPE:(hd + 1) * NOPE] / l_sc[hd] for hd in range(heads)]
        o = jnp.concatenate(parts, axis=1).astype(BF16)
        o_ref[...] = h_ref[...] + _dot(o, wo_ref[...])


def _mla_layer(h, positions, g, wqa, qnorm, wqb, wkva, kvnorm, wkvb, wo, invf, sgn, *, batch, seq, heads, lora):
    n, d = h.shape
    hn = heads * NOPE
    tp = min(256, n)
    c2 = lambda i: (0, 0)
    row = lambda i: (i, 0)
    q, kt, v = pl.pallas_call(
        functools.partial(_mla_proj_kernel, heads=heads, lora=lora),
        grid=(n // tp,),
        in_specs=[pl.BlockSpec((tp, d), row),
                  pl.BlockSpec((tp, 1), row),
                  pl.BlockSpec((1, d), c2),
                  pl.BlockSpec(wqa.shape, c2),
                  pl.BlockSpec(qnorm.shape, c2),
                  pl.BlockSpec(wqb.shape, c2),
                  pl.BlockSpec(wkva.shape, c2),
                  pl.BlockSpec(kvnorm.shape, c2),
                  pl.BlockSpec(wkvb.shape, c2),
                  pl.BlockSpec((1, LANES), c2),
                  pl.BlockSpec((1, LANES), c2)],
        out_specs=[pl.BlockSpec((tp, 2 * hn), row), pl.BlockSpec((2 * hn, tp), lambda i: (0, i)),
                   pl.BlockSpec((tp, hn), row)],
        out_shape=[jax.ShapeDtypeStruct((n, 2 * hn), BF16), jax.ShapeDtypeStruct((2 * hn, n), BF16),
                   jax.ShapeDtypeStruct((n, hn), BF16)],
        compiler_params=_params("arbitrary"),
        name="mla_proj",
    )(h, positions, g, wqa, qnorm, wqb, wkva, kvnorm, wkvb, invf, sgn)

    tq = tk = min(512, seq)
    nq = seq // tq
    scale = float(NOPE + ROPE) ** -0.5
    qmap = lambda b, i, j: (b * nq + i, 0)
    kmap = lambda b, i, j: (b * nq + jnp.minimum(j, i), 0)
    return pl.pallas_call(
        functools.partial(_attn_kernel, heads=heads, tq=tq, tk=tk, scale=scale),
        grid=(batch, nq, nq),
        in_specs=[pl.BlockSpec((tq, 2 * hn), qmap),
                  pl.BlockSpec((2 * hn, tk), lambda b, i, j: (0, b * nq + jnp.minimum(j, i))),
                  pl.BlockSpec((tk, hn), kmap),
                  pl.BlockSpec((tq, d), qmap),
                  pl.BlockSpec(wo.shape, lambda b, i, j: (0, 0))],
        out_specs=pl.BlockSpec((tq, d), qmap),
        out_shape=jax.ShapeDtypeStruct((n, d), F32),
        scratch_shapes=[pltpu.VMEM((heads, tq, 1), F32), pltpu.VMEM((heads, tq, 1), F32),
                        pltpu.VMEM((tq, hn), F32)],
        compiler_params=_params("arbitrary", "arbitrary", "arbitrary"),
        name="mla_attention",
    )(q, kt, v, h, wo)


def _store_row_tiles(ref, value, rows):
    slabs = value.shape[1] // LANES
    for s in range(slabs):
        ref[pl.ds(s, rows, stride=slabs), :] = value[:, s * LANES:(s + 1) * LANES]


def _load_row_tile_slab(ref, s, rows, slabs):
    return ref[pl.ds(s, rows, stride=slabs), :]


def _route_kernel(h_ref, g_ref, r_ref, u_ref, info_ref, cnt_ref, carry, *, tr, n_exp):
    i = pl.program_id(0)

    @pl.when(i == 0)
    def _():
        carry[...] = jnp.zeros(carry.shape, F32)

    u = _rms(h_ref[...], g_ref[...])
    _store_row_tiles(u_ref, u, tr)
    logits = jnp.dot(u, r_ref[...], preferred_element_type=F32, precision=lax.Precision.HIGHEST)
    lane = lax.broadcasted_iota(jnp.int32, (tr, LANES), 1)
    lg = jnp.where(lane < n_exp, logits, -jnp.inf)
    m1 = jnp.max(lg, axis=1, keepdims=True)
    i1 = jnp.min(jnp.where(lg == m1, lane, LANES), axis=1, keepdims=True)
    lg2 = jnp.where(lane == i1, -jnp.inf, lg)
    m2 = jnp.max(lg2, axis=1, keepdims=True)
    i2 = jnp.min(jnp.where(lg2 == m2, lane, LANES), axis=1, keepdims=True)
    e = jnp.exp(m2 - m1)
    g1 = 1.0 / (1.0 + e)
    g2 = e / (1.0 + e)
    sel1 = lane == i1
    sel2 = lane == i2
    onehot = jnp.where(sel1 | sel2, 1.0, 0.0)
    rr = lax.broadcasted_iota(jnp.int32, (tr, tr), 0)
    cc = lax.broadcasted_iota(jnp.int32, (tr, tr), 1)
    tri = jnp.where(cc < rr, 1.0, 0.0).astype(BF16)
    before = _dot(tri, onehot.astype(BF16)) + carry[0:1, :]
    rank1 = jnp.sum(jnp.where(sel1, before, 0.0), axis=1, keepdims=True)
    rank2 = jnp.sum(jnp.where(sel2, before, 0.0), axis=1, keepdims=True)
    carry[...] = carry[...] + jnp.sum(onehot, axis=0, keepdims=True)
    cols = (i1.astype(F32), i2.astype(F32), g1, g2, rank1, rank2)
    info = jnp.zeros((tr, LANES), F32)
    for k, col in enumerate(cols):
        info = jnp.where(lane == k, col, info)
    info_ref[...] = info
    cnt_ref[...] = carry[...]


def _route(h, g, router_padded, n_exp):
    n, d = h.shape
    tr = min(512, n)
    slabs = d // LANES
    return pl.pallas_call(
        functools.partial(_route_kernel, tr=tr, n_exp=n_exp),
        grid=(n // tr,),
        in_specs=[pl.BlockSpec((tr, d), lambda i: (i, 0)),
                  pl.BlockSpec((1, d), lambda i: (0, 0)),
                  pl.BlockSpec((d, LANES), lambda i: (0, 0))],
        out_specs=[pl.BlockSpec((tr * slabs, LANES), lambda i: (i, 0)),
                   pl.BlockSpec((tr, LANES), lambda i: (i, 0)),
                   pl.BlockSpec((8, LANES), lambda i: (0, 0))],
        out_shape=[jax.ShapeDtypeStruct((n * slabs, LANES), F32),
                   jax.ShapeDtypeStruct((n, LANES), F32),
                   jax.ShapeDtypeStruct((8, LANES), F32)],
        scratch_shapes=[pltpu.VMEM((8, LANES), F32)],
        compiler_params=_params("arbitrary"),
        name="moe_route",
    )(h, g, router_padded)


def _gather_row_tiles(idx_ref, k, src_ref, dst_ref, sem, rows, slabs):
    def start8(r8, c):
        for q in range(GATHER_UNROLL):
            r = r8 * GATHER_UNROLL + q
            src_row = pl.multiple_of(idx_ref[0, k, r] * slabs, slabs)
            dst_row = pl.multiple_of(r * slabs, slabs)
            pltpu.make_async_copy(src_ref.at[pl.ds(src_row, slabs)], dst_ref.at[pl.ds(dst_row, slabs)],
                                  sem).start(priority=q % 2)
        return c

    lax.fori_loop(0, rows // GATHER_UNROLL, start8, 0)
    pltpu.make_async_copy(src_ref.at[pl.ds(0, rows * slabs)], dst_ref, sem).wait()


def _dispatch_kernel(idx_ref, src_ref, o_ref, sem, *, rows, slabs):
    _gather_row_tiles(idx_ref, 0, src_ref, o_ref, sem, rows, slabs)


def _dispatch(src, inv, tile, slabs):
    rows = inv.shape[0]
    return pl.pallas_call(
        functools.partial(_dispatch_kernel, rows=tile, slabs=slabs),
        grid=(rows // tile,),
        in_specs=[pl.BlockSpec((1, 1, tile), lambda i: (i, 0, 0), memory_space=pltpu.SMEM),
                  pl.BlockSpec(memory_space=pl.ANY)],
        out_specs=pl.BlockSpec((tile * slabs, LANES), lambda i: (i, 0)),
        out_shape=jax.ShapeDtypeStruct((rows * slabs, LANES), src.dtype),
        scratch_shapes=[pltpu.SemaphoreType.DMA(())],
        compiler_params=_params("arbitrary"),
        name="moe_dispatch",
    )(inv.reshape(rows // tile, 1, tile), src)


def _moe_kernel(te_ref, nu_ref, x_ref, wg_ref, wu_ref, wd_ref, o_ref, x_sc, acc_sc, *, tm, slabs):
    t = pl.program_id(0)
    j = pl.program_id(1)

    @pl.when(t < nu_ref[0])
    def _():
        @pl.when(j == 0)
        def _():
            for s in range(slabs):
                x_sc[:, s * LANES:(s + 1) * LANES] = _load_row_tile_slab(x_ref, s, tm, slabs).astype(BF16)
            acc_sc[...] = jnp.zeros(acc_sc.shape, F32)

        acc_sc[...] += _swiglu_step(x_sc[...], wg_ref[0], wu_ref[0], wd_ref[0])

        @pl.when(j == pl.num_programs(1) - 1)
        def _():
            _store_row_tiles(o_ref, acc_sc[...], tm)

    @pl.when((t >= nu_ref[0]) & (j == 0))
    def _():
        o_ref[...] = jnp.zeros(o_ref.shape, F32)


def _moe_experts(x_sorted, tile_expert, n_used, wg, wu, wd, tm, slabs):
    d = slabs * LANES
    rows = x_sorted.shape[0] // slabs
    f = wg.shape[2]
    tf = _ff_chunk(f)
    nj = f // tf
    tmap = lambda t, j, te, nu: (jnp.minimum(t, nu[0] - 1), 0)
    jsel = lambda t, j, nu: jnp.where(t < nu[0], j, nj - 1)
    grid_spec = pltpu.PrefetchScalarGridSpec(
        num_scalar_prefetch=2,
        grid=(rows // tm, nj),
        in_specs=[pl.BlockSpec((tm * slabs, LANES), tmap),
                  pl.BlockSpec((1, d, tf), lambda t, j, te, nu: (te[t], 0, jsel(t, j, nu))),
                  pl.BlockSpec((1, d, tf), lambda t, j, te, nu: (te[t], 0, jsel(t, j, nu))),
                  pl.BlockSpec((1, tf, d), lambda t, j, te, nu: (te[t], jsel(t, j, nu), 0))],
        out_specs=pl.BlockSpec((tm * slabs, LANES), lambda t, j, te, nu: (t, 0)),
        scratch_shapes=[pltpu.VMEM((tm, d), BF16), pltpu.VMEM((tm, d), F32)],
    )
    return pl.pallas_call(
        functools.partial(_moe_kernel, tm=tm, slabs=slabs),
        grid_spec=grid_spec,
        out_shape=jax.ShapeDtypeStruct((rows * slabs, LANES), F32),
        compiler_params=_params("arbitrary", "arbitrary"),
        name="moe_experts",
    )(tile_expert, n_used, x_sorted, wg, wu, wd)


def _combine_kernel(pos_ref, h_ref, gate_ref, gf_ref, y_ref, o_ref, y0, y1, sem, *, rows, slabs, final_norm):
    _gather_row_tiles(pos_ref, 0, y_ref, y0, sem.at[0], rows, slabs)
    _gather_row_tiles(pos_ref, 1, y_ref, y1, sem.at[1], rows, slabs)
    gates = gate_ref[...]
    g0 = gates[:, 2:3]
    g1 = gates[:, 3:4]
    ssq = jnp.zeros((rows, 1), F32)
    for s in range(slabs):
        cs = slice(s * LANES, (s + 1) * LANES)
        out = (h_ref[:, cs] + g0 * _load_row_tile_slab(y0, s, rows, slabs)
               + g1 * _load_row_tile_slab(y1, s, rows, slabs))
        o_ref[:, cs] = out
        ssq = ssq + jnp.sum(out * out, axis=-1, keepdims=True)
    if final_norm:
        d = slabs * LANES
        o_ref[...] = o_ref[...] * lax.rsqrt(ssq / d + NORM_EPS) * gf_ref[...]


def _combine(h, info, pos, y_sorted, g_final, final_norm, slabs):
    n, d = h.shape
    tile = min(512, n)
    pos_t = pos.T.reshape(TOP_K, n // tile, tile).transpose(1, 0, 2)
    return pl.pallas_call(
        functools.partial(_combine_kernel, rows=tile, slabs=slabs, final_norm=final_norm),
        grid=(n // tile,),
        in_specs=[pl.BlockSpec((1, TOP_K, tile), lambda i: (i, 0, 0), memory_space=pltpu.SMEM),
                  pl.BlockSpec((tile, d), lambda i: (i, 0)),
                  pl.BlockSpec((tile, LANES), lambda i: (i, 0)),
                  pl.BlockSpec((1, d), lambda i: (0, 0)),
                  pl.BlockSpec(memory_space=pl.ANY)],
        out_specs=pl.BlockSpec((tile, d), lambda i: (i, 0)),
        out_shape=jax.ShapeDtypeStruct((n, d), F32),
        scratch_shapes=[pltpu.VMEM((tile * slabs, LANES), F32), pltpu.VMEM((tile * slabs, LANES), F32),
                        pltpu.SemaphoreType.DMA((2,))],
        compiler_params=_params("arbitrary"),
        name="moe_combine",
    )(pos_t, h, info, g_final, y_sorted)


def _moe_layer(h, g, router, wg, wu, wd, g_final, final_norm):
    n, d = h.shape
    n_exp = router.shape[1]
    tm = min(512, n)
    router_padded = jnp.pad(router, ((0, 0), (0, LANES - n_exp)))
    u, info, cnt = _route(h, g, router_padded, n_exp)

    idx = info[:, 0:TOP_K].astype(jnp.int32)
    rank = info[:, 4:4 + TOP_K].astype(jnp.int32)
    counts = cnt[0, :n_exp].astype(jnp.int32)
    padded = (counts + tm - 1) // tm * tm
    ends = jnp.cumsum(padded)
    starts = ends - padded
    pos = jnp.sum(jnp.where(idx[..., None] == jnp.arange(n_exp), starts, 0), axis=-1) + rank
    rows = TOP_K * n + n_exp * tm
    token = jnp.broadcast_to(jnp.arange(n, dtype=jnp.int32)[:, None], (n, TOP_K))
    inv = jnp.zeros((rows,), jnp.int32).at[pos.reshape(-1)].set(
        token.reshape(-1), unique_indices=True, mode="promise_in_bounds")
    n_used = (ends[-1] // tm).astype(jnp.int32)
    tile_ids = jnp.minimum(jnp.arange(rows // tm, dtype=jnp.int32), n_used - 1)
    tile_expert = jnp.sum(tile_ids[:, None] * tm >= ends[None, :], axis=1).astype(jnp.int32)

    slabs = d // LANES
    assert slabs == 8
    x_sorted = _dispatch(u, inv, min(1024, n), slabs)
    y_sorted = _moe_experts(x_sorted, tile_expert, n_used.reshape(1), wg, wu, wd, tm, slabs)
    return _combine(h, info, pos, y_sorted, g_final, final_norm, slabs)


def _mla_weights(wq_b, wkv_a, wkv_b, heads, lora):
    def padded_rope(w):
        pad = jnp.zeros(w.shape[:-1] + (LANES - ROPE,), w.dtype)
        swapped = jnp.concatenate([w[..., ROPE // 2:], w[..., :ROPE // 2]], axis=-1)
        return jnp.concatenate([w, pad], axis=-1), jnp.concatenate([swapped, pad], axis=-1)

    rq = wq_b.shape[0]
    wq = wq_b.reshape(rq, heads, NOPE + ROPE)
    q_r, q_rs = padded_rope(wq[..., NOPE:])
    wqb = jnp.concatenate([wq[..., :NOPE].reshape(rq, -1), q_r.reshape(rq, -1), q_rs.reshape(rq, -1)], axis=1)
    k_r, k_rs = padded_rope(wkv_a[:, lora:])
    wkva = jnp.concatenate([wkv_a[:, :lora], k_r, k_rs], axis=1)
    wkv = wkv_b.reshape(lora, heads, 2 * NOPE)
    wkvb = jnp.concatenate([wkv[..., :NOPE].reshape(lora, -1), wkv[..., NOPE:].reshape(lora, -1)], axis=1)
    return wqb.astype(BF16), wkva.astype(BF16), wkvb.astype(BF16)


def _cast_kernel(x_ref, o_ref):
    o_ref[...] = x_ref[0].astype(BF16)


def _to_bf16(w, layer):
    cols = w.shape[-1]
    w3 = w.reshape(w.shape[0], -1, cols)
    rows = w3.shape[1]
    cap = 256 if cols > 2048 else 1024
    tr = next(t for t in (1024, 512, 256, 128, 64, 32, 16, 8) if t <= cap and rows % t == 0)
    out = pl.pallas_call(
        _cast_kernel,
        grid=(rows // tr,),
        in_specs=[pl.BlockSpec((1, tr, cols), lambda i: (layer, i, 0))],
        out_specs=pl.BlockSpec((tr, cols), lambda i: (i, 0)),
        out_shape=jax.ShapeDtypeStruct((rows, cols), BF16),
        compiler_params=_params("arbitrary"),
        name="weights_to_bf16",
    )(w3)
    return out.reshape(w.shape[1:])


def _final_norm_kernel(h_ref, g_ref, o_ref):
    o_ref[...] = _rms(h_ref[...], g_ref[...])


def _final_norm(h, g):
    n, d = h.shape
    tm = min(512, n)
    return pl.pallas_call(
        _final_norm_kernel,
        grid=(n // tm,),
        in_specs=[pl.BlockSpec((tm, d), lambda i: (i, 0)), pl.BlockSpec((1, d), lambda i: (0, 0))],
        out_specs=pl.BlockSpec((tm, d), lambda i: (i, 0)),
        out_shape=jax.ShapeDtypeStruct((n, d), F32),
        compiler_params=_params("arbitrary"),
        name="final_norm",
    )(h, g)


@jax.jit
def _trunk(x, positions, norm_mix, norm_ffn, norm_final, pool_w, pool_scale,
           conv_w1, conv_b1, conv_dw, conv_bdw, conv_ln_g, conv_ln_b, conv_w2, conv_b2,
           mla_wq_a, mla_q_norm, mla_wq_b, mla_wkv_a, mla_kv_norm, mla_wkv_b, mla_wo,
           ffn_wg, ffn_wu, ffn_wd, moe_router, moe_wg, moe_wu, moe_wd):
    batch, seq, d = x.shape
    depth = norm_mix.shape[0]
    lora = mla_kv_norm.shape[1]
    heads = mla_wo.shape[1] // NOPE
    assert mla_wq_b.shape[2] == heads * (NOPE + ROPE) and mla_wkv_a.shape[2] == lora + ROPE
    n = batch * seq
    h = x.reshape(n, d)
    pos = positions.reshape(n, 1)
    row = lambda v: v.reshape(1, -1)

    half = jnp.arange(0, ROPE, 2, dtype=F32) / ROPE
    inv_freq = ROPE_THETA ** (-half)
    invf = row(jnp.concatenate([inv_freq, inv_freq, jnp.zeros((LANES - ROPE,), F32)]))
    sgn = row(jnp.concatenate([-jnp.ones((ROPE // 2,), F32), jnp.ones((ROPE // 2,), F32),
                               jnp.zeros((LANES - ROPE,), F32)]))

    for i in range(depth):
        kind, j = i % 3, i // 3
        g = row(norm_mix[i])
        if kind == 0:
            h = _pool_layer(h, g, pool_w[j].astype(BF16), row(pool_scale[j]), batch=batch, seq=seq)
        elif kind == 1:
            h = _conv_layer(h, g, conv_w1[j].astype(BF16), row(conv_b1[j]), conv_dw[j], row(conv_bdw[j]),
                            row(conv_ln_g[j]), row(conv_ln_b[j]), conv_w2[j].astype(BF16), row(conv_b2[j]),
                            batch=batch, seq=seq)
        else:
            wqb, wkva, wkvb = _mla_weights(mla_wq_b[j], mla_wkv_a[j], mla_wkv_b[j], heads, lora)
            h = _mla_layer(h, pos, g, mla_wq_a[j].astype(BF16), row(mla_q_norm[j]), wqb, wkva,
                           row(mla_kv_norm[j]), wkvb, mla_wo[j].astype(BF16), invf, sgn,
                           batch=batch, seq=seq, heads=heads, lora=lora)
        g = row(norm_ffn[i])
        f_idx = i // 2
        last = i == depth - 1
        if i % 2 == 0:
            h = _ffn_layer(h, g, _to_bf16(ffn_wg, f_idx), _to_bf16(ffn_wu, f_idx), _to_bf16(ffn_wd, f_idx))
            if last:
                h = _final_norm(h, row(norm_final))
        else:
            h = _moe_layer(h, g, moe_router[f_idx], _to_bf16(moe_wg, f_idx), _to_bf16(moe_wu, f_idx),
                           _to_bf16(moe_wd, f_idx), row(norm_final), last)
    return h.reshape(batch, seq, d)


def kernel(x, positions, norm_mix, norm_ffn, norm_final, pool_w, pool_scale, conv_w1, conv_b1, conv_dw, conv_bdw, conv_ln_g, conv_ln_b, conv_w2, conv_b2, mla_wq_a, mla_q_norm, mla_wq_b, mla_wkv_a, mla_kv_norm, mla_wkv_b, mla_wo, ffn_wg, ffn_wu, ffn_wd, moe_router, moe_wg, moe_wu, moe_wd):
    return _trunk(x, positions, norm_mix, norm_ffn, norm_final, pool_w, pool_scale,
                  conv_w1, conv_b1, conv_dw, conv_bdw, conv_ln_g, conv_ln_b, conv_w2, conv_b2,
                  mla_wq_a, mla_q_norm, mla_wq_b, mla_wkv_a, mla_kv_norm, mla_wkv_b, mla_wo,
                  ffn_wg, ffn_wu, ffn_wd, moe_router, moe_wg, moe_wu, moe_wd)
```

```python
import functools

import jax
import jax.numpy as jnp
from jax import lax
from jax.experimental import pallas as pl
from jax.experimental.pallas import tpu as pltpu

F32 = jnp.float32
BF16 = jnp.bfloat16

NORM_EPS = 1e-6
LN_EPS = 1e-5
ROPE_THETA = 10000.0
LOG2E = 1.4426950408889634

POOL_WINDOWS = (2, 4, 8, 16)
POOL_HALO = 16
CONV_HALO = 32
LANES = 128
SUBLANES = 8
NOPE = 128
ROPE = 64
TOP_K = 2
GATHER_UNROLL = 8
ZERO_CHUNK = 64

VMEM_LIMIT = 56 * 1024 * 1024


def _params(*sem):
    return pltpu.CompilerParams(dimension_semantics=sem, vmem_limit_bytes=VMEM_LIMIT)


def _rms(x, g, eps=NORM_EPS):
    return x * lax.rsqrt(jnp.mean(x * x, axis=-1, keepdims=True) + eps) * g


def _dot(a, b):
    return jnp.dot(a, b, preferred_element_type=F32)


def _ff_chunk(f):
    for c in (1792, 1024, 512, 256, 128):
        if f % c == 0:
            return c
    return f


def _pool_kernel(h_ref, g_ref, w_ref, sc_ref, o_ref, ubuf, *, ts, gdim):
    j = pl.program_id(1)
    h = h_ref[...]
    u = _rms(h, g_ref[...])

    @pl.when(j == 0)
    def _():
        ubuf[0:POOL_HALO, :] = jnp.zeros((POOL_HALO, ubuf.shape[1]), F32)

    ubuf[POOL_HALO:POOL_HALO + ts, :] = u
    t1 = lax.broadcasted_iota(jnp.int32, (ts, 1), 0) + (j * ts + 1)
    for g, win in enumerate(POOL_WINDOWS):
        cs = slice(g * gdim, (g + 1) * gdim)
        ug = u[:, cs]
        s = ug
        for k in range(1, win):
            s = s + ubuf[POOL_HALO - k:POOL_HALO - k + ts, cs]
        cnt = jnp.minimum(t1, win).astype(F32)
        pooled = s / cnt - ug
        m = _dot(pooled.astype(BF16), w_ref[g])
        o_ref[:, cs] = h[:, cs] + m * sc_ref[:, cs]
    ubuf[0:POOL_HALO, :] = ubuf[ts:ts + POOL_HALO, :]


def _pool_layer(h, g, w, scale, *, batch, seq):
    n, d = h.shape
    ts = min(512, seq)
    nt = seq // ts
    gdim = d // len(POOL_WINDOWS)
    row = lambda b, j: (b * nt + j, 0)
    const2 = lambda b, j: (0, 0)
    return pl.pallas_call(
        functools.partial(_pool_kernel, ts=ts, gdim=gdim),
        grid=(batch, nt),
        in_specs=[pl.BlockSpec((ts, d), row),
                  pl.BlockSpec((1, d), const2),
                  pl.BlockSpec(w.shape, lambda b, j: (0, 0, 0)),
                  pl.BlockSpec((1, d), const2)],
        out_specs=pl.BlockSpec((ts, d), row),
        out_shape=jax.ShapeDtypeStruct((n, d), F32),
        scratch_shapes=[pltpu.VMEM((ts + POOL_HALO, d), F32)],
        compiler_params=_params("arbitrary", "arbitrary"),
        name="pool_mixer",
    )(h, g, w, scale)


def _swiglu_step(u, wg, wu, wd):
    a = _dot(u, wg)
    b = _dot(u, wu)
    hh = a * jax.nn.sigmoid(a) * b
    return _dot(hh.astype(BF16), wd)


def _ffn_kernel(h_ref, g_ref, wg_ref, wu_ref, wd_ref, o_ref, u_sc, acc_sc):
    j = pl.program_id(1)

    @pl.when(j == 0)
    def _():
        u_sc[...] = _rms(h_ref[...], g_ref[...]).astype(BF16)
        acc_sc[...] = jnp.zeros(acc_sc.shape, F32)

    acc_sc[...] += _swiglu_step(u_sc[...], wg_ref[...], wu_ref[...], wd_ref[...])

    @pl.when(j == pl.num_programs(1) - 1)
    def _():
        o_ref[...] = h_ref[...] + acc_sc[...]


def _ffn_layer(h, g, wg, wu, wd):
    n, d = h.shape
    f = wg.shape[1]
    tm = min(512, n)
    tf = _ff_chunk(f)
    return pl.pallas_call(
        _ffn_kernel,
        grid=(n // tm, f // tf),
        in_specs=[pl.BlockSpec((tm, d), lambda i, j: (i, 0)),
                  pl.BlockSpec((1, d), lambda i, j: (0, 0)),
                  pl.BlockSpec((d, tf), lambda i, j: (0, j)),
                  pl.BlockSpec((d, tf), lambda i, j: (0, j)),
                  pl.BlockSpec((tf, d), lambda i, j: (j, 0))],
        out_specs=pl.BlockSpec((tm, d), lambda i, j: (i, 0)),
        out_shape=jax.ShapeDtypeStruct((n, d), F32),
        scratch_shapes=[pltpu.VMEM((tm, d), BF16), pltpu.VMEM((tm, d), F32)],
        compiler_params=_params("arbitrary", "arbitrary"),
        name="dense_swiglu",
    )(h, g, wg, wu, wd)


def _conv_kernel(h_ref, g_ref, w1_ref, b1_ref, dw_ref, bdw_ref, lng_ref, lnb_ref, w2_ref, b2_ref,
                 o_ref, gbuf, cbuf, *, ts, width, rb):
    j = pl.program_id(1)
    d = h_ref.shape[1]
    h = h_ref[...]
    u = _rms(h, g_ref[...]).astype(BF16)
    a = _dot(u, w1_ref[...]) + b1_ref[...]
    glu = a[:, :d] * jax.nn.sigmoid(a[:, d:])

    keep = CONV_HALO + SUBLANES

    @pl.when(j == 0)
    def _():
        gbuf[:, 0:keep, :] = jnp.zeros((SUBLANES, keep, d), F32)

    for b in range(SUBLANES):
        gbuf[b, CONV_HALO + b:CONV_HALO + b + ts, :] = glu
    for cb in range(d // LANES):
        cs = slice(cb * LANES, (cb + 1) * LANES)
        for r0 in range(0, ts, rb):
            acc = jnp.zeros((rb, LANES), F32)
            for k in range(width):
                a, b = divmod(width - 1 - k, SUBLANES)
                off = CONV_HALO + r0 - SUBLANES * a
                acc = acc + gbuf[b, off:off + rb, cs] * dw_ref[k:k + 1, cs]
            cbuf[r0:r0 + rb, cs] = acc + bdw_ref[:, cs]
    gbuf[:, 0:keep, :] = gbuf[:, ts:ts + keep, :]

    c = cbuf[...]
    mu = jnp.mean(c, axis=-1, keepdims=True)
    cc = c - mu
    var = jnp.mean(cc * cc, axis=-1, keepdims=True)
    un = cc * lax.rsqrt(var + LN_EPS) * lng_ref[...] + lnb_ref[...]
    un = un * jax.nn.sigmoid(un)
    o_ref[...] = h + _dot(un.astype(BF16), w2_ref[...]) + b2_ref[...]


def _conv_layer(h, g, w1, b1, dw, bdw, lng, lnb, w2, b2, *, batch, seq):
    n, d = h.shape
    width = dw.shape[0]
    assert width - 1 <= CONV_HALO
    ts = min(256, seq)
    rb = min(128, ts)
    nt = seq // ts
    row = lambda b, j: (b * nt + j, 0)
    c2 = lambda b, j: (0, 0)
    return pl.pallas_call(
        functools.partial(_conv_kernel, ts=ts, width=width, rb=rb),
        grid=(batch, nt),
        in_specs=[pl.BlockSpec((ts, d), row),
                  pl.BlockSpec((1, d), c2),
                  pl.BlockSpec(w1.shape, c2),
                  pl.BlockSpec(b1.shape, c2),
                  pl.BlockSpec(dw.shape, c2),
                  pl.BlockSpec((1, d), c2),
                  pl.BlockSpec((1, d), c2),
                  pl.BlockSpec((1, d), c2),
                  pl.BlockSpec(w2.shape, c2),
                  pl.BlockSpec((1, d), c2)],
        out_specs=pl.BlockSpec((ts, d), row),
        out_shape=jax.ShapeDtypeStruct((n, d), F32),
        scratch_shapes=[pltpu.VMEM((SUBLANES, ts + CONV_HALO + SUBLANES, d), F32), pltpu.VMEM((ts, d), F32)],
        compiler_params=_params("arbitrary", "arbitrary"),
        name="conv_module",
    )(h, g, w1, b1, dw, bdw, lng, lnb, w2, b2)


def _mla_proj_kernel(h_ref, pos_ref, g_ref, wqa_ref, qnorm_ref, wqb_ref, wkva_ref, kvnorm_ref, wkvb_ref,
                     invf_ref, sgn_ref, q_o, kt_o, v_o, *, heads, lora):
    hn = heads * NOPE
    u = _rms(h_ref[...], g_ref[...]).astype(BF16)
    qa = _rms(_dot(u, wqa_ref[...]), qnorm_ref[...]).astype(BF16)
    q = _dot(qa, wqb_ref[...])
    kva = _dot(u, wkva_ref[...])
    ckv = _rms(kva[:, :lora], kvnorm_ref[...]).astype(BF16)
    kv = _dot(ckv, wkvb_ref[...])
    ang = pos_ref[...].astype(F32) * invf_ref[...]
    c = jnp.cos(ang)
    s = jnp.sin(ang) * sgn_ref[...]
    kr_t = (kva[:, lora:lora + LANES] * c + kva[:, lora + LANES:lora + 2 * LANES] * s).T.astype(BF16)
    for hd in range(heads):
        ns = slice(hd * NOPE, (hd + 1) * NOPE)
        qr = q[:, hn + hd * LANES:hn + (hd + 1) * LANES] * c + q[:, 2 * hn + hd * LANES:2 * hn + (hd + 1) * LANES] * s
        q_o[:, 2 * hd * NOPE:(2 * hd + 1) * NOPE] = q[:, ns].astype(BF16)
        q_o[:, (2 * hd + 1) * NOPE:(2 * hd + 2) * NOPE] = qr.astype(BF16)
        kt_o[2 * hd * NOPE:(2 * hd + 1) * NOPE, :] = kv[:, ns].T.astype(BF16)
        kt_o[(2 * hd + 1) * NOPE:(2 * hd + 2) * NOPE, :] = kr_t
        v_o[:, 2 * hd * NOPE:(2 * hd + 1) * NOPE] = kv[:, hn + hd * NOPE:hn + (hd + 1) * NOPE].astype(BF16)
        v_o[:, (2 * hd + 1) * NOPE:(2 * hd + 2) * NOPE] = jnp.ones((v_o.shape[0], NOPE), BF16)


def _attn_kernel(q_ref, kt_ref, v_ref, h_ref, wo_ref, o_ref, m_sc, acc_sc, *, heads, tq, tk, scale):
    i = pl.program_id(1)
    j = pl.program_id(2)
    c = scale * LOG2E

    @pl.when(j == 0)
    def _():
        m_sc[...] = jnp.full(m_sc.shape, -jnp.inf, F32)
        acc_sc[...] = jnp.zeros(acc_sc.shape, F32)

    def step(diagonal):
        if diagonal:
            causal = (lax.broadcasted_iota(jnp.int32, (tq, tk), 1)
                      <= lax.broadcasted_iota(jnp.int32, (tq, tk), 0))
        es = [slice(2 * hd * NOPE, (2 * hd + 2) * NOPE) for hd in range(heads)]
        ss = [_dot(q_ref[:, e], kt_ref[e, :]) for e in es]
        if diagonal:
            ss = [jnp.where(causal, s, -jnp.inf) for s in ss]
        m_prev = [m_sc[hd] for hd in range(heads)]
        m_new = [jnp.maximum(mp, jnp.max(s, axis=-1, keepdims=True)) for mp, s in zip(m_prev, ss)]
        ps = [jnp.exp2((s - mn) * c).astype(BF16) for s, mn in zip(ss, m_new)]
        pv = [_dot(p, v_ref[:, e]) for p, e in zip(ps, es)]
        for hd in range(heads):
            alpha = jnp.exp2((m_prev[hd] - m_new[hd]) * c)
            acc_sc[:, es[hd]] = alpha * acc_sc[:, es[hd]] + pv[hd]
            m_sc[hd] = m_new[hd]

    @pl.when(j < i)
    def _():
        step(False)

    @pl.when(j == i)
    def _():
        step(True)
        parts = [acc_sc[:, 2 * hd * NOPE:(2 * hd + 1) * NOPE] / acc_sc[:, (2 * hd + 1) * NOPE:(2 * hd + 2) * NOPE]
                 for hd in range(heads)]
        o = jnp.concatenate(parts, axis=1).astype(BF16)
        o_ref[...] = h_ref[...] + _dot(o, wo_ref[...])


def _mla_layer(h, positions, g, wqa, qnorm, wqb, wkva, kvnorm, wkvb, wo, invf, sgn, *, batch, seq, heads, lora):
    n, d = h.shape
    hn = heads * NOPE
    tp = min(256, n)
    c2 = lambda i: (0, 0)
    row = lambda i: (i, 0)
    q, kt, v = pl.pallas_call(
        functools.partial(_mla_proj_kernel, heads=heads, lora=lora),
        grid=(n // tp,),
        in_specs=[pl.BlockSpec((tp, d), row),
                  pl.BlockSpec((tp, 1), row),
                  pl.BlockSpec((1, d), c2),
                  pl.BlockSpec(wqa.shape, c2),
                  pl.BlockSpec(qnorm.shape, c2),
                  pl.BlockSpec(wqb.shape, c2),
                  pl.BlockSpec(wkva.shape, c2),
                  pl.BlockSpec(kvnorm.shape, c2),
                  pl.BlockSpec(wkvb.shape, c2),
                  pl.BlockSpec((1, LANES), c2),
                  pl.BlockSpec((1, LANES), c2)],
        out_specs=[pl.BlockSpec((tp, 2 * hn), row), pl.BlockSpec((2 * hn, tp), lambda i: (0, i)),
                   pl.BlockSpec((tp, 2 * hn), row)],
        out_shape=[jax.ShapeDtypeStruct((n, 2 * hn), BF16), jax.ShapeDtypeStruct((2 * hn, n), BF16),
                   jax.ShapeDtypeStruct((n, 2 * hn), BF16)],
        compiler_params=_params("arbitrary"),
        name="mla_proj",
    )(h, positions, g, wqa, qnorm, wqb, wkva, kvnorm, wkvb, invf, sgn)

    tq = tk = min(512, seq)
    nq = seq // tq
    scale = float(NOPE + ROPE) ** -0.5
    qmap = lambda b, i, j: (b * nq + i, 0)
    kmap = lambda b, i, j: (b * nq + jnp.minimum(j, i), 0)
    return pl.pallas_call(
        functools.partial(_attn_kernel, heads=heads, tq=tq, tk=tk, scale=scale),
        grid=(batch, nq, nq),
        in_specs=[pl.BlockSpec((tq, 2 * hn), qmap),
                  pl.BlockSpec((2 * hn, tk), lambda b, i, j: (0, b * nq + jnp.minimum(j, i))),
                  pl.BlockSpec((tk, 2 * hn), kmap),
                  pl.BlockSpec((tq, d), qmap),
                  pl.BlockSpec(wo.shape, lambda b, i, j: (0, 0))],
        out_specs=pl.BlockSpec((tq, d), qmap),
        out_shape=jax.ShapeDtypeStruct((n, d), F32),
        scratch_shapes=[pltpu.VMEM((heads, tq, 1), F32), pltpu.VMEM((tq, 2 * hn), F32)],
        compiler_params=_params("arbitrary", "arbitrary", "arbitrary"),
        name="mla_attention",
    )(q, kt, v, h, wo)


def _store_row_tiles(ref, value, rows):
    slabs = value.shape[1] // LANES
    for s in range(slabs):
        ref[pl.ds(s, rows, stride=slabs), :] = value[:, s * LANES:(s + 1) * LANES]


def _load_row_tile_slab(ref, s, rows, slabs):
    return ref[pl.ds(s, rows, stride=slabs), :]


def _route_kernel(h_ref, g_ref, r_ref, u_ref, info_ref, cnt_ref, carry, *, tr, n_exp):
    i = pl.program_id(0)

    @pl.when(i == 0)
    def _():
        carry[...] = jnp.zeros(carry.shape, F32)

    u = _rms(h_ref[...], g_ref[...])
    _store_row_tiles(u_ref, u, tr)
    logits = jnp.dot(u, r_ref[...], preferred_element_type=F32, precision=lax.Precision.HIGHEST)
    lane = lax.broadcasted_iota(jnp.int32, (tr, LANES), 1)
    lg = jnp.where(lane < n_exp, logits, -jnp.inf)
    m1 = jnp.max(lg, axis=1, keepdims=True)
    i1 = jnp.min(jnp.where(lg == m1, lane, LANES), axis=1, keepdims=True)
    lg2 = jnp.where(lane == i1, -jnp.inf, lg)
    m2 = jnp.max(lg2, axis=1, keepdims=True)
    i2 = jnp.min(jnp.where(lg2 == m2, lane, LANES), axis=1, keepdims=True)
    e = jnp.exp(m2 - m1)
    g1 = 1.0 / (1.0 + e)
    g2 = e / (1.0 + e)
    sel1 = lane == i1
    sel2 = lane == i2
    onehot = jnp.where(sel1 | sel2, 1.0, 0.0)
    rr = lax.broadcasted_iota(jnp.int32, (tr, tr), 0)
    cc = lax.broadcasted_iota(jnp.int32, (tr, tr), 1)
    tri = jnp.where(cc < rr, 1.0, 0.0).astype(BF16)
    before = _dot(tri, onehot.astype(BF16)) + carry[0:1, :]
    rank1 = jnp.sum(jnp.where(sel1, before, 0.0), axis=1, keepdims=True)
    rank2 = jnp.sum(jnp.where(sel2, before, 0.0), axis=1, keepdims=True)
    carry[...] = carry[...] + jnp.sum(onehot, axis=0, keepdims=True)
    cols = (i1.astype(F32), i2.astype(F32), g1, g2, rank1, rank2)
    info = jnp.zeros((tr, LANES), F32)
    for k, col in enumerate(cols):
        info = jnp.where(lane == k, col, info)
    info_ref[...] = info
    cnt_ref[...] = carry[...]


def _route(h, g, router_padded, n_exp):
    n, d = h.shape
    tr = min(512, n)
    slabs = d // LANES
    return pl.pallas_call(
        functools.partial(_route_kernel, tr=tr, n_exp=n_exp),
        grid=(n // tr,),
        in_specs=[pl.BlockSpec((tr, d), lambda i: (i, 0)),
                  pl.BlockSpec((1, d), lambda i: (0, 0)),
                  pl.BlockSpec((d, LANES), lambda i: (0, 0))],
        out_specs=[pl.BlockSpec((tr * slabs, LANES), lambda i: (i, 0)),
                   pl.BlockSpec((tr, LANES), lambda i: (i, 0)),
                   pl.BlockSpec((8, LANES), lambda i: (0, 0))],
        out_shape=[jax.ShapeDtypeStruct((n * slabs, LANES), F32),
                   jax.ShapeDtypeStruct((n, LANES), F32),
                   jax.ShapeDtypeStruct((8, LANES), F32)],
        scratch_shapes=[pltpu.VMEM((8, LANES), F32)],
        compiler_params=_params("arbitrary"),
        name="moe_route",
    )(h, g, router_padded)


def _gather_row_tiles(idx_ref, k, src_ref, dst_ref, sem, rows, slabs):
    def start8(r8, c):
        for q in range(GATHER_UNROLL):
            r = r8 * GATHER_UNROLL + q
            src_row = pl.multiple_of(idx_ref[0, k, r] * slabs, slabs)
            dst_row = pl.multiple_of(r * slabs, slabs)
            pltpu.make_async_copy(src_ref.at[pl.ds(src_row, slabs)], dst_ref.at[pl.ds(dst_row, slabs)],
                                  sem).start(priority=q % 2)
        return c

    lax.fori_loop(0, rows // GATHER_UNROLL, start8, 0)
    pltpu.make_async_copy(src_ref.at[pl.ds(0, rows * slabs)], dst_ref, sem).wait()


def _dispatch_kernel(pad_start_ref, pad_cnt_ref, pos_ref, u_ref, x_ref, zero_sc, sem, *, rows, slabs, n_exp):
    def row_copy(src, src_row, dst_row):
        return pltpu.make_async_copy(src.at[pl.ds(pl.multiple_of(src_row * slabs, slabs), slabs)],
                                     x_ref.at[pl.ds(pl.multiple_of(dst_row * slabs, slabs), slabs)], sem)

    @pl.when(pl.program_id(0) == 0)
    def _():
        zero_sc[...] = jnp.zeros(zero_sc.shape, F32)
        for e in range(n_exp):
            def fill(i, c):
                row_copy(zero_sc, 0, pad_start_ref[e] + i).start()
                return c

            def drain(i, c):
                row_copy(zero_sc, 0, pad_start_ref[e] + i).wait()
                return c

            lax.fori_loop(0, pad_cnt_ref[e], fill, 0)
            lax.fori_loop(0, pad_cnt_ref[e], drain, 0)

        def chunk_copy(i):
            dst = pl.multiple_of((pad_start_ref[n_exp] + i * ZERO_CHUNK) * slabs, slabs)
            return pltpu.make_async_copy(zero_sc, x_ref.at[pl.ds(dst, ZERO_CHUNK * slabs)], sem)

        def fill_tail(i, c):
            chunk_copy(i).start()
            return c

        def drain_tail(i, c):
            chunk_copy(i).wait()
            return c

        lax.fori_loop(0, pad_cnt_ref[n_exp], fill_tail, 0)
        lax.fori_loop(0, pad_cnt_ref[n_exp], drain_tail, 0)

    def start8(r8, c):
        for q in range(GATHER_UNROLL):
            r = r8 * GATHER_UNROLL + q
            for k in range(TOP_K):
                row_copy(u_ref, r, pos_ref[0, k, r]).start(priority=k)
        return c

    lax.fori_loop(0, rows // GATHER_UNROLL, start8, 0)
    for k in range(TOP_K):
        pltpu.make_async_copy(u_ref, x_ref.at[pl.ds(0, rows * slabs)], sem).wait()


def _dispatch(u, pos_t, pad_start, pad_cnt, sorted_rows, tile, slabs):
    n = u.shape[0] // slabs
    grid_spec = pltpu.PrefetchScalarGridSpec(
        num_scalar_prefetch=2,
        grid=(n // tile,),
        in_specs=[pl.BlockSpec((1, TOP_K, tile), lambda i, ps, pc: (i, 0, 0), memory_space=pltpu.SMEM),
                  pl.BlockSpec((tile * slabs, LANES), lambda i, ps, pc: (i, 0))],
        out_specs=pl.BlockSpec(memory_space=pl.ANY),
        scratch_shapes=[pltpu.VMEM((ZERO_CHUNK * slabs, LANES), F32), pltpu.SemaphoreType.DMA(())],
    )
    return pl.pallas_call(
        functools.partial(_dispatch_kernel, rows=tile, slabs=slabs, n_exp=pad_start.shape[0] - 1),
        grid_spec=grid_spec,
        out_shape=jax.ShapeDtypeStruct((sorted_rows * slabs, LANES), F32),
        compiler_params=_params("arbitrary"),
        name="moe_dispatch",
    )(pad_start, pad_cnt, pos_t, u)


def _moe_kernel(te_ref, nu_ref, x_ref, wg_ref, wu_ref, wd_ref, o_ref, x_sc, acc_sc, *, tm, slabs):
    t = pl.program_id(0)
    j = pl.program_id(1)

    @pl.when(t < nu_ref[0])
    def _():
        @pl.when(j == 0)
        def _():
            for s in range(slabs):
                x_sc[:, s * LANES:(s + 1) * LANES] = _load_row_tile_slab(x_ref, s, tm, slabs).astype(BF16)
            acc_sc[...] = jnp.zeros(acc_sc.shape, F32)

        acc_sc[...] += _swiglu_step(x_sc[...], wg_ref[0], wu_ref[0], wd_ref[0])

        @pl.when(j == pl.num_programs(1) - 1)
        def _():
            _store_row_tiles(o_ref, acc_sc[...], tm)

    @pl.when((t >= nu_ref[0]) & (j == 0))
    def _():
        o_ref[...] = jnp.zeros(o_ref.shape, F32)


def _moe_experts(x_sorted, tile_expert, n_used, wg, wu, wd, tm, slabs):
    d = slabs * LANES
    rows = x_sorted.shape[0] // slabs
    f = wg.shape[2]
    tf = _ff_chunk(f)
    nj = f // tf
    tmap = lambda t, j, te, nu: (jnp.minimum(t, nu[0] - 1), 0)
    jsel = lambda t, j, nu: jnp.where(t < nu[0], j, nj - 1)
    grid_spec = pltpu.PrefetchScalarGridSpec(
        num_scalar_prefetch=2,
        grid=(rows // tm, nj),
        in_specs=[pl.BlockSpec((tm * slabs, LANES), tmap),
                  pl.BlockSpec((1, d, tf), lambda t, j, te, nu: (te[t], 0, jsel(t, j, nu))),
                  pl.BlockSpec((1, d, tf), lambda t, j, te, nu: (te[t], 0, jsel(t, j, nu))),
                  pl.BlockSpec((1, tf, d), lambda t, j, te, nu: (te[t], jsel(t, j, nu), 0))],
        out_specs=pl.BlockSpec((tm * slabs, LANES), lambda t, j, te, nu: (t, 0)),
        scratch_shapes=[pltpu.VMEM((tm, d), BF16), pltpu.VMEM((tm, d), F32)],
    )
    return pl.pallas_call(
        functools.partial(_moe_kernel, tm=tm, slabs=slabs),
        grid_spec=grid_spec,
        out_shape=jax.ShapeDtypeStruct((rows * slabs, LANES), F32),
        compiler_params=_params("arbitrary", "arbitrary"),
        name="moe_experts",
    )(tile_expert, n_used, x_sorted, wg, wu, wd)


def _combine_kernel(pos_ref, h_ref, gate_ref, gf_ref, y_ref, o_ref, y0, y1, sem, *, rows, slabs, final_norm):
    _gather_row_tiles(pos_ref, 0, y_ref, y0, sem.at[0], rows, slabs)
    _gather_row_tiles(pos_ref, 1, y_ref, y1, sem.at[1], rows, slabs)
    gates = gate_ref[...]
    g0 = gates[:, 2:3]
    g1 = gates[:, 3:4]
    ssq = jnp.zeros((rows, 1), F32)
    for s in range(slabs):
        cs = slice(s * LANES, (s + 1) * LANES)
        out = (h_ref[:, cs] + g0 * _load_row_tile_slab(y0, s, rows, slabs)
               + g1 * _load_row_tile_slab(y1, s, rows, slabs))
        o_ref[:, cs] = out
        ssq = ssq + jnp.sum(out * out, axis=-1, keepdims=True)
    if final_norm:
        d = slabs * LANES
        o_ref[...] = o_ref[...] * lax.rsqrt(ssq / d + NORM_EPS) * gf_ref[...]


def _combine(h, info, pos_t, y_sorted, g_final, final_norm, slabs):
    n, d = h.shape
    tile = pos_t.shape[2]
    return pl.pallas_call(
        functools.partial(_combine_kernel, rows=tile, slabs=slabs, final_norm=final_norm),
        grid=(n // tile,),
        in_specs=[pl.BlockSpec((1, TOP_K, tile), lambda i: (i, 0, 0), memory_space=pltpu.SMEM),
                  pl.BlockSpec((tile, d), lambda i: (i, 0)),
                  pl.BlockSpec((tile, LANES), lambda i: (i, 0)),
                  pl.BlockSpec((1, d), lambda i: (0, 0)),
                  pl.BlockSpec(memory_space=pl.ANY)],
        out_specs=pl.BlockSpec((tile, d), lambda i: (i, 0)),
        out_shape=jax.ShapeDtypeStruct((n, d), F32),
        scratch_shapes=[pltpu.VMEM((tile * slabs, LANES), F32), pltpu.VMEM((tile * slabs, LANES), F32),
                        pltpu.SemaphoreType.DMA((2,))],
        compiler_params=_params("arbitrary"),
        name="moe_combine",
    )(pos_t, h, info, g_final, y_sorted)


def _moe_layer(h, g, router, wg, wu, wd, g_final, final_norm):
    n, d = h.shape
    n_exp = router.shape[1]
    tm = min(512, n)
    router_padded = jnp.pad(router, ((0, 0), (0, LANES - n_exp)))
    u, info, cnt = _route(h, g, router_padded, n_exp)

    idx = info[:, 0:TOP_K].astype(jnp.int32)
    rank = info[:, 4:4 + TOP_K].astype(jnp.int32)
    counts = cnt[0, :n_exp].astype(jnp.int32)
    padded = (counts + tm - 1) // tm * tm
    ends = jnp.cumsum(padded)
    starts = ends - padded
    pos = jnp.sum(jnp.where(idx[..., None] == jnp.arange(n_exp), starts, 0), axis=-1) + rank
    rows = TOP_K * n + n_exp * tm
    n_used = (ends[-1] // tm).astype(jnp.int32)
    tile_ids = jnp.minimum(jnp.arange(rows // tm, dtype=jnp.int32), n_used - 1)
    tile_expert = jnp.sum(tile_ids[:, None] * tm >= ends[None, :], axis=1).astype(jnp.int32)

    slabs = d // LANES
    assert slabs == 8
    tile = min(512, n)
    pos_t = pos.T.reshape(TOP_K, n // tile, tile).transpose(1, 0, 2)
    assert tm % ZERO_CHUNK == 0
    pad_start = jnp.concatenate([starts + counts, ends[-1:]]).astype(jnp.int32)
    pad_cnt = jnp.concatenate([padded - counts, (rows - ends[-1:]) // ZERO_CHUNK]).astype(jnp.int32)
    x_sorted = _dispatch(u, pos_t, pad_start, pad_cnt, rows, tile, slabs)
    y_sorted = _moe_experts(x_sorted, tile_expert, n_used.reshape(1), wg, wu, wd, tm, slabs)
    return _combine(h, info, pos_t, y_sorted, g_final, final_norm, slabs)


def _mla_weights(wq_b, wkv_a, wkv_b, heads, lora):
    def padded_rope(w):
        pad = jnp.zeros(w.shape[:-1] + (LANES - ROPE,), w.dtype)
        swapped = jnp.concatenate([w[..., ROPE // 2:], w[..., :ROPE // 2]], axis=-1)
        return jnp.concatenate([w, pad], axis=-1), jnp.concatenate([swapped, pad], axis=-1)

    rq = wq_b.shape[0]
    wq = wq_b.reshape(rq, heads, NOPE + ROPE)
    q_r, q_rs = padded_rope(wq[..., NOPE:])
    wqb = jnp.concatenate([wq[..., :NOPE].reshape(rq, -1), q_r.reshape(rq, -1), q_rs.reshape(rq, -1)], axis=1)
    k_r, k_rs = padded_rope(wkv_a[:, lora:])
    wkva = jnp.concatenate([wkv_a[:, :lora], k_r, k_rs], axis=1)
    wkv = wkv_b.reshape(lora, heads, 2 * NOPE)
    wkvb = jnp.concatenate([wkv[..., :NOPE].reshape(lora, -1), wkv[..., NOPE:].reshape(lora, -1)], axis=1)
    return wqb.astype(BF16), wkva.astype(BF16), wkvb.astype(BF16)


def _cast_kernel(x_ref, o_ref):
    o_ref[...] = x_ref[0].astype(BF16)


def _to_bf16(w, layer):
    cols = w.shape[-1]
    w3 = w.reshape(w.shape[0], -1, cols)
    rows = w3.shape[1]
    cap = 256 if cols > 2048 else 1024
    tr = next(t for t in (1024, 512, 256, 128, 64, 32, 16, 8) if t <= cap and rows % t == 0)
    out = pl.pallas_call(
        _cast_kernel,
        grid=(rows // tr,),
        in_specs=[pl.BlockSpec((1, tr, cols), lambda i: (layer, i, 0))],
        out_specs=pl.BlockSpec((tr, cols), lambda i: (i, 0)),
        out_shape=jax.ShapeDtypeStruct((rows, cols), BF16),
        compiler_params=_params("arbitrary"),
        name="weights_to_bf16",
    )(w3)
    return out.reshape(w.shape[1:])


def _final_norm_kernel(h_ref, g_ref, o_ref):
    o_ref[...] = _rms(h_ref[...], g_ref[...])


def _final_norm(h, g):
    n, d = h.shape
    tm = min(512, n)
    return pl.pallas_call(
        _final_norm_kernel,
        grid=(n // tm,),
        in_specs=[pl.BlockSpec((tm, d), lambda i: (i, 0)), pl.BlockSpec((1, d), lambda i: (0, 0))],
        out_specs=pl.BlockSpec((tm, d), lambda i: (i, 0)),
        out_shape=jax.ShapeDtypeStruct((n, d), F32),
        compiler_params=_params("arbitrary"),
        name="final_norm",
    )(h, g)


@jax.jit
def _trunk(x, positions, norm_mix, norm_ffn, norm_final, pool_w, pool_scale,
           conv_w1, conv_b1, conv_dw, conv_bdw, conv_ln_g, conv_ln_b, conv_w2, conv_b2,
           mla_wq_a, mla_q_norm, mla_wq_b, mla_wkv_a, mla_kv_norm, mla_wkv_b, mla_wo,
           ffn_wg, ffn_wu, ffn_wd, moe_router, moe_wg, moe_wu, moe_wd):
    batch, seq, d = x.shape
    depth = norm_mix.shape[0]
    lora = mla_kv_norm.shape[1]
    heads = mla_wo.shape[1] // NOPE
    assert mla_wq_b.shape[2] == heads * (NOPE + ROPE) and mla_wkv_a.shape[2] == lora + ROPE
    n = batch * seq
    h = x.reshape(n, d)
    pos = positions.reshape(n, 1)
    row = lambda v: v.reshape(1, -1)

    half = jnp.arange(0, ROPE, 2, dtype=F32) / ROPE
    inv_freq = ROPE_THETA ** (-half)
    invf = row(jnp.concatenate([inv_freq, inv_freq, jnp.zeros((LANES - ROPE,), F32)]))
    sgn = row(jnp.concatenate([-jnp.ones((ROPE // 2,), F32), jnp.ones((ROPE // 2,), F32),
                               jnp.zeros((LANES - ROPE,), F32)]))

    for i in range(depth):
        kind, j = i % 3, i // 3
        g = row(norm_mix[i])
        if kind == 0:
            h = _pool_layer(h, g, pool_w[j].astype(BF16), row(pool_scale[j]), batch=batch, seq=seq)
        elif kind == 1:
            h = _conv_layer(h, g, conv_w1[j].astype(BF16), row(conv_b1[j]), conv_dw[j], row(conv_bdw[j]),
                            row(conv_ln_g[j]), row(conv_ln_b[j]), conv_w2[j].astype(BF16), row(conv_b2[j]),
                            batch=batch, seq=seq)
        else:
            wqb, wkva, wkvb = _mla_weights(mla_wq_b[j], mla_wkv_a[j], mla_wkv_b[j], heads, lora)
            h = _mla_layer(h, pos, g, mla_wq_a[j].astype(BF16), row(mla_q_norm[j]), wqb, wkva,
                           row(mla_kv_norm[j]), wkvb, mla_wo[j].astype(BF16), invf, sgn,
                           batch=batch, seq=seq, heads=heads, lora=lora)
        g = row(norm_ffn[i])
        f_idx = i // 2
        last = i == depth - 1
        if i % 2 == 0:
            h = _ffn_layer(h, g, _to_bf16(ffn_wg, f_idx), _to_bf16(ffn_wu, f_idx), _to_bf16(ffn_wd, f_idx))
            if last:
                h = _final_norm(h, row(norm_final))
        else:
            h = _moe_layer(h, g, moe_router[f_idx], _to_bf16(moe_wg, f_idx), _to_bf16(moe_wu, f_idx),
                           _to_bf16(moe_wd, f_idx), row(norm_final), last)
    return h.reshape(batch, seq, d)


def kernel(x, positions, norm_mix, norm_ffn, norm_final, pool_w, pool_scale, conv_w1, conv_b1, conv_dw, conv_bdw, conv_ln_g, conv_ln_b, conv_w2, conv_b2, mla_wq_a, mla_q_norm, mla_wq_b, mla_wkv_a, mla_kv_norm, mla_wkv_b, mla_wo, ffn_wg, ffn_wu, ffn_wd, moe_router, moe_wg, moe_wu, moe_wd):
    return _trunk(x, positions, norm_mix, norm_ffn, norm_final, pool_w, pool_scale,
                  conv_w1, conv_b1, conv_dw, conv_bdw, conv_ln_g, conv_ln_b, conv_w2, conv_b2,
                  mla_wq_a, mla_q_norm, mla_wq_b, mla_wkv_a, mla_kv_norm, mla_wkv_b, mla_wo,
                  ffn_wg, ffn_wu, ffn_wd, moe_router, moe_wg, moe_wu, moe_wd)
```

```python
import functools

import jax
import jax.numpy as jnp
from jax import lax
from jax.experimental import pallas as pl
from jax.experimental.pallas import tpu as pltpu

F32 = jnp.float32
BF16 = jnp.bfloat16

NORM_EPS = 1e-6
LN_EPS = 1e-5
ROPE_THETA = 10000.0
LOG2E = 1.4426950408889634

POOL_WINDOWS = (2, 4, 8, 16)
POOL_HALO = 16
CONV_HALO = 32
LANES = 128
SUBLANES = 8
NOPE = 128
ROPE = 64
V_ROWS = NOPE + 16
TOP_K = 2
GATHER_UNROLL = 8
ZERO_CHUNK = 64

VMEM_LIMIT = 56 * 1024 * 1024


def _params(*sem):
    return pltpu.CompilerParams(dimension_semantics=sem, vmem_limit_bytes=VMEM_LIMIT)


def _rms(x, g, eps=NORM_EPS):
    return x * lax.rsqrt(jnp.mean(x * x, axis=-1, keepdims=True) + eps) * g


def _dot(a, b):
    return jnp.dot(a, b, preferred_element_type=F32)


def _ff_chunk(f):
    for c in (1792, 1024, 512, 256, 128):
        if f % c == 0:
            return c
    return f


def _pool_kernel(h_ref, g_ref, w_ref, sc_ref, o_ref, ubuf, *, ts, gdim):
    j = pl.program_id(1)
    h = h_ref[...]
    u = _rms(h, g_ref[...])

    @pl.when(j == 0)
    def _():
        ubuf[0:POOL_HALO, :] = jnp.zeros((POOL_HALO, ubuf.shape[1]), F32)

    ubuf[POOL_HALO:POOL_HALO + ts, :] = u
    t1 = lax.broadcasted_iota(jnp.int32, (ts, 1), 0) + (j * ts + 1)
    for g, win in enumerate(POOL_WINDOWS):
        cs = slice(g * gdim, (g + 1) * gdim)
        ug = u[:, cs]
        s = ug
        for k in range(1, win):
            s = s + ubuf[POOL_HALO - k:POOL_HALO - k + ts, cs]
        cnt = jnp.minimum(t1, win).astype(F32)
        pooled = s / cnt - ug
        m = _dot(pooled.astype(BF16), w_ref[g])
        o_ref[:, cs] = h[:, cs] + m * sc_ref[:, cs]
    ubuf[0:POOL_HALO, :] = ubuf[ts:ts + POOL_HALO, :]


def _pool_layer(h, g, w, scale, *, batch, seq):
    n, d = h.shape
    ts = min(512, seq)
    nt = seq // ts
    gdim = d // len(POOL_WINDOWS)
    row = lambda b, j: (b * nt + j, 0)
    const2 = lambda b, j: (0, 0)
    return pl.pallas_call(
        functools.partial(_pool_kernel, ts=ts, gdim=gdim),
        grid=(batch, nt),
        in_specs=[pl.BlockSpec((ts, d), row),
                  pl.BlockSpec((1, d), const2),
                  pl.BlockSpec(w.shape, lambda b, j: (0, 0, 0)),
                  pl.BlockSpec((1, d), const2)],
        out_specs=pl.BlockSpec((ts, d), row),
        out_shape=jax.ShapeDtypeStruct((n, d), F32),
        scratch_shapes=[pltpu.VMEM((ts + POOL_HALO, d), F32)],
        compiler_params=_params("arbitrary", "arbitrary"),
        name="pool_mixer",
    )(h, g, w, scale)


def _swiglu_step(u, wg, wu, wd):
    a = _dot(u, wg)
    b = _dot(u, wu)
    hh = a * jax.nn.sigmoid(a) * b
    return _dot(hh.astype(BF16), wd)


def _ffn_kernel(h_ref, g_ref, wg_ref, wu_ref, wd_ref, o_ref, u_sc, acc_sc):
    j = pl.program_id(1)

    @pl.when(j == 0)
    def _():
        u_sc[...] = _rms(h_ref[...], g_ref[...]).astype(BF16)
        acc_sc[...] = jnp.zeros(acc_sc.shape, F32)

    acc_sc[...] += _swiglu_step(u_sc[...], wg_ref[...], wu_ref[...], wd_ref[...])

    @pl.when(j == pl.num_programs(1) - 1)
    def _():
        o_ref[...] = h_ref[...] + acc_sc[...]


def _ffn_layer(h, g, wg, wu, wd):
    n, d = h.shape
    f = wg.shape[1]
    tm = min(512, n)
    tf = _ff_chunk(f)
    return pl.pallas_call(
        _ffn_kernel,
        grid=(n // tm, f // tf),
        in_specs=[pl.BlockSpec((tm, d), lambda i, j: (i, 0)),
                  pl.BlockSpec((1, d), lambda i, j: (0, 0)),
                  pl.BlockSpec((d, tf), lambda i, j: (0, j)),
                  pl.BlockSpec((d, tf), lambda i, j: (0, j)),
                  pl.BlockSpec((tf, d), lambda i, j: (j, 0))],
        out_specs=pl.BlockSpec((tm, d), lambda i, j: (i, 0)),
        out_shape=jax.ShapeDtypeStruct((n, d), F32),
        scratch_shapes=[pltpu.VMEM((tm, d), BF16), pltpu.VMEM((tm, d), F32)],
        compiler_params=_params("arbitrary", "arbitrary"),
        name="dense_swiglu",
    )(h, g, wg, wu, wd)


def _conv_kernel(h_ref, g_ref, w1_ref, b1_ref, dw_ref, bdw_ref, lng_ref, lnb_ref, w2_ref, b2_ref,
                 o_ref, gbuf, cbuf, *, ts, width, rb):
    j = pl.program_id(1)
    d = h_ref.shape[1]
    h = h_ref[...]
    u = _rms(h, g_ref[...]).astype(BF16)
    a = _dot(u, w1_ref[...]) + b1_ref[...]
    glu = a[:, :d] * jax.nn.sigmoid(a[:, d:])

    keep = CONV_HALO + SUBLANES

    @pl.when(j == 0)
    def _():
        gbuf[:, 0:keep, :] = jnp.zeros((SUBLANES, keep, d), F32)

    for b in range(SUBLANES):
        gbuf[b, CONV_HALO + b:CONV_HALO + b + ts, :] = glu
    for cb in range(d // LANES):
        cs = slice(cb * LANES, (cb + 1) * LANES)
        for r0 in range(0, ts, rb):
            acc = jnp.zeros((rb, LANES), F32)
            for k in range(width):
                a, b = divmod(width - 1 - k, SUBLANES)
                off = CONV_HALO + r0 - SUBLANES * a
                acc = acc + gbuf[b, off:off + rb, cs] * dw_ref[k:k + 1, cs]
            cbuf[r0:r0 + rb, cs] = acc + bdw_ref[:, cs]
    gbuf[:, 0:keep, :] = gbuf[:, ts:ts + keep, :]

    c = cbuf[...]
    mu = jnp.mean(c, axis=-1, keepdims=True)
    cc = c - mu
    var = jnp.mean(cc * cc, axis=-1, keepdims=True)
    un = cc * lax.rsqrt(var + LN_EPS) * lng_ref[...] + lnb_ref[...]
    un = un * jax.nn.sigmoid(un)
    o_ref[...] = h + _dot(un.astype(BF16), w2_ref[...]) + b2_ref[...]


def _conv_layer(h, g, w1, b1, dw, bdw, lng, lnb, w2, b2, *, batch, seq):
    n, d = h.shape
    width = dw.shape[0]
    assert width - 1 <= CONV_HALO
    ts = min(256, seq)
    rb = min(128, ts)
    nt = seq // ts
    row = lambda b, j: (b * nt + j, 0)
    c2 = lambda b, j: (0, 0)
    return pl.pallas_call(
        functools.partial(_conv_kernel, ts=ts, width=width, rb=rb),
        grid=(batch, nt),
        in_specs=[pl.BlockSpec((ts, d), row),
                  pl.BlockSpec((1, d), c2),
                  pl.BlockSpec(w1.shape, c2),
                  pl.BlockSpec(b1.shape, c2),
                  pl.BlockSpec(dw.shape, c2),
                  pl.BlockSpec((1, d), c2),
                  pl.BlockSpec((1, d), c2),
                  pl.BlockSpec((1, d), c2),
                  pl.BlockSpec(w2.shape, c2),
                  pl.BlockSpec((1, d), c2)],
        out_specs=pl.BlockSpec((ts, d), row),
        out_shape=jax.ShapeDtypeStruct((n, d), F32),
        scratch_shapes=[pltpu.VMEM((SUBLANES, ts + CONV_HALO + SUBLANES, d), F32), pltpu.VMEM((ts, d), F32)],
        compiler_params=_params("arbitrary", "arbitrary"),
        name="conv_module",
    )(h, g, w1, b1, dw, bdw, lng, lnb, w2, b2)


def _mla_proj_kernel(h_ref, pos_ref, g_ref, wqa_ref, qnorm_ref, wqb_ref, wkva_ref, kvnorm_ref, wkvb_ref,
                     invf_ref, sgn_ref, qt_o, k_o, vt_o, *, heads, lora):
    hn = heads * NOPE
    rows = h_ref.shape[0]
    u = _rms(h_ref[...], g_ref[...]).astype(BF16)
    qa = _rms(_dot(u, wqa_ref[...]), qnorm_ref[...]).astype(BF16)
    q = _dot(qa, wqb_ref[...])
    kva = _dot(u, wkva_ref[...])
    ckv = _rms(kva[:, :lora], kvnorm_ref[...]).astype(BF16)
    kv = _dot(ckv, wkvb_ref[...])
    ang = pos_ref[...].astype(F32) * invf_ref[...]
    c = jnp.cos(ang)
    s = jnp.sin(ang) * sgn_ref[...]
    kr = (kva[:, lora:lora + LANES] * c + kva[:, lora + LANES:lora + 2 * LANES] * s).astype(BF16)
    ones = jnp.ones((V_ROWS - NOPE, rows), BF16)
    for hd in range(heads):
        ns = slice(hd * NOPE, (hd + 1) * NOPE)
        qr = q[:, hn + hd * LANES:hn + (hd + 1) * LANES] * c + q[:, 2 * hn + hd * LANES:2 * hn + (hd + 1) * LANES] * s
        qt_o[2 * hd * NOPE:(2 * hd + 1) * NOPE, :] = q[:, ns].T.astype(BF16)
        qt_o[(2 * hd + 1) * NOPE:(2 * hd + 2) * NOPE, :] = qr.T.astype(BF16)
        k_o[:, 2 * hd * NOPE:(2 * hd + 1) * NOPE] = kv[:, ns].astype(BF16)
        k_o[:, (2 * hd + 1) * NOPE:(2 * hd + 2) * NOPE] = kr
        vt_o[hd * V_ROWS:hd * V_ROWS + NOPE, :] = kv[:, hn + hd * NOPE:hn + (hd + 1) * NOPE].T.astype(BF16)
        vt_o[hd * V_ROWS + NOPE:(hd + 1) * V_ROWS, :] = ones


def _attn_kernel(qt_ref, k_ref, vt_ref, h_ref, wot_ref, o_ref, m_sc, acc_sc, *, heads, tq, tk, scale):
    i = pl.program_id(1)
    j = pl.program_id(2)
    c = scale * LOG2E

    @pl.when(j == 0)
    def _():
        m_sc[...] = jnp.full(m_sc.shape, -jnp.inf, F32)
        acc_sc[...] = jnp.zeros(acc_sc.shape, F32)

    def step(diagonal):
        if diagonal:
            causal = (lax.broadcasted_iota(jnp.int32, (tk, tq), 0)
                      <= lax.broadcasted_iota(jnp.int32, (tk, tq), 1))
        es = [slice(2 * hd * NOPE, (2 * hd + 2) * NOPE) for hd in range(heads)]
        vs = [slice(hd * V_ROWS, (hd + 1) * V_ROWS) for hd in range(heads)]
        ss = [_dot(k_ref[:, e], qt_ref[e, :]) for e in es]
        if diagonal:
            ss = [jnp.where(causal, s, -jnp.inf) for s in ss]
        m_prev = [m_sc[hd] for hd in range(heads)]
        m_new = [jnp.maximum(mp, jnp.max(s, axis=0, keepdims=True)) for mp, s in zip(m_prev, ss)]
        ps = [jnp.exp2((s - mn) * c).astype(BF16) for s, mn in zip(ss, m_new)]
        pv = [_dot(vt_ref[v, :], p) for p, v in zip(ps, vs)]
        for hd in range(heads):
            alpha = jnp.exp2((m_prev[hd] - m_new[hd]) * c)
            acc_sc[vs[hd], :] = alpha * acc_sc[vs[hd], :] + pv[hd]
            m_sc[hd] = m_new[hd]

    @pl.when(j < i)
    def _():
        step(False)

    @pl.when(j == i)
    def _():
        step(True)
        parts = [acc_sc[hd * V_ROWS:hd * V_ROWS + NOPE, :] / acc_sc[hd * V_ROWS + NOPE:hd * V_ROWS + NOPE + 1, :]
                 for hd in range(heads)]
        ot = jnp.concatenate(parts, axis=0).astype(BF16)
        o_ref[...] = h_ref[...] + _dot(wot_ref[...], ot).T


def _mla_layer(h, positions, g, wqa, qnorm, wqb, wkva, kvnorm, wkvb, wot, invf, sgn, *, batch, seq, heads, lora):
    n, d = h.shape
    hn = heads * NOPE
    tp = min(256, n)
    c2 = lambda i: (0, 0)
    row = lambda i: (i, 0)
    col = lambda i: (0, i)
    qt, k, vt = pl.pallas_call(
        functools.partial(_mla_proj_kernel, heads=heads, lora=lora),
        grid=(n // tp,),
        in_specs=[pl.BlockSpec((tp, d), row),
                  pl.BlockSpec((tp, 1), row),
                  pl.BlockSpec((1, d), c2),
                  pl.BlockSpec(wqa.shape, c2),
                  pl.BlockSpec(qnorm.shape, c2),
                  pl.BlockSpec(wqb.shape, c2),
                  pl.BlockSpec(wkva.shape, c2),
                  pl.BlockSpec(kvnorm.shape, c2),
                  pl.BlockSpec(wkvb.shape, c2),
                  pl.BlockSpec((1, LANES), c2),
                  pl.BlockSpec((1, LANES), c2)],
        out_specs=[pl.BlockSpec((2 * hn, tp), col), pl.BlockSpec((tp, 2 * hn), row),
                   pl.BlockSpec((heads * V_ROWS, tp), col)],
        out_shape=[jax.ShapeDtypeStruct((2 * hn, n), BF16), jax.ShapeDtypeStruct((n, 2 * hn), BF16),
                   jax.ShapeDtypeStruct((heads * V_ROWS, n), BF16)],
        compiler_params=_params("arbitrary"),
        name="mla_proj",
    )(h, positions, g, wqa, qnorm, wqb, wkva, kvnorm, wkvb, invf, sgn)

    tq = tk = min(512, seq)
    nq = seq // tq
    scale = float(NOPE + ROPE) ** -0.5
    qrow = lambda b, i, j: (b * nq + i, 0)
    qcol = lambda b, i, j: (0, b * nq + i)
    krow = lambda b, i, j: (b * nq + jnp.minimum(j, i), 0)
    kcol = lambda b, i, j: (0, b * nq + jnp.minimum(j, i))
    return pl.pallas_call(
        functools.partial(_attn_kernel, heads=heads, tq=tq, tk=tk, scale=scale),
        grid=(batch, nq, nq),
        in_specs=[pl.BlockSpec((2 * hn, tq), qcol),
                  pl.BlockSpec((tk, 2 * hn), krow),
                  pl.BlockSpec((heads * V_ROWS, tk), kcol),
                  pl.BlockSpec((tq, d), qrow),
                  pl.BlockSpec(wot.shape, lambda b, i, j: (0, 0))],
        out_specs=pl.BlockSpec((tq, d), qrow),
        out_shape=jax.ShapeDtypeStruct((n, d), F32),
        scratch_shapes=[pltpu.VMEM((heads, 1, tq), F32), pltpu.VMEM((heads * V_ROWS, tq), F32)],
        compiler_params=_params("arbitrary", "arbitrary", "arbitrary"),
        name="mla_attention",
    )(qt, k, vt, h, wot)


def _store_row_tiles(ref, value, rows):
    slabs = value.shape[1] // LANES
    for s in range(slabs):
        ref[pl.ds(s, rows, stride=slabs), :] = value[:, s * LANES:(s + 1) * LANES]


def _load_row_tile_slab(ref, s, rows, slabs):
    return ref[pl.ds(s, rows, stride=slabs), :]


def _route_kernel(h_ref, g_ref, r_ref, u_ref, info_ref, cnt_ref, carry, *, tr, n_exp):
    i = pl.program_id(0)

    @pl.when(i == 0)
    def _():
        carry[...] = jnp.zeros(carry.shape, F32)

    u = _rms(h_ref[...], g_ref[...])
    _store_row_tiles(u_ref, u, tr)
    logits = jnp.dot(u, r_ref[...], preferred_element_type=F32, precision=lax.Precision.HIGHEST)
    lane = lax.broadcasted_iota(jnp.int32, (tr, LANES), 1)
    lg = jnp.where(lane < n_exp, logits, -jnp.inf)
    m1 = jnp.max(lg, axis=1, keepdims=True)
    i1 = jnp.min(jnp.where(lg == m1, lane, LANES), axis=1, keepdims=True)
    lg2 = jnp.where(lane == i1, -jnp.inf, lg)
    m2 = jnp.max(lg2, axis=1, keepdims=True)
    i2 = jnp.min(jnp.where(lg2 == m2, lane, LANES), axis=1, keepdims=True)
    e = jnp.exp(m2 - m1)
    g1 = 1.0 / (1.0 + e)
    g2 = e / (1.0 + e)
    sel1 = lane == i1
    sel2 = lane == i2
    onehot = jnp.where(sel1 | sel2, 1.0, 0.0)
    rr = lax.broadcasted_iota(jnp.int32, (tr, tr), 0)
    cc = lax.broadcasted_iota(jnp.int32, (tr, tr), 1)
    tri = jnp.where(cc < rr, 1.0, 0.0).astype(BF16)
    before = _dot(tri, onehot.astype(BF16)) + carry[0:1, :]
    rank1 = jnp.sum(jnp.where(sel1, before, 0.0), axis=1, keepdims=True)
    rank2 = jnp.sum(jnp.where(sel2, before, 0.0), axis=1, keepdims=True)
    carry[...] = carry[...] + jnp.sum(onehot, axis=0, keepdims=True)
    cols = (i1.astype(F32), i2.astype(F32), g1, g2, rank1, rank2)
    info = jnp.zeros((tr, LANES), F32)
    for k, col in enumerate(cols):
        info = jnp.where(lane == k, col, info)
    info_ref[...] = info
    cnt_ref[...] = carry[...]


def _route(h, g, router_padded, n_exp):
    n, d = h.shape
    tr = min(512, n)
    slabs = d // LANES
    return pl.pallas_call(
        functools.partial(_route_kernel, tr=tr, n_exp=n_exp),
        grid=(n // tr,),
        in_specs=[pl.BlockSpec((tr, d), lambda i: (i, 0)),
                  pl.BlockSpec((1, d), lambda i: (0, 0)),
                  pl.BlockSpec((d, LANES), lambda i: (0, 0))],
        out_specs=[pl.BlockSpec((tr * slabs, LANES), lambda i: (i, 0)),
                   pl.BlockSpec((tr, LANES), lambda i: (i, 0)),
                   pl.BlockSpec((8, LANES), lambda i: (0, 0))],
        out_shape=[jax.ShapeDtypeStruct((n * slabs, LANES), F32),
                   jax.ShapeDtypeStruct((n, LANES), F32),
                   jax.ShapeDtypeStruct((8, LANES), F32)],
        scratch_shapes=[pltpu.VMEM((8, LANES), F32)],
        compiler_params=_params("arbitrary"),
        name="moe_route",
    )(h, g, router_padded)


def _gather_row_tiles(idx_ref, k, src_ref, dst_ref, sem, rows, slabs):
    def start8(r8, c):
        for q in range(GATHER_UNROLL):
            r = r8 * GATHER_UNROLL + q
            src_row = pl.multiple_of(idx_ref[0, k, r] * slabs, slabs)
            dst_row = pl.multiple_of(r * slabs, slabs)
            pltpu.make_async_copy(src_ref.at[pl.ds(src_row, slabs)], dst_ref.at[pl.ds(dst_row, slabs)],
                                  sem).start(priority=q % 2)
        return c

    lax.fori_loop(0, rows // GATHER_UNROLL, start8, 0)
    pltpu.make_async_copy(src_ref.at[pl.ds(0, rows * slabs)], dst_ref, sem).wait()


def _dispatch_kernel(pad_start_ref, pad_cnt_ref, pos_ref, u_ref, x_ref, zero_sc, sem, *, rows, slabs, n_exp):
    def row_copy(src, src_row, dst_row):
        return pltpu.make_async_copy(src.at[pl.ds(pl.multiple_of(src_row * slabs, slabs), slabs)],
                                     x_ref.at[pl.ds(pl.multiple_of(dst_row * slabs, slabs), slabs)], sem)

    @pl.when(pl.program_id(0) == 0)
    def _():
        zero_sc[...] = jnp.zeros(zero_sc.shape, F32)
        for e in range(n_exp):
            def fill(i, c):
                row_copy(zero_sc, 0, pad_start_ref[e] + i).start()
                return c

            def drain(i, c):
                row_copy(zero_sc, 0, pad_start_ref[e] + i).wait()
                return c

            lax.fori_loop(0, pad_cnt_ref[e], fill, 0)
            lax.fori_loop(0, pad_cnt_ref[e], drain, 0)

        def chunk_copy(i):
            dst = pl.multiple_of((pad_start_ref[n_exp] + i * ZERO_CHUNK) * slabs, slabs)
            return pltpu.make_async_copy(zero_sc, x_ref.at[pl.ds(dst, ZERO_CHUNK * slabs)], sem)

        def fill_tail(i, c):
            chunk_copy(i).start()
            return c

        def drain_tail(i, c):
            chunk_copy(i).wait()
            return c

        lax.fori_loop(0, pad_cnt_ref[n_exp], fill_tail, 0)
        lax.fori_loop(0, pad_cnt_ref[n_exp], drain_tail, 0)

    def start8(r8, c):
        for q in range(GATHER_UNROLL):
            r = r8 * GATHER_UNROLL + q
            for k in range(TOP_K):
                row_copy(u_ref, r, pos_ref[0, k, r]).start(priority=k)
        return c

    lax.fori_loop(0, rows // GATHER_UNROLL, start8, 0)
    for k in range(TOP_K):
        pltpu.make_async_copy(u_ref, x_ref.at[pl.ds(0, rows * slabs)], sem).wait()


def _dispatch(u, pos_t, pad_start, pad_cnt, sorted_rows, tile, slabs):
    n = u.shape[0] // slabs
    grid_spec = pltpu.PrefetchScalarGridSpec(
        num_scalar_prefetch=2,
        grid=(n // tile,),
        in_specs=[pl.BlockSpec((1, TOP_K, tile), lambda i, ps, pc: (i, 0, 0), memory_space=pltpu.SMEM),
                  pl.BlockSpec((tile * slabs, LANES), lambda i, ps, pc: (i, 0))],
        out_specs=pl.BlockSpec(memory_space=pl.ANY),
        scratch_shapes=[pltpu.VMEM((ZERO_CHUNK * slabs, LANES), F32), pltpu.SemaphoreType.DMA(())],
    )
    return pl.pallas_call(
        functools.partial(_dispatch_kernel, rows=tile, slabs=slabs, n_exp=pad_start.shape[0] - 1),
        grid_spec=grid_spec,
        out_shape=jax.ShapeDtypeStruct((sorted_rows * slabs, LANES), F32),
        compiler_params=_params("arbitrary"),
        name="moe_dispatch",
    )(pad_start, pad_cnt, pos_t, u)


def _moe_kernel(te_ref, nu_ref, x_ref, wg_ref, wu_ref, wd_ref, o_ref, x_sc, acc_sc, *, tm, slabs):
    t = pl.program_id(0)
    j = pl.program_id(1)

    @pl.when(t < nu_ref[0])
    def _():
        @pl.when(j == 0)
        def _():
            for s in range(slabs):
                x_sc[:, s * LANES:(s + 1) * LANES] = _load_row_tile_slab(x_ref, s, tm, slabs).astype(BF16)
            acc_sc[...] = jnp.zeros(acc_sc.shape, F32)

        acc_sc[...] += _swiglu_step(x_sc[...], wg_ref[0], wu_ref[0], wd_ref[0])

        @pl.when(j == pl.num_programs(1) - 1)
        def _():
            _store_row_tiles(o_ref, acc_sc[...], tm)

    @pl.when((t >= nu_ref[0]) & (j == 0))
    def _():
        o_ref[...] = jnp.zeros(o_ref.shape, F32)


def _moe_experts(x_sorted, tile_expert, n_used, wg, wu, wd, tm, slabs):
    d = slabs * LANES
    rows = x_sorted.shape[0] // slabs
    f = wg.shape[2]
    tf = _ff_chunk(f)
    nj = f // tf
    tmap = lambda t, j, te, nu: (jnp.minimum(t, nu[0] - 1), 0)
    jsel = lambda t, j, nu: jnp.where(t < nu[0], j, nj - 1)
    grid_spec = pltpu.PrefetchScalarGridSpec(
        num_scalar_prefetch=2,
        grid=(rows // tm, nj),
        in_specs=[pl.BlockSpec((tm * slabs, LANES), tmap),
                  pl.BlockSpec((1, d, tf), lambda t, j, te, nu: (te[t], 0, jsel(t, j, nu))),
                  pl.BlockSpec((1, d, tf), lambda t, j, te, nu: (te[t], 0, jsel(t, j, nu))),
                  pl.BlockSpec((1, tf, d), lambda t, j, te, nu: (te[t], jsel(t, j, nu), 0))],
        out_specs=pl.BlockSpec((tm * slabs, LANES), lambda t, j, te, nu: (t, 0)),
        scratch_shapes=[pltpu.VMEM((tm, d), BF16), pltpu.VMEM((tm, d), F32)],
    )
    return pl.pallas_call(
        functools.partial(_moe_kernel, tm=tm, slabs=slabs),
        grid_spec=grid_spec,
        out_shape=jax.ShapeDtypeStruct((rows * slabs, LANES), F32),
        compiler_params=_params("arbitrary", "arbitrary"),
        name="moe_experts",
    )(tile_expert, n_used, x_sorted, wg, wu, wd)


def _combine_kernel(pos_ref, h_ref, gate_ref, gf_ref, y_ref, o_ref, y0, y1, sem, *, rows, slabs, final_norm):
    _gather_row_tiles(pos_ref, 0, y_ref, y0, sem.at[0], rows, slabs)
    _gather_row_tiles(pos_ref, 1, y_ref, y1, sem.at[1], rows, slabs)
    gates = gate_ref[...]
    g0 = gates[:, 2:3]
    g1 = gates[:, 3:4]
    ssq = jnp.zeros((rows, 1), F32)
    for s in range(slabs):
        cs = slice(s * LANES, (s + 1) * LANES)
        out = (h_ref[:, cs] + g0 * _load_row_tile_slab(y0, s, rows, slabs)
               + g1 * _load_row_tile_slab(y1, s, rows, slabs))
        o_ref[:, cs] = out
        ssq = ssq + jnp.sum(out * out, axis=-1, keepdims=True)
    if final_norm:
        d = slabs * LANES
        o_ref[...] = o_ref[...] * lax.rsqrt(ssq / d + NORM_EPS) * gf_ref[...]


def _combine(h, info, pos_t, y_sorted, g_final, final_norm, slabs):
    n, d = h.shape
    tile = pos_t.shape[2]
    return pl.pallas_call(
        functools.partial(_combine_kernel, rows=tile, slabs=slabs, final_norm=final_norm),
        grid=(n // tile,),
        in_specs=[pl.BlockSpec((1, TOP_K, tile), lambda i: (i, 0, 0), memory_space=pltpu.SMEM),
                  pl.BlockSpec((tile, d), lambda i: (i, 0)),
                  pl.BlockSpec((tile, LANES), lambda i: (i, 0)),
                  pl.BlockSpec((1, d), lambda i: (0, 0)),
                  pl.BlockSpec(memory_space=pl.ANY)],
        out_specs=pl.BlockSpec((tile, d), lambda i: (i, 0)),
        out_shape=jax.ShapeDtypeStruct((n, d), F32),
        scratch_shapes=[pltpu.VMEM((tile * slabs, LANES), F32), pltpu.VMEM((tile * slabs, LANES), F32),
                        pltpu.SemaphoreType.DMA((2,))],
        compiler_params=_params("arbitrary"),
        name="moe_combine",
    )(pos_t, h, info, g_final, y_sorted)


def _moe_layer(h, g, router, wg, wu, wd, g_final, final_norm):
    n, d = h.shape
    n_exp = router.shape[1]
    tm = min(512, n)
    router_padded = jnp.pad(router, ((0, 0), (0, LANES - n_exp)))
    u, info, cnt = _route(h, g, router_padded, n_exp)

    idx = info[:, 0:TOP_K].astype(jnp.int32)
    rank = info[:, 4:4 + TOP_K].astype(jnp.int32)
    counts = cnt[0, :n_exp].astype(jnp.int32)
    padded = (counts + tm - 1) // tm * tm
    ends = jnp.cumsum(padded)
    starts = ends - padded
    pos = jnp.sum(jnp.where(idx[..., None] == jnp.arange(n_exp), starts, 0), axis=-1) + rank
    rows = TOP_K * n + n_exp * tm
    n_used = (ends[-1] // tm).astype(jnp.int32)
    tile_ids = jnp.minimum(jnp.arange(rows // tm, dtype=jnp.int32), n_used - 1)
    tile_expert = jnp.sum(tile_ids[:, None] * tm >= ends[None, :], axis=1).astype(jnp.int32)

    slabs = d // LANES
    assert slabs == 8
    tile = min(512, n)
    pos_t = pos.T.reshape(TOP_K, n // tile, tile).transpose(1, 0, 2)
    assert tm % ZERO_CHUNK == 0
    pad_start = jnp.concatenate([starts + counts, ends[-1:]]).astype(jnp.int32)
    pad_cnt = jnp.concatenate([padded - counts, (rows - ends[-1:]) // ZERO_CHUNK]).astype(jnp.int32)
    x_sorted = _dispatch(u, pos_t, pad_start, pad_cnt, rows, tile, slabs)
    y_sorted = _moe_experts(x_sorted, tile_expert, n_used.reshape(1), wg, wu, wd, tm, slabs)
    return _combine(h, info, pos_t, y_sorted, g_final, final_norm, slabs)


def _mla_weights(wq_b, wkv_a, wkv_b, heads, lora):
    def padded_rope(w):
        pad = jnp.zeros(w.shape[:-1] + (LANES - ROPE,), w.dtype)
        swapped = jnp.concatenate([w[..., ROPE // 2:], w[..., :ROPE // 2]], axis=-1)
        return jnp.concatenate([w, pad], axis=-1), jnp.concatenate([swapped, pad], axis=-1)

    rq = wq_b.shape[0]
    wq = wq_b.reshape(rq, heads, NOPE + ROPE)
    q_r, q_rs = padded_rope(wq[..., NOPE:])
    wqb = jnp.concatenate([wq[..., :NOPE].reshape(rq, -1), q_r.reshape(rq, -1), q_rs.reshape(rq, -1)], axis=1)
    k_r, k_rs = padded_rope(wkv_a[:, lora:])
    wkva = jnp.concatenate([wkv_a[:, :lora], k_r, k_rs], axis=1)
    wkv = wkv_b.reshape(lora, heads, 2 * NOPE)
    wkvb = jnp.concatenate([wkv[..., :NOPE].reshape(lora, -1), wkv[..., NOPE:].reshape(lora, -1)], axis=1)
    return wqb.astype(BF16), wkva.astype(BF16), wkvb.astype(BF16)


def _cast_kernel(x_ref, o_ref):
    o_ref[...] = x_ref[0].astype(BF16)


def _to_bf16(w, layer):
    cols = w.shape[-1]
    w3 = w.reshape(w.shape[0], -1, cols)
    rows = w3.shape[1]
    cap = 256 if cols > 2048 else 1024
    tr = next(t for t in (1024, 512, 256, 128, 64, 32, 16, 8) if t <= cap and rows % t == 0)
    out = pl.pallas_call(
        _cast_kernel,
        grid=(rows // tr,),
        in_specs=[pl.BlockSpec((1, tr, cols), lambda i: (layer, i, 0))],
        out_specs=pl.BlockSpec((tr, cols), lambda i: (i, 0)),
        out_shape=jax.ShapeDtypeStruct((rows, cols), BF16),
        compiler_params=_params("arbitrary"),
        name="weights_to_bf16",
    )(w3)
    return out.reshape(w.shape[1:])


def _final_norm_kernel(h_ref, g_ref, o_ref):
    o_ref[...] = _rms(h_ref[...], g_ref[...])


def _final_norm(h, g):
    n, d = h.shape
    tm = min(512, n)
    return pl.pallas_call(
        _final_norm_kernel,
        grid=(n // tm,),
        in_specs=[pl.BlockSpec((tm, d), lambda i: (i, 0)), pl.BlockSpec((1, d), lambda i: (0, 0))],
        out_specs=pl.BlockSpec((tm, d), lambda i: (i, 0)),
        out_shape=jax.ShapeDtypeStruct((n, d), F32),
        compiler_params=_params("arbitrary"),
        name="final_norm",
    )(h, g)


@jax.jit
def _trunk(x, positions, norm_mix, norm_ffn, norm_final, pool_w, pool_scale,
           conv_w1, conv_b1, conv_dw, conv_bdw, conv_ln_g, conv_ln_b, conv_w2, conv_b2,
           mla_wq_a, mla_q_norm, mla_wq_b, mla_wkv_a, mla_kv_norm, mla_wkv_b, mla_wo,
           ffn_wg, ffn_wu, ffn_wd, moe_router, moe_wg, moe_wu, moe_wd):
    batch, seq, d = x.shape
    depth = norm_mix.shape[0]
    lora = mla_kv_norm.shape[1]
    heads = mla_wo.shape[1] // NOPE
    assert mla_wq_b.shape[2] == heads * (NOPE + ROPE) and mla_wkv_a.shape[2] == lora + ROPE
    n = batch * seq
    h = x.reshape(n, d)
    pos = positions.reshape(n, 1)
    row = lambda v: v.reshape(1, -1)

    half = jnp.arange(0, ROPE, 2, dtype=F32) / ROPE
    inv_freq = ROPE_THETA ** (-half)
    invf = row(jnp.concatenate([inv_freq, inv_freq, jnp.zeros((LANES - ROPE,), F32)]))
    sgn = row(jnp.concatenate([-jnp.ones((ROPE // 2,), F32), jnp.ones((ROPE // 2,), F32),
                               jnp.zeros((LANES - ROPE,), F32)]))

    for i in range(depth):
        kind, j = i % 3, i // 3
        g = row(norm_mix[i])
        if kind == 0:
            h = _pool_layer(h, g, pool_w[j].astype(BF16), row(pool_scale[j]), batch=batch, seq=seq)
        elif kind == 1:
            h = _conv_layer(h, g, conv_w1[j].astype(BF16), row(conv_b1[j]), conv_dw[j], row(conv_bdw[j]),
                            row(conv_ln_g[j]), row(conv_ln_b[j]), conv_w2[j].astype(BF16), row(conv_b2[j]),
                            batch=batch, seq=seq)
        else:
            wqb, wkva, wkvb = _mla_weights(mla_wq_b[j], mla_wkv_a[j], mla_wkv_b[j], heads, lora)
            h = _mla_layer(h, pos, g, mla_wq_a[j].astype(BF16), row(mla_q_norm[j]), wqb, wkva,
                           row(mla_kv_norm[j]), wkvb, mla_wo[j].T.astype(BF16), invf, sgn,
                           batch=batch, seq=seq, heads=heads, lora=lora)
        g = row(norm_ffn[i])
        f_idx = i // 2
        last = i == depth - 1
        if i % 2 == 0:
            h = _ffn_layer(h, g, _to_bf16(ffn_wg, f_idx), _to_bf16(ffn_wu, f_idx), _to_bf16(ffn_wd, f_idx))
            if last:
                h = _final_norm(h, row(norm_final))
        else:
            h = _moe_layer(h, g, moe_router[f_idx], _to_bf16(moe_wg, f_idx), _to_bf16(moe_wu, f_idx),
                           _to_bf16(moe_wd, f_idx), row(norm_final), last)
    return h.reshape(batch, seq, d)


def kernel(x, positions, norm_mix, norm_ffn, norm_final, pool_w, pool_scale, conv_w1, conv_b1, conv_dw, conv_bdw, conv_ln_g, conv_ln_b, conv_w2, conv_b2, mla_wq_a, mla_q_norm, mla_wq_b, mla_wkv_a, mla_kv_norm, mla_wkv_b, mla_wo, ffn_wg, ffn_wu, ffn_wd, moe_router, moe_wg, moe_wu, moe_wd):
    return _trunk(x, positions, norm_mix, norm_ffn, norm_final, pool_w, pool_scale,
                  conv_w1, conv_b1, conv_dw, conv_bdw, conv_ln_g, conv_ln_b, conv_w2, conv_b2,
                  mla_wq_a, mla_q_norm, mla_wq_b, mla_wkv_a, mla_kv_norm, mla_wkv_b, mla_wo,
                  ffn_wg, ffn_wu, ffn_wd, moe_router, moe_wg, moe_wu, moe_wd)
```

```python
import functools

import jax
import jax.numpy as jnp
from jax import lax
from jax.experimental import pallas as pl
from jax.experimental.pallas import tpu as pltpu

F32 = jnp.float32
BF16 = jnp.bfloat16

NORM_EPS = 1e-6
LN_EPS = 1e-5
ROPE_THETA = 10000.0
LOG2E = 1.4426950408889634

POOL_WINDOWS = (2, 4, 8, 16)
POOL_HALO = 16
CONV_HALO = 32
LANES = 128
SUBLANES = 8
NOPE = 128
ROPE = 64
V_ROWS = NOPE + 16
TOP_K = 2
GATHER_UNROLL = 8
ZERO_CHUNK = 64

VMEM_LIMIT = 56 * 1024 * 1024


def _params(*sem):
    return pltpu.CompilerParams(dimension_semantics=sem, vmem_limit_bytes=VMEM_LIMIT)


def _rms(x, g, eps=NORM_EPS):
    return x * lax.rsqrt(jnp.mean(x * x, axis=-1, keepdims=True) + eps) * g


def _dot(a, b):
    return jnp.dot(a, b, preferred_element_type=F32)


def _ff_chunk(f):
    for c in (1792, 1024, 512, 256, 128):
        if f % c == 0:
            return c
    return f


def _pool_kernel(h_ref, g_ref, w_ref, sc_ref, o_ref, ubuf, *, ts, gdim):
    j = pl.program_id(1)
    h = h_ref[...]
    u = _rms(h, g_ref[...])

    @pl.when(j == 0)
    def _():
        ubuf[0:POOL_HALO, :] = jnp.zeros((POOL_HALO, ubuf.shape[1]), F32)

    ubuf[POOL_HALO:POOL_HALO + ts, :] = u
    t1 = lax.broadcasted_iota(jnp.int32, (ts, 1), 0) + (j * ts + 1)
    for g, win in enumerate(POOL_WINDOWS):
        cs = slice(g * gdim, (g + 1) * gdim)
        ug = u[:, cs]
        s = ug
        for k in range(1, win):
            s = s + ubuf[POOL_HALO - k:POOL_HALO - k + ts, cs]
        cnt = jnp.minimum(t1, win).astype(F32)
        pooled = s / cnt - ug
        m = _dot(pooled.astype(BF16), w_ref[g])
        o_ref[:, cs] = h[:, cs] + m * sc_ref[:, cs]
    ubuf[0:POOL_HALO, :] = ubuf[ts:ts + POOL_HALO, :]


def _pool_layer(h, g, w, scale, *, batch, seq):
    n, d = h.shape
    ts = min(512, seq)
    nt = seq // ts
    gdim = d // len(POOL_WINDOWS)
    row = lambda b, j: (b * nt + j, 0)
    const2 = lambda b, j: (0, 0)
    return pl.pallas_call(
        functools.partial(_pool_kernel, ts=ts, gdim=gdim),
        grid=(batch, nt),
        in_specs=[pl.BlockSpec((ts, d), row),
                  pl.BlockSpec((1, d), const2),
                  pl.BlockSpec(w.shape, lambda b, j: (0, 0, 0)),
                  pl.BlockSpec((1, d), const2)],
        out_specs=pl.BlockSpec((ts, d), row),
        out_shape=jax.ShapeDtypeStruct((n, d), F32),
        scratch_shapes=[pltpu.VMEM((ts + POOL_HALO, d), F32)],
        compiler_params=_params("arbitrary", "arbitrary"),
        name="pool_mixer",
    )(h, g, w, scale)


def _swiglu_step(u, wg, wu, wd):
    a = _dot(u, wg)
    b = _dot(u, wu)
    hh = a * jax.nn.sigmoid(a) * b
    return _dot(hh.astype(BF16), wd)


def _ffn_kernel(h_ref, g_ref, wg_ref, wu_ref, wd_ref, o_ref, u_sc, acc_sc):
    j = pl.program_id(1)

    @pl.when(j == 0)
    def _():
        u_sc[...] = _rms(h_ref[...], g_ref[...]).astype(BF16)
        acc_sc[...] = jnp.zeros(acc_sc.shape, F32)

    acc_sc[...] += _swiglu_step(u_sc[...], wg_ref[...], wu_ref[...], wd_ref[...])

    @pl.when(j == pl.num_programs(1) - 1)
    def _():
        o_ref[...] = h_ref[...] + acc_sc[...]


def _ffn_layer(h, g, wg, wu, wd):
    n, d = h.shape
    f = wg.shape[1]
    tm = min(512, n)
    tf = _ff_chunk(f)
    return pl.pallas_call(
        _ffn_kernel,
        grid=(n // tm, f // tf),
        in_specs=[pl.BlockSpec((tm, d), lambda i, j: (i, 0)),
                  pl.BlockSpec((1, d), lambda i, j: (0, 0)),
                  pl.BlockSpec((d, tf), lambda i, j: (0, j)),
                  pl.BlockSpec((d, tf), lambda i, j: (0, j)),
                  pl.BlockSpec((tf, d), lambda i, j: (j, 0))],
        out_specs=pl.BlockSpec((tm, d), lambda i, j: (i, 0)),
        out_shape=jax.ShapeDtypeStruct((n, d), F32),
        scratch_shapes=[pltpu.VMEM((tm, d), BF16), pltpu.VMEM((tm, d), F32)],
        compiler_params=_params("arbitrary", "arbitrary"),
        name="dense_swiglu",
    )(h, g, wg, wu, wd)


def _side_cast_plan(weights, layer, steps, step_index):
    arrays, in_specs, out_specs, out_shapes = [], [], [], []
    for w in weights:
        cols = w.shape[-1]
        w3 = w.reshape(w.shape[0], -1, cols)
        rows = w3.shape[1]
        br = rows // steps
        assert rows % steps == 0 and br % 16 == 0
        arrays.append(w3)
        in_specs.append(pl.BlockSpec((1, br, cols), lambda *g: (layer, step_index(*g), 0)))
        out_specs.append(pl.BlockSpec((br, cols), lambda *g: (step_index(*g), 0)))
        out_shapes.append(jax.ShapeDtypeStruct((rows, cols), BF16))
    return arrays, in_specs, out_specs, out_shapes


def _side_cast(in_refs, out_refs):
    for src, dst in zip(in_refs, out_refs):
        dst[...] = src[0].astype(BF16)


def _conv_kernel(h_ref, g_ref, w1_ref, b1_ref, dw_ref, bdw_ref, lng_ref, lnb_ref, w2_ref, b2_ref, *rest,
                 ts, width, rb, n_side):
    side_in, o_ref, side_out, (gbuf, cbuf) = rest[:n_side], rest[n_side], rest[n_side + 1:-2], rest[-2:]
    _side_cast(side_in, side_out)
    j = pl.program_id(1)
    d = h_ref.shape[1]
    h = h_ref[...]
    u = _rms(h, g_ref[...]).astype(BF16)
    a = _dot(u, w1_ref[...]) + b1_ref[...]
    glu = a[:, :d] * jax.nn.sigmoid(a[:, d:])

    keep = CONV_HALO + SUBLANES

    @pl.when(j == 0)
    def _():
        gbuf[:, 0:keep, :] = jnp.zeros((SUBLANES, keep, d), F32)

    for b in range(SUBLANES):
        gbuf[b, CONV_HALO + b:CONV_HALO + b + ts, :] = glu
    for cb in range(d // LANES):
        cs = slice(cb * LANES, (cb + 1) * LANES)
        for r0 in range(0, ts, rb):
            acc = jnp.zeros((rb, LANES), F32)
            for k in range(width):
                a, b = divmod(width - 1 - k, SUBLANES)
                off = CONV_HALO + r0 - SUBLANES * a
                acc = acc + gbuf[b, off:off + rb, cs] * dw_ref[k:k + 1, cs]
            cbuf[r0:r0 + rb, cs] = acc + bdw_ref[:, cs]
    gbuf[:, 0:keep, :] = gbuf[:, ts:ts + keep, :]

    c = cbuf[...]
    mu = jnp.mean(c, axis=-1, keepdims=True)
    cc = c - mu
    var = jnp.mean(cc * cc, axis=-1, keepdims=True)
    un = cc * lax.rsqrt(var + LN_EPS) * lng_ref[...] + lnb_ref[...]
    un = un * jax.nn.sigmoid(un)
    o_ref[...] = h + _dot(un.astype(BF16), w2_ref[...]) + b2_ref[...]


def _conv_layer(h, g, w1, b1, dw, bdw, lng, lnb, w2, b2, side_weights, side_layer, *, batch, seq):
    n, d = h.shape
    width = dw.shape[0]
    assert width - 1 <= CONV_HALO
    ts = min(256, seq)
    rb = min(128, ts)
    nt = seq // ts
    row = lambda b, j: (b * nt + j, 0)
    c2 = lambda b, j: (0, 0)
    side, side_in, side_out, side_shapes = _side_cast_plan(side_weights, side_layer, batch * nt,
                                                           lambda b, j: b * nt + j)
    outs = pl.pallas_call(
        functools.partial(_conv_kernel, ts=ts, width=width, rb=rb, n_side=len(side)),
        grid=(batch, nt),
        in_specs=[pl.BlockSpec((ts, d), row),
                  pl.BlockSpec((1, d), c2),
                  pl.BlockSpec(w1.shape, c2),
                  pl.BlockSpec(b1.shape, c2),
                  pl.BlockSpec(dw.shape, c2),
                  pl.BlockSpec((1, d), c2),
                  pl.BlockSpec((1, d), c2),
                  pl.BlockSpec((1, d), c2),
                  pl.BlockSpec(w2.shape, c2),
                  pl.BlockSpec((1, d), c2)] + side_in,
        out_specs=[pl.BlockSpec((ts, d), row)] + side_out,
        out_shape=[jax.ShapeDtypeStruct((n, d), F32)] + side_shapes,
        scratch_shapes=[pltpu.VMEM((SUBLANES, ts + CONV_HALO + SUBLANES, d), F32), pltpu.VMEM((ts, d), F32)],
        compiler_params=_params("arbitrary", "arbitrary"),
        name="conv_module",
    )(h, g, w1, b1, dw, bdw, lng, lnb, w2, b2, *side)
    return outs[0], [o.reshape(w.shape[1:]) for o, w in zip(outs[1:], side_weights)]


def _mla_proj_kernel(h_ref, pos_ref, g_ref, wqa_ref, qnorm_ref, wqb_ref, wkva_ref, kvnorm_ref, wkvb_ref,
                     invf_ref, sgn_ref, qt_o, k_o, vt_o, *, heads, lora):
    hn = heads * NOPE
    rows = h_ref.shape[0]
    u = _rms(h_ref[...], g_ref[...]).astype(BF16)
    qa = _rms(_dot(u, wqa_ref[...]), qnorm_ref[...]).astype(BF16)
    q = _dot(qa, wqb_ref[...])
    kva = _dot(u, wkva_ref[...])
    ckv = _rms(kva[:, :lora], kvnorm_ref[...]).astype(BF16)
    kv = _dot(ckv, wkvb_ref[...])
    ang = pos_ref[...].astype(F32) * invf_ref[...]
    c = jnp.cos(ang)
    s = jnp.sin(ang) * sgn_ref[...]
    kr = (kva[:, lora:lora + LANES] * c + kva[:, lora + LANES:lora + 2 * LANES] * s).astype(BF16)
    ones = jnp.ones((V_ROWS - NOPE, rows), BF16)
    for hd in range(heads):
        ns = slice(hd * NOPE, (hd + 1) * NOPE)
        qr = q[:, hn + hd * LANES:hn + (hd + 1) * LANES] * c + q[:, 2 * hn + hd * LANES:2 * hn + (hd + 1) * LANES] * s
        qt_o[2 * hd * NOPE:(2 * hd + 1) * NOPE, :] = q[:, ns].T.astype(BF16)
        qt_o[(2 * hd + 1) * NOPE:(2 * hd + 2) * NOPE, :] = qr.T.astype(BF16)
        k_o[:, 2 * hd * NOPE:(2 * hd + 1) * NOPE] = kv[:, ns].astype(BF16)
        k_o[:, (2 * hd + 1) * NOPE:(2 * hd + 2) * NOPE] = kr
        vt_o[hd * V_ROWS:hd * V_ROWS + NOPE, :] = kv[:, hn + hd * NOPE:hn + (hd + 1) * NOPE].T.astype(BF16)
        vt_o[hd * V_ROWS + NOPE:(hd + 1) * V_ROWS, :] = ones


def _attn_kernel(qt_ref, k_ref, vt_ref, h_ref, wot_ref, *rest, heads, tq, tk, scale, n_side):
    side_in, o_ref, side_out, (m_sc, acc_sc) = rest[:n_side], rest[n_side], rest[n_side + 1:-2], rest[-2:]
    _side_cast(side_in, side_out)
    i = pl.program_id(1)
    j = pl.program_id(2)
    c = scale * LOG2E

    @pl.when(j == 0)
    def _():
        m_sc[...] = jnp.full(m_sc.shape, -jnp.inf, F32)
        acc_sc[...] = jnp.zeros(acc_sc.shape, F32)

    def step(diagonal):
        if diagonal:
            causal = (lax.broadcasted_iota(jnp.int32, (tk, tq), 0)
                      <= lax.broadcasted_iota(jnp.int32, (tk, tq), 1))
        es = [slice(2 * hd * NOPE, (2 * hd + 2) * NOPE) for hd in range(heads)]
        vs = [slice(hd * V_ROWS, (hd + 1) * V_ROWS) for hd in range(heads)]
        ss = [_dot(k_ref[:, e], qt_ref[e, :]) for e in es]
        if diagonal:
            ss = [jnp.where(causal, s, -jnp.inf) for s in ss]
        m_prev = [m_sc[hd] for hd in range(heads)]
        m_new = [jnp.maximum(mp, jnp.max(s, axis=0, keepdims=True)) for mp, s in zip(m_prev, ss)]
        ps = [jnp.exp2((s - mn) * c).astype(BF16) for s, mn in zip(ss, m_new)]
        pv = [_dot(vt_ref[v, :], p) for p, v in zip(ps, vs)]
        for hd in range(heads):
            alpha = jnp.exp2((m_prev[hd] - m_new[hd]) * c)
            acc_sc[vs[hd], :] = alpha * acc_sc[vs[hd], :] + pv[hd]
            m_sc[hd] = m_new[hd]

    @pl.when(j < i)
    def _():
        step(False)

    @pl.when(j == i)
    def _():
        step(True)
        parts = [acc_sc[hd * V_ROWS:hd * V_ROWS + NOPE, :] / acc_sc[hd * V_ROWS + NOPE:hd * V_ROWS + NOPE + 1, :]
                 for hd in range(heads)]
        ot = jnp.concatenate(parts, axis=0).astype(BF16)
        o_ref[...] = h_ref[...] + _dot(wot_ref[...], ot).T


def _mla_layer(h, positions, g, wqa, qnorm, wqb, wkva, kvnorm, wkvb, wot, invf, sgn, side_weights, side_layer,
               *, batch, seq, heads, lora):
    n, d = h.shape
    hn = heads * NOPE
    tp = min(256, n)
    c2 = lambda i: (0, 0)
    row = lambda i: (i, 0)
    col = lambda i: (0, i)
    qt, k, vt = pl.pallas_call(
        functools.partial(_mla_proj_kernel, heads=heads, lora=lora),
        grid=(n // tp,),
        in_specs=[pl.BlockSpec((tp, d), row),
                  pl.BlockSpec((tp, 1), row),
                  pl.BlockSpec((1, d), c2),
                  pl.BlockSpec(wqa.shape, c2),
                  pl.BlockSpec(qnorm.shape, c2),
                  pl.BlockSpec(wqb.shape, c2),
                  pl.BlockSpec(wkva.shape, c2),
                  pl.BlockSpec(kvnorm.shape, c2),
                  pl.BlockSpec(wkvb.shape, c2),
                  pl.BlockSpec((1, LANES), c2),
                  pl.BlockSpec((1, LANES), c2)],
        out_specs=[pl.BlockSpec((2 * hn, tp), col), pl.BlockSpec((tp, 2 * hn), row),
                   pl.BlockSpec((heads * V_ROWS, tp), col)],
        out_shape=[jax.ShapeDtypeStruct((2 * hn, n), BF16), jax.ShapeDtypeStruct((n, 2 * hn), BF16),
                   jax.ShapeDtypeStruct((heads * V_ROWS, n), BF16)],
        compiler_params=_params("arbitrary"),
        name="mla_proj",
    )(h, positions, g, wqa, qnorm, wqb, wkva, kvnorm, wkvb, invf, sgn)

    tq = tk = min(512, seq)
    nq = seq // tq
    scale = float(NOPE + ROPE) ** -0.5
    qrow = lambda b, i, j: (b * nq + i, 0)
    qcol = lambda b, i, j: (0, b * nq + i)
    krow = lambda b, i, j: (b * nq + jnp.minimum(j, i), 0)
    kcol = lambda b, i, j: (0, b * nq + jnp.minimum(j, i))
    side, side_in, side_out, side_shapes = _side_cast_plan(side_weights, side_layer, batch * nq * nq,
                                                           lambda b, i, j: (b * nq + i) * nq + j)
    outs = pl.pallas_call(
        functools.partial(_attn_kernel, heads=heads, tq=tq, tk=tk, scale=scale, n_side=len(side)),
        grid=(batch, nq, nq),
        in_specs=[pl.BlockSpec((2 * hn, tq), qcol),
                  pl.BlockSpec((tk, 2 * hn), krow),
                  pl.BlockSpec((heads * V_ROWS, tk), kcol),
                  pl.BlockSpec((tq, d), qrow),
                  pl.BlockSpec(wot.shape, lambda b, i, j: (0, 0))] + side_in,
        out_specs=[pl.BlockSpec((tq, d), qrow)] + side_out,
        out_shape=[jax.ShapeDtypeStruct((n, d), F32)] + side_shapes,
        scratch_shapes=[pltpu.VMEM((heads, 1, tq), F32), pltpu.VMEM((heads * V_ROWS, tq), F32)],
        compiler_params=_params("arbitrary", "arbitrary", "arbitrary"),
        name="mla_attention",
    )(qt, k, vt, h, wot, *side)
    return outs[0], [o.reshape(w.shape[1:]) for o, w in zip(outs[1:], side_weights)]


def _store_row_tiles(ref, value, rows):
    slabs = value.shape[1] // LANES
    for s in range(slabs):
        ref[pl.ds(s, rows, stride=slabs), :] = value[:, s * LANES:(s + 1) * LANES]


def _load_row_tile_slab(ref, s, rows, slabs):
    return ref[pl.ds(s, rows, stride=slabs), :]


def _route_kernel(h_ref, g_ref, r_ref, u_ref, info_ref, cnt_ref, carry, *, tr, n_exp):
    i = pl.program_id(0)

    @pl.when(i == 0)
    def _():
        carry[...] = jnp.zeros(carry.shape, F32)

    u = _rms(h_ref[...], g_ref[...])
    _store_row_tiles(u_ref, u, tr)
    logits = jnp.dot(u, r_ref[...], preferred_element_type=F32, precision=lax.Precision.HIGHEST)
    lane = lax.broadcasted_iota(jnp.int32, (tr, LANES), 1)
    lg = jnp.where(lane < n_exp, logits, -jnp.inf)
    m1 = jnp.max(lg, axis=1, keepdims=True)
    i1 = jnp.min(jnp.where(lg == m1, lane, LANES), axis=1, keepdims=True)
    lg2 = jnp.where(lane == i1, -jnp.inf, lg)
    m2 = jnp.max(lg2, axis=1, keepdims=True)
    i2 = jnp.min(jnp.where(lg2 == m2, lane, LANES), axis=1, keepdims=True)
    e = jnp.exp(m2 - m1)
    g1 = 1.0 / (1.0 + e)
    g2 = e / (1.0 + e)
    sel1 = lane == i1
    sel2 = lane == i2
    onehot = jnp.where(sel1 | sel2, 1.0, 0.0)
    rr = lax.broadcasted_iota(jnp.int32, (tr, tr), 0)
    cc = lax.broadcasted_iota(jnp.int32, (tr, tr), 1)
    tri = jnp.where(cc < rr, 1.0, 0.0).astype(BF16)
    before = _dot(tri, onehot.astype(BF16)) + carry[0:1, :]
    rank1 = jnp.sum(jnp.where(sel1, before, 0.0), axis=1, keepdims=True)
    rank2 = jnp.sum(jnp.where(sel2, before, 0.0), axis=1, keepdims=True)
    carry[...] = carry[...] + jnp.sum(onehot, axis=0, keepdims=True)
    cols = (i1.astype(F32), i2.astype(F32), g1, g2, rank1, rank2)
    info = jnp.zeros((tr, LANES), F32)
    for k, col in enumerate(cols):
        info = jnp.where(lane == k, col, info)
    info_ref[...] = info
    cnt_ref[...] = carry[...]


def _route(h, g, router_padded, n_exp):
    n, d = h.shape
    tr = min(512, n)
    slabs = d // LANES
    return pl.pallas_call(
        functools.partial(_route_kernel, tr=tr, n_exp=n_exp),
        grid=(n // tr,),
        in_specs=[pl.BlockSpec((tr, d), lambda i: (i, 0)),
                  pl.BlockSpec((1, d), lambda i: (0, 0)),
                  pl.BlockSpec((d, LANES), lambda i: (0, 0))],
        out_specs=[pl.BlockSpec((tr * slabs, LANES), lambda i: (i, 0)),
                   pl.BlockSpec((tr, LANES), lambda i: (i, 0)),
                   pl.BlockSpec((8, LANES), lambda i: (0, 0))],
        out_shape=[jax.ShapeDtypeStruct((n * slabs, LANES), F32),
                   jax.ShapeDtypeStruct((n, LANES), F32),
                   jax.ShapeDtypeStruct((8, LANES), F32)],
        scratch_shapes=[pltpu.VMEM((8, LANES), F32)],
        compiler_params=_params("arbitrary"),
        name="moe_route",
    )(h, g, router_padded)


def _gather_row_tiles(idx_ref, k, src_ref, dst_ref, sem, rows, slabs):
    def start8(r8, c):
        for q in range(GATHER_UNROLL):
            r = r8 * GATHER_UNROLL + q
            src_row = pl.multiple_of(idx_ref[0, k, r] * slabs, slabs)
            dst_row = pl.multiple_of(r * slabs, slabs)
            pltpu.make_async_copy(src_ref.at[pl.ds(src_row, slabs)], dst_ref.at[pl.ds(dst_row, slabs)],
                                  sem).start(priority=q % 2)
        return c

    lax.fori_loop(0, rows // GATHER_UNROLL, start8, 0)
    pltpu.make_async_copy(src_ref.at[pl.ds(0, rows * slabs)], dst_ref, sem).wait()


def _dispatch_kernel(pad_start_ref, pad_cnt_ref, pos_ref, u_ref, x_ref, zero_sc, sem, *, rows, slabs, n_exp):
    def row_copy(src, src_row, dst_row):
        return pltpu.make_async_copy(src.at[pl.ds(pl.multiple_of(src_row * slabs, slabs), slabs)],
                                     x_ref.at[pl.ds(pl.multiple_of(dst_row * slabs, slabs), slabs)], sem)

    @pl.when(pl.program_id(0) == 0)
    def _():
        zero_sc[...] = jnp.zeros(zero_sc.shape, F32)
        for e in range(n_exp):
            def fill(i, c):
                row_copy(zero_sc, 0, pad_start_ref[e] + i).start()
                return c

            def drain(i, c):
                row_copy(zero_sc, 0, pad_start_ref[e] + i).wait()
                return c

            lax.fori_loop(0, pad_cnt_ref[e], fill, 0)
            lax.fori_loop(0, pad_cnt_ref[e], drain, 0)

        def chunk_copy(i):
            dst = pl.multiple_of((pad_start_ref[n_exp] + i * ZERO_CHUNK) * slabs, slabs)
            return pltpu.make_async_copy(zero_sc, x_ref.at[pl.ds(dst, ZERO_CHUNK * slabs)], sem)

        def fill_tail(i, c):
            chunk_copy(i).start()
            return c

        def drain_tail(i, c):
            chunk_copy(i).wait()
            return c

        lax.fori_loop(0, pad_cnt_ref[n_exp], fill_tail, 0)
        lax.fori_loop(0, pad_cnt_ref[n_exp], drain_tail, 0)

    def start8(r8, c):
        for q in range(GATHER_UNROLL):
            r = r8 * GATHER_UNROLL + q
            for k in range(TOP_K):
                row_copy(u_ref, r, pos_ref[0, k, r]).start(priority=k)
        return c

    lax.fori_loop(0, rows // GATHER_UNROLL, start8, 0)
    for k in range(TOP_K):
        pltpu.make_async_copy(u_ref, x_ref.at[pl.ds(0, rows * slabs)], sem).wait()


def _dispatch(u, pos_t, pad_start, pad_cnt, sorted_rows, tile, slabs):
    n = u.shape[0] // slabs
    grid_spec = pltpu.PrefetchScalarGridSpec(
        num_scalar_prefetch=2,
        grid=(n // tile,),
        in_specs=[pl.BlockSpec((1, TOP_K, tile), lambda i, ps, pc: (i, 0, 0), memory_space=pltpu.SMEM),
                  pl.BlockSpec((tile * slabs, LANES), lambda i, ps, pc: (i, 0))],
        out_specs=pl.BlockSpec(memory_space=pl.ANY),
        scratch_shapes=[pltpu.VMEM((ZERO_CHUNK * slabs, LANES), F32), pltpu.SemaphoreType.DMA(())],
    )
    return pl.pallas_call(
        functools.partial(_dispatch_kernel, rows=tile, slabs=slabs, n_exp=pad_start.shape[0] - 1),
        grid_spec=grid_spec,
        out_shape=jax.ShapeDtypeStruct((sorted_rows * slabs, LANES), F32),
        compiler_params=_params("arbitrary"),
        name="moe_dispatch",
    )(pad_start, pad_cnt, pos_t, u)


def _moe_kernel(te_ref, nu_ref, x_ref, wg_ref, wu_ref, wd_ref, o_ref, x_sc, acc_sc, *, tm, slabs):
    t = pl.program_id(0)
    j = pl.program_id(1)

    @pl.when(t < nu_ref[0])
    def _():
        @pl.when(j == 0)
        def _():
            for s in range(slabs):
                x_sc[:, s * LANES:(s + 1) * LANES] = _load_row_tile_slab(x_ref, s, tm, slabs).astype(BF16)
            acc_sc[...] = jnp.zeros(acc_sc.shape, F32)

        acc_sc[...] += _swiglu_step(x_sc[...], wg_ref[0], wu_ref[0], wd_ref[0])

        @pl.when(j == pl.num_programs(1) - 1)
        def _():
            _store_row_tiles(o_ref, acc_sc[...], tm)

    @pl.when((t >= nu_ref[0]) & (j == 0))
    def _():
        o_ref[...] = jnp.zeros(o_ref.shape, F32)


def _moe_experts(x_sorted, tile_expert, n_used, wg, wu, wd, tm, slabs):
    d = slabs * LANES
    rows = x_sorted.shape[0] // slabs
    f = wg.shape[2]
    tf = _ff_chunk(f)
    nj = f // tf
    tmap = lambda t, j, te, nu: (jnp.minimum(t, nu[0] - 1), 0)
    jsel = lambda t, j, nu: jnp.where(t < nu[0], j, nj - 1)
    grid_spec = pltpu.PrefetchScalarGridSpec(
        num_scalar_prefetch=2,
        grid=(rows // tm, nj),
        in_specs=[pl.BlockSpec((tm * slabs, LANES), tmap),
                  pl.BlockSpec((1, d, tf), lambda t, j, te, nu: (te[t], 0, jsel(t, j, nu))),
                  pl.BlockSpec((1, d, tf), lambda t, j, te, nu: (te[t], 0, jsel(t, j, nu))),
                  pl.BlockSpec((1, tf, d), lambda t, j, te, nu: (te[t], jsel(t, j, nu), 0))],
        out_specs=pl.BlockSpec((tm * slabs, LANES), lambda t, j, te, nu: (t, 0)),
        scratch_shapes=[pltpu.VMEM((tm, d), BF16), pltpu.VMEM((tm, d), F32)],
    )
    return pl.pallas_call(
        functools.partial(_moe_kernel, tm=tm, slabs=slabs),
        grid_spec=grid_spec,
        out_shape=jax.ShapeDtypeStruct((rows * slabs, LANES), F32),
        compiler_params=_params("arbitrary", "arbitrary"),
        name="moe_experts",
    )(tile_expert, n_used, x_sorted, wg, wu, wd)


def _combine_kernel(pos_ref, h_ref, gate_ref, gf_ref, y_ref, o_ref, y0, y1, sem, *, rows, slabs, final_norm):
    _gather_row_tiles(pos_ref, 0, y_ref, y0, sem.at[0], rows, slabs)
    _gather_row_tiles(pos_ref, 1, y_ref, y1, sem.at[1], rows, slabs)
    gates = gate_ref[...]
    g0 = gates[:, 2:3]
    g1 = gates[:, 3:4]
    ssq = jnp.zeros((rows, 1), F32)
    for s in range(slabs):
        cs = slice(s * LANES, (s + 1) * LANES)
        out = (h_ref[:, cs] + g0 * _load_row_tile_slab(y0, s, rows, slabs)
               + g1 * _load_row_tile_slab(y1, s, rows, slabs))
        o_ref[:, cs] = out
        ssq = ssq + jnp.sum(out * out, axis=-1, keepdims=True)
    if final_norm:
        d = slabs * LANES
        o_ref[...] = o_ref[...] * lax.rsqrt(ssq / d + NORM_EPS) * gf_ref[...]


def _combine(h, info, pos_t, y_sorted, g_final, final_norm, slabs):
    n, d = h.shape
    tile = pos_t.shape[2]
    return pl.pallas_call(
        functools.partial(_combine_kernel, rows=tile, slabs=slabs, final_norm=final_norm),
        grid=(n // tile,),
        in_specs=[pl.BlockSpec((1, TOP_K, tile), lambda i: (i, 0, 0), memory_space=pltpu.SMEM),
                  pl.BlockSpec((tile, d), lambda i: (i, 0)),
                  pl.BlockSpec((tile, LANES), lambda i: (i, 0)),
                  pl.BlockSpec((1, d), lambda i: (0, 0)),
                  pl.BlockSpec(memory_space=pl.ANY)],
        out_specs=pl.BlockSpec((tile, d), lambda i: (i, 0)),
        out_shape=jax.ShapeDtypeStruct((n, d), F32),
        scratch_shapes=[pltpu.VMEM((tile * slabs, LANES), F32), pltpu.VMEM((tile * slabs, LANES), F32),
                        pltpu.SemaphoreType.DMA((2,))],
        compiler_params=_params("arbitrary"),
        name="moe_combine",
    )(pos_t, h, info, g_final, y_sorted)


def _moe_layer(h, g, router, wg, wu, wd, g_final, final_norm):
    n, d = h.shape
    n_exp = router.shape[1]
    tm = min(512, n)
    router_padded = jnp.pad(router, ((0, 0), (0, LANES - n_exp)))
    u, info, cnt = _route(h, g, router_padded, n_exp)

    idx = info[:, 0:TOP_K].astype(jnp.int32)
    rank = info[:, 4:4 + TOP_K].astype(jnp.int32)
    counts = cnt[0, :n_exp].astype(jnp.int32)
    padded = (counts + tm - 1) // tm * tm
    ends = jnp.cumsum(padded)
    starts = ends - padded
    pos = jnp.sum(jnp.where(idx[..., None] == jnp.arange(n_exp), starts, 0), axis=-1) + rank
    rows = TOP_K * n + n_exp * tm
    n_used = (ends[-1] // tm).astype(jnp.int32)
    tile_ids = jnp.minimum(jnp.arange(rows // tm, dtype=jnp.int32), n_used - 1)
    tile_expert = jnp.sum(tile_ids[:, None] * tm >= ends[None, :], axis=1).astype(jnp.int32)

    slabs = d // LANES
    assert slabs == 8
    tile = min(512, n)
    pos_t = pos.T.reshape(TOP_K, n // tile, tile).transpose(1, 0, 2)
    assert tm % ZERO_CHUNK == 0
    pad_start = jnp.concatenate([starts + counts, ends[-1:]]).astype(jnp.int32)
    pad_cnt = jnp.concatenate([padded - counts, (rows - ends[-1:]) // ZERO_CHUNK]).astype(jnp.int32)
    x_sorted = _dispatch(u, pos_t, pad_start, pad_cnt, rows, tile, slabs)
    y_sorted = _moe_experts(x_sorted, tile_expert, n_used.reshape(1), wg, wu, wd, tm, slabs)
    return _combine(h, info, pos_t, y_sorted, g_final, final_norm, slabs)


def _mla_weights(wq_b, wkv_a, wkv_b, heads, lora):
    def padded_rope(w):
        pad = jnp.zeros(w.shape[:-1] + (LANES - ROPE,), w.dtype)
        swapped = jnp.concatenate([w[..., ROPE // 2:], w[..., :ROPE // 2]], axis=-1)
        return jnp.concatenate([w, pad], axis=-1), jnp.concatenate([swapped, pad], axis=-1)

    rq = wq_b.shape[0]
    wq = wq_b.reshape(rq, heads, NOPE + ROPE)
    q_r, q_rs = padded_rope(wq[..., NOPE:])
    wqb = jnp.concatenate([wq[..., :NOPE].reshape(rq, -1), q_r.reshape(rq, -1), q_rs.reshape(rq, -1)], axis=1)
    k_r, k_rs = padded_rope(wkv_a[:, lora:])
    wkva = jnp.concatenate([wkv_a[:, :lora], k_r, k_rs], axis=1)
    wkv = wkv_b.reshape(lora, heads, 2 * NOPE)
    wkvb = jnp.concatenate([wkv[..., :NOPE].reshape(lora, -1), wkv[..., NOPE:].reshape(lora, -1)], axis=1)
    return wqb.astype(BF16), wkva.astype(BF16), wkvb.astype(BF16)


def _cast_kernel(x_ref, o_ref):
    o_ref[...] = x_ref[0].astype(BF16)


def _to_bf16(w, layer):
    cols = w.shape[-1]
    w3 = w.reshape(w.shape[0], -1, cols)
    rows = w3.shape[1]
    cap = 256 if cols > 2048 else 1024
    tr = next(t for t in (1024, 512, 256, 128, 64, 32, 16, 8) if t <= cap and rows % t == 0)
    out = pl.pallas_call(
        _cast_kernel,
        grid=(rows // tr,),
        in_specs=[pl.BlockSpec((1, tr, cols), lambda i: (layer, i, 0))],
        out_specs=pl.BlockSpec((tr, cols), lambda i: (i, 0)),
        out_shape=jax.ShapeDtypeStruct((rows, cols), BF16),
        compiler_params=_params("arbitrary"),
        name="weights_to_bf16",
    )(w3)
    return out.reshape(w.shape[1:])


def _final_norm_kernel(h_ref, g_ref, o_ref):
    o_ref[...] = _rms(h_ref[...], g_ref[...])


def _final_norm(h, g):
    n, d = h.shape
    tm = min(512, n)
    return pl.pallas_call(
        _final_norm_kernel,
        grid=(n // tm,),
        in_specs=[pl.BlockSpec((tm, d), lambda i: (i, 0)), pl.BlockSpec((1, d), lambda i: (0, 0))],
        out_specs=pl.BlockSpec((tm, d), lambda i: (i, 0)),
        out_shape=jax.ShapeDtypeStruct((n, d), F32),
        compiler_params=_params("arbitrary"),
        name="final_norm",
    )(h, g)


@jax.jit
def _trunk(x, positions, norm_mix, norm_ffn, norm_final, pool_w, pool_scale,
           conv_w1, conv_b1, conv_dw, conv_bdw, conv_ln_g, conv_ln_b, conv_w2, conv_b2,
           mla_wq_a, mla_q_norm, mla_wq_b, mla_wkv_a, mla_kv_norm, mla_wkv_b, mla_wo,
           ffn_wg, ffn_wu, ffn_wd, moe_router, moe_wg, moe_wu, moe_wd):
    batch, seq, d = x.shape
    depth = norm_mix.shape[0]
    lora = mla_kv_norm.shape[1]
    heads = mla_wo.shape[1] // NOPE
    assert mla_wq_b.shape[2] == heads * (NOPE + ROPE) and mla_wkv_a.shape[2] == lora + ROPE
    n = batch * seq
    h = x.reshape(n, d)
    pos = positions.reshape(n, 1)
    row = lambda v: v.reshape(1, -1)

    half = jnp.arange(0, ROPE, 2, dtype=F32) / ROPE
    inv_freq = ROPE_THETA ** (-half)
    invf = row(jnp.concatenate([inv_freq, inv_freq, jnp.zeros((LANES - ROPE,), F32)]))
    sgn = row(jnp.concatenate([-jnp.ones((ROPE // 2,), F32), jnp.ones((ROPE // 2,), F32),
                               jnp.zeros((LANES - ROPE,), F32)]))

    moe_w = (moe_wg, moe_wu, moe_wd)
    moe_bf16 = {}

    def moe_cast_job(i):
        nxt = i if i % 2 == 1 else i + 1
        if nxt < depth and nxt // 2 not in moe_bf16:
            return moe_w, nxt // 2
        return (), 0

    for i in range(depth):
        kind, j = i % 3, i // 3
        g = row(norm_mix[i])
        if kind == 0:
            h = _pool_layer(h, g, pool_w[j].astype(BF16), row(pool_scale[j]), batch=batch, seq=seq)
        elif kind == 1:
            side_w, side_layer = moe_cast_job(i)
            h, cast = _conv_layer(h, g, conv_w1[j].astype(BF16), row(conv_b1[j]), conv_dw[j], row(conv_bdw[j]),
                                  row(conv_ln_g[j]), row(conv_ln_b[j]), conv_w2[j].astype(BF16), row(conv_b2[j]),
                                  side_w, side_layer, batch=batch, seq=seq)
            if cast:
                moe_bf16[side_layer] = cast
        else:
            wqb, wkva, wkvb = _mla_weights(mla_wq_b[j], mla_wkv_a[j], mla_wkv_b[j], heads, lora)
            side_w, side_layer = moe_cast_job(i)
            h, cast = _mla_layer(h, pos, g, mla_wq_a[j].astype(BF16), row(mla_q_norm[j]), wqb, wkva,
                                 row(mla_kv_norm[j]), wkvb, mla_wo[j].T.astype(BF16), invf, sgn,
                                 side_w, side_layer, batch=batch, seq=seq, heads=heads, lora=lora)
            if cast:
                moe_bf16[side_layer] = cast
        g = row(norm_ffn[i])
        f_idx = i // 2
        last = i == depth - 1
        if i % 2 == 0:
            h = _ffn_layer(h, g, _to_bf16(ffn_wg, f_idx), _to_bf16(ffn_wu, f_idx), _to_bf16(ffn_wd, f_idx))
            if last:
                h = _final_norm(h, row(norm_final))
        else:
            wg, wu, wd = moe_bf16.get(f_idx) or [_to_bf16(w, f_idx) for w in moe_w]
            h = _moe_layer(h, g, moe_router[f_idx], wg, wu, wd, row(norm_final), last)
    return h.reshape(batch, seq, d)


def kernel(x, positions, norm_mix, norm_ffn, norm_final, pool_w, pool_scale, conv_w1, conv_b1, conv_dw, conv_bdw, conv_ln_g, conv_ln_b, conv_w2, conv_b2, mla_wq_a, mla_q_norm, mla_wq_b, mla_wkv_a, mla_kv_norm, mla_wkv_b, mla_wo, ffn_wg, ffn_wu, ffn_wd, moe_router, moe_wg, moe_wu, moe_wd):
    return _trunk(x, positions, norm_mix, norm_ffn, norm_final, pool_w, pool_scale,
                  conv_w1, conv_b1, conv_dw, conv_bdw, conv_ln_g, conv_ln_b, conv_w2, conv_b2,
                  mla_wq_a, mla_q_norm, mla_wq_b, mla_wkv_a, mla_kv_norm, mla_wkv_b, mla_wo,
                  ffn_wg, ffn_wu, ffn_wd, moe_router, moe_wg, moe_wu, moe_wd)
```

```python
import functools

import jax
import jax.numpy as jnp
from jax import lax
from jax.experimental import pallas as pl
from jax.experimental.pallas import tpu as pltpu

F32 = jnp.float32
BF16 = jnp.bfloat16

NORM_EPS = 1e-6
LN_EPS = 1e-5
ROPE_THETA = 10000.0
LOG2E = 1.4426950408889634

POOL_WINDOWS = (2, 4, 8, 16)
POOL_HALO = 16
CONV_HALO = 32
LANES = 128
SUBLANES = 8
POOL_TOP = POOL_HALO + SUBLANES
NOPE = 128
ROPE = 64
V_ROWS = NOPE + 16
TOP_K = 2
GATHER_UNROLL = 8
ZERO_CHUNK = 64

VMEM_LIMIT = 56 * 1024 * 1024


def _params(*sem):
    return pltpu.CompilerParams(dimension_semantics=sem, vmem_limit_bytes=VMEM_LIMIT)


def _rms(x, g, eps=NORM_EPS):
    return x * lax.rsqrt(jnp.mean(x * x, axis=-1, keepdims=True) + eps) * g


def _dot(a, b):
    return jnp.dot(a, b, preferred_element_type=F32)


def _ff_chunk(f):
    for c in (1792, 1024, 512, 256, 128):
        if f % c == 0:
            return c
    return f


def _pool_kernel(h_ref, g_ref, w_ref, sc_ref, o_ref, ubuf, lvl_a, lvl_b, *, ts, gdim):
    j = pl.program_id(1)
    h = h_ref[...]
    u = _rms(h, g_ref[...])
    ext = POOL_TOP + ts

    @pl.when(j == 0)
    def _():
        ubuf[0:POOL_TOP, :] = jnp.zeros((POOL_TOP, ubuf.shape[1]), F32)
        lvl_a[0:SUBLANES, :] = jnp.zeros((SUBLANES, gdim), F32)
        lvl_b[0:SUBLANES, :] = jnp.zeros((SUBLANES, gdim), F32)

    ubuf[POOL_TOP:ext, :] = u
    t1 = lax.broadcasted_iota(jnp.int32, (ts, 1), 0) + (j * ts + 1)
    for g, win in enumerate(POOL_WINDOWS):
        cs = slice(g * gdim, (g + 1) * gdim)
        src, cols, spare, k = ubuf, cs, [lvl_a, lvl_b], 1
        while k < win:
            s = src[SUBLANES:ext, cols] + src[SUBLANES - k:ext - k, cols]
            k *= 2
            if k < win:
                dst = spare.pop(0)
                dst[SUBLANES:ext, :] = s
                spare.append(dst)
                src, cols = dst, slice(None)
        s = s[POOL_HALO:POOL_HALO + ts]
        ug = u[:, cs]
        cnt = jnp.minimum(t1, win).astype(F32)
        pooled = s / cnt - ug
        m = _dot(pooled.astype(BF16), w_ref[g])
        o_ref[:, cs] = h[:, cs] + m * sc_ref[:, cs]
    ubuf[SUBLANES:POOL_TOP, :] = ubuf[ts + SUBLANES:ts + POOL_TOP, :]


def _pool_layer(h, g, w, scale, *, batch, seq):
    n, d = h.shape
    ts = min(512, seq)
    nt = seq // ts
    gdim = d // len(POOL_WINDOWS)
    row = lambda b, j: (b * nt + j, 0)
    const2 = lambda b, j: (0, 0)
    return pl.pallas_call(
        functools.partial(_pool_kernel, ts=ts, gdim=gdim),
        grid=(batch, nt),
        in_specs=[pl.BlockSpec((ts, d), row),
                  pl.BlockSpec((1, d), const2),
                  pl.BlockSpec(w.shape, lambda b, j: (0, 0, 0)),
                  pl.BlockSpec((1, d), const2)],
        out_specs=pl.BlockSpec((ts, d), row),
        out_shape=jax.ShapeDtypeStruct((n, d), F32),
        scratch_shapes=[pltpu.VMEM((ts + POOL_TOP, d), F32), pltpu.VMEM((ts + POOL_TOP, gdim), F32),
                        pltpu.VMEM((ts + POOL_TOP, gdim), F32)],
        compiler_params=_params("arbitrary", "arbitrary"),
        name="pool_mixer",
    )(h, g, w, scale)


def _swiglu_step(u, wg, wu, wd):
    a = _dot(u, wg)
    b = _dot(u, wu)
    hh = a * jax.nn.sigmoid(a) * b
    return _dot(hh.astype(BF16), wd)


def _ffn_kernel(h_ref, g_ref, wg_ref, wu_ref, wd_ref, o_ref, u_sc, acc_sc):
    j = pl.program_id(1)

    @pl.when(j == 0)
    def _():
        u_sc[...] = _rms(h_ref[...], g_ref[...]).astype(BF16)
        acc_sc[...] = jnp.zeros(acc_sc.shape, F32)

    acc_sc[...] += _swiglu_step(u_sc[...], wg_ref[...], wu_ref[...], wd_ref[...])

    @pl.when(j == pl.num_programs(1) - 1)
    def _():
        o_ref[...] = h_ref[...] + acc_sc[...]


def _ffn_layer(h, g, wg, wu, wd):
    n, d = h.shape
    f = wg.shape[1]
    tm = min(512, n)
    tf = _ff_chunk(f)
    return pl.pallas_call(
        _ffn_kernel,
        grid=(n // tm, f // tf),
        in_specs=[pl.BlockSpec((tm, d), lambda i, j: (i, 0)),
                  pl.BlockSpec((1, d), lambda i, j: (0, 0)),
                  pl.BlockSpec((d, tf), lambda i, j: (0, j)),
                  pl.BlockSpec((d, tf), lambda i, j: (0, j)),
                  pl.BlockSpec((tf, d), lambda i, j: (j, 0))],
        out_specs=pl.BlockSpec((tm, d), lambda i, j: (i, 0)),
        out_shape=jax.ShapeDtypeStruct((n, d), F32),
        scratch_shapes=[pltpu.VMEM((tm, d), BF16), pltpu.VMEM((tm, d), F32)],
        compiler_params=_params("arbitrary", "arbitrary"),
        name="dense_swiglu",
    )(h, g, wg, wu, wd)


def _side_cast_plan(weights, layer, steps, step_index):
    arrays = [w.reshape(w.shape[0], -1, w.shape[-1]) for w in weights]
    if not arrays:
        return [], [], [], []
    nblk = next(n for n in range(steps, 0, -1)
                if all(a.shape[1] % n == 0 and (a.shape[1] // n) % 16 == 0 for a in arrays))
    blk = lambda *g: jnp.minimum(step_index(*g), nblk - 1)
    in_specs, out_specs, out_shapes = [], [], []
    for a in arrays:
        rows, cols = a.shape[1:]
        in_specs.append(pl.BlockSpec((1, rows // nblk, cols), lambda *g: (layer, blk(*g), 0)))
        out_specs.append(pl.BlockSpec((rows // nblk, cols), lambda *g: (blk(*g), 0)))
        out_shapes.append(jax.ShapeDtypeStruct((rows, cols), BF16))
    return arrays, in_specs, out_specs, out_shapes


def _side_cast(in_refs, out_refs):
    for src, dst in zip(in_refs, out_refs):
        dst[...] = src[0].astype(BF16)


def _conv_kernel(h_ref, g_ref, w1_ref, b1_ref, dw_ref, bdw_ref, lng_ref, lnb_ref, w2_ref, b2_ref, *rest,
                 ts, width, rb, n_side):
    side_in, o_ref, side_out, (gbuf, cbuf) = rest[:n_side], rest[n_side], rest[n_side + 1:-2], rest[-2:]
    _side_cast(side_in, side_out)
    j = pl.program_id(1)
    d = h_ref.shape[1]
    h = h_ref[...]
    u = _rms(h, g_ref[...]).astype(BF16)
    a = _dot(u, w1_ref[...]) + b1_ref[...]
    glu = a[:, :d] * jax.nn.sigmoid(a[:, d:])

    keep = CONV_HALO + SUBLANES

    @pl.when(j == 0)
    def _():
        gbuf[:, 0:keep, :] = jnp.zeros((SUBLANES, keep, d), F32)

    for b in range(SUBLANES):
        gbuf[b, CONV_HALO + b:CONV_HALO + b + ts, :] = glu
    for cb in range(d // LANES):
        cs = slice(cb * LANES, (cb + 1) * LANES)
        for r0 in range(0, ts, rb):
            acc = jnp.zeros((rb, LANES), F32)
            for k in range(width):
                a, b = divmod(width - 1 - k, SUBLANES)
                off = CONV_HALO + r0 - SUBLANES * a
                acc = acc + gbuf[b, off:off + rb, cs] * dw_ref[k:k + 1, cs]
            cbuf[r0:r0 + rb, cs] = acc + bdw_ref[:, cs]
    gbuf[:, 0:keep, :] = gbuf[:, ts:ts + keep, :]

    c = cbuf[...]
    mu = jnp.mean(c, axis=-1, keepdims=True)
    cc = c - mu
    var = jnp.mean(cc * cc, axis=-1, keepdims=True)
    un = cc * lax.rsqrt(var + LN_EPS) * lng_ref[...] + lnb_ref[...]
    un = un * jax.nn.sigmoid(un)
    o_ref[...] = h + _dot(un.astype(BF16), w2_ref[...]) + b2_ref[...]


def _conv_layer(h, g, w1, b1, dw, bdw, lng, lnb, w2, b2, side_weights, side_layer, *, batch, seq):
    n, d = h.shape
    width = dw.shape[0]
    assert width - 1 <= CONV_HALO
    ts = min(256, seq)
    rb = min(128, ts)
    nt = seq // ts
    row = lambda b, j: (b * nt + j, 0)
    c2 = lambda b, j: (0, 0)
    side, side_in, side_out, side_shapes = _side_cast_plan(side_weights, side_layer, batch * nt,
                                                           lambda b, j: b * nt + j)
    outs = pl.pallas_call(
        functools.partial(_conv_kernel, ts=ts, width=width, rb=rb, n_side=len(side)),
        grid=(batch, nt),
        in_specs=[pl.BlockSpec((ts, d), row),
                  pl.BlockSpec((1, d), c2),
                  pl.BlockSpec(w1.shape, c2),
                  pl.BlockSpec(b1.shape, c2),
                  pl.BlockSpec(dw.shape, c2),
                  pl.BlockSpec((1, d), c2),
                  pl.BlockSpec((1, d), c2),
                  pl.BlockSpec((1, d), c2),
                  pl.BlockSpec(w2.shape, c2),
                  pl.BlockSpec((1, d), c2)] + side_in,
        out_specs=[pl.BlockSpec((ts, d), row)] + side_out,
        out_shape=[jax.ShapeDtypeStruct((n, d), F32)] + side_shapes,
        scratch_shapes=[pltpu.VMEM((SUBLANES, ts + CONV_HALO + SUBLANES, d), F32), pltpu.VMEM((ts, d), F32)],
        compiler_params=_params("arbitrary", "arbitrary"),
        name="conv_module",
    )(h, g, w1, b1, dw, bdw, lng, lnb, w2, b2, *side)
    return outs[0], [o.reshape(w.shape[1:]) for o, w in zip(outs[1:], side_weights)]


def _mla_proj_kernel(h_ref, pos_ref, g_ref, wqa_ref, qnorm_ref, wqb_ref, wkva_ref, kvnorm_ref, wkvb_ref,
                     invf_ref, sgn_ref, qt_o, k_o, vt_o, *, heads, lora):
    hn = heads * NOPE
    rows = h_ref.shape[0]
    u = _rms(h_ref[...], g_ref[...]).astype(BF16)
    qa = _rms(_dot(u, wqa_ref[...]), qnorm_ref[...]).astype(BF16)
    q = _dot(qa, wqb_ref[...])
    kva = _dot(u, wkva_ref[...])
    ckv = _rms(kva[:, :lora], kvnorm_ref[...]).astype(BF16)
    kv = _dot(ckv, wkvb_ref[...])
    ang = pos_ref[...].astype(F32) * invf_ref[...]
    c = jnp.cos(ang)
    s = jnp.sin(ang) * sgn_ref[...]
    kr = (kva[:, lora:lora + LANES] * c + kva[:, lora + LANES:lora + 2 * LANES] * s).astype(BF16)
    ones = jnp.ones((V_ROWS - NOPE, rows), BF16)
    for hd in range(heads):
        ns = slice(hd * NOPE, (hd + 1) * NOPE)
        qr = q[:, hn + hd * LANES:hn + (hd + 1) * LANES] * c + q[:, 2 * hn + hd * LANES:2 * hn + (hd + 1) * LANES] * s
        qt_o[2 * hd * NOPE:(2 * hd + 1) * NOPE, :] = q[:, ns].T.astype(BF16)
        qt_o[(2 * hd + 1) * NOPE:(2 * hd + 2) * NOPE, :] = qr.T.astype(BF16)
        k_o[:, 2 * hd * NOPE:(2 * hd + 1) * NOPE] = kv[:, ns].astype(BF16)
        k_o[:, (2 * hd + 1) * NOPE:(2 * hd + 2) * NOPE] = kr
        vt_o[hd * V_ROWS:hd * V_ROWS + NOPE, :] = kv[:, hn + hd * NOPE:hn + (hd + 1) * NOPE].T.astype(BF16)
        vt_o[hd * V_ROWS + NOPE:(hd + 1) * V_ROWS, :] = ones


def _attn_kernel(qi_ref, kj_ref, qt_ref, k_ref, vt_ref, h_ref, wot_ref, *rest, heads, tq, tk, scale, n_side):
    side_in, o_ref, side_out, (m_sc, acc_sc) = rest[:n_side], rest[n_side], rest[n_side + 1:-2], rest[-2:]
    _side_cast(side_in, side_out)
    i = qi_ref[pl.program_id(1)]
    j = kj_ref[pl.program_id(1)]
    c = scale * LOG2E

    @pl.when(j == 0)
    def _():
        m_sc[...] = jnp.full(m_sc.shape, -jnp.inf, F32)
        acc_sc[...] = jnp.zeros(acc_sc.shape, F32)

    def step(diagonal):
        if diagonal:
            causal = (lax.broadcasted_iota(jnp.int32, (tk, tq), 0)
                      <= lax.broadcasted_iota(jnp.int32, (tk, tq), 1))
        es = [slice(2 * hd * NOPE, (2 * hd + 2) * NOPE) for hd in range(heads)]
        vs = [slice(hd * V_ROWS, (hd + 1) * V_ROWS) for hd in range(heads)]
        ss = [_dot(k_ref[:, e], qt_ref[e, :]) for e in es]
        if diagonal:
            ss = [jnp.where(causal, s, -jnp.inf) for s in ss]
        m_prev = [m_sc[hd] for hd in range(heads)]
        m_new = [jnp.maximum(mp, jnp.max(s, axis=0, keepdims=True)) for mp, s in zip(m_prev, ss)]
        ps = [jnp.exp2((s - mn) * c).astype(BF16) for s, mn in zip(ss, m_new)]
        pv = [_dot(vt_ref[v, :], p) for p, v in zip(ps, vs)]
        for hd in range(heads):
            alpha = jnp.exp2((m_prev[hd] - m_new[hd]) * c)
            acc_sc[vs[hd], :] = alpha * acc_sc[vs[hd], :] + pv[hd]
            m_sc[hd] = m_new[hd]

    @pl.when(j < i)
    def _():
        step(False)

    @pl.when(j == i)
    def _():
        step(True)
        parts = [acc_sc[hd * V_ROWS:hd * V_ROWS + NOPE, :] / acc_sc[hd * V_ROWS + NOPE:hd * V_ROWS + NOPE + 1, :]
                 for hd in range(heads)]
        ot = jnp.concatenate(parts, axis=0).astype(BF16)
        o_ref[...] = h_ref[...] + _dot(wot_ref[...], ot).T


def _mla_layer(h, positions, g, wqa, qnorm, wqb, wkva, kvnorm, wkvb, wot, invf, sgn, side_weights, side_layer,
               *, batch, seq, heads, lora):
    n, d = h.shape
    hn = heads * NOPE
    tp = min(256, n)
    c2 = lambda i: (0, 0)
    row = lambda i: (i, 0)
    col = lambda i: (0, i)
    qt, k, vt = pl.pallas_call(
        functools.partial(_mla_proj_kernel, heads=heads, lora=lora),
        grid=(n // tp,),
        in_specs=[pl.BlockSpec((tp, d), row),
                  pl.BlockSpec((tp, 1), row),
                  pl.BlockSpec((1, d), c2),
                  pl.BlockSpec(wqa.shape, c2),
                  pl.BlockSpec(qnorm.shape, c2),
                  pl.BlockSpec(wqb.shape, c2),
                  pl.BlockSpec(wkva.shape, c2),
                  pl.BlockSpec(kvnorm.shape, c2),
                  pl.BlockSpec(wkvb.shape, c2),
                  pl.BlockSpec((1, LANES), c2),
                  pl.BlockSpec((1, LANES), c2)],
        out_specs=[pl.BlockSpec((2 * hn, tp), col), pl.BlockSpec((tp, 2 * hn), row),
                   pl.BlockSpec((heads * V_ROWS, tp), col)],
        out_shape=[jax.ShapeDtypeStruct((2 * hn, n), BF16), jax.ShapeDtypeStruct((n, 2 * hn), BF16),
                   jax.ShapeDtypeStruct((heads * V_ROWS, n), BF16)],
        compiler_params=_params("arbitrary"),
        name="mla_proj",
    )(h, positions, g, wqa, qnorm, wqb, wkva, kvnorm, wkvb, invf, sgn)

    tq = tk = min(512, seq)
    nq = seq // tq
    scale = float(NOPE + ROPE) ** -0.5
    pairs = [(i, j) for i in range(nq) for j in range(i + 1)]
    qi = jnp.asarray([p[0] for p in pairs], jnp.int32)
    kj = jnp.asarray([p[1] for p in pairs], jnp.int32)
    npairs = len(pairs)
    qrow = lambda b, p, qi, kj: (b * nq + qi[p], 0)
    qcol = lambda b, p, qi, kj: (0, b * nq + qi[p])
    krow = lambda b, p, qi, kj: (b * nq + kj[p], 0)
    kcol = lambda b, p, qi, kj: (0, b * nq + kj[p])
    side, side_in, side_out, side_shapes = _side_cast_plan(side_weights, side_layer, batch * npairs,
                                                           lambda b, p, qi, kj: b * npairs + p)
    grid_spec = pltpu.PrefetchScalarGridSpec(
        num_scalar_prefetch=2,
        grid=(batch, npairs),
        in_specs=[pl.BlockSpec((2 * hn, tq), qcol),
                  pl.BlockSpec((tk, 2 * hn), krow),
                  pl.BlockSpec((heads * V_ROWS, tk), kcol),
                  pl.BlockSpec((tq, d), qrow),
                  pl.BlockSpec(wot.shape, lambda b, p, qi, kj: (0, 0))] + side_in,
        out_specs=[pl.BlockSpec((tq, d), qrow)] + side_out,
        scratch_shapes=[pltpu.VMEM((heads, 1, tq), F32), pltpu.VMEM((heads * V_ROWS, tq), F32)],
    )
    outs = pl.pallas_call(
        functools.partial(_attn_kernel, heads=heads, tq=tq, tk=tk, scale=scale, n_side=len(side)),
        grid_spec=grid_spec,
        out_shape=[jax.ShapeDtypeStruct((n, d), F32)] + side_shapes,
        compiler_params=_params("arbitrary", "arbitrary"),
        name="mla_attention",
    )(qi, kj, qt, k, vt, h, wot, *side)
    return outs[0], [o.reshape(w.shape[1:]) for o, w in zip(outs[1:], side_weights)]


def _store_row_tiles(ref, value, rows):
    slabs = value.shape[1] // LANES
    for s in range(slabs):
        ref[pl.ds(s, rows, stride=slabs), :] = value[:, s * LANES:(s + 1) * LANES]


def _load_row_tile_slab(ref, s, rows, slabs):
    return ref[pl.ds(s, rows, stride=slabs), :]


def _route_kernel(h_ref, g_ref, r_ref, u_ref, info_ref, cnt_ref, carry, *, tr, n_exp):
    i = pl.program_id(0)

    @pl.when(i == 0)
    def _():
        carry[...] = jnp.zeros(carry.shape, F32)

    u = _rms(h_ref[...], g_ref[...])
    _store_row_tiles(u_ref, u, tr)
    logits = jnp.dot(u, r_ref[...], preferred_element_type=F32, precision=lax.Precision.HIGHEST)
    lane = lax.broadcasted_iota(jnp.int32, (tr, LANES), 1)
    lg = jnp.where(lane < n_exp, logits, -jnp.inf)
    m1 = jnp.max(lg, axis=1, keepdims=True)
    i1 = jnp.min(jnp.where(lg == m1, lane, LANES), axis=1, keepdims=True)
    lg2 = jnp.where(lane == i1, -jnp.inf, lg)
    m2 = jnp.max(lg2, axis=1, keepdims=True)
    i2 = jnp.min(jnp.where(lg2 == m2, lane, LANES), axis=1, keepdims=True)
    e = jnp.exp(m2 - m1)
    g1 = 1.0 / (1.0 + e)
    g2 = e / (1.0 + e)
    sel1 = lane == i1
    sel2 = lane == i2
    onehot = jnp.where(sel1 | sel2, 1.0, 0.0)
    rr = lax.broadcasted_iota(jnp.int32, (tr, tr), 0)
    cc = lax.broadcasted_iota(jnp.int32, (tr, tr), 1)
    tri = jnp.where(cc < rr, 1.0, 0.0).astype(BF16)
    before = _dot(tri, onehot.astype(BF16)) + carry[0:1, :]
    rank1 = jnp.sum(jnp.where(sel1, before, 0.0), axis=1, keepdims=True)
    rank2 = jnp.sum(jnp.where(sel2, before, 0.0), axis=1, keepdims=True)
    carry[...] = carry[...] + jnp.sum(onehot, axis=0, keepdims=True)
    cols = (i1.astype(F32), i2.astype(F32), g1, g2, rank1, rank2)
    info = jnp.zeros((tr, LANES), F32)
    for k, col in enumerate(cols):
        info = jnp.where(lane == k, col, info)
    info_ref[...] = info
    cnt_ref[...] = carry[...]


def _route(h, g, router_padded, n_exp):
    n, d = h.shape
    tr = min(512, n)
    slabs = d // LANES
    return pl.pallas_call(
        functools.partial(_route_kernel, tr=tr, n_exp=n_exp),
        grid=(n // tr,),
        in_specs=[pl.BlockSpec((tr, d), lambda i: (i, 0)),
                  pl.BlockSpec((1, d), lambda i: (0, 0)),
                  pl.BlockSpec((d, LANES), lambda i: (0, 0))],
        out_specs=[pl.BlockSpec((tr * slabs, LANES), lambda i: (i, 0)),
                   pl.BlockSpec((tr, LANES), lambda i: (i, 0)),
                   pl.BlockSpec((8, LANES), lambda i: (0, 0))],
        out_shape=[jax.ShapeDtypeStruct((n * slabs, LANES), F32),
                   jax.ShapeDtypeStruct((n, LANES), F32),
                   jax.ShapeDtypeStruct((8, LANES), F32)],
        scratch_shapes=[pltpu.VMEM((8, LANES), F32)],
        compiler_params=_params("arbitrary"),
        name="moe_route",
    )(h, g, router_padded)


def _gather_row_tiles(idx_ref, k, src_ref, dst_ref, sem, rows, slabs):
    def start8(r8, c):
        for q in range(GATHER_UNROLL):
            r = r8 * GATHER_UNROLL + q
            src_row = pl.multiple_of(idx_ref[0, k, r] * slabs, slabs)
            dst_row = pl.multiple_of(r * slabs, slabs)
            pltpu.make_async_copy(src_ref.at[pl.ds(src_row, slabs)], dst_ref.at[pl.ds(dst_row, slabs)],
                                  sem).start(priority=q % 2)
        return c

    lax.fori_loop(0, rows // GATHER_UNROLL, start8, 0)
    pltpu.make_async_copy(src_ref.at[pl.ds(0, rows * slabs)], dst_ref, sem).wait()


def _dispatch_kernel(pad_start_ref, pad_cnt_ref, pos_ref, u_ref, x_ref, zero_sc, sem, *, rows, slabs, n_exp):
    def row_copy(src, src_row, dst_row):
        return pltpu.make_async_copy(src.at[pl.ds(pl.multiple_of(src_row * slabs, slabs), slabs)],
                                     x_ref.at[pl.ds(pl.multiple_of(dst_row * slabs, slabs), slabs)], sem)

    @pl.when(pl.program_id(0) == 0)
    def _():
        zero_sc[...] = jnp.zeros(zero_sc.shape, F32)
        for e in range(n_exp):
            def fill(i, c):
                row_copy(zero_sc, 0, pad_start_ref[e] + i).start()
                return c

            def drain(i, c):
                row_copy(zero_sc, 0, pad_start_ref[e] + i).wait()
                return c

            lax.fori_loop(0, pad_cnt_ref[e], fill, 0)
            lax.fori_loop(0, pad_cnt_ref[e], drain, 0)

        def chunk_copy(i):
            dst = pl.multiple_of((pad_start_ref[n_exp] + i * ZERO_CHUNK) * slabs, slabs)
            return pltpu.make_async_copy(zero_sc, x_ref.at[pl.ds(dst, ZERO_CHUNK * slabs)], sem)

        def fill_tail(i, c):
            chunk_copy(i).start()
            return c

        def drain_tail(i, c):
            chunk_copy(i).wait()
            return c

        lax.fori_loop(0, pad_cnt_ref[n_exp], fill_tail, 0)
        lax.fori_loop(0, pad_cnt_ref[n_exp], drain_tail, 0)

    def start8(r8, c):
        for q in range(GATHER_UNROLL):
            r = r8 * GATHER_UNROLL + q
            for k in range(TOP_K):
                row_copy(u_ref, r, pos_ref[0, k, r]).start(priority=k)
        return c

    lax.fori_loop(0, rows // GATHER_UNROLL, start8, 0)
    for k in range(TOP_K):
        pltpu.make_async_copy(u_ref, x_ref.at[pl.ds(0, rows * slabs)], sem).wait()


def _dispatch(u, pos_t, pad_start, pad_cnt, sorted_rows, tile, slabs):
    n = u.shape[0] // slabs
    grid_spec = pltpu.PrefetchScalarGridSpec(
        num_scalar_prefetch=2,
        grid=(n // tile,),
        in_specs=[pl.BlockSpec((1, TOP_K, tile), lambda i, ps, pc: (i, 0, 0), memory_space=pltpu.SMEM),
                  pl.BlockSpec((tile * slabs, LANES), lambda i, ps, pc: (i, 0))],
        out_specs=pl.BlockSpec(memory_space=pl.ANY),
        scratch_shapes=[pltpu.VMEM((ZERO_CHUNK * slabs, LANES), F32), pltpu.SemaphoreType.DMA(())],
    )
    return pl.pallas_call(
        functools.partial(_dispatch_kernel, rows=tile, slabs=slabs, n_exp=pad_start.shape[0] - 1),
        grid_spec=grid_spec,
        out_shape=jax.ShapeDtypeStruct((sorted_rows * slabs, LANES), F32),
        compiler_params=_params("arbitrary"),
        name="moe_dispatch",
    )(pad_start, pad_cnt, pos_t, u)


def _moe_kernel(te_ref, nu_ref, x_ref, wg_ref, wu_ref, wd_ref, o_ref, x_sc, acc_sc, *, tm, slabs):
    t = pl.program_id(0)
    j = pl.program_id(1)

    @pl.when(t < nu_ref[0])
    def _():
        @pl.when(j == 0)
        def _():
            for s in range(slabs):
                x_sc[:, s * LANES:(s + 1) * LANES] = _load_row_tile_slab(x_ref, s, tm, slabs).astype(BF16)
            acc_sc[...] = jnp.zeros(acc_sc.shape, F32)

        acc_sc[...] += _swiglu_step(x_sc[...], wg_ref[0], wu_ref[0], wd_ref[0])

        @pl.when(j == pl.num_programs(1) - 1)
        def _():
            _store_row_tiles(o_ref, acc_sc[...], tm)

    @pl.when((t >= nu_ref[0]) & (j == 0))
    def _():
        o_ref[...] = jnp.zeros(o_ref.shape, F32)


def _moe_experts(x_sorted, tile_expert, n_used, wg, wu, wd, tm, slabs):
    d = slabs * LANES
    rows = x_sorted.shape[0] // slabs
    f = wg.shape[2]
    tf = _ff_chunk(f)
    nj = f // tf
    tmap = lambda t, j, te, nu: (jnp.minimum(t, nu[0] - 1), 0)
    jsel = lambda t, j, nu: jnp.where(t < nu[0], j, nj - 1)
    grid_spec = pltpu.PrefetchScalarGridSpec(
        num_scalar_prefetch=2,
        grid=(rows // tm, nj),
        in_specs=[pl.BlockSpec((tm * slabs, LANES), tmap),
                  pl.BlockSpec((1, d, tf), lambda t, j, te, nu: (te[t], 0, jsel(t, j, nu))),
                  pl.BlockSpec((1, d, tf), lambda t, j, te, nu: (te[t], 0, jsel(t, j, nu))),
                  pl.BlockSpec((1, tf, d), lambda t, j, te, nu: (te[t], jsel(t, j, nu), 0))],
        out_specs=pl.BlockSpec((tm * slabs, LANES), lambda t, j, te, nu: (t, 0)),
        scratch_shapes=[pltpu.VMEM((tm, d), BF16), pltpu.VMEM((tm, d), F32)],
    )
    return pl.pallas_call(
        functools.partial(_moe_kernel, tm=tm, slabs=slabs),
        grid_spec=grid_spec,
        out_shape=jax.ShapeDtypeStruct((rows * slabs, LANES), F32),
        compiler_params=_params("arbitrary", "arbitrary"),
        name="moe_experts",
    )(tile_expert, n_used, x_sorted, wg, wu, wd)


def _combine_kernel(pos_ref, h_ref, gate_ref, gf_ref, y_ref, o_ref, y0, y1, sem, *, rows, slabs, final_norm):
    _gather_row_tiles(pos_ref, 0, y_ref, y0, sem.at[0], rows, slabs)
    _gather_row_tiles(pos_ref, 1, y_ref, y1, sem.at[1], rows, slabs)
    gates = gate_ref[...]
    g0 = gates[:, 2:3]
    g1 = gates[:, 3:4]
    ssq = jnp.zeros((rows, 1), F32)
    for s in range(slabs):
        cs = slice(s * LANES, (s + 1) * LANES)
        out = (h_ref[:, cs] + g0 * _load_row_tile_slab(y0, s, rows, slabs)
               + g1 * _load_row_tile_slab(y1, s, rows, slabs))
        o_ref[:, cs] = out
        ssq = ssq + jnp.sum(out * out, axis=-1, keepdims=True)
    if final_norm:
        d = slabs * LANES
        o_ref[...] = o_ref[...] * lax.rsqrt(ssq / d + NORM_EPS) * gf_ref[...]


def _combine(h, info, pos_t, y_sorted, g_final, final_norm, slabs):
    n, d = h.shape
    tile = pos_t.shape[2]
    return pl.pallas_call(
        functools.partial(_combine_kernel, rows=tile, slabs=slabs, final_norm=final_norm),
        grid=(n // tile,),
        in_specs=[pl.BlockSpec((1, TOP_K, tile), lambda i: (i, 0, 0), memory_space=pltpu.SMEM),
                  pl.BlockSpec((tile, d), lambda i: (i, 0)),
                  pl.BlockSpec((tile, LANES), lambda i: (i, 0)),
                  pl.BlockSpec((1, d), lambda i: (0, 0)),
                  pl.BlockSpec(memory_space=pl.ANY)],
        out_specs=pl.BlockSpec((tile, d), lambda i: (i, 0)),
        out_shape=jax.ShapeDtypeStruct((n, d), F32),
        scratch_shapes=[pltpu.VMEM((tile * slabs, LANES), F32), pltpu.VMEM((tile * slabs, LANES), F32),
                        pltpu.SemaphoreType.DMA((2,))],
        compiler_params=_params("arbitrary"),
        name="moe_combine",
    )(pos_t, h, info, g_final, y_sorted)


def _moe_layer(h, g, router, wg, wu, wd, g_final, final_norm):
    n, d = h.shape
    n_exp = router.shape[1]
    tm = min(512, n)
    router_padded = jnp.pad(router, ((0, 0), (0, LANES - n_exp)))
    u, info, cnt = _route(h, g, router_padded, n_exp)

    idx = info[:, 0:TOP_K].astype(jnp.int32)
    rank = info[:, 4:4 + TOP_K].astype(jnp.int32)
    counts = cnt[0, :n_exp].astype(jnp.int32)
    padded = (counts + tm - 1) // tm * tm
    ends = jnp.cumsum(padded)
    starts = ends - padded
    pos = jnp.sum(jnp.where(idx[..., None] == jnp.arange(n_exp), starts, 0), axis=-1) + rank
    rows = TOP_K * n + n_exp * tm
    n_used = (ends[-1] // tm).astype(jnp.int32)
    tile_ids = jnp.minimum(jnp.arange(rows // tm, dtype=jnp.int32), n_used - 1)
    tile_expert = jnp.sum(tile_ids[:, None] * tm >= ends[None, :], axis=1).astype(jnp.int32)

    slabs = d // LANES
    assert slabs == 8
    tile = min(512, n)
    pos_t = pos.T.reshape(TOP_K, n // tile, tile).transpose(1, 0, 2)
    assert tm % ZERO_CHUNK == 0
    pad_start = jnp.concatenate([starts + counts, ends[-1:]]).astype(jnp.int32)
    pad_cnt = jnp.concatenate([padded - counts, (rows - ends[-1:]) // ZERO_CHUNK]).astype(jnp.int32)
    x_sorted = _dispatch(u, pos_t, pad_start, pad_cnt, rows, tile, slabs)
    y_sorted = _moe_experts(x_sorted, tile_expert, n_used.reshape(1), wg, wu, wd, tm, slabs)
    return _combine(h, info, pos_t, y_sorted, g_final, final_norm, slabs)


def _mla_weights(wq_b, wkv_a, wkv_b, heads, lora):
    def padded_rope(w):
        pad = jnp.zeros(w.shape[:-1] + (LANES - ROPE,), w.dtype)
        swapped = jnp.concatenate([w[..., ROPE // 2:], w[..., :ROPE // 2]], axis=-1)
        return jnp.concatenate([w, pad], axis=-1), jnp.concatenate([swapped, pad], axis=-1)

    rq = wq_b.shape[0]
    wq = wq_b.reshape(rq, heads, NOPE + ROPE)
    q_r, q_rs = padded_rope(wq[..., NOPE:])
    wqb = jnp.concatenate([wq[..., :NOPE].reshape(rq, -1), q_r.reshape(rq, -1), q_rs.reshape(rq, -1)], axis=1)
    k_r, k_rs = padded_rope(wkv_a[:, lora:])
    wkva = jnp.concatenate([wkv_a[:, :lora], k_r, k_rs], axis=1)
    wkv = wkv_b.reshape(lora, heads, 2 * NOPE)
    wkvb = jnp.concatenate([wkv[..., :NOPE].reshape(lora, -1), wkv[..., NOPE:].reshape(lora, -1)], axis=1)
    return wqb.astype(BF16), wkva.astype(BF16), wkvb.astype(BF16)


def _cast_kernel(x_ref, o_ref):
    o_ref[...] = x_ref[0].astype(BF16)


def _to_bf16(w, layer):
    cols = w.shape[-1]
    w3 = w.reshape(w.shape[0], -1, cols)
    rows = w3.shape[1]
    cap = 256 if cols > 2048 else 1024
    tr = next(t for t in (1024, 512, 256, 128, 64, 32, 16, 8) if t <= cap and rows % t == 0)
    out = pl.pallas_call(
        _cast_kernel,
        grid=(rows // tr,),
        in_specs=[pl.BlockSpec((1, tr, cols), lambda i: (layer, i, 0))],
        out_specs=pl.BlockSpec((tr, cols), lambda i: (i, 0)),
        out_shape=jax.ShapeDtypeStruct((rows, cols), BF16),
        compiler_params=_params("arbitrary"),
        name="weights_to_bf16",
    )(w3)
    return out.reshape(w.shape[1:])


def _final_norm_kernel(h_ref, g_ref, o_ref):
    o_ref[...] = _rms(h_ref[...], g_ref[...])


def _final_norm(h, g):
    n, d = h.shape
    tm = min(512, n)
    return pl.pallas_call(
        _final_norm_kernel,
        grid=(n // tm,),
        in_specs=[pl.BlockSpec((tm, d), lambda i: (i, 0)), pl.BlockSpec((1, d), lambda i: (0, 0))],
        out_specs=pl.BlockSpec((tm, d), lambda i: (i, 0)),
        out_shape=jax.ShapeDtypeStruct((n, d), F32),
        compiler_params=_params("arbitrary"),
        name="final_norm",
    )(h, g)


@jax.jit
def _trunk(x, positions, norm_mix, norm_ffn, norm_final, pool_w, pool_scale,
           conv_w1, conv_b1, conv_dw, conv_bdw, conv_ln_g, conv_ln_b, conv_w2, conv_b2,
           mla_wq_a, mla_q_norm, mla_wq_b, mla_wkv_a, mla_kv_norm, mla_wkv_b, mla_wo,
           ffn_wg, ffn_wu, ffn_wd, moe_router, moe_wg, moe_wu, moe_wd):
    batch, seq, d = x.shape
    depth = norm_mix.shape[0]
    lora = mla_kv_norm.shape[1]
    heads = mla_wo.shape[1] // NOPE
    assert mla_wq_b.shape[2] == heads * (NOPE + ROPE) and mla_wkv_a.shape[2] == lora + ROPE
    n = batch * seq
    h = x.reshape(n, d)
    pos = positions.reshape(n, 1)
    row = lambda v: v.reshape(1, -1)

    half = jnp.arange(0, ROPE, 2, dtype=F32) / ROPE
    inv_freq = ROPE_THETA ** (-half)
    invf = row(jnp.concatenate([inv_freq, inv_freq, jnp.zeros((LANES - ROPE,), F32)]))
    sgn = row(jnp.concatenate([-jnp.ones((ROPE // 2,), F32), jnp.ones((ROPE // 2,), F32),
                               jnp.zeros((LANES - ROPE,), F32)]))

    moe_w = (moe_wg, moe_wu, moe_wd)
    moe_bf16 = {}

    def moe_cast_job(i):
        nxt = i if i % 2 == 1 else i + 1
        if nxt < depth and nxt // 2 not in moe_bf16:
            return moe_w, nxt // 2
        return (), 0

    for i in range(depth):
        kind, j = i % 3, i // 3
        g = row(norm_mix[i])
        if kind == 0:
            h = _pool_layer(h, g, pool_w[j].astype(BF16), row(pool_scale[j]), batch=batch, seq=seq)
        elif kind == 1:
            side_w, side_layer = moe_cast_job(i)
            h, cast = _conv_layer(h, g, conv_w1[j].astype(BF16), row(conv_b1[j]), conv_dw[j], row(conv_bdw[j]),
                                  row(conv_ln_g[j]), row(conv_ln_b[j]), conv_w2[j].astype(BF16), row(conv_b2[j]),
                                  side_w, side_layer, batch=batch, seq=seq)
            if cast:
                moe_bf16[side_layer] = cast
        else:
            wqb, wkva, wkvb = _mla_weights(mla_wq_b[j], mla_wkv_a[j], mla_wkv_b[j], heads, lora)
            side_w, side_layer = moe_cast_job(i)
            h, cast = _mla_layer(h, pos, g, mla_wq_a[j].astype(BF16), row(mla_q_norm[j]), wqb, wkva,
                                 row(mla_kv_norm[j]), wkvb, mla_wo[j].T.astype(BF16), invf, sgn,
                                 side_w, side_layer, batch=batch, seq=seq, heads=heads, lora=lora)
            if cast:
                moe_bf16[side_layer] = cast
        g = row(norm_ffn[i])
        f_idx = i // 2
        last = i == depth - 1
        if i % 2 == 0:
            h = _ffn_layer(h, g, _to_bf16(ffn_wg, f_idx), _to_bf16(ffn_wu, f_idx), _to_bf16(ffn_wd, f_idx))
            if last:
                h = _final_norm(h, row(norm_final))
        else:
            wg, wu, wd = moe_bf16.get(f_idx) or [_to_bf16(w, f_idx) for w in moe_w]
            h = _moe_layer(h, g, moe_router[f_idx], wg, wu, wd, row(norm_final), last)
    return h.reshape(batch, seq, d)


def kernel(x, positions, norm_mix, norm_ffn, norm_final, pool_w, pool_scale, conv_w1, conv_b1, conv_dw, conv_bdw, conv_ln_g, conv_ln_b, conv_w2, conv_b2, mla_wq_a, mla_q_norm, mla_wq_b, mla_wkv_a, mla_kv_norm, mla_wkv_b, mla_wo, ffn_wg, ffn_wu, ffn_wd, moe_router, moe_wg, moe_wu, moe_wd):
    return _trunk(x, positions, norm_mix, norm_ffn, norm_final, pool_w, pool_scale,
                  conv_w1, conv_b1, conv_dw, conv_bdw, conv_ln_g, conv_ln_b, conv_w2, conv_b2,
                  mla_wq_a, mla_q_norm, mla_wq_b, mla_wkv_a, mla_kv_norm, mla_wkv_b, mla_wo,
                  ffn_wg, ffn_wu, ffn_wd, moe_router, moe_wg, moe_wu, moe_wd)
```

```python
import functools

import jax
import jax.numpy as jnp
from jax import lax
from jax.experimental import pallas as pl
from jax.experimental.pallas import tpu as pltpu

F32 = jnp.float32
BF16 = jnp.bfloat16

NORM_EPS = 1e-6
LN_EPS = 1e-5
ROPE_THETA = 10000.0
LOG2E = 1.4426950408889634

POOL_WINDOWS = (2, 4, 8, 16)
POOL_HALO = 16
CONV_HALO = 32
LANES = 128
SUBLANES = 8
POOL_TOP = POOL_HALO + SUBLANES
NOPE = 128
ROPE = 64
V_ROWS = NOPE + 16
TOP_K = 2
GATHER_UNROLL = 8
ZERO_CHUNK = 64

VMEM_LIMIT = 56 * 1024 * 1024


def _params(*sem):
    return pltpu.CompilerParams(dimension_semantics=sem, vmem_limit_bytes=VMEM_LIMIT)


def _rms(x, g, eps=NORM_EPS):
    return x * lax.rsqrt(jnp.mean(x * x, axis=-1, keepdims=True) + eps) * g


def _dot(a, b):
    return jnp.dot(a, b, preferred_element_type=F32)


def _split_bf16(x):
    hi = x.astype(BF16)
    return hi, (x - hi.astype(F32)).astype(BF16)


def _dot_3pass(a, b):
    a_hi, a_lo = _split_bf16(a)
    b_hi, b_lo = _split_bf16(b)
    return _dot(a_hi, b_hi) + (_dot(a_hi, b_lo) + _dot(a_lo, b_hi))


def _ff_chunk(f):
    for c in (1792, 1024, 512, 256, 128):
        if f % c == 0:
            return c
    return f


def _pool_kernel(h_ref, g_ref, w_ref, sc_ref, o_ref, ubuf, lvl_a, lvl_b, *, ts, gdim):
    j = pl.program_id(1)
    h = h_ref[...]
    u = _rms(h, g_ref[...])
    ext = POOL_TOP + ts

    @pl.when(j == 0)
    def _():
        ubuf[0:POOL_TOP, :] = jnp.zeros((POOL_TOP, ubuf.shape[1]), F32)
        lvl_a[0:SUBLANES, :] = jnp.zeros((SUBLANES, gdim), F32)
        lvl_b[0:SUBLANES, :] = jnp.zeros((SUBLANES, gdim), F32)

    ubuf[POOL_TOP:ext, :] = u
    t1 = lax.broadcasted_iota(jnp.int32, (ts, 1), 0) + (j * ts + 1)
    for g, win in enumerate(POOL_WINDOWS):
        cs = slice(g * gdim, (g + 1) * gdim)
        src, cols, spare, k = ubuf, cs, [lvl_a, lvl_b], 1
        while k < win:
            s = src[SUBLANES:ext, cols] + src[SUBLANES - k:ext - k, cols]
            k *= 2
            if k < win:
                dst = spare.pop(0)
                dst[SUBLANES:ext, :] = s
                spare.append(dst)
                src, cols = dst, slice(None)
        s = s[POOL_HALO:POOL_HALO + ts]
        ug = u[:, cs]
        cnt = jnp.minimum(t1, win).astype(F32)
        pooled = s / cnt - ug
        m = _dot(pooled.astype(BF16), w_ref[g])
        o_ref[:, cs] = h[:, cs] + m * sc_ref[:, cs]
    ubuf[SUBLANES:POOL_TOP, :] = ubuf[ts + SUBLANES:ts + POOL_TOP, :]


def _pool_layer(h, g, w, scale, *, batch, seq):
    n, d = h.shape
    ts = min(512, seq)
    nt = seq // ts
    gdim = d // len(POOL_WINDOWS)
    row = lambda b, j: (b * nt + j, 0)
    const2 = lambda b, j: (0, 0)
    return pl.pallas_call(
        functools.partial(_pool_kernel, ts=ts, gdim=gdim),
        grid=(batch, nt),
        in_specs=[pl.BlockSpec((ts, d), row),
                  pl.BlockSpec((1, d), const2),
                  pl.BlockSpec(w.shape, lambda b, j: (0, 0, 0)),
                  pl.BlockSpec((1, d), const2)],
        out_specs=pl.BlockSpec((ts, d), row),
        out_shape=jax.ShapeDtypeStruct((n, d), F32),
        scratch_shapes=[pltpu.VMEM((ts + POOL_TOP, d), F32), pltpu.VMEM((ts + POOL_TOP, gdim), F32),
                        pltpu.VMEM((ts + POOL_TOP, gdim), F32)],
        compiler_params=_params("arbitrary", "arbitrary"),
        name="pool_mixer",
    )(h, g, w, scale)


def _swiglu_step(u, wg, wu, wd):
    a = _dot(u, wg)
    b = _dot(u, wu)
    hh = a * jax.nn.sigmoid(a) * b
    return _dot(hh.astype(BF16), wd)


def _accumulate_steps(j, nj, partial, acc_sc, finish):
    def run(first, last):
        r = partial()
        if not first:
            r = acc_sc[...] + r
        if last:
            finish(r)
        else:
            acc_sc[...] = r

    if nj == 1:
        run(True, True)
        return
    pl.when(j == 0)(functools.partial(run, True, False))
    if nj > 2:
        pl.when((j > 0) & (j < nj - 1))(functools.partial(run, False, False))
    pl.when(j == nj - 1)(functools.partial(run, False, True))


def _ffn_kernel(h_ref, g_ref, wg_ref, wu_ref, wd_ref, o_ref, u_sc, acc_sc, *, nj):
    j = pl.program_id(1)

    @pl.when(j == 0)
    def _():
        u_sc[...] = _rms(h_ref[...], g_ref[...]).astype(BF16)

    def finish(r):
        o_ref[...] = h_ref[...] + r

    _accumulate_steps(j, nj, lambda: _swiglu_step(u_sc[...], wg_ref[...], wu_ref[...], wd_ref[...]), acc_sc, finish)


def _ffn_layer(h, g, wg, wu, wd):
    n, d = h.shape
    f = wg.shape[1]
    tm = min(512, n)
    tf = _ff_chunk(f)
    return pl.pallas_call(
        functools.partial(_ffn_kernel, nj=f // tf),
        grid=(n // tm, f // tf),
        in_specs=[pl.BlockSpec((tm, d), lambda i, j: (i, 0)),
                  pl.BlockSpec((1, d), lambda i, j: (0, 0)),
                  pl.BlockSpec((d, tf), lambda i, j: (0, j)),
                  pl.BlockSpec((d, tf), lambda i, j: (0, j)),
                  pl.BlockSpec((tf, d), lambda i, j: (j, 0))],
        out_specs=pl.BlockSpec((tm, d), lambda i, j: (i, 0)),
        out_shape=jax.ShapeDtypeStruct((n, d), F32),
        scratch_shapes=[pltpu.VMEM((tm, d), BF16), pltpu.VMEM((tm, d), F32)],
        compiler_params=_params("arbitrary", "arbitrary"),
        name="dense_swiglu",
    )(h, g, wg, wu, wd)


def _side_cast_plan(weights, layer, steps, step_index):
    arrays = [w.reshape(w.shape[0], -1, w.shape[-1]) for w in weights]
    if not arrays:
        return [], [], [], []
    nblk = next(n for n in range(steps, 0, -1)
                if all(a.shape[1] % n == 0 and (a.shape[1] // n) % 16 == 0 for a in arrays))
    blk = lambda *g: jnp.minimum(step_index(*g), nblk - 1)
    in_specs, out_specs, out_shapes = [], [], []
    for a in arrays:
        rows, cols = a.shape[1:]
        in_specs.append(pl.BlockSpec((1, rows // nblk, cols), lambda *g: (layer, blk(*g), 0)))
        out_specs.append(pl.BlockSpec((rows // nblk, cols), lambda *g: (blk(*g), 0)))
        out_shapes.append(jax.ShapeDtypeStruct((rows, cols), BF16))
    return arrays, in_specs, out_specs, out_shapes


def _side_cast(in_refs, out_refs):
    for src, dst in zip(in_refs, out_refs):
        dst[...] = src[0].astype(BF16)


def _conv_kernel(h_ref, g_ref, w1_ref, b1_ref, dw_ref, bdw_ref, lng_ref, lnb_ref, w2_ref, b2_ref, *rest,
                 ts, width, rb, n_side):
    side_in, o_ref, side_out, (gbuf, cbuf) = rest[:n_side], rest[n_side], rest[n_side + 1:-2], rest[-2:]
    _side_cast(side_in, side_out)
    j = pl.program_id(1)
    d = h_ref.shape[1]
    h = h_ref[...]
    u = _rms(h, g_ref[...]).astype(BF16)
    a = _dot(u, w1_ref[...]) + b1_ref[...]
    glu = a[:, :d] * jax.nn.sigmoid(a[:, d:])

    keep = CONV_HALO + SUBLANES

    @pl.when(j == 0)
    def _():
        gbuf[:, 0:keep, :] = jnp.zeros((SUBLANES, keep, d), F32)

    for b in range(SUBLANES):
        gbuf[b, CONV_HALO + b:CONV_HALO + b + ts, :] = glu
    for cb in range(d // LANES):
        cs = slice(cb * LANES, (cb + 1) * LANES)
        for r0 in range(0, ts, rb):
            acc = jnp.zeros((rb, LANES), F32)
            for k in range(width):
                a, b = divmod(width - 1 - k, SUBLANES)
                off = CONV_HALO + r0 - SUBLANES * a
                acc = acc + gbuf[b, off:off + rb, cs] * dw_ref[k:k + 1, cs]
            cbuf[r0:r0 + rb, cs] = acc + bdw_ref[:, cs]
    gbuf[:, 0:keep, :] = gbuf[:, ts:ts + keep, :]

    c = cbuf[...]
    mu = jnp.mean(c, axis=-1, keepdims=True)
    cc = c - mu
    var = jnp.mean(cc * cc, axis=-1, keepdims=True)
    un = cc * lax.rsqrt(var + LN_EPS) * lng_ref[...] + lnb_ref[...]
    un = un * jax.nn.sigmoid(un)
    o_ref[...] = h + _dot(un.astype(BF16), w2_ref[...]) + b2_ref[...]


def _conv_layer(h, g, w1, b1, dw, bdw, lng, lnb, w2, b2, side_weights, side_layer, *, batch, seq):
    n, d = h.shape
    width = dw.shape[0]
    assert width - 1 <= CONV_HALO
    ts = min(256, seq)
    rb = min(128, ts)
    nt = seq // ts
    row = lambda b, j: (b * nt + j, 0)
    c2 = lambda b, j: (0, 0)
    side, side_in, side_out, side_shapes = _side_cast_plan(side_weights, side_layer, batch * nt,
                                                           lambda b, j: b * nt + j)
    outs = pl.pallas_call(
        functools.partial(_conv_kernel, ts=ts, width=width, rb=rb, n_side=len(side)),
        grid=(batch, nt),
        in_specs=[pl.BlockSpec((ts, d), row),
                  pl.BlockSpec((1, d), c2),
                  pl.BlockSpec(w1.shape, c2),
                  pl.BlockSpec(b1.shape, c2),
                  pl.BlockSpec(dw.shape, c2),
                  pl.BlockSpec((1, d), c2),
                  pl.BlockSpec((1, d), c2),
                  pl.BlockSpec((1, d), c2),
                  pl.BlockSpec(w2.shape, c2),
                  pl.BlockSpec((1, d), c2)] + side_in,
        out_specs=[pl.BlockSpec((ts, d), row)] + side_out,
        out_shape=[jax.ShapeDtypeStruct((n, d), F32)] + side_shapes,
        scratch_shapes=[pltpu.VMEM((SUBLANES, ts + CONV_HALO + SUBLANES, d), F32), pltpu.VMEM((ts, d), F32)],
        compiler_params=_params("arbitrary", "arbitrary"),
        name="conv_module",
    )(h, g, w1, b1, dw, bdw, lng, lnb, w2, b2, *side)
    return outs[0], [o.reshape(w.shape[1:]) for o, w in zip(outs[1:], side_weights)]


def _mla_proj_kernel(h_ref, pos_ref, g_ref, wqa_ref, qnorm_ref, wqb_ref, wkva_ref, kvnorm_ref, wkvb_ref,
                     invf_ref, sgn_ref, qt_o, k_o, vt_o, *, heads, lora):
    hn = heads * NOPE
    rows = h_ref.shape[0]
    u = _rms(h_ref[...], g_ref[...]).astype(BF16)
    qa = _rms(_dot(u, wqa_ref[...]), qnorm_ref[...]).astype(BF16)
    q = _dot(qa, wqb_ref[...])
    kva = _dot(u, wkva_ref[...])
    ckv = _rms(kva[:, :lora], kvnorm_ref[...]).astype(BF16)
    kv = _dot(ckv, wkvb_ref[...])
    ang = pos_ref[...].astype(F32) * invf_ref[...]
    c = jnp.cos(ang)
    s = jnp.sin(ang) * sgn_ref[...]
    lane = lax.broadcasted_iota(jnp.int32, (rows, LANES), 1)

    def rope(x):
        swapped = jnp.where(lane < ROPE // 2, pltpu.roll(x, LANES - ROPE // 2, 1), pltpu.roll(x, ROPE // 2, 1))
        return x * c + swapped * s

    kr = rope(kva[:, lora:lora + LANES]).astype(BF16)
    ones = jnp.ones((V_ROWS - NOPE, rows), BF16)
    for hd in range(heads):
        ns = slice(hd * NOPE, (hd + 1) * NOPE)
        qr = rope(q[:, hn + hd * LANES:hn + (hd + 1) * LANES])
        qt_o[2 * hd * NOPE:(2 * hd + 1) * NOPE, :] = q[:, ns].T.astype(BF16)
        qt_o[(2 * hd + 1) * NOPE:(2 * hd + 2) * NOPE, :] = qr.T.astype(BF16)
        k_o[:, 2 * hd * NOPE:(2 * hd + 1) * NOPE] = kv[:, ns].astype(BF16)
        k_o[:, (2 * hd + 1) * NOPE:(2 * hd + 2) * NOPE] = kr
        vt_o[hd * V_ROWS:hd * V_ROWS + NOPE, :] = kv[:, hn + hd * NOPE:hn + (hd + 1) * NOPE].T.astype(BF16)
        vt_o[hd * V_ROWS + NOPE:(hd + 1) * V_ROWS, :] = ones


def _attn_kernel(qi_ref, kj_ref, qt_ref, k_ref, vt_ref, h_ref, wot_ref, *rest, heads, tq, tk, scale, n_side):
    side_in, o_ref, side_out, (m_sc, acc_sc) = rest[:n_side], rest[n_side], rest[n_side + 1:-2], rest[-2:]
    _side_cast(side_in, side_out)
    i = qi_ref[pl.program_id(1)]
    j = kj_ref[pl.program_id(1)]
    c = scale * LOG2E

    @pl.when(j == 0)
    def _():
        m_sc[...] = jnp.full(m_sc.shape, -jnp.inf, F32)
        acc_sc[...] = jnp.zeros(acc_sc.shape, F32)

    def step(diagonal):
        if diagonal:
            causal = (lax.broadcasted_iota(jnp.int32, (tk, tq), 0)
                      <= lax.broadcasted_iota(jnp.int32, (tk, tq), 1))
        es = [slice(2 * hd * NOPE, (2 * hd + 2) * NOPE) for hd in range(heads)]
        vs = [slice(hd * V_ROWS, (hd + 1) * V_ROWS) for hd in range(heads)]
        ss = [_dot(k_ref[:, e], qt_ref[e, :]) for e in es]
        if diagonal:
            ss = [jnp.where(causal, s, -jnp.inf) for s in ss]
        m_prev = [m_sc[hd] for hd in range(heads)]
        m_new = [jnp.maximum(mp, jnp.max(s, axis=0, keepdims=True)) for mp, s in zip(m_prev, ss)]
        ps = [jnp.exp2((s - mn) * c).astype(BF16) for s, mn in zip(ss, m_new)]
        pv = [_dot(vt_ref[v, :], p) for p, v in zip(ps, vs)]
        for hd in range(heads):
            alpha = jnp.exp2((m_prev[hd] - m_new[hd]) * c)
            acc_sc[vs[hd], :] = alpha * acc_sc[vs[hd], :] + pv[hd]
            m_sc[hd] = m_new[hd]

    @pl.when(j < i)
    def _():
        step(False)

    @pl.when(j == i)
    def _():
        step(True)
        parts = [acc_sc[hd * V_ROWS:hd * V_ROWS + NOPE, :] / acc_sc[hd * V_ROWS + NOPE:hd * V_ROWS + NOPE + 1, :]
                 for hd in range(heads)]
        ot = jnp.concatenate(parts, axis=0).astype(BF16)
        o_ref[...] = h_ref[...] + _dot(wot_ref[...], ot).T


def _mla_layer(h, positions, g, wqa, qnorm, wqb, wkva, kvnorm, wkvb, wot, invf, sgn, side_weights, side_layer,
               *, batch, seq, heads, lora):
    n, d = h.shape
    hn = heads * NOPE
    tp = min(256, n)
    c2 = lambda i: (0, 0)
    row = lambda i: (i, 0)
    col = lambda i: (0, i)
    qt, k, vt = pl.pallas_call(
        functools.partial(_mla_proj_kernel, heads=heads, lora=lora),
        grid=(n // tp,),
        in_specs=[pl.BlockSpec((tp, d), row),
                  pl.BlockSpec((tp, 1), row),
                  pl.BlockSpec((1, d), c2),
                  pl.BlockSpec(wqa.shape, c2),
                  pl.BlockSpec(qnorm.shape, c2),
                  pl.BlockSpec(wqb.shape, c2),
                  pl.BlockSpec(wkva.shape, c2),
                  pl.BlockSpec(kvnorm.shape, c2),
                  pl.BlockSpec(wkvb.shape, c2),
                  pl.BlockSpec((1, LANES), c2),
                  pl.BlockSpec((1, LANES), c2)],
        out_specs=[pl.BlockSpec((2 * hn, tp), col), pl.BlockSpec((tp, 2 * hn), row),
                   pl.BlockSpec((heads * V_ROWS, tp), col)],
        out_shape=[jax.ShapeDtypeStruct((2 * hn, n), BF16), jax.ShapeDtypeStruct((n, 2 * hn), BF16),
                   jax.ShapeDtypeStruct((heads * V_ROWS, n), BF16)],
        compiler_params=_params("arbitrary"),
        name="mla_proj",
    )(h, positions, g, wqa, qnorm, wqb, wkva, kvnorm, wkvb, invf, sgn)

    tq = tk = min(512, seq)
    nq = seq // tq
    scale = float(NOPE + ROPE) ** -0.5
    pairs = [(i, j) for i in range(nq) for j in range(i + 1)]
    qi = jnp.asarray([p[0] for p in pairs], jnp.int32)
    kj = jnp.asarray([p[1] for p in pairs], jnp.int32)
    npairs = len(pairs)
    qrow = lambda b, p, qi, kj: (b * nq + qi[p], 0)
    qcol = lambda b, p, qi, kj: (0, b * nq + qi[p])
    krow = lambda b, p, qi, kj: (b * nq + kj[p], 0)
    kcol = lambda b, p, qi, kj: (0, b * nq + kj[p])
    side, side_in, side_out, side_shapes = _side_cast_plan(side_weights, side_layer, batch * npairs,
                                                           lambda b, p, qi, kj: b * npairs + p)
    grid_spec = pltpu.PrefetchScalarGridSpec(
        num_scalar_prefetch=2,
        grid=(batch, npairs),
        in_specs=[pl.BlockSpec((2 * hn, tq), qcol),
                  pl.BlockSpec((tk, 2 * hn), krow),
                  pl.BlockSpec((heads * V_ROWS, tk), kcol),
                  pl.BlockSpec((tq, d), qrow),
                  pl.BlockSpec(wot.shape, lambda b, p, qi, kj: (0, 0))] + side_in,
        out_specs=[pl.BlockSpec((tq, d), qrow)] + side_out,
        scratch_shapes=[pltpu.VMEM((heads, 1, tq), F32), pltpu.VMEM((heads * V_ROWS, tq), F32)],
    )
    outs = pl.pallas_call(
        functools.partial(_attn_kernel, heads=heads, tq=tq, tk=tk, scale=scale, n_side=len(side)),
        grid_spec=grid_spec,
        out_shape=[jax.ShapeDtypeStruct((n, d), F32)] + side_shapes,
        compiler_params=_params("arbitrary", "arbitrary"),
        name="mla_attention",
    )(qi, kj, qt, k, vt, h, wot, *side)
    return outs[0], [o.reshape(w.shape[1:]) for o, w in zip(outs[1:], side_weights)]


def _store_row_tiles(ref, value, rows):
    slabs = value.shape[1] // LANES
    for s in range(slabs):
        ref[pl.ds(s, rows, stride=slabs), :] = value[:, s * LANES:(s + 1) * LANES]


def _load_row_tile_slab(ref, s, rows, slabs):
    return ref[pl.ds(s, rows, stride=slabs), :]


def _route_kernel(h_ref, g_ref, r_ref, u_ref, info_ref, cnt_ref, carry, *, tr, n_exp):
    i = pl.program_id(0)

    @pl.when(i == 0)
    def _():
        carry[...] = jnp.zeros(carry.shape, F32)

    u = _rms(h_ref[...], g_ref[...])
    _store_row_tiles(u_ref, u, tr)
    logits = _dot_3pass(u, r_ref[...])
    lane = lax.broadcasted_iota(jnp.int32, (tr, LANES), 1)
    lg = jnp.where(lane < n_exp, logits, -jnp.inf)
    m1 = jnp.max(lg, axis=1, keepdims=True)
    i1 = jnp.min(jnp.where(lg == m1, lane, LANES), axis=1, keepdims=True)
    lg2 = jnp.where(lane == i1, -jnp.inf, lg)
    m2 = jnp.max(lg2, axis=1, keepdims=True)
    i2 = jnp.min(jnp.where(lg2 == m2, lane, LANES), axis=1, keepdims=True)
    e = jnp.exp(m2 - m1)
    g1 = 1.0 / (1.0 + e)
    g2 = e / (1.0 + e)
    sel1 = lane == i1
    sel2 = lane == i2
    onehot = jnp.where(sel1 | sel2, 1.0, 0.0)
    rr = lax.broadcasted_iota(jnp.int32, (tr, tr), 0)
    cc = lax.broadcasted_iota(jnp.int32, (tr, tr), 1)
    tri = jnp.where(cc < rr, 1.0, 0.0).astype(BF16)
    before = _dot(tri, onehot.astype(BF16)) + carry[0:1, :]
    rank1 = jnp.sum(jnp.where(sel1, before, 0.0), axis=1, keepdims=True)
    rank2 = jnp.sum(jnp.where(sel2, before, 0.0), axis=1, keepdims=True)
    carry[...] = carry[...] + jnp.sum(onehot, axis=0, keepdims=True)
    cols = (i1.astype(F32), i2.astype(F32), g1, g2, rank1, rank2)
    info = jnp.zeros((tr, LANES), F32)
    for k, col in enumerate(cols):
        info = jnp.where(lane == k, col, info)
    info_ref[...] = info
    cnt_ref[...] = carry[...]


def _route(h, g, router_padded, n_exp):
    n, d = h.shape
    tr = min(512, n)
    slabs = d // LANES
    return pl.pallas_call(
        functools.partial(_route_kernel, tr=tr, n_exp=n_exp),
        grid=(n // tr,),
        in_specs=[pl.BlockSpec((tr, d), lambda i: (i, 0)),
                  pl.BlockSpec((1, d), lambda i: (0, 0)),
                  pl.BlockSpec((d, LANES), lambda i: (0, 0))],
        out_specs=[pl.BlockSpec((tr * slabs, LANES), lambda i: (i, 0)),
                   pl.BlockSpec((tr, LANES), lambda i: (i, 0)),
                   pl.BlockSpec((8, LANES), lambda i: (0, 0))],
        out_shape=[jax.ShapeDtypeStruct((n * slabs, LANES), F32),
                   jax.ShapeDtypeStruct((n, LANES), F32),
                   jax.ShapeDtypeStruct((8, LANES), F32)],
        scratch_shapes=[pltpu.VMEM((8, LANES), F32)],
        compiler_params=_params("arbitrary"),
        name="moe_route",
    )(h, g, router_padded)


def _gather_row_tiles(idx_ref, k, src_ref, dst_ref, sem, rows, slabs):
    def start8(r8, c):
        for q in range(GATHER_UNROLL):
            r = r8 * GATHER_UNROLL + q
            src_row = pl.multiple_of(idx_ref[0, k, r] * slabs, slabs)
            dst_row = pl.multiple_of(r * slabs, slabs)
            pltpu.make_async_copy(src_ref.at[pl.ds(src_row, slabs)], dst_ref.at[pl.ds(dst_row, slabs)],
                                  sem).start(priority=q % 2)
        return c

    lax.fori_loop(0, rows // GATHER_UNROLL, start8, 0)
    pltpu.make_async_copy(src_ref.at[pl.ds(0, rows * slabs)], dst_ref, sem).wait()


def _dispatch_kernel(pad_start_ref, pad_cnt_ref, pos_ref, u_ref, x_ref, zero_sc, sem, *, rows, slabs, n_exp):
    def row_copy(src, src_row, dst_row):
        return pltpu.make_async_copy(src.at[pl.ds(pl.multiple_of(src_row * slabs, slabs), slabs)],
                                     x_ref.at[pl.ds(pl.multiple_of(dst_row * slabs, slabs), slabs)], sem)

    @pl.when(pl.program_id(0) == 0)
    def _():
        zero_sc[...] = jnp.zeros(zero_sc.shape, F32)
        for e in range(n_exp):
            def fill(i, c):
                row_copy(zero_sc, 0, pad_start_ref[e] + i).start()
                return c

            def drain(i, c):
                row_copy(zero_sc, 0, pad_start_ref[e] + i).wait()
                return c

            lax.fori_loop(0, pad_cnt_ref[e], fill, 0)
            lax.fori_loop(0, pad_cnt_ref[e], drain, 0)

        def chunk_copy(i):
            dst = pl.multiple_of((pad_start_ref[n_exp] + i * ZERO_CHUNK) * slabs, slabs)
            return pltpu.make_async_copy(zero_sc, x_ref.at[pl.ds(dst, ZERO_CHUNK * slabs)], sem)

        def fill_tail(i, c):
            chunk_copy(i).start()
            return c

        def drain_tail(i, c):
            chunk_copy(i).wait()
            return c

        lax.fori_loop(0, pad_cnt_ref[n_exp], fill_tail, 0)
        lax.fori_loop(0, pad_cnt_ref[n_exp], drain_tail, 0)

    def start8(r8, c):
        for q in range(GATHER_UNROLL):
            r = r8 * GATHER_UNROLL + q
            for k in range(TOP_K):
                row_copy(u_ref, r, pos_ref[0, k, r]).start(priority=k)
        return c

    lax.fori_loop(0, rows // GATHER_UNROLL, start8, 0)
    for k in range(TOP_K):
        pltpu.make_async_copy(u_ref, x_ref.at[pl.ds(0, rows * slabs)], sem).wait()


def _dispatch(u, pos_t, pad_start, pad_cnt, sorted_rows, tile, slabs):
    n = u.shape[0] // slabs
    grid_spec = pltpu.PrefetchScalarGridSpec(
        num_scalar_prefetch=2,
        grid=(n // tile,),
        in_specs=[pl.BlockSpec((1, TOP_K, tile), lambda i, ps, pc: (i, 0, 0), memory_space=pltpu.SMEM),
                  pl.BlockSpec((tile * slabs, LANES), lambda i, ps, pc: (i, 0))],
        out_specs=pl.BlockSpec(memory_space=pl.ANY),
        scratch_shapes=[pltpu.VMEM((ZERO_CHUNK * slabs, LANES), F32), pltpu.SemaphoreType.DMA(())],
    )
    return pl.pallas_call(
        functools.partial(_dispatch_kernel, rows=tile, slabs=slabs, n_exp=pad_start.shape[0] - 1),
        grid_spec=grid_spec,
        out_shape=jax.ShapeDtypeStruct((sorted_rows * slabs, LANES), F32),
        compiler_params=_params("arbitrary"),
        name="moe_dispatch",
    )(pad_start, pad_cnt, pos_t, u)


def _moe_kernel(te_ref, nu_ref, x_ref, wg_ref, wu_ref, wd_ref, o_ref, x_sc, acc_sc, *, tm, slabs, nj):
    t = pl.program_id(0)
    j = pl.program_id(1)

    @pl.when(t < nu_ref[0])
    def _():
        @pl.when(j == 0)
        def _():
            for s in range(slabs):
                x_sc[:, s * LANES:(s + 1) * LANES] = _load_row_tile_slab(x_ref, s, tm, slabs).astype(BF16)

        _accumulate_steps(j, nj, lambda: _swiglu_step(x_sc[...], wg_ref[0], wu_ref[0], wd_ref[0]), acc_sc,
                          lambda r: _store_row_tiles(o_ref, r, tm))

    @pl.when((t >= nu_ref[0]) & (j == 0))
    def _():
        o_ref[...] = jnp.zeros(o_ref.shape, F32)


def _moe_experts(x_sorted, tile_expert, n_used, wg, wu, wd, tm, slabs):
    d = slabs * LANES
    rows = x_sorted.shape[0] // slabs
    f = wg.shape[2]
    tf = _ff_chunk(f)
    nj = f // tf
    tmap = lambda t, j, te, nu: (jnp.minimum(t, nu[0] - 1), 0)
    jsel = lambda t, j, nu: jnp.where(t < nu[0], j, nj - 1)
    grid_spec = pltpu.PrefetchScalarGridSpec(
        num_scalar_prefetch=2,
        grid=(rows // tm, nj),
        in_specs=[pl.BlockSpec((tm * slabs, LANES), tmap),
                  pl.BlockSpec((1, d, tf), lambda t, j, te, nu: (te[t], 0, jsel(t, j, nu))),
                  pl.BlockSpec((1, d, tf), lambda t, j, te, nu: (te[t], 0, jsel(t, j, nu))),
                  pl.BlockSpec((1, tf, d), lambda t, j, te, nu: (te[t], jsel(t, j, nu), 0))],
        out_specs=pl.BlockSpec((tm * slabs, LANES), lambda t, j, te, nu: (t, 0)),
        scratch_shapes=[pltpu.VMEM((tm, d), BF16), pltpu.VMEM((tm, d), F32)],
    )
    return pl.pallas_call(
        functools.partial(_moe_kernel, tm=tm, slabs=slabs, nj=nj),
        grid_spec=grid_spec,
        out_shape=jax.ShapeDtypeStruct((rows * slabs, LANES), F32),
        compiler_params=_params("arbitrary", "arbitrary"),
        name="moe_experts",
    )(tile_expert, n_used, x_sorted, wg, wu, wd)


def _combine_kernel(pos_ref, h_ref, gate_ref, gf_ref, y_ref, o_ref, y0, y1, sem, *, rows, slabs, final_norm):
    _gather_row_tiles(pos_ref, 0, y_ref, y0, sem.at[0], rows, slabs)
    _gather_row_tiles(pos_ref, 1, y_ref, y1, sem.at[1], rows, slabs)
    gates = gate_ref[...]
    g0 = gates[:, 2:3]
    g1 = gates[:, 3:4]
    ssq = jnp.zeros((rows, 1), F32)
    for s in range(slabs):
        cs = slice(s * LANES, (s + 1) * LANES)
        out = (h_ref[:, cs] + g0 * _load_row_tile_slab(y0, s, rows, slabs)
               + g1 * _load_row_tile_slab(y1, s, rows, slabs))
        o_ref[:, cs] = out
        ssq = ssq + jnp.sum(out * out, axis=-1, keepdims=True)
    if final_norm:
        d = slabs * LANES
        o_ref[...] = o_ref[...] * lax.rsqrt(ssq / d + NORM_EPS) * gf_ref[...]


def _combine(h, info, pos_t, y_sorted, g_final, final_norm, slabs):
    n, d = h.shape
    tile = pos_t.shape[2]
    return pl.pallas_call(
        functools.partial(_combine_kernel, rows=tile, slabs=slabs, final_norm=final_norm),
        grid=(n // tile,),
        in_specs=[pl.BlockSpec((1, TOP_K, tile), lambda i: (i, 0, 0), memory_space=pltpu.SMEM),
                  pl.BlockSpec((tile, d), lambda i: (i, 0)),
                  pl.BlockSpec((tile, LANES), lambda i: (i, 0)),
                  pl.BlockSpec((1, d), lambda i: (0, 0)),
                  pl.BlockSpec(memory_space=pl.ANY)],
        out_specs=pl.BlockSpec((tile, d), lambda i: (i, 0)),
        out_shape=jax.ShapeDtypeStruct((n, d), F32),
        scratch_shapes=[pltpu.VMEM((tile * slabs, LANES), F32), pltpu.VMEM((tile * slabs, LANES), F32),
                        pltpu.SemaphoreType.DMA((2,))],
        compiler_params=_params("arbitrary"),
        name="moe_combine",
    )(pos_t, h, info, g_final, y_sorted)


def _moe_layer(h, g, router, wg, wu, wd, g_final, final_norm):
    n, d = h.shape
    n_exp = router.shape[1]
    tm = min(512, n)
    router_padded = jnp.pad(router, ((0, 0), (0, LANES - n_exp)))
    u, info, cnt = _route(h, g, router_padded, n_exp)

    idx = info[:, 0:TOP_K].astype(jnp.int32)
    rank = info[:, 4:4 + TOP_K].astype(jnp.int32)
    counts = cnt[0, :n_exp].astype(jnp.int32)
    padded = (counts + tm - 1) // tm * tm
    ends = jnp.cumsum(padded)
    starts = ends - padded
    pos = jnp.sum(jnp.where(idx[..., None] == jnp.arange(n_exp), starts, 0), axis=-1) + rank
    rows = TOP_K * n + n_exp * tm
    n_used = (ends[-1] // tm).astype(jnp.int32)
    tile_ids = jnp.minimum(jnp.arange(rows // tm, dtype=jnp.int32), n_used - 1)
    tile_expert = jnp.sum(tile_ids[:, None] * tm >= ends[None, :], axis=1).astype(jnp.int32)

    slabs = d // LANES
    assert slabs == 8
    tile = min(512, n)
    pos_t = pos.T.reshape(TOP_K, n // tile, tile).transpose(1, 0, 2)
    assert tm % ZERO_CHUNK == 0
    pad_start = jnp.concatenate([starts + counts, ends[-1:]]).astype(jnp.int32)
    pad_cnt = jnp.concatenate([padded - counts, (rows - ends[-1:]) // ZERO_CHUNK]).astype(jnp.int32)
    x_sorted = _dispatch(u, pos_t, pad_start, pad_cnt, rows, tile, slabs)
    y_sorted = _moe_experts(x_sorted, tile_expert, n_used.reshape(1), wg, wu, wd, tm, slabs)
    return _combine(h, info, pos_t, y_sorted, g_final, final_norm, slabs)


def _mla_weights(wq_b, wkv_a, wkv_b, heads, lora):
    def padded(w):
        return jnp.concatenate([w, jnp.zeros(w.shape[:-1] + (LANES - ROPE,), w.dtype)], axis=-1)

    rq = wq_b.shape[0]
    wq = wq_b.reshape(rq, heads, NOPE + ROPE)
    wqb = jnp.concatenate([wq[..., :NOPE].reshape(rq, -1), padded(wq[..., NOPE:]).reshape(rq, -1)], axis=1)
    wkva = jnp.concatenate([wkv_a[:, :lora], padded(wkv_a[:, lora:])], axis=1)
    wkv = wkv_b.reshape(lora, heads, 2 * NOPE)
    wkvb = jnp.concatenate([wkv[..., :NOPE].reshape(lora, -1), wkv[..., NOPE:].reshape(lora, -1)], axis=1)
    return wqb.astype(BF16), wkva.astype(BF16), wkvb.astype(BF16)


def _cast_kernel(x_ref, o_ref):
    o_ref[...] = x_ref[0].astype(BF16)


def _to_bf16(w, layer):
    cols = w.shape[-1]
    w3 = w.reshape(w.shape[0], -1, cols)
    rows = w3.shape[1]
    cap = 256 if cols > 2048 else 1024
    tr = next(t for t in (1024, 512, 256, 128, 64, 32, 16, 8) if t <= cap and rows % t == 0)
    out = pl.pallas_call(
        _cast_kernel,
        grid=(rows // tr,),
        in_specs=[pl.BlockSpec((1, tr, cols), lambda i: (layer, i, 0))],
        out_specs=pl.BlockSpec((tr, cols), lambda i: (i, 0)),
        out_shape=jax.ShapeDtypeStruct((rows, cols), BF16),
        compiler_params=_params("arbitrary"),
        name="weights_to_bf16",
    )(w3)
    return out.reshape(w.shape[1:])


def _final_norm_kernel(h_ref, g_ref, o_ref):
    o_ref[...] = _rms(h_ref[...], g_ref[...])


def _final_norm(h, g):
    n, d = h.shape
    tm = min(512, n)
    return pl.pallas_call(
        _final_norm_kernel,
        grid=(n // tm,),
        in_specs=[pl.BlockSpec((tm, d), lambda i: (i, 0)), pl.BlockSpec((1, d), lambda i: (0, 0))],
        out_specs=pl.BlockSpec((tm, d), lambda i: (i, 0)),
        out_shape=jax.ShapeDtypeStruct((n, d), F32),
        compiler_params=_params("arbitrary"),
        name="final_norm",
    )(h, g)


@jax.jit
def _trunk(x, positions, norm_mix, norm_ffn, norm_final, pool_w, pool_scale,
           conv_w1, conv_b1, conv_dw, conv_bdw, conv_ln_g, conv_ln_b, conv_w2, conv_b2,
           mla_wq_a, mla_q_norm, mla_wq_b, mla_wkv_a, mla_kv_norm, mla_wkv_b, mla_wo,
           ffn_wg, ffn_wu, ffn_wd, moe_router, moe_wg, moe_wu, moe_wd):
    batch, seq, d = x.shape
    depth = norm_mix.shape[0]
    lora = mla_kv_norm.shape[1]
    heads = mla_wo.shape[1] // NOPE
    assert mla_wq_b.shape[2] == heads * (NOPE + ROPE) and mla_wkv_a.shape[2] == lora + ROPE
    n = batch * seq
    h = x.reshape(n, d)
    pos = positions.reshape(n, 1)
    row = lambda v: v.reshape(1, -1)

    half = jnp.arange(0, ROPE, 2, dtype=F32) / ROPE
    inv_freq = ROPE_THETA ** (-half)
    invf = row(jnp.concatenate([inv_freq, inv_freq, jnp.zeros((LANES - ROPE,), F32)]))
    sgn = row(jnp.concatenate([-jnp.ones((ROPE // 2,), F32), jnp.ones((ROPE // 2,), F32),
                               jnp.zeros((LANES - ROPE,), F32)]))

    moe_w = (moe_wg, moe_wu, moe_wd)
    moe_bf16 = {}

    def moe_cast_job(i):
        nxt = i if i % 2 == 1 else i + 1
        if nxt < depth and nxt // 2 not in moe_bf16:
            return moe_w, nxt // 2
        return (), 0

    for i in range(depth):
        kind, j = i % 3, i // 3
        g = row(norm_mix[i])
        if kind == 0:
            h = _pool_layer(h, g, pool_w[j].astype(BF16), row(pool_scale[j]), batch=batch, seq=seq)
        elif kind == 1:
            side_w, side_layer = moe_cast_job(i)
            h, cast = _conv_layer(h, g, conv_w1[j].astype(BF16), row(conv_b1[j]), conv_dw[j], row(conv_bdw[j]),
                                  row(conv_ln_g[j]), row(conv_ln_b[j]), conv_w2[j].astype(BF16), row(conv_b2[j]),
                                  side_w, side_layer, batch=batch, seq=seq)
            if cast:
                moe_bf16[side_layer] = cast
        else:
            wqb, wkva, wkvb = _mla_weights(mla_wq_b[j], mla_wkv_a[j], mla_wkv_b[j], heads, lora)
            side_w, side_layer = moe_cast_job(i)
            h, cast = _mla_layer(h, pos, g, mla_wq_a[j].astype(BF16), row(mla_q_norm[j]), wqb, wkva,
                                 row(mla_kv_norm[j]), wkvb, mla_wo[j].T.astype(BF16), invf, sgn,
                                 side_w, side_layer, batch=batch, seq=seq, heads=heads, lora=lora)
            if cast:
                moe_bf16[side_layer] = cast
        g = row(norm_ffn[i])
        f_idx = i // 2
        last = i == depth - 1
        if i % 2 == 0:
            h = _ffn_layer(h, g, _to_bf16(ffn_wg, f_idx), _to_bf16(ffn_wu, f_idx), _to_bf16(ffn_wd, f_idx))
            if last:
                h = _final_norm(h, row(norm_final))
        else:
            wg, wu, wd = moe_bf16.get(f_idx) or [_to_bf16(w, f_idx) for w in moe_w]
            h = _moe_layer(h, g, moe_router[f_idx], wg, wu, wd, row(norm_final), last)
    return h.reshape(batch, seq, d)


def kernel(x, positions, norm_mix, norm_ffn, norm_final, pool_w, pool_scale, conv_w1, conv_b1, conv_dw, conv_bdw, conv_ln_g, conv_ln_b, conv_w2, conv_b2, mla_wq_a, mla_q_norm, mla_wq_b, mla_wkv_a, mla_kv_norm, mla_wkv_b, mla_wo, ffn_wg, ffn_wu, ffn_wd, moe_router, moe_wg, moe_wu, moe_wd):
    return _trunk(x, positions, norm_mix, norm_ffn, norm_final, pool_w, pool_scale,
                  conv_w1, conv_b1, conv_dw, conv_bdw, conv_ln_g, conv_ln_b, conv_w2, conv_b2,
                  mla_wq_a, mla_q_norm, mla_wq_b, mla_wkv_a, mla_kv_norm, mla_wkv_b, mla_wo,
                  ffn_wg, ffn_wu, ffn_wd, moe_router, moe_wg, moe_wu, moe_wd)
```

```python
import functools

import jax
import jax.numpy as jnp
from jax import lax
from jax.experimental import pallas as pl
from jax.experimental.pallas import tpu as pltpu

F32 = jnp.float32
BF16 = jnp.bfloat16

NORM_EPS = 1e-6
LN_EPS = 1e-5
ROPE_THETA = 10000.0
LOG2E = 1.4426950408889634

POOL_WINDOWS = (2, 4, 8, 16)
POOL_HALO = 16
CONV_HALO = 32
LANES = 128
SUBLANES = 8
POOL_TOP = POOL_HALO + SUBLANES
NOPE = 128
ROPE = 64
V_ROWS = NOPE + 16
TOP_K = 2
GATHER_UNROLL = 8
ZERO_CHUNK = 64
COMBINE_CHUNK = 64

VMEM_LIMIT = 56 * 1024 * 1024


def _params(*sem):
    return pltpu.CompilerParams(dimension_semantics=sem, vmem_limit_bytes=VMEM_LIMIT)


def _rms(x, g, eps=NORM_EPS):
    return x * lax.rsqrt(jnp.mean(x * x, axis=-1, keepdims=True) + eps) * g


def _dot(a, b):
    return jnp.dot(a, b, preferred_element_type=F32)


def _split_bf16(x):
    hi = x.astype(BF16)
    return hi, (x - hi.astype(F32)).astype(BF16)


def _dot_3pass(a, b):
    a_hi, a_lo = _split_bf16(a)
    b_hi, b_lo = _split_bf16(b)
    return _dot(a_hi, b_hi) + (_dot(a_hi, b_lo) + _dot(a_lo, b_hi))


def _ff_chunk(f):
    for c in (1792, 1024, 512, 256, 128):
        if f % c == 0:
            return c
    return f


def _pool_kernel(h_ref, g_ref, w_ref, sc_ref, o_ref, ubuf, lvl_a, lvl_b, *, ts, gdim):
    j = pl.program_id(1)
    h = h_ref[...]
    u = _rms(h, g_ref[...])
    ext = POOL_TOP + ts

    @pl.when(j == 0)
    def _():
        ubuf[0:POOL_TOP, :] = jnp.zeros((POOL_TOP, ubuf.shape[1]), F32)
        lvl_a[0:SUBLANES, :] = jnp.zeros((SUBLANES, gdim), F32)
        lvl_b[0:SUBLANES, :] = jnp.zeros((SUBLANES, gdim), F32)

    ubuf[POOL_TOP:ext, :] = u
    t1 = lax.broadcasted_iota(jnp.int32, (ts, 1), 0) + (j * ts + 1)
    for g, win in enumerate(POOL_WINDOWS):
        cs = slice(g * gdim, (g + 1) * gdim)
        src, cols, spare, k = ubuf, cs, [lvl_a, lvl_b], 1
        while k < win:
            s = src[SUBLANES:ext, cols] + src[SUBLANES - k:ext - k, cols]
            k *= 2
            if k < win:
                dst = spare.pop(0)
                dst[SUBLANES:ext, :] = s
                spare.append(dst)
                src, cols = dst, slice(None)
        s = s[POOL_HALO:POOL_HALO + ts]
        ug = u[:, cs]
        cnt = jnp.minimum(t1, win).astype(F32)
        pooled = s / cnt - ug
        m = _dot(pooled.astype(BF16), w_ref[g])
        o_ref[:, cs] = h[:, cs] + m * sc_ref[:, cs]
    ubuf[SUBLANES:POOL_TOP, :] = ubuf[ts + SUBLANES:ts + POOL_TOP, :]


def _pool_layer(h, g, w, scale, *, batch, seq):
    n, d = h.shape
    ts = min(512, seq)
    nt = seq // ts
    gdim = d // len(POOL_WINDOWS)
    row = lambda b, j: (b * nt + j, 0)
    const2 = lambda b, j: (0, 0)
    return pl.pallas_call(
        functools.partial(_pool_kernel, ts=ts, gdim=gdim),
        grid=(batch, nt),
        in_specs=[pl.BlockSpec((ts, d), row),
                  pl.BlockSpec((1, d), const2),
                  pl.BlockSpec(w.shape, lambda b, j: (0, 0, 0)),
                  pl.BlockSpec((1, d), const2)],
        out_specs=pl.BlockSpec((ts, d), row),
        out_shape=jax.ShapeDtypeStruct((n, d), F32),
        scratch_shapes=[pltpu.VMEM((ts + POOL_TOP, d), F32), pltpu.VMEM((ts + POOL_TOP, gdim), F32),
                        pltpu.VMEM((ts + POOL_TOP, gdim), F32)],
        compiler_params=_params("arbitrary", "arbitrary"),
        name="pool_mixer",
    )(h, g, w, scale)


def _swiglu_step(u, wg, wu, wd):
    a = _dot(u, wg)
    b = _dot(u, wu)
    hh = a * jax.nn.sigmoid(a) * b
    return _dot(hh.astype(BF16), wd)


def _accumulate_steps(j, nj, partial, acc_sc, finish):
    def run(first, last):
        r = partial()
        if not first:
            r = acc_sc[...] + r
        if last:
            finish(r)
        else:
            acc_sc[...] = r

    if nj == 1:
        run(True, True)
        return
    pl.when(j == 0)(functools.partial(run, True, False))
    if nj > 2:
        pl.when((j > 0) & (j < nj - 1))(functools.partial(run, False, False))
    pl.when(j == nj - 1)(functools.partial(run, False, True))


def _ffn_kernel(h_ref, g_ref, wg_ref, wu_ref, wd_ref, o_ref, u_sc, acc_sc, *, nj):
    j = pl.program_id(1)

    @pl.when(j == 0)
    def _():
        u_sc[...] = _rms(h_ref[...], g_ref[...]).astype(BF16)

    def finish(r):
        o_ref[...] = h_ref[...] + r

    _accumulate_steps(j, nj, lambda: _swiglu_step(u_sc[...], wg_ref[...], wu_ref[...], wd_ref[...]), acc_sc, finish)


def _ffn_layer(h, g, wg, wu, wd):
    n, d = h.shape
    f = wg.shape[1]
    tm = min(512, n)
    tf = _ff_chunk(f)
    return pl.pallas_call(
        functools.partial(_ffn_kernel, nj=f // tf),
        grid=(n // tm, f // tf),
        in_specs=[pl.BlockSpec((tm, d), lambda i, j: (i, 0)),
                  pl.BlockSpec((1, d), lambda i, j: (0, 0)),
                  pl.BlockSpec((d, tf), lambda i, j: (0, j)),
                  pl.BlockSpec((d, tf), lambda i, j: (0, j)),
                  pl.BlockSpec((tf, d), lambda i, j: (j, 0))],
        out_specs=pl.BlockSpec((tm, d), lambda i, j: (i, 0)),
        out_shape=jax.ShapeDtypeStruct((n, d), F32),
        scratch_shapes=[pltpu.VMEM((tm, d), BF16), pltpu.VMEM((tm, d), F32)],
        compiler_params=_params("arbitrary", "arbitrary"),
        name="dense_swiglu",
    )(h, g, wg, wu, wd)


def _side_cast_plan(weights, layer, steps, step_index):
    arrays = [w.reshape(w.shape[0], -1, w.shape[-1]) for w in weights]
    if not arrays:
        return [], [], [], []
    nblk = next(n for n in range(steps, 0, -1)
                if all(a.shape[1] % n == 0 and (a.shape[1] // n) % 16 == 0 for a in arrays))
    blk = lambda *g: jnp.minimum(step_index(*g), nblk - 1)
    in_specs, out_specs, out_shapes = [], [], []
    for a in arrays:
        rows, cols = a.shape[1:]
        in_specs.append(pl.BlockSpec((1, rows // nblk, cols), lambda *g: (layer, blk(*g), 0)))
        out_specs.append(pl.BlockSpec((rows // nblk, cols), lambda *g: (blk(*g), 0)))
        out_shapes.append(jax.ShapeDtypeStruct((rows, cols), BF16))
    return arrays, in_specs, out_specs, out_shapes


def _side_cast(in_refs, out_refs):
    for src, dst in zip(in_refs, out_refs):
        dst[...] = src[0].astype(BF16)


def _conv_kernel(h_ref, g_ref, w1_ref, b1_ref, dw_ref, bdw_ref, lng_ref, lnb_ref, w2_ref, b2_ref, *rest,
                 ts, width, rb, n_side):
    side_in, o_ref, side_out, (gbuf, cbuf) = rest[:n_side], rest[n_side], rest[n_side + 1:-2], rest[-2:]
    _side_cast(side_in, side_out)
    j = pl.program_id(1)
    d = h_ref.shape[1]
    h = h_ref[...]
    u = _rms(h, g_ref[...]).astype(BF16)
    a = _dot(u, w1_ref[...]) + b1_ref[...]
    glu = a[:, :d] * jax.nn.sigmoid(a[:, d:])

    keep = CONV_HALO + SUBLANES

    @pl.when(j == 0)
    def _():
        gbuf[:, 0:keep, :] = jnp.zeros((SUBLANES, keep, d), F32)

    for b in range(SUBLANES):
        gbuf[b, CONV_HALO + b:CONV_HALO + b + ts, :] = glu
    for cb in range(d // LANES):
        cs = slice(cb * LANES, (cb + 1) * LANES)
        for r0 in range(0, ts, rb):
            acc = jnp.zeros((rb, LANES), F32)
            for k in range(width):
                a, b = divmod(width - 1 - k, SUBLANES)
                off = CONV_HALO + r0 - SUBLANES * a
                acc = acc + gbuf[b, off:off + rb, cs] * dw_ref[k:k + 1, cs]
            cbuf[r0:r0 + rb, cs] = acc + bdw_ref[:, cs]
    gbuf[:, 0:keep, :] = gbuf[:, ts:ts + keep, :]

    c = cbuf[...]
    mu = jnp.mean(c, axis=-1, keepdims=True)
    cc = c - mu
    var = jnp.mean(cc * cc, axis=-1, keepdims=True)
    un = cc * lax.rsqrt(var + LN_EPS) * lng_ref[...] + lnb_ref[...]
    un = un * jax.nn.sigmoid(un)
    o_ref[...] = h + _dot(un.astype(BF16), w2_ref[...]) + b2_ref[...]


def _conv_layer(h, g, w1, b1, dw, bdw, lng, lnb, w2, b2, side_weights, side_layer, *, batch, seq):
    n, d = h.shape
    width = dw.shape[0]
    assert width - 1 <= CONV_HALO
    ts = min(256, seq)
    rb = min(128, ts)
    nt = seq // ts
    row = lambda b, j: (b * nt + j, 0)
    c2 = lambda b, j: (0, 0)
    side, side_in, side_out, side_shapes = _side_cast_plan(side_weights, side_layer, batch * nt,
                                                           lambda b, j: b * nt + j)
    outs = pl.pallas_call(
        functools.partial(_conv_kernel, ts=ts, width=width, rb=rb, n_side=len(side)),
        grid=(batch, nt),
        in_specs=[pl.BlockSpec((ts, d), row),
                  pl.BlockSpec((1, d), c2),
                  pl.BlockSpec(w1.shape, c2),
                  pl.BlockSpec(b1.shape, c2),
                  pl.BlockSpec(dw.shape, c2),
                  pl.BlockSpec((1, d), c2),
                  pl.BlockSpec((1, d), c2),
                  pl.BlockSpec((1, d), c2),
                  pl.BlockSpec(w2.shape, c2),
                  pl.BlockSpec((1, d), c2)] + side_in,
        out_specs=[pl.BlockSpec((ts, d), row)] + side_out,
        out_shape=[jax.ShapeDtypeStruct((n, d), F32)] + side_shapes,
        scratch_shapes=[pltpu.VMEM((SUBLANES, ts + CONV_HALO + SUBLANES, d), F32), pltpu.VMEM((ts, d), F32)],
        compiler_params=_params("arbitrary", "arbitrary"),
        name="conv_module",
    )(h, g, w1, b1, dw, bdw, lng, lnb, w2, b2, *side)
    return outs[0], [o.reshape(w.shape[1:]) for o, w in zip(outs[1:], side_weights)]


def _mla_proj_kernel(h_ref, pos_ref, g_ref, wqa_ref, qnorm_ref, wqb_ref, wkva_ref, kvnorm_ref, wkvb_ref,
                     invf_ref, sgn_ref, qt_o, k_o, vt_o, *, heads, lora):
    hn = heads * NOPE
    rows = h_ref.shape[0]
    u = _rms(h_ref[...], g_ref[...]).astype(BF16)
    qa = _rms(_dot(u, wqa_ref[...]), qnorm_ref[...]).astype(BF16)
    q = _dot(qa, wqb_ref[...])
    kva = _dot(u, wkva_ref[...])
    ckv = _rms(kva[:, :lora], kvnorm_ref[...]).astype(BF16)
    kv = _dot(ckv, wkvb_ref[...])
    ang = pos_ref[...].astype(F32) * invf_ref[...]
    c = jnp.cos(ang)
    s = jnp.sin(ang) * sgn_ref[...]
    lane = lax.broadcasted_iota(jnp.int32, (rows, LANES), 1)

    def rope(x):
        swapped = jnp.where(lane < ROPE // 2, pltpu.roll(x, LANES - ROPE // 2, 1), pltpu.roll(x, ROPE // 2, 1))
        return x * c + swapped * s

    kr = rope(kva[:, lora:lora + LANES]).astype(BF16)
    ones = jnp.ones((V_ROWS - NOPE, rows), BF16)
    for hd in range(heads):
        ns = slice(hd * NOPE, (hd + 1) * NOPE)
        qr = rope(q[:, hn + hd * LANES:hn + (hd + 1) * LANES])
        qt_o[2 * hd * NOPE:(2 * hd + 1) * NOPE, :] = q[:, ns].T.astype(BF16)
        qt_o[(2 * hd + 1) * NOPE:(2 * hd + 2) * NOPE, :] = qr.T.astype(BF16)
        k_o[:, 2 * hd * NOPE:(2 * hd + 1) * NOPE] = kv[:, ns].astype(BF16)
        k_o[:, (2 * hd + 1) * NOPE:(2 * hd + 2) * NOPE] = kr
        vt_o[hd * V_ROWS:hd * V_ROWS + NOPE, :] = kv[:, hn + hd * NOPE:hn + (hd + 1) * NOPE].T.astype(BF16)
        vt_o[hd * V_ROWS + NOPE:(hd + 1) * V_ROWS, :] = ones


def _attn_kernel(qi_ref, kj_ref, qt_ref, k_ref, vt_ref, h_ref, wot_ref, *rest, heads, tq, tk, scale, n_side):
    side_in, o_ref, side_out, (m_sc, acc_sc) = rest[:n_side], rest[n_side], rest[n_side + 1:-2], rest[-2:]
    _side_cast(side_in, side_out)
    i = qi_ref[pl.program_id(1)]
    j = kj_ref[pl.program_id(1)]
    c = scale * LOG2E

    @pl.when(j == 0)
    def _():
        m_sc[...] = jnp.full(m_sc.shape, -jnp.inf, F32)
        acc_sc[...] = jnp.zeros(acc_sc.shape, F32)

    def step(diagonal):
        if diagonal:
            causal = (lax.broadcasted_iota(jnp.int32, (tk, tq), 0)
                      <= lax.broadcasted_iota(jnp.int32, (tk, tq), 1))
        es = [slice(2 * hd * NOPE, (2 * hd + 2) * NOPE) for hd in range(heads)]
        vs = [slice(hd * V_ROWS, (hd + 1) * V_ROWS) for hd in range(heads)]
        ss = [_dot(k_ref[:, e], qt_ref[e, :]) for e in es]
        if diagonal:
            ss = [jnp.where(causal, s, -jnp.inf) for s in ss]
        m_prev = [m_sc[hd] for hd in range(heads)]
        m_new = [jnp.maximum(mp, jnp.max(s, axis=0, keepdims=True)) for mp, s in zip(m_prev, ss)]
        ps = [jnp.exp2((s - mn) * c).astype(BF16) for s, mn in zip(ss, m_new)]
        pv = [_dot(vt_ref[v, :], p) for p, v in zip(ps, vs)]
        for hd in range(heads):
            alpha = jnp.exp2((m_prev[hd] - m_new[hd]) * c)
            acc_sc[vs[hd], :] = alpha * acc_sc[vs[hd], :] + pv[hd]
            m_sc[hd] = m_new[hd]

    @pl.when(j < i)
    def _():
        step(False)

    @pl.when(j == i)
    def _():
        step(True)
        parts = [acc_sc[hd * V_ROWS:hd * V_ROWS + NOPE, :] / acc_sc[hd * V_ROWS + NOPE:hd * V_ROWS + NOPE + 1, :]
                 for hd in range(heads)]
        ot = jnp.concatenate(parts, axis=0).astype(BF16)
        o_ref[...] = h_ref[...] + _dot(wot_ref[...], ot).T


def _mla_layer(h, positions, g, wqa, qnorm, wqb, wkva, kvnorm, wkvb, wot, invf, sgn, side_weights, side_layer,
               *, batch, seq, heads, lora):
    n, d = h.shape
    hn = heads * NOPE
    tp = min(256, n)
    c2 = lambda i: (0, 0)
    row = lambda i: (i, 0)
    col = lambda i: (0, i)
    qt, k, vt = pl.pallas_call(
        functools.partial(_mla_proj_kernel, heads=heads, lora=lora),
        grid=(n // tp,),
        in_specs=[pl.BlockSpec((tp, d), row),
                  pl.BlockSpec((tp, 1), row),
                  pl.BlockSpec((1, d), c2),
                  pl.BlockSpec(wqa.shape, c2),
                  pl.BlockSpec(qnorm.shape, c2),
                  pl.BlockSpec(wqb.shape, c2),
                  pl.BlockSpec(wkva.shape, c2),
                  pl.BlockSpec(kvnorm.shape, c2),
                  pl.BlockSpec(wkvb.shape, c2),
                  pl.BlockSpec((1, LANES), c2),
                  pl.BlockSpec((1, LANES), c2)],
        out_specs=[pl.BlockSpec((2 * hn, tp), col), pl.BlockSpec((tp, 2 * hn), row),
                   pl.BlockSpec((heads * V_ROWS, tp), col)],
        out_shape=[jax.ShapeDtypeStruct((2 * hn, n), BF16), jax.ShapeDtypeStruct((n, 2 * hn), BF16),
                   jax.ShapeDtypeStruct((heads * V_ROWS, n), BF16)],
        compiler_params=_params("arbitrary"),
        name="mla_proj",
    )(h, positions, g, wqa, qnorm, wqb, wkva, kvnorm, wkvb, invf, sgn)

    tq = tk = min(512, seq)
    nq = seq // tq
    scale = float(NOPE + ROPE) ** -0.5
    pairs = [(i, j) for i in range(nq) for j in range(i + 1)]
    qi = jnp.asarray([p[0] for p in pairs], jnp.int32)
    kj = jnp.asarray([p[1] for p in pairs], jnp.int32)
    npairs = len(pairs)
    qrow = lambda b, p, qi, kj: (b * nq + qi[p], 0)
    qcol = lambda b, p, qi, kj: (0, b * nq + qi[p])
    krow = lambda b, p, qi, kj: (b * nq + kj[p], 0)
    kcol = lambda b, p, qi, kj: (0, b * nq + kj[p])
    side, side_in, side_out, side_shapes = _side_cast_plan(side_weights, side_layer, batch * npairs,
                                                           lambda b, p, qi, kj: b * npairs + p)
    grid_spec = pltpu.PrefetchScalarGridSpec(
        num_scalar_prefetch=2,
        grid=(batch, npairs),
        in_specs=[pl.BlockSpec((2 * hn, tq), qcol),
                  pl.BlockSpec((tk, 2 * hn), krow),
                  pl.BlockSpec((heads * V_ROWS, tk), kcol),
                  pl.BlockSpec((tq, d), qrow),
                  pl.BlockSpec(wot.shape, lambda b, p, qi, kj: (0, 0))] + side_in,
        out_specs=[pl.BlockSpec((tq, d), qrow)] + side_out,
        scratch_shapes=[pltpu.VMEM((heads, 1, tq), F32), pltpu.VMEM((heads * V_ROWS, tq), F32)],
    )
    outs = pl.pallas_call(
        functools.partial(_attn_kernel, heads=heads, tq=tq, tk=tk, scale=scale, n_side=len(side)),
        grid_spec=grid_spec,
        out_shape=[jax.ShapeDtypeStruct((n, d), F32)] + side_shapes,
        compiler_params=_params("arbitrary", "arbitrary"),
        name="mla_attention",
    )(qi, kj, qt, k, vt, h, wot, *side)
    return outs[0], [o.reshape(w.shape[1:]) for o, w in zip(outs[1:], side_weights)]


def _store_row_tiles(ref, value, rows):
    slabs = value.shape[1] // LANES
    for s in range(slabs):
        ref[pl.ds(s, rows, stride=slabs), :] = value[:, s * LANES:(s + 1) * LANES]


def _load_row_tile_slab(ref, s, rows, slabs):
    return ref[pl.ds(s, rows, stride=slabs), :]


def _route_kernel(h_ref, g_ref, r_ref, u_ref, info_ref, cnt_ref, carry, *, tr, n_exp):
    i = pl.program_id(0)

    @pl.when(i == 0)
    def _():
        carry[...] = jnp.zeros(carry.shape, F32)

    u = _rms(h_ref[...], g_ref[...])
    _store_row_tiles(u_ref, u, tr)
    logits = _dot_3pass(u, r_ref[...])
    lane = lax.broadcasted_iota(jnp.int32, (tr, LANES), 1)
    lg = jnp.where(lane < n_exp, logits, -jnp.inf)
    m1 = jnp.max(lg, axis=1, keepdims=True)
    i1 = jnp.min(jnp.where(lg == m1, lane, LANES), axis=1, keepdims=True)
    lg2 = jnp.where(lane == i1, -jnp.inf, lg)
    m2 = jnp.max(lg2, axis=1, keepdims=True)
    i2 = jnp.min(jnp.where(lg2 == m2, lane, LANES), axis=1, keepdims=True)
    e = jnp.exp(m2 - m1)
    g1 = 1.0 / (1.0 + e)
    g2 = e / (1.0 + e)
    sel1 = lane == i1
    sel2 = lane == i2
    onehot = jnp.where(sel1 | sel2, 1.0, 0.0)
    rr = lax.broadcasted_iota(jnp.int32, (tr, tr), 0)
    cc = lax.broadcasted_iota(jnp.int32, (tr, tr), 1)
    tri = jnp.where(cc < rr, 1.0, 0.0).astype(BF16)
    before = _dot(tri, onehot.astype(BF16)) + carry[0:1, :]
    rank1 = jnp.sum(jnp.where(sel1, before, 0.0), axis=1, keepdims=True)
    rank2 = jnp.sum(jnp.where(sel2, before, 0.0), axis=1, keepdims=True)
    carry[...] = carry[...] + jnp.sum(onehot, axis=0, keepdims=True)
    cols = (i1.astype(F32), i2.astype(F32), g1, g2, rank1, rank2)
    info = jnp.zeros((tr, LANES), F32)
    for k, col in enumerate(cols):
        info = jnp.where(lane == k, col, info)
    info_ref[...] = info
    cnt_ref[...] = carry[...]


def _route(h, g, router_padded, n_exp):
    n, d = h.shape
    tr = min(512, n)
    slabs = d // LANES
    return pl.pallas_call(
        functools.partial(_route_kernel, tr=tr, n_exp=n_exp),
        grid=(n // tr,),
        in_specs=[pl.BlockSpec((tr, d), lambda i: (i, 0)),
                  pl.BlockSpec((1, d), lambda i: (0, 0)),
                  pl.BlockSpec((d, LANES), lambda i: (0, 0))],
        out_specs=[pl.BlockSpec((tr * slabs, LANES), lambda i: (i, 0)),
                   pl.BlockSpec((tr, LANES), lambda i: (i, 0)),
                   pl.BlockSpec((8, LANES), lambda i: (i, 0))],
        out_shape=[jax.ShapeDtypeStruct((n * slabs, LANES), F32),
                   jax.ShapeDtypeStruct((n, LANES), F32),
                   jax.ShapeDtypeStruct((n // tr * 8, LANES), F32)],
        scratch_shapes=[pltpu.VMEM((8, LANES), F32)],
        compiler_params=_params("arbitrary"),
        name="moe_route",
    )(h, g, router_padded)


def _dispatch_kernel(pad_start_ref, pad_cnt_ref, pos_ref, u_ref, x_ref, zero_sc, sem, *, rows, slabs, n_exp):
    def row_copy(src, src_row, dst_row):
        return pltpu.make_async_copy(src.at[pl.ds(pl.multiple_of(src_row * slabs, slabs), slabs)],
                                     x_ref.at[pl.ds(pl.multiple_of(dst_row * slabs, slabs), slabs)], sem)

    @pl.when(pl.program_id(0) == 0)
    def _():
        zero_sc[...] = jnp.zeros(zero_sc.shape, F32)
        for e in range(n_exp):
            def fill(i, c):
                row_copy(zero_sc, 0, pad_start_ref[e] + i).start()
                return c

            def drain(i, c):
                row_copy(zero_sc, 0, pad_start_ref[e] + i).wait()
                return c

            lax.fori_loop(0, pad_cnt_ref[e], fill, 0)
            lax.fori_loop(0, pad_cnt_ref[e], drain, 0)

        def chunk_copy(i):
            dst = pl.multiple_of((pad_start_ref[n_exp] + i * ZERO_CHUNK) * slabs, slabs)
            return pltpu.make_async_copy(zero_sc, x_ref.at[pl.ds(dst, ZERO_CHUNK * slabs)], sem)

        def fill_tail(i, c):
            chunk_copy(i).start()
            return c

        def drain_tail(i, c):
            chunk_copy(i).wait()
            return c

        lax.fori_loop(0, pad_cnt_ref[n_exp], fill_tail, 0)
        lax.fori_loop(0, pad_cnt_ref[n_exp], drain_tail, 0)

    def start8(r8, c):
        for q in range(GATHER_UNROLL):
            r = r8 * GATHER_UNROLL + q
            for k in range(TOP_K):
                row_copy(u_ref, r, pos_ref[0, k, r]).start(priority=k)
        return c

    lax.fori_loop(0, rows // GATHER_UNROLL, start8, 0)
    for k in range(TOP_K):
        pltpu.make_async_copy(u_ref, x_ref.at[pl.ds(0, rows * slabs)], sem).wait()


def _dispatch(u, pos_t, pad_start, pad_cnt, sorted_rows, tile, slabs):
    n = u.shape[0] // slabs
    grid_spec = pltpu.PrefetchScalarGridSpec(
        num_scalar_prefetch=2,
        grid=(n // tile,),
        in_specs=[pl.BlockSpec((1, TOP_K, tile), lambda i, ps, pc: (i, 0, 0), memory_space=pltpu.SMEM),
                  pl.BlockSpec((tile * slabs, LANES), lambda i, ps, pc: (i, 0))],
        out_specs=pl.BlockSpec(memory_space=pl.ANY),
        scratch_shapes=[pltpu.VMEM((ZERO_CHUNK * slabs, LANES), F32), pltpu.SemaphoreType.DMA(())],
    )
    return pl.pallas_call(
        functools.partial(_dispatch_kernel, rows=tile, slabs=slabs, n_exp=pad_start.shape[0] - 1),
        grid_spec=grid_spec,
        out_shape=jax.ShapeDtypeStruct((sorted_rows * slabs, LANES), F32),
        compiler_params=_params("arbitrary"),
        name="moe_dispatch",
    )(pad_start, pad_cnt, pos_t, u)


def _moe_kernel(te_ref, nu_ref, x_ref, wg_ref, wu_ref, wd_ref, o_ref, x_sc, acc_sc, *, tm, slabs, nj):
    t = pl.program_id(0)
    j = pl.program_id(1)

    @pl.when(t < nu_ref[0])
    def _():
        @pl.when(j == 0)
        def _():
            for s in range(slabs):
                x_sc[:, s * LANES:(s + 1) * LANES] = _load_row_tile_slab(x_ref, s, tm, slabs).astype(BF16)

        def finish(r):
            o_ref[...] = r.astype(BF16)

        _accumulate_steps(j, nj, lambda: _swiglu_step(x_sc[...], wg_ref[0], wu_ref[0], wd_ref[0]), acc_sc, finish)

    @pl.when((t >= nu_ref[0]) & (j == 0))
    def _():
        o_ref[...] = jnp.zeros(o_ref.shape, BF16)


def _moe_experts(x_sorted, tile_expert, n_used, wg, wu, wd, tm, slabs):
    d = slabs * LANES
    rows = x_sorted.shape[0] // slabs
    f = wg.shape[2]
    tf = _ff_chunk(f)
    nj = f // tf
    tmap = lambda t, j, te, nu: (jnp.minimum(t, nu[0] - 1), 0)
    jsel = lambda t, j, nu: jnp.where(t < nu[0], j, nj - 1)
    grid_spec = pltpu.PrefetchScalarGridSpec(
        num_scalar_prefetch=2,
        grid=(rows // tm, nj),
        in_specs=[pl.BlockSpec((tm * slabs, LANES), tmap),
                  pl.BlockSpec((1, d, tf), lambda t, j, te, nu: (te[t], 0, jsel(t, j, nu))),
                  pl.BlockSpec((1, d, tf), lambda t, j, te, nu: (te[t], 0, jsel(t, j, nu))),
                  pl.BlockSpec((1, tf, d), lambda t, j, te, nu: (te[t], jsel(t, j, nu), 0))],
        out_specs=pl.BlockSpec((tm, d), lambda t, j, te, nu: (t, 0)),
        scratch_shapes=[pltpu.VMEM((tm, d), BF16), pltpu.VMEM((tm, d), F32)],
    )
    return pl.pallas_call(
        functools.partial(_moe_kernel, tm=tm, slabs=slabs, nj=nj),
        grid_spec=grid_spec,
        out_shape=jax.ShapeDtypeStruct((rows, d), BF16),
        compiler_params=_params("arbitrary", "arbitrary"),
        name="moe_experts",
    )(tile_expert, n_used, x_sorted, wg, wu, wd)


def _combine_stage_rows(tile, n_exp):
    max_chunks = (TOP_K * tile + n_exp * (15 + COMBINE_CHUNK - 1)) // COMBINE_CHUNK
    return -(-max_chunks * COMBINE_CHUNK // 256) * 256


def _combine_kernel(plan_ref, h_ref, info_ref, loc_ref, gf_ref, y_ref, o_ref, stage, sem, *, tile, n_exp, final_norm):
    i = pl.program_id(0)

    @pl.when(i == 0)
    def _():
        stage[...] = jnp.zeros(stage.shape, BF16)

    def chunk(e, c):
        base = (i * 3) * n_exp + e
        src = pl.multiple_of(plan_ref[base] + c * COMBINE_CHUNK, 16)
        dst = pl.multiple_of(plan_ref[base + 2 * n_exp] + c * COMBINE_CHUNK, COMBINE_CHUNK)
        return pltpu.make_async_copy(y_ref.at[pl.ds(src, COMBINE_CHUNK)], stage.at[pl.ds(dst, COMBINE_CHUNK)], sem)

    def each_chunk(action):
        for e in range(n_exp):
            def body(c, carry):
                action(chunk(e, c))
                return carry

            lax.fori_loop(0, plan_ref[(i * 3 + 1) * n_exp + e], body, 0)

    each_chunk(lambda copy: copy.start())
    each_chunk(lambda copy: copy.wait())

    info = info_ref[...]
    loc = loc_ref[...]
    col = lax.broadcasted_iota(jnp.int32, (tile, stage.shape[0]), 1)
    p = jnp.where(col == loc[:, 0:1], info[:, 2:3], 0.0) + jnp.where(col == loc[:, 1:2], info[:, 3:4], 0.0)
    out = h_ref[...] + _dot(p.astype(BF16), stage[...])
    if final_norm:
        out = _rms(out, gf_ref[...])
    o_ref[...] = out


def _combine(h, info, loc, plan, y_sorted, g_final, final_norm, n_exp):
    n, d = h.shape
    tile = n // (plan.shape[0] // (3 * n_exp))
    grid_spec = pltpu.PrefetchScalarGridSpec(
        num_scalar_prefetch=1,
        grid=(n // tile,),
        in_specs=[pl.BlockSpec((tile, d), lambda i, plan: (i, 0)),
                  pl.BlockSpec((tile, LANES), lambda i, plan: (i, 0)),
                  pl.BlockSpec((tile, TOP_K), lambda i, plan: (i, 0)),
                  pl.BlockSpec((1, d), lambda i, plan: (0, 0)),
                  pl.BlockSpec(memory_space=pl.ANY)],
        out_specs=pl.BlockSpec((tile, d), lambda i, plan: (i, 0)),
        scratch_shapes=[pltpu.VMEM((_combine_stage_rows(tile, n_exp), d), BF16), pltpu.SemaphoreType.DMA(())],
    )
    return pl.pallas_call(
        functools.partial(_combine_kernel, tile=tile, n_exp=n_exp, final_norm=final_norm),
        grid_spec=grid_spec,
        out_shape=jax.ShapeDtypeStruct((n, d), F32),
        compiler_params=_params("arbitrary"),
        name="moe_combine",
    )(plan, h, info, loc, g_final, y_sorted)


def _moe_layer(h, g, router, wg, wu, wd, g_final, final_norm):
    n, d = h.shape
    n_exp = router.shape[1]
    tm = min(512, n)
    router_padded = jnp.pad(router, ((0, 0), (0, LANES - n_exp)))
    u, info, cnt = _route(h, g, router_padded, n_exp)

    tile = min(512, n)
    n_blocks = n // tile
    idx = info[:, 0:TOP_K].astype(jnp.int32)
    rank = info[:, 4:4 + TOP_K].astype(jnp.int32)
    after = cnt.reshape(n_blocks, 8, LANES)[:, 0, :n_exp].astype(jnp.int32)
    before = jnp.concatenate([jnp.zeros((1, n_exp), jnp.int32), after[:-1]])
    counts = after[-1]
    padded = (counts + tm - 1) // tm * tm
    ends = jnp.cumsum(padded)
    starts = ends - padded
    pos = jnp.sum(jnp.where(idx[..., None] == jnp.arange(n_exp), starts, 0), axis=-1) + rank
    rows = TOP_K * n + n_exp * tm
    n_used = (ends[-1] // tm).astype(jnp.int32)
    tile_ids = jnp.minimum(jnp.arange(rows // tm, dtype=jnp.int32), n_used - 1)
    tile_expert = jnp.sum(tile_ids[:, None] * tm >= ends[None, :], axis=1).astype(jnp.int32)

    run_start = starts[None, :] + before
    run_len = after - before
    fetch_start = run_start // 16 * 16
    n_chunks = jnp.where(run_len > 0, (run_start - fetch_start + run_len + COMBINE_CHUNK - 1) // COMBINE_CHUNK, 0)
    stage_off = COMBINE_CHUNK * (jnp.cumsum(n_chunks, axis=1) - n_chunks)
    plan = jnp.stack([fetch_start, n_chunks, stage_off], axis=1).reshape(-1).astype(jnp.int32)
    shift = jnp.repeat(stage_off - fetch_start, tile, axis=0)
    loc = pos + jnp.sum(jnp.where(idx[..., None] == jnp.arange(n_exp), shift[:, None, :], 0), axis=-1)

    slabs = d // LANES
    assert slabs == 8
    pos_t = pos.T.reshape(TOP_K, n_blocks, tile).transpose(1, 0, 2)
    assert tm % ZERO_CHUNK == 0
    pad_start = jnp.concatenate([starts + counts, ends[-1:]]).astype(jnp.int32)
    pad_cnt = jnp.concatenate([padded - counts, (rows - ends[-1:]) // ZERO_CHUNK]).astype(jnp.int32)
    x_sorted = _dispatch(u, pos_t, pad_start, pad_cnt, rows, tile, slabs)
    y_sorted = _moe_experts(x_sorted, tile_expert, n_used.reshape(1), wg, wu, wd, tm, slabs)
    return _combine(h, info, loc.astype(jnp.int32), plan, y_sorted, g_final, final_norm, n_exp)


def _mla_weights(wq_b, wkv_a, wkv_b, heads, lora):
    def padded(w):
        return jnp.concatenate([w, jnp.zeros(w.shape[:-1] + (LANES - ROPE,), w.dtype)], axis=-1)

    rq = wq_b.shape[0]
    wq = wq_b.reshape(rq, heads, NOPE + ROPE)
    wqb = jnp.concatenate([wq[..., :NOPE].reshape(rq, -1), padded(wq[..., NOPE:]).reshape(rq, -1)], axis=1)
    wkva = jnp.concatenate([wkv_a[:, :lora], padded(wkv_a[:, lora:])], axis=1)
    wkv = wkv_b.reshape(lora, heads, 2 * NOPE)
    wkvb = jnp.concatenate([wkv[..., :NOPE].reshape(lora, -1), wkv[..., NOPE:].reshape(lora, -1)], axis=1)
    return wqb.astype(BF16), wkva.astype(BF16), wkvb.astype(BF16)


def _cast_kernel(x_ref, o_ref):
    o_ref[...] = x_ref[0].astype(BF16)


def _to_bf16(w, layer):
    cols = w.shape[-1]
    w3 = w.reshape(w.shape[0], -1, cols)
    rows = w3.shape[1]
    cap = 256 if cols > 2048 else 1024
    tr = next(t for t in (1024, 512, 256, 128, 64, 32, 16, 8) if t <= cap and rows % t == 0)
    out = pl.pallas_call(
        _cast_kernel,
        grid=(rows // tr,),
        in_specs=[pl.BlockSpec((1, tr, cols), lambda i: (layer, i, 0))],
        out_specs=pl.BlockSpec((tr, cols), lambda i: (i, 0)),
        out_shape=jax.ShapeDtypeStruct((rows, cols), BF16),
        compiler_params=_params("arbitrary"),
        name="weights_to_bf16",
    )(w3)
    return out.reshape(w.shape[1:])


def _final_norm_kernel(h_ref, g_ref, o_ref):
    o_ref[...] = _rms(h_ref[...], g_ref[...])


def _final_norm(h, g):
    n, d = h.shape
    tm = min(512, n)
    return pl.pallas_call(
        _final_norm_kernel,
        grid=(n // tm,),
        in_specs=[pl.BlockSpec((tm, d), lambda i: (i, 0)), pl.BlockSpec((1, d), lambda i: (0, 0))],
        out_specs=pl.BlockSpec((tm, d), lambda i: (i, 0)),
        out_shape=jax.ShapeDtypeStruct((n, d), F32),
        compiler_params=_params("arbitrary"),
        name="final_norm",
    )(h, g)


@jax.jit
def _trunk(x, positions, norm_mix, norm_ffn, norm_final, pool_w, pool_scale,
           conv_w1, conv_b1, conv_dw, conv_bdw, conv_ln_g, conv_ln_b, conv_w2, conv_b2,
           mla_wq_a, mla_q_norm, mla_wq_b, mla_wkv_a, mla_kv_norm, mla_wkv_b, mla_wo,
           ffn_wg, ffn_wu, ffn_wd, moe_router, moe_wg, moe_wu, moe_wd):
    batch, seq, d = x.shape
    depth = norm_mix.shape[0]
    lora = mla_kv_norm.shape[1]
    heads = mla_wo.shape[1] // NOPE
    assert mla_wq_b.shape[2] == heads * (NOPE + ROPE) and mla_wkv_a.shape[2] == lora + ROPE
    n = batch * seq
    h = x.reshape(n, d)
    pos = positions.reshape(n, 1)
    row = lambda v: v.reshape(1, -1)

    half = jnp.arange(0, ROPE, 2, dtype=F32) / ROPE
    inv_freq = ROPE_THETA ** (-half)
    invf = row(jnp.concatenate([inv_freq, inv_freq, jnp.zeros((LANES - ROPE,), F32)]))
    sgn = row(jnp.concatenate([-jnp.ones((ROPE // 2,), F32), jnp.ones((ROPE // 2,), F32),
                               jnp.zeros((LANES - ROPE,), F32)]))

    moe_w = (moe_wg, moe_wu, moe_wd)
    moe_bf16 = {}

    def moe_cast_job(i):
        nxt = i if i % 2 == 1 else i + 1
        if nxt < depth and nxt // 2 not in moe_bf16:
            return moe_w, nxt // 2
        return (), 0

    for i in range(depth):
        kind, j = i % 3, i // 3
        g = row(norm_mix[i])
        if kind == 0:
            h = _pool_layer(h, g, pool_w[j].astype(BF16), row(pool_scale[j]), batch=batch, seq=seq)
        elif kind == 1:
            side_w, side_layer = moe_cast_job(i)
            h, cast = _conv_layer(h, g, conv_w1[j].astype(BF16), row(conv_b1[j]), conv_dw[j], row(conv_bdw[j]),
                                  row(conv_ln_g[j]), row(conv_ln_b[j]), conv_w2[j].astype(BF16), row(conv_b2[j]),
                                  side_w, side_layer, batch=batch, seq=seq)
            if cast:
                moe_bf16[side_layer] = cast
        else:
            wqb, wkva, wkvb = _mla_weights(mla_wq_b[j], mla_wkv_a[j], mla_wkv_b[j], heads, lora)
            side_w, side_layer = moe_cast_job(i)
            h, cast = _mla_layer(h, pos, g, mla_wq_a[j].astype(BF16), row(mla_q_norm[j]), wqb, wkva,
                                 row(mla_kv_norm[j]), wkvb, mla_wo[j].T.astype(BF16), invf, sgn,
                                 side_w, side_layer, batch=batch, seq=seq, heads=heads, lora=lora)
            if cast:
                moe_bf16[side_layer] = cast
        g = row(norm_ffn[i])
        f_idx = i // 2
        last = i == depth - 1
        if i % 2 == 0:
            h = _ffn_layer(h, g, _to_bf16(ffn_wg, f_idx), _to_bf16(ffn_wu, f_idx), _to_bf16(ffn_wd, f_idx))
            if last:
                h = _final_norm(h, row(norm_final))
        else:
            wg, wu, wd = moe_bf16.get(f_idx) or [_to_bf16(w, f_idx) for w in moe_w]
            h = _moe_layer(h, g, moe_router[f_idx], wg, wu, wd, row(norm_final), last)
    return h.reshape(batch, seq, d)


def kernel(x, positions, norm_mix, norm_ffn, norm_final, pool_w, pool_scale, conv_w1, conv_b1, conv_dw, conv_bdw, conv_ln_g, conv_ln_b, conv_w2, conv_b2, mla_wq_a, mla_q_norm, mla_wq_b, mla_wkv_a, mla_kv_norm, mla_wkv_b, mla_wo, ffn_wg, ffn_wu, ffn_wd, moe_router, moe_wg, moe_wu, moe_wd):
    return _trunk(x, positions, norm_mix, norm_ffn, norm_final, pool_w, pool_scale,
                  conv_w1, conv_b1, conv_dw, conv_bdw, conv_ln_g, conv_ln_b, conv_w2, conv_b2,
                  mla_wq_a, mla_q_norm, mla_wq_b, mla_wkv_a, mla_kv_norm, mla_wkv_b, mla_wo,
                  ffn_wg, ffn_wu, ffn_wd, moe_router, moe_wg, moe_wu, moe_wd)
```

```python
import functools

import jax
import jax.numpy as jnp
from jax import lax
from jax.experimental import pallas as pl
from jax.experimental.pallas import tpu as pltpu

F32 = jnp.float32
BF16 = jnp.bfloat16

NORM_EPS = 1e-6
LN_EPS = 1e-5
ROPE_THETA = 10000.0
LOG2E = 1.4426950408889634

POOL_WINDOWS = (2, 4, 8, 16)
POOL_HALO = 16
CONV_HALO = 32
LANES = 128
SUBLANES = 8
POOL_TOP = POOL_HALO + SUBLANES
NOPE = 128
ROPE = 64
V_ROWS = NOPE + 16
TOP_K = 2
GATHER_UNROLL = 8
ZERO_CHUNK = 64
COMBINE_CHUNK = 64

VMEM_LIMIT = 56 * 1024 * 1024


def _params(*sem):
    return pltpu.CompilerParams(dimension_semantics=sem, vmem_limit_bytes=VMEM_LIMIT)


def _rms(x, g, eps=NORM_EPS):
    return x * lax.rsqrt(jnp.mean(x * x, axis=-1, keepdims=True) + eps) * g


def _dot(a, b):
    return jnp.dot(a, b, preferred_element_type=F32)


def _split_bf16(x):
    hi = x.astype(BF16)
    return hi, (x - hi.astype(F32)).astype(BF16)


def _dot_3pass(a, b):
    a_hi, a_lo = _split_bf16(a)
    b_hi, b_lo = _split_bf16(b)
    return _dot(a_hi, b_hi) + (_dot(a_hi, b_lo) + _dot(a_lo, b_hi))


def _ff_chunk(f):
    for c in (1792, 1024, 512, 256, 128):
        if f % c == 0:
            return c
    return f


def _pool_kernel(h_ref, g_ref, w_ref, sc_ref, *rest, ts, gdim, n_side):
    side_in, o_ref, side_out, (ubuf, lvl_a, lvl_b) = rest[:n_side], rest[n_side], rest[n_side + 1:-3], rest[-3:]
    _side_cast(side_in, side_out)
    j = pl.program_id(1)
    h = h_ref[...]
    u = _rms(h, g_ref[...])
    ext = POOL_TOP + ts

    @pl.when(j == 0)
    def _():
        ubuf[0:POOL_TOP, :] = jnp.zeros((POOL_TOP, ubuf.shape[1]), F32)
        lvl_a[0:SUBLANES, :] = jnp.zeros((SUBLANES, gdim), F32)
        lvl_b[0:SUBLANES, :] = jnp.zeros((SUBLANES, gdim), F32)

    ubuf[POOL_TOP:ext, :] = u
    t1 = lax.broadcasted_iota(jnp.int32, (ts, 1), 0) + (j * ts + 1)
    for g, win in enumerate(POOL_WINDOWS):
        cs = slice(g * gdim, (g + 1) * gdim)
        src, cols, spare, k = ubuf, cs, [lvl_a, lvl_b], 1
        while k < win:
            s = src[SUBLANES:ext, cols] + src[SUBLANES - k:ext - k, cols]
            k *= 2
            if k < win:
                dst = spare.pop(0)
                dst[SUBLANES:ext, :] = s
                spare.append(dst)
                src, cols = dst, slice(None)
        s = s[POOL_HALO:POOL_HALO + ts]
        ug = u[:, cs]
        cnt = jnp.minimum(t1, win).astype(F32)
        pooled = s / cnt - ug
        m = _dot(pooled.astype(BF16), w_ref[g])
        o_ref[:, cs] = h[:, cs] + m * sc_ref[:, cs]
    ubuf[SUBLANES:POOL_TOP, :] = ubuf[ts + SUBLANES:ts + POOL_TOP, :]


def _pool_layer(h, g, w, scale, side_weights, side_layers, *, batch, seq):
    n, d = h.shape
    ts = min(512, seq)
    nt = seq // ts
    gdim = d // len(POOL_WINDOWS)
    row = lambda b, j: (b * nt + j, 0)
    const2 = lambda b, j: (0, 0)
    side, side_in, side_out, side_shapes = _side_cast_plan(side_weights, list(side_layers), batch * nt,
                                                           lambda b, j: b * nt + j)
    outs = pl.pallas_call(
        functools.partial(_pool_kernel, ts=ts, gdim=gdim, n_side=len(side)),
        grid=(batch, nt),
        in_specs=[pl.BlockSpec((ts, d), row),
                  pl.BlockSpec((1, d), const2),
                  pl.BlockSpec(w.shape, lambda b, j: (0, 0, 0)),
                  pl.BlockSpec((1, d), const2)] + side_in,
        out_specs=[pl.BlockSpec((ts, d), row)] + side_out,
        out_shape=[jax.ShapeDtypeStruct((n, d), F32)] + side_shapes,
        scratch_shapes=[pltpu.VMEM((ts + POOL_TOP, d), F32), pltpu.VMEM((ts + POOL_TOP, gdim), F32),
                        pltpu.VMEM((ts + POOL_TOP, gdim), F32)],
        compiler_params=_params("arbitrary", "arbitrary"),
        name="pool_mixer",
    )(h, g, w, scale, *side)
    return outs[0], [o.reshape(w.shape[1:]) for o, w in zip(outs[1:], side_weights)]


def _swiglu_step(u, wg, wu, wd):
    a = _dot(u, wg)
    b = _dot(u, wu)
    hh = a * jax.nn.sigmoid(a) * b
    return _dot(hh.astype(BF16), wd)


def _accumulate_steps(j, nj, partial, acc_sc, finish):
    def run(first, last):
        r = partial()
        if not first:
            r = acc_sc[...] + r
        if last:
            finish(r)
        else:
            acc_sc[...] = r

    if nj == 1:
        run(True, True)
        return
    pl.when(j == 0)(functools.partial(run, True, False))
    if nj > 2:
        pl.when((j > 0) & (j < nj - 1))(functools.partial(run, False, False))
    pl.when(j == nj - 1)(functools.partial(run, False, True))


def _ffn_kernel(h_ref, g_ref, wg_ref, wu_ref, wd_ref, o_ref, u_sc, acc_sc, *, nj):
    j = pl.program_id(1)

    @pl.when(j == 0)
    def _():
        u_sc[...] = _rms(h_ref[...], g_ref[...]).astype(BF16)

    def finish(r):
        o_ref[...] = h_ref[...] + r

    _accumulate_steps(j, nj, lambda: _swiglu_step(u_sc[...], wg_ref[...], wu_ref[...], wd_ref[...]), acc_sc, finish)


def _ffn_layer(h, g, wg, wu, wd):
    n, d = h.shape
    f = wg.shape[1]
    tm = min(512, n)
    tf = _ff_chunk(f)
    return pl.pallas_call(
        functools.partial(_ffn_kernel, nj=f // tf),
        grid=(n // tm, f // tf),
        in_specs=[pl.BlockSpec((tm, d), lambda i, j: (i, 0)),
                  pl.BlockSpec((1, d), lambda i, j: (0, 0)),
                  pl.BlockSpec((d, tf), lambda i, j: (0, j)),
                  pl.BlockSpec((d, tf), lambda i, j: (0, j)),
                  pl.BlockSpec((tf, d), lambda i, j: (j, 0))],
        out_specs=pl.BlockSpec((tm, d), lambda i, j: (i, 0)),
        out_shape=jax.ShapeDtypeStruct((n, d), F32),
        scratch_shapes=[pltpu.VMEM((tm, d), BF16), pltpu.VMEM((tm, d), F32)],
        compiler_params=_params("arbitrary", "arbitrary"),
        name="dense_swiglu",
    )(h, g, wg, wu, wd)


def _side_cast_plan(weights, layer, steps, step_index):
    arrays = [w.reshape(w.shape[0], -1, w.shape[-1]) for w in weights]
    if not arrays:
        return [], [], [], []
    nblk = next(n for n in range(steps, 0, -1)
                if all(a.shape[1] % n == 0 and (a.shape[1] // n) % 16 == 0 for a in arrays))
    blk = lambda *g: jnp.minimum(step_index(*g), nblk - 1)
    layers = layer if isinstance(layer, (list, tuple)) else [layer] * len(arrays)
    in_specs, out_specs, out_shapes = [], [], []
    for a, lyr in zip(arrays, layers):
        rows, cols = a.shape[1:]
        in_specs.append(pl.BlockSpec((1, rows // nblk, cols), lambda *g, _l=lyr: (_l, blk(*g), 0)))
        out_specs.append(pl.BlockSpec((rows // nblk, cols), lambda *g: (blk(*g), 0)))
        out_shapes.append(jax.ShapeDtypeStruct((rows, cols), BF16))
    return arrays, in_specs, out_specs, out_shapes


def _side_cast(in_refs, out_refs):
    for src, dst in zip(in_refs, out_refs):
        dst[...] = src[0].astype(BF16)


def _conv_kernel(h_ref, g_ref, w1_ref, b1_ref, dw_ref, bdw_ref, lng_ref, lnb_ref, w2_ref, b2_ref, *rest,
                 ts, width, rb, n_side):
    side_in, o_ref, side_out, (gbuf, cbuf) = rest[:n_side], rest[n_side], rest[n_side + 1:-2], rest[-2:]
    _side_cast(side_in, side_out)
    j = pl.program_id(1)
    d = h_ref.shape[1]
    h = h_ref[...]
    u = _rms(h, g_ref[...]).astype(BF16)
    a = _dot(u, w1_ref[...]) + b1_ref[...]
    glu = a[:, :d] * jax.nn.sigmoid(a[:, d:])

    keep = CONV_HALO + SUBLANES

    @pl.when(j == 0)
    def _():
        gbuf[:, 0:keep, :] = jnp.zeros((SUBLANES, keep, d), F32)

    for b in range(SUBLANES):
        gbuf[b, CONV_HALO + b:CONV_HALO + b + ts, :] = glu
    for cb in range(d // LANES):
        cs = slice(cb * LANES, (cb + 1) * LANES)
        for r0 in range(0, ts, rb):
            acc = jnp.zeros((rb, LANES), F32)
            for k in range(width):
                a, b = divmod(width - 1 - k, SUBLANES)
                off = CONV_HALO + r0 - SUBLANES * a
                acc = acc + gbuf[b, off:off + rb, cs] * dw_ref[k:k + 1, cs]
            cbuf[r0:r0 + rb, cs] = acc + bdw_ref[:, cs]
    gbuf[:, 0:keep, :] = gbuf[:, ts:ts + keep, :]

    c = cbuf[...]
    mu = jnp.mean(c, axis=-1, keepdims=True)
    cc = c - mu
    var = jnp.mean(cc * cc, axis=-1, keepdims=True)
    un = cc * lax.rsqrt(var + LN_EPS) * lng_ref[...] + lnb_ref[...]
    un = un * jax.nn.sigmoid(un)
    o_ref[...] = h + _dot(un.astype(BF16), w2_ref[...]) + b2_ref[...]


def _conv_layer(h, g, w1, b1, dw, bdw, lng, lnb, w2, b2, side_weights, side_layer, *, batch, seq):
    n, d = h.shape
    width = dw.shape[0]
    assert width - 1 <= CONV_HALO
    ts = min(256, seq)
    rb = min(128, ts)
    nt = seq // ts
    row = lambda b, j: (b * nt + j, 0)
    c2 = lambda b, j: (0, 0)
    side, side_in, side_out, side_shapes = _side_cast_plan(side_weights, side_layer, batch * nt,
                                                           lambda b, j: b * nt + j)
    outs = pl.pallas_call(
        functools.partial(_conv_kernel, ts=ts, width=width, rb=rb, n_side=len(side)),
        grid=(batch, nt),
        in_specs=[pl.BlockSpec((ts, d), row),
                  pl.BlockSpec((1, d), c2),
                  pl.BlockSpec(w1.shape, c2),
                  pl.BlockSpec(b1.shape, c2),
                  pl.BlockSpec(dw.shape, c2),
                  pl.BlockSpec((1, d), c2),
                  pl.BlockSpec((1, d), c2),
                  pl.BlockSpec((1, d), c2),
                  pl.BlockSpec(w2.shape, c2),
                  pl.BlockSpec((1, d), c2)] + side_in,
        out_specs=[pl.BlockSpec((ts, d), row)] + side_out,
        out_shape=[jax.ShapeDtypeStruct((n, d), F32)] + side_shapes,
        scratch_shapes=[pltpu.VMEM((SUBLANES, ts + CONV_HALO + SUBLANES, d), F32), pltpu.VMEM((ts, d), F32)],
        compiler_params=_params("arbitrary", "arbitrary"),
        name="conv_module",
    )(h, g, w1, b1, dw, bdw, lng, lnb, w2, b2, *side)
    return outs[0], [o.reshape(w.shape[1:]) for o, w in zip(outs[1:], side_weights)]


def _mla_proj_kernel(h_ref, pos_ref, g_ref, wqa_ref, qnorm_ref, wqb_ref, wkva_ref, kvnorm_ref, wkvb_ref,
                     invf_ref, sgn_ref, qt_o, k_o, vt_o, *, heads, lora):
    hn = heads * NOPE
    rows = h_ref.shape[0]
    u = _rms(h_ref[...], g_ref[...]).astype(BF16)
    qa = _rms(_dot(u, wqa_ref[...]), qnorm_ref[...]).astype(BF16)
    q = _dot(qa, wqb_ref[...])
    kva = _dot(u, wkva_ref[...])
    ckv = _rms(kva[:, :lora], kvnorm_ref[...]).astype(BF16)
    kv = _dot(ckv, wkvb_ref[...])
    ang = pos_ref[...].astype(F32) * invf_ref[...]
    c = jnp.cos(ang)
    s = jnp.sin(ang) * sgn_ref[...]
    lane = lax.broadcasted_iota(jnp.int32, (rows, LANES), 1)

    def rope(x):
        swapped = jnp.where(lane < ROPE // 2, pltpu.roll(x, LANES - ROPE // 2, 1), pltpu.roll(x, ROPE // 2, 1))
        return x * c + swapped * s

    kr = rope(kva[:, lora:lora + LANES]).astype(BF16)
    ones = jnp.ones((V_ROWS - NOPE, rows), BF16)
    for hd in range(heads):
        ns = slice(hd * NOPE, (hd + 1) * NOPE)
        qr = rope(q[:, hn + hd * LANES:hn + (hd + 1) * LANES])
        qt_o[2 * hd * NOPE:(2 * hd + 1) * NOPE, :] = q[:, ns].T.astype(BF16)
        qt_o[(2 * hd + 1) * NOPE:(2 * hd + 2) * NOPE, :] = qr.T.astype(BF16)
        k_o[:, 2 * hd * NOPE:(2 * hd + 1) * NOPE] = kv[:, ns].astype(BF16)
        k_o[:, (2 * hd + 1) * NOPE:(2 * hd + 2) * NOPE] = kr
        vt_o[hd * V_ROWS:hd * V_ROWS + NOPE, :] = kv[:, hn + hd * NOPE:hn + (hd + 1) * NOPE].T.astype(BF16)
        vt_o[hd * V_ROWS + NOPE:(hd + 1) * V_ROWS, :] = ones


def _attn_kernel(qi_ref, kj_ref, qt_ref, k_ref, vt_ref, h_ref, wot_ref, *rest, heads, tq, tk, scale, n_side):
    side_in, o_ref, side_out, (m_sc, acc_sc) = rest[:n_side], rest[n_side], rest[n_side + 1:-2], rest[-2:]
    _side_cast(side_in, side_out)
    i = qi_ref[pl.program_id(1)]
    j = kj_ref[pl.program_id(1)]
    c = scale * LOG2E

    @pl.when(j == 0)
    def _():
        m_sc[...] = jnp.full(m_sc.shape, -jnp.inf, F32)
        acc_sc[...] = jnp.zeros(acc_sc.shape, F32)

    def step(diagonal):
        if diagonal:
            causal = (lax.broadcasted_iota(jnp.int32, (tk, tq), 0)
                      <= lax.broadcasted_iota(jnp.int32, (tk, tq), 1))
        es = [slice(2 * hd * NOPE, (2 * hd + 2) * NOPE) for hd in range(heads)]
        vs = [slice(hd * V_ROWS, (hd + 1) * V_ROWS) for hd in range(heads)]
        ss = [_dot(k_ref[:, e], qt_ref[e, :]) for e in es]
        if diagonal:
            ss = [jnp.where(causal, s, -jnp.inf) for s in ss]
        m_prev = [m_sc[hd] for hd in range(heads)]
        m_new = [jnp.maximum(mp, jnp.max(s, axis=0, keepdims=True)) for mp, s in zip(m_prev, ss)]
        ps = [jnp.exp2((s - mn) * c).astype(BF16) for s, mn in zip(ss, m_new)]
        pv = [_dot(vt_ref[v, :], p) for p, v in zip(ps, vs)]
        for hd in range(heads):
            alpha = jnp.exp2((m_prev[hd] - m_new[hd]) * c)
            acc_sc[vs[hd], :] = alpha * acc_sc[vs[hd], :] + pv[hd]
            m_sc[hd] = m_new[hd]

    @pl.when(j < i)
    def _():
        step(False)

    @pl.when(j == i)
    def _():
        step(True)
        parts = [acc_sc[hd * V_ROWS:hd * V_ROWS + NOPE, :] / acc_sc[hd * V_ROWS + NOPE:hd * V_ROWS + NOPE + 1, :]
                 for hd in range(heads)]
        ot = jnp.concatenate(parts, axis=0).astype(BF16)
        o_ref[...] = h_ref[...] + _dot(wot_ref[...], ot).T


def _mla_layer(h, positions, g, wqa, qnorm, wqb, wkva, kvnorm, wkvb, wot, invf, sgn, side_weights, side_layer,
               *, batch, seq, heads, lora):
    n, d = h.shape
    hn = heads * NOPE
    tp = min(256, n)
    c2 = lambda i: (0, 0)
    row = lambda i: (i, 0)
    col = lambda i: (0, i)
    qt, k, vt = pl.pallas_call(
        functools.partial(_mla_proj_kernel, heads=heads, lora=lora),
        grid=(n // tp,),
        in_specs=[pl.BlockSpec((tp, d), row),
                  pl.BlockSpec((tp, 1), row),
                  pl.BlockSpec((1, d), c2),
                  pl.BlockSpec(wqa.shape, c2),
                  pl.BlockSpec(qnorm.shape, c2),
                  pl.BlockSpec(wqb.shape, c2),
                  pl.BlockSpec(wkva.shape, c2),
                  pl.BlockSpec(kvnorm.shape, c2),
                  pl.BlockSpec(wkvb.shape, c2),
                  pl.BlockSpec((1, LANES), c2),
                  pl.BlockSpec((1, LANES), c2)],
        out_specs=[pl.BlockSpec((2 * hn, tp), col), pl.BlockSpec((tp, 2 * hn), row),
                   pl.BlockSpec((heads * V_ROWS, tp), col)],
        out_shape=[jax.ShapeDtypeStruct((2 * hn, n), BF16), jax.ShapeDtypeStruct((n, 2 * hn), BF16),
                   jax.ShapeDtypeStruct((heads * V_ROWS, n), BF16)],
        compiler_params=_params("arbitrary"),
        name="mla_proj",
    )(h, positions, g, wqa, qnorm, wqb, wkva, kvnorm, wkvb, invf, sgn)

    tq = tk = min(512, seq)
    nq = seq // tq
    scale = float(NOPE + ROPE) ** -0.5
    pairs = [(i, j) for i in range(nq) for j in range(i + 1)]
    qi = jnp.asarray([p[0] for p in pairs], jnp.int32)
    kj = jnp.asarray([p[1] for p in pairs], jnp.int32)
    npairs = len(pairs)
    qrow = lambda b, p, qi, kj: (b * nq + qi[p], 0)
    qcol = lambda b, p, qi, kj: (0, b * nq + qi[p])
    krow = lambda b, p, qi, kj: (b * nq + kj[p], 0)
    kcol = lambda b, p, qi, kj: (0, b * nq + kj[p])
    side, side_in, side_out, side_shapes = _side_cast_plan(side_weights, side_layer, batch * npairs,
                                                           lambda b, p, qi, kj: b * npairs + p)
    grid_spec = pltpu.PrefetchScalarGridSpec(
        num_scalar_prefetch=2,
        grid=(batch, npairs),
        in_specs=[pl.BlockSpec((2 * hn, tq), qcol),
                  pl.BlockSpec((tk, 2 * hn), krow),
                  pl.BlockSpec((heads * V_ROWS, tk), kcol),
                  pl.BlockSpec((tq, d), qrow),
                  pl.BlockSpec(wot.shape, lambda b, p, qi, kj: (0, 0))] + side_in,
        out_specs=[pl.BlockSpec((tq, d), qrow)] + side_out,
        scratch_shapes=[pltpu.VMEM((heads, 1, tq), F32), pltpu.VMEM((heads * V_ROWS, tq), F32)],
    )
    outs = pl.pallas_call(
        functools.partial(_attn_kernel, heads=heads, tq=tq, tk=tk, scale=scale, n_side=len(side)),
        grid_spec=grid_spec,
        out_shape=[jax.ShapeDtypeStruct((n, d), F32)] + side_shapes,
        compiler_params=_params("arbitrary", "arbitrary"),
        name="mla_attention",
    )(qi, kj, qt, k, vt, h, wot, *side)
    return outs[0], [o.reshape(w.shape[1:]) for o, w in zip(outs[1:], side_weights)]


def _store_row_tiles(ref, value, rows):
    slabs = value.shape[1] // LANES
    for s in range(slabs):
        ref[pl.ds(s, rows, stride=slabs), :] = value[:, s * LANES:(s + 1) * LANES]


def _load_row_tile_slab(ref, s, rows, slabs):
    return ref[pl.ds(s, rows, stride=slabs), :]


def _route_kernel(h_ref, g_ref, r_ref, u_ref, info_ref, cnt_ref, carry, *, tr, n_exp):
    i = pl.program_id(0)

    @pl.when(i == 0)
    def _():
        carry[...] = jnp.zeros(carry.shape, F32)

    u = _rms(h_ref[...], g_ref[...])
    _store_row_tiles(u_ref, u, tr)
    logits = _dot_3pass(u, r_ref[...])
    lane = lax.broadcasted_iota(jnp.int32, (tr, LANES), 1)
    lg = jnp.where(lane < n_exp, logits, -jnp.inf)
    m1 = jnp.max(lg, axis=1, keepdims=True)
    i1 = jnp.min(jnp.where(lg == m1, lane, LANES), axis=1, keepdims=True)
    lg2 = jnp.where(lane == i1, -jnp.inf, lg)
    m2 = jnp.max(lg2, axis=1, keepdims=True)
    i2 = jnp.min(jnp.where(lg2 == m2, lane, LANES), axis=1, keepdims=True)
    e = jnp.exp(m2 - m1)
    g1 = 1.0 / (1.0 + e)
    g2 = e / (1.0 + e)
    sel1 = lane == i1
    sel2 = lane == i2
    onehot = jnp.where(sel1 | sel2, 1.0, 0.0)
    rr = lax.broadcasted_iota(jnp.int32, (tr, tr), 0)
    cc = lax.broadcasted_iota(jnp.int32, (tr, tr), 1)
    tri = jnp.where(cc < rr, 1.0, 0.0).astype(BF16)
    before = _dot(tri, onehot.astype(BF16)) + carry[0:1, :]
    rank1 = jnp.sum(jnp.where(sel1, before, 0.0), axis=1, keepdims=True)
    rank2 = jnp.sum(jnp.where(sel2, before, 0.0), axis=1, keepdims=True)
    carry[...] = carry[...] + jnp.sum(onehot, axis=0, keepdims=True)
    cols = (i1.astype(F32), i2.astype(F32), g1, g2, rank1, rank2)
    info = jnp.zeros((tr, LANES), F32)
    for k, col in enumerate(cols):
        info = jnp.where(lane == k, col, info)
    info_ref[...] = info
    cnt_ref[...] = carry[...]


def _route(h, g, router_padded, n_exp):
    n, d = h.shape
    tr = min(512, n)
    slabs = d // LANES
    return pl.pallas_call(
        functools.partial(_route_kernel, tr=tr, n_exp=n_exp),
        grid=(n // tr,),
        in_specs=[pl.BlockSpec((tr, d), lambda i: (i, 0)),
                  pl.BlockSpec((1, d), lambda i: (0, 0)),
                  pl.BlockSpec((d, LANES), lambda i: (0, 0))],
        out_specs=[pl.BlockSpec((tr * slabs, LANES), lambda i: (i, 0)),
                   pl.BlockSpec((tr, LANES), lambda i: (i, 0)),
                   pl.BlockSpec((8, LANES), lambda i: (i, 0))],
        out_shape=[jax.ShapeDtypeStruct((n * slabs, LANES), F32),
                   jax.ShapeDtypeStruct((n, LANES), F32),
                   jax.ShapeDtypeStruct((n // tr * 8, LANES), F32)],
        scratch_shapes=[pltpu.VMEM((8, LANES), F32)],
        compiler_params=_params("arbitrary"),
        name="moe_route",
    )(h, g, router_padded)


def _dispatch_kernel(pad_start_ref, pad_cnt_ref, pos_ref, u_ref, x_ref, zero_sc, sem, *, rows, slabs, n_exp):
    def row_copy(src, src_row, dst_row):
        return pltpu.make_async_copy(src.at[pl.ds(pl.multiple_of(src_row * slabs, slabs), slabs)],
                                     x_ref.at[pl.ds(pl.multiple_of(dst_row * slabs, slabs), slabs)], sem)

    @pl.when(pl.program_id(0) == 0)
    def _():
        zero_sc[...] = jnp.zeros(zero_sc.shape, F32)
        for e in range(n_exp):
            def fill(i, c):
                row_copy(zero_sc, 0, pad_start_ref[e] + i).start()
                return c

            def drain(i, c):
                row_copy(zero_sc, 0, pad_start_ref[e] + i).wait()
                return c

            lax.fori_loop(0, pad_cnt_ref[e], fill, 0)
            lax.fori_loop(0, pad_cnt_ref[e], drain, 0)

        def chunk_copy(i):
            dst = pl.multiple_of((pad_start_ref[n_exp] + i * ZERO_CHUNK) * slabs, slabs)
            return pltpu.make_async_copy(zero_sc, x_ref.at[pl.ds(dst, ZERO_CHUNK * slabs)], sem)

        def fill_tail(i, c):
            chunk_copy(i).start()
            return c

        def drain_tail(i, c):
            chunk_copy(i).wait()
            return c

        lax.fori_loop(0, pad_cnt_ref[n_exp], fill_tail, 0)
        lax.fori_loop(0, pad_cnt_ref[n_exp], drain_tail, 0)

    def start8(r8, c):
        for q in range(GATHER_UNROLL):
            r = r8 * GATHER_UNROLL + q
            for k in range(TOP_K):
                row_copy(u_ref, r, pos_ref[0, k, r]).start(priority=k)
        return c

    lax.fori_loop(0, rows // GATHER_UNROLL, start8, 0)
    for k in range(TOP_K):
        pltpu.make_async_copy(u_ref, x_ref.at[pl.ds(0, rows * slabs)], sem).wait()


def _dispatch(u, pos_t, pad_start, pad_cnt, sorted_rows, tile, slabs):
    n = u.shape[0] // slabs
    grid_spec = pltpu.PrefetchScalarGridSpec(
        num_scalar_prefetch=2,
        grid=(n // tile,),
        in_specs=[pl.BlockSpec((1, TOP_K, tile), lambda i, ps, pc: (i, 0, 0), memory_space=pltpu.SMEM),
                  pl.BlockSpec((tile * slabs, LANES), lambda i, ps, pc: (i, 0))],
        out_specs=pl.BlockSpec(memory_space=pl.ANY),
        scratch_shapes=[pltpu.VMEM((ZERO_CHUNK * slabs, LANES), F32), pltpu.SemaphoreType.DMA(())],
    )
    return pl.pallas_call(
        functools.partial(_dispatch_kernel, rows=tile, slabs=slabs, n_exp=pad_start.shape[0] - 1),
        grid_spec=grid_spec,
        out_shape=jax.ShapeDtypeStruct((sorted_rows * slabs, LANES), F32),
        compiler_params=_params("arbitrary"),
        name="moe_dispatch",
    )(pad_start, pad_cnt, pos_t, u)


def _moe_kernel(te_ref, nu_ref, x_ref, wg_ref, wu_ref, wd_ref, o_ref, x_sc, acc_sc, *, tm, slabs, nj):
    t = pl.program_id(0)
    j = pl.program_id(1)

    @pl.when(t < nu_ref[0])
    def _():
        @pl.when(j == 0)
        def _():
            for s in range(slabs):
                x_sc[:, s * LANES:(s + 1) * LANES] = _load_row_tile_slab(x_ref, s, tm, slabs).astype(BF16)

        def finish(r):
            o_ref[...] = r.astype(BF16)

        _accumulate_steps(j, nj, lambda: _swiglu_step(x_sc[...], wg_ref[0], wu_ref[0], wd_ref[0]), acc_sc, finish)

    @pl.when((t >= nu_ref[0]) & (j == 0))
    def _():
        o_ref[...] = jnp.zeros(o_ref.shape, BF16)


def _moe_experts(x_sorted, tile_expert, n_used, wg, wu, wd, tm, slabs):
    d = slabs * LANES
    rows = x_sorted.shape[0] // slabs
    f = wg.shape[2]
    tf = _ff_chunk(f)
    nj = f // tf
    tmap = lambda t, j, te, nu: (jnp.minimum(t, nu[0] - 1), 0)
    jsel = lambda t, j, nu: jnp.where(t < nu[0], j, nj - 1)
    grid_spec = pltpu.PrefetchScalarGridSpec(
        num_scalar_prefetch=2,
        grid=(rows // tm, nj),
        in_specs=[pl.BlockSpec((tm * slabs, LANES), tmap),
                  pl.BlockSpec((1, d, tf), lambda t, j, te, nu: (te[t], 0, jsel(t, j, nu))),
                  pl.BlockSpec((1, d, tf), lambda t, j, te, nu: (te[t], 0, jsel(t, j, nu))),
                  pl.BlockSpec((1, tf, d), lambda t, j, te, nu: (te[t], jsel(t, j, nu), 0))],
        out_specs=pl.BlockSpec((tm, d), lambda t, j, te, nu: (t, 0)),
        scratch_shapes=[pltpu.VMEM((tm, d), BF16), pltpu.VMEM((tm, d), F32)],
    )
    return pl.pallas_call(
        functools.partial(_moe_kernel, tm=tm, slabs=slabs, nj=nj),
        grid_spec=grid_spec,
        out_shape=jax.ShapeDtypeStruct((rows, d), BF16),
        compiler_params=_params("arbitrary", "arbitrary"),
        name="moe_experts",
    )(tile_expert, n_used, x_sorted, wg, wu, wd)


def _combine_stage_rows(tile, n_exp):
    max_chunks = (TOP_K * tile + n_exp * (15 + COMBINE_CHUNK - 1)) // COMBINE_CHUNK
    return -(-max_chunks * COMBINE_CHUNK // 256) * 256


def _combine_kernel(plan_ref, h_ref, info_ref, loc_ref, gf_ref, y_ref, o_ref, stage, sem, *, tile, n_exp, final_norm):
    i = pl.program_id(0)
    slot = i % 2

    def each_chunk(block, buf, action):
        for e in range(n_exp):
            base = (block * 3) * n_exp + e

            def body(c, carry):
                src = pl.multiple_of(plan_ref[base] + c * COMBINE_CHUNK, 16)
                dst = pl.multiple_of(plan_ref[base + 2 * n_exp] + c * COMBINE_CHUNK, COMBINE_CHUNK)
                action(pltpu.make_async_copy(y_ref.at[pl.ds(src, COMBINE_CHUNK)],
                                             stage.at[buf, pl.ds(dst, COMBINE_CHUNK)], sem.at[buf]))
                return carry

            lax.fori_loop(0, plan_ref[base + n_exp], body, 0)

    @pl.when(i == 0)
    def _():
        stage[...] = jnp.zeros(stage.shape, BF16)
        each_chunk(i, slot, lambda copy: copy.start())

    @pl.when(i + 1 < pl.num_programs(0))
    def _():
        each_chunk(i + 1, 1 - slot, lambda copy: copy.start())

    each_chunk(i, slot, lambda copy: copy.wait())

    info = info_ref[...]
    loc = loc_ref[...]
    col = lax.broadcasted_iota(jnp.int32, (tile, stage.shape[1]), 1)
    p = jnp.where(col == loc[:, 0:1], info[:, 2:3], 0.0) + jnp.where(col == loc[:, 1:2], info[:, 3:4], 0.0)
    out = h_ref[...] + _dot(p.astype(BF16), stage[slot])
    if final_norm:
        out = _rms(out, gf_ref[...])
    o_ref[...] = out


def _combine(h, info, loc, plan, y_sorted, g_final, final_norm, n_exp):
    n, d = h.shape
    tile = n // (plan.shape[0] // (3 * n_exp))
    grid_spec = pltpu.PrefetchScalarGridSpec(
        num_scalar_prefetch=1,
        grid=(n // tile,),
        in_specs=[pl.BlockSpec((tile, d), lambda i, plan: (i, 0)),
                  pl.BlockSpec((tile, LANES), lambda i, plan: (i, 0)),
                  pl.BlockSpec((tile, TOP_K), lambda i, plan: (i, 0)),
                  pl.BlockSpec((1, d), lambda i, plan: (0, 0)),
                  pl.BlockSpec(memory_space=pl.ANY)],
        out_specs=pl.BlockSpec((tile, d), lambda i, plan: (i, 0)),
        scratch_shapes=[pltpu.VMEM((2, _combine_stage_rows(tile, n_exp), d), BF16), pltpu.SemaphoreType.DMA((2,))],
    )
    return pl.pallas_call(
        functools.partial(_combine_kernel, tile=tile, n_exp=n_exp, final_norm=final_norm),
        grid_spec=grid_spec,
        out_shape=jax.ShapeDtypeStruct((n, d), F32),
        compiler_params=_params("arbitrary"),
        name="moe_combine",
    )(plan, h, info, loc, g_final, y_sorted)


def _moe_layer(h, g, router, wg, wu, wd, g_final, final_norm):
    n, d = h.shape
    n_exp = router.shape[1]
    tm = min(512, n)
    router_padded = jnp.pad(router, ((0, 0), (0, LANES - n_exp)))
    u, info, cnt = _route(h, g, router_padded, n_exp)

    tile = min(512, n)
    n_blocks = n // tile
    idx = info[:, 0:TOP_K].astype(jnp.int32)
    rank = info[:, 4:4 + TOP_K].astype(jnp.int32)
    after = cnt.reshape(n_blocks, 8, LANES)[:, 0, :n_exp].astype(jnp.int32)
    before = jnp.concatenate([jnp.zeros((1, n_exp), jnp.int32), after[:-1]])
    counts = after[-1]
    padded = (counts + tm - 1) // tm * tm
    ends = jnp.cumsum(padded)
    starts = ends - padded
    pos = jnp.sum(jnp.where(idx[..., None] == jnp.arange(n_exp), starts, 0), axis=-1) + rank
    rows = TOP_K * n + n_exp * tm
    n_used = (ends[-1] // tm).astype(jnp.int32)
    tile_ids = jnp.minimum(jnp.arange(rows // tm, dtype=jnp.int32), n_used - 1)
    tile_expert = jnp.sum(tile_ids[:, None] * tm >= ends[None, :], axis=1).astype(jnp.int32)

    run_start = starts[None, :] + before
    run_len = after - before
    fetch_start = run_start // 16 * 16
    n_chunks = jnp.where(run_len > 0, (run_start - fetch_start + run_len + COMBINE_CHUNK - 1) // COMBINE_CHUNK, 0)
    stage_off = COMBINE_CHUNK * (jnp.cumsum(n_chunks, axis=1) - n_chunks)
    plan = jnp.stack([fetch_start, n_chunks, stage_off], axis=1).reshape(-1).astype(jnp.int32)
    shift = jnp.repeat(stage_off - fetch_start, tile, axis=0)
    loc = pos + jnp.sum(jnp.where(idx[..., None] == jnp.arange(n_exp), shift[:, None, :], 0), axis=-1)

    slabs = d // LANES
    assert slabs == 8
    pos_t = pos.T.reshape(TOP_K, n_blocks, tile).transpose(1, 0, 2)
    assert tm % ZERO_CHUNK == 0
    pad_start = jnp.concatenate([starts + counts, ends[-1:]]).astype(jnp.int32)
    pad_cnt = jnp.concatenate([padded - counts, (rows - ends[-1:]) // ZERO_CHUNK]).astype(jnp.int32)
    x_sorted = _dispatch(u, pos_t, pad_start, pad_cnt, rows, tile, slabs)
    y_sorted = _moe_experts(x_sorted, tile_expert, n_used.reshape(1), wg, wu, wd, tm, slabs)
    return _combine(h, info, loc.astype(jnp.int32), plan, y_sorted, g_final, final_norm, n_exp)


def _mla_weights(wq_b, wkv_a, wkv_b, heads, lora):
    def padded(w):
        return jnp.concatenate([w, jnp.zeros(w.shape[:-1] + (LANES - ROPE,), w.dtype)], axis=-1)

    rq = wq_b.shape[0]
    wq = wq_b.reshape(rq, heads, NOPE + ROPE)
    wqb = jnp.concatenate([wq[..., :NOPE].reshape(rq, -1), padded(wq[..., NOPE:]).reshape(rq, -1)], axis=1)
    wkva = jnp.concatenate([wkv_a[:, :lora], padded(wkv_a[:, lora:])], axis=1)
    wkv = wkv_b.reshape(lora, heads, 2 * NOPE)
    wkvb = jnp.concatenate([wkv[..., :NOPE].reshape(lora, -1), wkv[..., NOPE:].reshape(lora, -1)], axis=1)
    return wqb.astype(BF16), wkva.astype(BF16), wkvb.astype(BF16)


def _cast_kernel(x_ref, o_ref):
    o_ref[...] = x_ref[0].astype(BF16)


def _to_bf16(w, layer):
    cols = w.shape[-1]
    w3 = w.reshape(w.shape[0], -1, cols)
    rows = w3.shape[1]
    cap = 256 if cols > 2048 else 1024
    tr = next(t for t in (1024, 512, 256, 128, 64, 32, 16, 8) if t <= cap and rows % t == 0)
    out = pl.pallas_call(
        _cast_kernel,
        grid=(rows // tr,),
        in_specs=[pl.BlockSpec((1, tr, cols), lambda i: (layer, i, 0))],
        out_specs=pl.BlockSpec((tr, cols), lambda i: (i, 0)),
        out_shape=jax.ShapeDtypeStruct((rows, cols), BF16),
        compiler_params=_params("arbitrary"),
        name="weights_to_bf16",
    )(w3)
    return out.reshape(w.shape[1:])


def _final_norm_kernel(h_ref, g_ref, o_ref):
    o_ref[...] = _rms(h_ref[...], g_ref[...])


def _final_norm(h, g):
    n, d = h.shape
    tm = min(512, n)
    return pl.pallas_call(
        _final_norm_kernel,
        grid=(n // tm,),
        in_specs=[pl.BlockSpec((tm, d), lambda i: (i, 0)), pl.BlockSpec((1, d), lambda i: (0, 0))],
        out_specs=pl.BlockSpec((tm, d), lambda i: (i, 0)),
        out_shape=jax.ShapeDtypeStruct((n, d), F32),
        compiler_params=_params("arbitrary"),
        name="final_norm",
    )(h, g)


@jax.jit
def _trunk(x, positions, norm_mix, norm_ffn, norm_final, pool_w, pool_scale,
           conv_w1, conv_b1, conv_dw, conv_bdw, conv_ln_g, conv_ln_b, conv_w2, conv_b2,
           mla_wq_a, mla_q_norm, mla_wq_b, mla_wkv_a, mla_kv_norm, mla_wkv_b, mla_wo,
           ffn_wg, ffn_wu, ffn_wd, moe_router, moe_wg, moe_wu, moe_wd):
    batch, seq, d = x.shape
    depth = norm_mix.shape[0]
    lora = mla_kv_norm.shape[1]
    heads = mla_wo.shape[1] // NOPE
    assert mla_wq_b.shape[2] == heads * (NOPE + ROPE) and mla_wkv_a.shape[2] == lora + ROPE
    n = batch * seq
    h = x.reshape(n, d)
    pos = positions.reshape(n, 1)
    row = lambda v: v.reshape(1, -1)

    half = jnp.arange(0, ROPE, 2, dtype=F32) / ROPE
    inv_freq = ROPE_THETA ** (-half)
    invf = row(jnp.concatenate([inv_freq, inv_freq, jnp.zeros((LANES - ROPE,), F32)]))
    sgn = row(jnp.concatenate([-jnp.ones((ROPE // 2,), F32), jnp.ones((ROPE // 2,), F32),
                               jnp.zeros((LANES - ROPE,), F32)]))

    moe_w = (moe_wg, moe_wu, moe_wd)
    moe_bf16 = {}
    ffn_bf16 = {}

    def moe_cast_job(i):
        nxt = i if i % 2 == 1 else i + 1
        if nxt < depth and nxt // 2 not in moe_bf16:
            return moe_w, nxt // 2
        return (), 0

    for i in range(depth):
        kind, j = i % 3, i // 3
        g = row(norm_mix[i])
        if kind == 0:
            jobs = [(w, f) for f in range(ffn_wg.shape[0]) for w in (ffn_wg, ffn_wu, ffn_wd)] if not ffn_bf16 else []
            h, cast = _pool_layer(h, g, pool_w[j].astype(BF16), row(pool_scale[j]), [w for w, _ in jobs],
                                  [f for _, f in jobs], batch=batch, seq=seq)
            for k in range(0, len(cast), 3):
                ffn_bf16[jobs[k][1]] = cast[k:k + 3]
        elif kind == 1:
            side_w, side_layer = moe_cast_job(i)
            h, cast = _conv_layer(h, g, conv_w1[j].astype(BF16), row(conv_b1[j]), conv_dw[j], row(conv_bdw[j]),
                                  row(conv_ln_g[j]), row(conv_ln_b[j]), conv_w2[j].astype(BF16), row(conv_b2[j]),
                                  side_w, side_layer, batch=batch, seq=seq)
            if cast:
                moe_bf16[side_layer] = cast
        else:
            wqb, wkva, wkvb = _mla_weights(mla_wq_b[j], mla_wkv_a[j], mla_wkv_b[j], heads, lora)
            side_w, side_layer = moe_cast_job(i)
            h, cast = _mla_layer(h, pos, g, mla_wq_a[j].astype(BF16), row(mla_q_norm[j]), wqb, wkva,
                                 row(mla_kv_norm[j]), wkvb, mla_wo[j].T.astype(BF16), invf, sgn,
                                 side_w, side_layer, batch=batch, seq=seq, heads=heads, lora=lora)
            if cast:
                moe_bf16[side_layer] = cast
        g = row(norm_ffn[i])
        f_idx = i // 2
        last = i == depth - 1
        if i % 2 == 0:
            wg, wu, wd = ffn_bf16.get(f_idx) or [_to_bf16(w, f_idx) for w in (ffn_wg, ffn_wu, ffn_wd)]
            h = _ffn_layer(h, g, wg, wu, wd)
            if last:
                h = _final_norm(h, row(norm_final))
        else:
            wg, wu, wd = moe_bf16.get(f_idx) or [_to_bf16(w, f_idx) for w in moe_w]
            h = _moe_layer(h, g, moe_router[f_idx], wg, wu, wd, row(norm_final), last)
    return h.reshape(batch, seq, d)


def kernel(x, positions, norm_mix, norm_ffn, norm_final, pool_w, pool_scale, conv_w1, conv_b1, conv_dw, conv_bdw, conv_ln_g, conv_ln_b, conv_w2, conv_b2, mla_wq_a, mla_q_norm, mla_wq_b, mla_wkv_a, mla_kv_norm, mla_wkv_b, mla_wo, ffn_wg, ffn_wu, ffn_wd, moe_router, moe_wg, moe_wu, moe_wd):
    return _trunk(x, positions, norm_mix, norm_ffn, norm_final, pool_w, pool_scale,
                  conv_w1, conv_b1, conv_dw, conv_bdw, conv_ln_g, conv_ln_b, conv_w2, conv_b2,
                  mla_wq_a, mla_q_norm, mla_wq_b, mla_wkv_a, mla_kv_norm, mla_wkv_b, mla_wo,
                  ffn_wg, ffn_wu, ffn_wd, moe_router, moe_wg, moe_wu, moe_wd)
```

```python
import functools

import jax
import jax.numpy as jnp
from jax import lax
from jax.experimental import pallas as pl
from jax.experimental.pallas import tpu as pltpu

F32 = jnp.float32
BF16 = jnp.bfloat16

NORM_EPS = 1e-6
LN_EPS = 1e-5
ROPE_THETA = 10000.0
LOG2E = 1.4426950408889634

POOL_WINDOWS = (2, 4, 8, 16)
POOL_HALO = 16
CONV_HALO = 32
LANES = 128
SUBLANES = 8
POOL_TOP = POOL_HALO + SUBLANES
NOPE = 128
ROPE = 64
V_ROWS = NOPE + 16
TOP_K = 2
ZERO_CHUNK = 64
COMBINE_CHUNK = 64
RUN_CHUNK = 64

VMEM_LIMIT = 56 * 1024 * 1024


def _params(*sem):
    return pltpu.CompilerParams(dimension_semantics=sem, vmem_limit_bytes=VMEM_LIMIT)


def _rms(x, g, eps=NORM_EPS):
    return x * lax.rsqrt(jnp.mean(x * x, axis=-1, keepdims=True) + eps) * g


def _dot(a, b):
    return jnp.dot(a, b, preferred_element_type=F32)


def _split_bf16(x):
    hi = x.astype(BF16)
    return hi, (x - hi.astype(F32)).astype(BF16)


def _dot_3pass(a, b):
    a_hi, a_lo = _split_bf16(a)
    b_hi, b_lo = _split_bf16(b)
    return _dot(a_hi, b_hi) + (_dot(a_hi, b_lo) + _dot(a_lo, b_hi))


def _ff_chunk(f):
    for c in (1792, 1024, 512, 256, 128):
        if f % c == 0:
            return c
    return f


def _pool_kernel(h_ref, g_ref, w_ref, sc_ref, *rest, ts, gdim, n_side):
    side_in, o_ref, side_out, (ubuf, lvl_a, lvl_b) = rest[:n_side], rest[n_side], rest[n_side + 1:-3], rest[-3:]
    _side_cast(side_in, side_out)
    j = pl.program_id(1)
    h = h_ref[...]
    u = _rms(h, g_ref[...])
    ext = POOL_TOP + ts

    @pl.when(j == 0)
    def _():
        ubuf[0:POOL_TOP, :] = jnp.zeros((POOL_TOP, ubuf.shape[1]), F32)
        lvl_a[0:SUBLANES, :] = jnp.zeros((SUBLANES, gdim), F32)
        lvl_b[0:SUBLANES, :] = jnp.zeros((SUBLANES, gdim), F32)

    ubuf[POOL_TOP:ext, :] = u
    t1 = lax.broadcasted_iota(jnp.int32, (ts, 1), 0) + (j * ts + 1)
    for g, win in enumerate(POOL_WINDOWS):
        cs = slice(g * gdim, (g + 1) * gdim)
        src, cols, spare, k = ubuf, cs, [lvl_a, lvl_b], 1
        while k < win:
            s = src[SUBLANES:ext, cols] + src[SUBLANES - k:ext - k, cols]
            k *= 2
            if k < win:
                dst = spare.pop(0)
                dst[SUBLANES:ext, :] = s
                spare.append(dst)
                src, cols = dst, slice(None)
        s = s[POOL_HALO:POOL_HALO + ts]
        ug = u[:, cs]
        cnt = jnp.minimum(t1, win).astype(F32)
        pooled = s / cnt - ug
        m = _dot(pooled.astype(BF16), w_ref[g])
        o_ref[:, cs] = h[:, cs] + m * sc_ref[:, cs]
    ubuf[SUBLANES:POOL_TOP, :] = ubuf[ts + SUBLANES:ts + POOL_TOP, :]


def _pool_layer(h, g, w, scale, side_weights, side_layers, *, batch, seq):
    n, d = h.shape
    ts = min(512, seq)
    nt = seq // ts
    gdim = d // len(POOL_WINDOWS)
    row = lambda b, j: (b * nt + j, 0)
    const2 = lambda b, j: (0, 0)
    side, side_in, side_out, side_shapes = _side_cast_plan(side_weights, list(side_layers), batch * nt,
                                                           lambda b, j: b * nt + j)
    outs = pl.pallas_call(
        functools.partial(_pool_kernel, ts=ts, gdim=gdim, n_side=len(side)),
        grid=(batch, nt),
        in_specs=[pl.BlockSpec((ts, d), row),
                  pl.BlockSpec((1, d), const2),
                  pl.BlockSpec(w.shape, lambda b, j: (0, 0, 0)),
                  pl.BlockSpec((1, d), const2)] + side_in,
        out_specs=[pl.BlockSpec((ts, d), row)] + side_out,
        out_shape=[jax.ShapeDtypeStruct((n, d), F32)] + side_shapes,
        scratch_shapes=[pltpu.VMEM((ts + POOL_TOP, d), F32), pltpu.VMEM((ts + POOL_TOP, gdim), F32),
                        pltpu.VMEM((ts + POOL_TOP, gdim), F32)],
        compiler_params=_params("arbitrary", "arbitrary"),
        name="pool_mixer",
    )(h, g, w, scale, *side)
    return outs[0], [o.reshape(w.shape[1:]) for o, w in zip(outs[1:], side_weights)]


def _swiglu_step(u, wg, wu, wd):
    a = _dot(u, wg)
    b = _dot(u, wu)
    hh = a * jax.nn.sigmoid(a) * b
    return _dot(hh.astype(BF16), wd)


def _accumulate_steps(j, nj, partial, acc_sc, finish):
    def run(first, last):
        r = partial()
        if not first:
            r = acc_sc[...] + r
        if last:
            finish(r)
        else:
            acc_sc[...] = r

    if nj == 1:
        run(True, True)
        return
    pl.when(j == 0)(functools.partial(run, True, False))
    if nj > 2:
        pl.when((j > 0) & (j < nj - 1))(functools.partial(run, False, False))
    pl.when(j == nj - 1)(functools.partial(run, False, True))


def _ffn_kernel(h_ref, g_ref, wg_ref, wu_ref, wd_ref, o_ref, u_sc, acc_sc, *, nj):
    j = pl.program_id(1)

    @pl.when(j == 0)
    def _():
        u_sc[...] = _rms(h_ref[...], g_ref[...]).astype(BF16)

    def finish(r):
        o_ref[...] = h_ref[...] + r

    _accumulate_steps(j, nj, lambda: _swiglu_step(u_sc[...], wg_ref[...], wu_ref[...], wd_ref[...]), acc_sc, finish)


def _ffn_layer(h, g, wg, wu, wd):
    n, d = h.shape
    f = wg.shape[1]
    tm = min(512, n)
    tf = _ff_chunk(f)
    return pl.pallas_call(
        functools.partial(_ffn_kernel, nj=f // tf),
        grid=(n // tm, f // tf),
        in_specs=[pl.BlockSpec((tm, d), lambda i, j: (i, 0)),
                  pl.BlockSpec((1, d), lambda i, j: (0, 0)),
                  pl.BlockSpec((d, tf), lambda i, j: (0, j)),
                  pl.BlockSpec((d, tf), lambda i, j: (0, j)),
                  pl.BlockSpec((tf, d), lambda i, j: (j, 0))],
        out_specs=pl.BlockSpec((tm, d), lambda i, j: (i, 0)),
        out_shape=jax.ShapeDtypeStruct((n, d), F32),
        scratch_shapes=[pltpu.VMEM((tm, d), BF16), pltpu.VMEM((tm, d), F32)],
        compiler_params=_params("arbitrary", "arbitrary"),
        name="dense_swiglu",
    )(h, g, wg, wu, wd)


def _side_cast_plan(weights, layer, steps, step_index):
    arrays = [w.reshape(w.shape[0], -1, w.shape[-1]) for w in weights]
    if not arrays:
        return [], [], [], []
    nblk = next(n for n in range(steps, 0, -1)
                if all(a.shape[1] % n == 0 and (a.shape[1] // n) % 16 == 0 for a in arrays))
    blk = lambda *g: jnp.minimum(step_index(*g), nblk - 1)
    layers = layer if isinstance(layer, (list, tuple)) else [layer] * len(arrays)
    in_specs, out_specs, out_shapes = [], [], []
    for a, lyr in zip(arrays, layers):
        rows, cols = a.shape[1:]
        in_specs.append(pl.BlockSpec((1, rows // nblk, cols), lambda *g, _l=lyr: (_l, blk(*g), 0)))
        out_specs.append(pl.BlockSpec((rows // nblk, cols), lambda *g: (blk(*g), 0)))
        out_shapes.append(jax.ShapeDtypeStruct((rows, cols), BF16))
    return arrays, in_specs, out_specs, out_shapes


def _side_cast(in_refs, out_refs):
    for src, dst in zip(in_refs, out_refs):
        dst[...] = src[0].astype(BF16)


def _conv_kernel(h_ref, g_ref, w1_ref, b1_ref, dw_ref, bdw_ref, lng_ref, lnb_ref, w2_ref, b2_ref, *rest,
                 ts, width, rb, n_side):
    side_in, o_ref, side_out, (gbuf, cbuf) = rest[:n_side], rest[n_side], rest[n_side + 1:-2], rest[-2:]
    _side_cast(side_in, side_out)
    j = pl.program_id(1)
    d = h_ref.shape[1]
    h = h_ref[...]
    u = _rms(h, g_ref[...]).astype(BF16)
    a = _dot(u, w1_ref[...]) + b1_ref[...]
    glu = a[:, :d] * jax.nn.sigmoid(a[:, d:])

    keep = CONV_HALO + SUBLANES

    @pl.when(j == 0)
    def _():
        gbuf[:, 0:keep, :] = jnp.zeros((SUBLANES, keep, d), F32)

    for b in range(SUBLANES):
        gbuf[b, CONV_HALO + b:CONV_HALO + b + ts, :] = glu
    for cb in range(d // LANES):
        cs = slice(cb * LANES, (cb + 1) * LANES)
        for r0 in range(0, ts, rb):
            acc = jnp.zeros((rb, LANES), F32)
            for k in range(width):
                a, b = divmod(width - 1 - k, SUBLANES)
                off = CONV_HALO + r0 - SUBLANES * a
                acc = acc + gbuf[b, off:off + rb, cs] * dw_ref[k:k + 1, cs]
            cbuf[r0:r0 + rb, cs] = acc + bdw_ref[:, cs]
    gbuf[:, 0:keep, :] = gbuf[:, ts:ts + keep, :]

    c = cbuf[...]
    mu = jnp.mean(c, axis=-1, keepdims=True)
    cc = c - mu
    var = jnp.mean(cc * cc, axis=-1, keepdims=True)
    un = cc * lax.rsqrt(var + LN_EPS) * lng_ref[...] + lnb_ref[...]
    un = un * jax.nn.sigmoid(un)
    o_ref[...] = h + _dot(un.astype(BF16), w2_ref[...]) + b2_ref[...]


def _conv_layer(h, g, w1, b1, dw, bdw, lng, lnb, w2, b2, side_weights, side_layer, *, batch, seq):
    n, d = h.shape
    width = dw.shape[0]
    assert width - 1 <= CONV_HALO
    ts = min(256, seq)
    rb = min(128, ts)
    nt = seq // ts
    row = lambda b, j: (b * nt + j, 0)
    c2 = lambda b, j: (0, 0)
    side, side_in, side_out, side_shapes = _side_cast_plan(side_weights, side_layer, batch * nt,
                                                           lambda b, j: b * nt + j)
    outs = pl.pallas_call(
        functools.partial(_conv_kernel, ts=ts, width=width, rb=rb, n_side=len(side)),
        grid=(batch, nt),
        in_specs=[pl.BlockSpec((ts, d), row),
                  pl.BlockSpec((1, d), c2),
                  pl.BlockSpec(w1.shape, c2),
                  pl.BlockSpec(b1.shape, c2),
                  pl.BlockSpec(dw.shape, c2),
                  pl.BlockSpec((1, d), c2),
                  pl.BlockSpec((1, d), c2),
                  pl.BlockSpec((1, d), c2),
                  pl.BlockSpec(w2.shape, c2),
                  pl.BlockSpec((1, d), c2)] + side_in,
        out_specs=[pl.BlockSpec((ts, d), row)] + side_out,
        out_shape=[jax.ShapeDtypeStruct((n, d), F32)] + side_shapes,
        scratch_shapes=[pltpu.VMEM((SUBLANES, ts + CONV_HALO + SUBLANES, d), F32), pltpu.VMEM((ts, d), F32)],
        compiler_params=_params("arbitrary", "arbitrary"),
        name="conv_module",
    )(h, g, w1, b1, dw, bdw, lng, lnb, w2, b2, *side)
    return outs[0], [o.reshape(w.shape[1:]) for o, w in zip(outs[1:], side_weights)]


def _mla_proj_kernel(h_ref, pos_ref, g_ref, wqa_ref, qnorm_ref, wqb_ref, wkva_ref, kvnorm_ref, wkvb_ref,
                     invf_ref, sgn_ref, qt_o, k_o, vt_o, *, heads, lora):
    hn = heads * NOPE
    rows = h_ref.shape[0]
    u = _rms(h_ref[...], g_ref[...]).astype(BF16)
    qa = _rms(_dot(u, wqa_ref[...]), qnorm_ref[...]).astype(BF16)
    q = _dot(qa, wqb_ref[...])
    kva = _dot(u, wkva_ref[...])
    ckv = _rms(kva[:, :lora], kvnorm_ref[...]).astype(BF16)
    kv = _dot(ckv, wkvb_ref[...])
    ang = pos_ref[...].astype(F32) * invf_ref[...]
    c = jnp.cos(ang)
    s = jnp.sin(ang) * sgn_ref[...]
    lane = lax.broadcasted_iota(jnp.int32, (rows, LANES), 1)

    def rope(x):
        swapped = jnp.where(lane < ROPE // 2, pltpu.roll(x, LANES - ROPE // 2, 1), pltpu.roll(x, ROPE // 2, 1))
        return x * c + swapped * s

    kr = rope(kva[:, lora:lora + LANES]).astype(BF16)
    ones = jnp.ones((V_ROWS - NOPE, rows), BF16)
    for hd in range(heads):
        ns = slice(hd * NOPE, (hd + 1) * NOPE)
        qr = rope(q[:, hn + hd * LANES:hn + (hd + 1) * LANES])
        qt_o[2 * hd * NOPE:(2 * hd + 1) * NOPE, :] = q[:, ns].T.astype(BF16)
        qt_o[(2 * hd + 1) * NOPE:(2 * hd + 2) * NOPE, :] = qr.T.astype(BF16)
        k_o[:, 2 * hd * NOPE:(2 * hd + 1) * NOPE] = kv[:, ns].astype(BF16)
        k_o[:, (2 * hd + 1) * NOPE:(2 * hd + 2) * NOPE] = kr
        vt_o[hd * V_ROWS:hd * V_ROWS + NOPE, :] = kv[:, hn + hd * NOPE:hn + (hd + 1) * NOPE].T.astype(BF16)
        vt_o[hd * V_ROWS + NOPE:(hd + 1) * V_ROWS, :] = ones


def _attn_kernel(qi_ref, kj_ref, qt_ref, k_ref, vt_ref, h_ref, wot_ref, *rest, heads, tq, tk, scale, n_side):
    side_in, o_ref, side_out, (m_sc, acc_sc) = rest[:n_side], rest[n_side], rest[n_side + 1:-2], rest[-2:]
    _side_cast(side_in, side_out)
    i = qi_ref[pl.program_id(1)]
    j = kj_ref[pl.program_id(1)]
    c = scale * LOG2E

    @pl.when(j == 0)
    def _():
        m_sc[...] = jnp.full(m_sc.shape, -jnp.inf, F32)
        acc_sc[...] = jnp.zeros(acc_sc.shape, F32)

    def step(diagonal):
        if diagonal:
            causal = (lax.broadcasted_iota(jnp.int32, (tk, tq), 0)
                      <= lax.broadcasted_iota(jnp.int32, (tk, tq), 1))
        es = [slice(2 * hd * NOPE, (2 * hd + 2) * NOPE) for hd in range(heads)]
        vs = [slice(hd * V_ROWS, (hd + 1) * V_ROWS) for hd in range(heads)]
        ss = [_dot(k_ref[:, e], qt_ref[e, :]) for e in es]
        if diagonal:
            ss = [jnp.where(causal, s, -jnp.inf) for s in ss]
        m_prev = [m_sc[hd] for hd in range(heads)]
        m_new = [jnp.maximum(mp, jnp.max(s, axis=0, keepdims=True)) for mp, s in zip(m_prev, ss)]
        ps = [jnp.exp2((s - mn) * c).astype(BF16) for s, mn in zip(ss, m_new)]
        pv = [_dot(vt_ref[v, :], p) for p, v in zip(ps, vs)]
        for hd in range(heads):
            alpha = jnp.exp2((m_prev[hd] - m_new[hd]) * c)
            acc_sc[vs[hd], :] = alpha * acc_sc[vs[hd], :] + pv[hd]
            m_sc[hd] = m_new[hd]

    @pl.when(j < i)
    def _():
        step(False)

    @pl.when(j == i)
    def _():
        step(True)
        parts = [acc_sc[hd * V_ROWS:hd * V_ROWS + NOPE, :] / acc_sc[hd * V_ROWS + NOPE:hd * V_ROWS + NOPE + 1, :]
                 for hd in range(heads)]
        ot = jnp.concatenate(parts, axis=0).astype(BF16)
        o_ref[...] = h_ref[...] + _dot(wot_ref[...], ot).T


def _mla_layer(h, positions, g, wqa, qnorm, wqb, wkva, kvnorm, wkvb, wot, invf, sgn, side_weights, side_layer,
               *, batch, seq, heads, lora):
    n, d = h.shape
    hn = heads * NOPE
    tp = min(256, n)
    c2 = lambda i: (0, 0)
    row = lambda i: (i, 0)
    col = lambda i: (0, i)
    qt, k, vt = pl.pallas_call(
        functools.partial(_mla_proj_kernel, heads=heads, lora=lora),
        grid=(n // tp,),
        in_specs=[pl.BlockSpec((tp, d), row),
                  pl.BlockSpec((tp, 1), row),
                  pl.BlockSpec((1, d), c2),
                  pl.BlockSpec(wqa.shape, c2),
                  pl.BlockSpec(qnorm.shape, c2),
                  pl.BlockSpec(wqb.shape, c2),
                  pl.BlockSpec(wkva.shape, c2),
                  pl.BlockSpec(kvnorm.shape, c2),
                  pl.BlockSpec(wkvb.shape, c2),
                  pl.BlockSpec((1, LANES), c2),
                  pl.BlockSpec((1, LANES), c2)],
        out_specs=[pl.BlockSpec((2 * hn, tp), col), pl.BlockSpec((tp, 2 * hn), row),
                   pl.BlockSpec((heads * V_ROWS, tp), col)],
        out_shape=[jax.ShapeDtypeStruct((2 * hn, n), BF16), jax.ShapeDtypeStruct((n, 2 * hn), BF16),
                   jax.ShapeDtypeStruct((heads * V_ROWS, n), BF16)],
        compiler_params=_params("arbitrary"),
        name="mla_proj",
    )(h, positions, g, wqa, qnorm, wqb, wkva, kvnorm, wkvb, invf, sgn)

    tq = tk = min(512, seq)
    nq = seq // tq
    scale = float(NOPE + ROPE) ** -0.5
    pairs = [(i, j) for i in range(nq) for j in range(i + 1)]
    qi = jnp.asarray([p[0] for p in pairs], jnp.int32)
    kj = jnp.asarray([p[1] for p in pairs], jnp.int32)
    npairs = len(pairs)
    qrow = lambda b, p, qi, kj: (b * nq + qi[p], 0)
    qcol = lambda b, p, qi, kj: (0, b * nq + qi[p])
    krow = lambda b, p, qi, kj: (b * nq + kj[p], 0)
    kcol = lambda b, p, qi, kj: (0, b * nq + kj[p])
    side, side_in, side_out, side_shapes = _side_cast_plan(side_weights, side_layer, batch * npairs,
                                                           lambda b, p, qi, kj: b * npairs + p)
    grid_spec = pltpu.PrefetchScalarGridSpec(
        num_scalar_prefetch=2,
        grid=(batch, npairs),
        in_specs=[pl.BlockSpec((2 * hn, tq), qcol),
                  pl.BlockSpec((tk, 2 * hn), krow),
                  pl.BlockSpec((heads * V_ROWS, tk), kcol),
                  pl.BlockSpec((tq, d), qrow),
                  pl.BlockSpec(wot.shape, lambda b, p, qi, kj: (0, 0))] + side_in,
        out_specs=[pl.BlockSpec((tq, d), qrow)] + side_out,
        scratch_shapes=[pltpu.VMEM((heads, 1, tq), F32), pltpu.VMEM((heads * V_ROWS, tq), F32)],
    )
    outs = pl.pallas_call(
        functools.partial(_attn_kernel, heads=heads, tq=tq, tk=tk, scale=scale, n_side=len(side)),
        grid_spec=grid_spec,
        out_shape=[jax.ShapeDtypeStruct((n, d), F32)] + side_shapes,
        compiler_params=_params("arbitrary", "arbitrary"),
        name="mla_attention",
    )(qi, kj, qt, k, vt, h, wot, *side)
    return outs[0], [o.reshape(w.shape[1:]) for o, w in zip(outs[1:], side_weights)]


def _store_row_tiles(ref, value, rows):
    slabs = value.shape[1] // LANES
    for s in range(slabs):
        ref[pl.ds(s, rows, stride=slabs), :] = value[:, s * LANES:(s + 1) * LANES]


def _load_row_tile_slab(ref, s, rows, slabs):
    return ref[pl.ds(s, rows, stride=slabs), :]


def _route_kernel(h_ref, g_ref, r_ref, u_ref, info_ref, cnt_ref, carry, *, tr, n_exp):
    i = pl.program_id(0)

    @pl.when(i == 0)
    def _():
        carry[...] = jnp.zeros(carry.shape, F32)

    u = _rms(h_ref[...], g_ref[...])
    u_ref[...] = u.astype(BF16)
    logits = _dot_3pass(u, r_ref[...])
    lane = lax.broadcasted_iota(jnp.int32, (tr, LANES), 1)
    lg = jnp.where(lane < n_exp, logits, -jnp.inf)
    m1 = jnp.max(lg, axis=1, keepdims=True)
    i1 = jnp.min(jnp.where(lg == m1, lane, LANES), axis=1, keepdims=True)
    lg2 = jnp.where(lane == i1, -jnp.inf, lg)
    m2 = jnp.max(lg2, axis=1, keepdims=True)
    i2 = jnp.min(jnp.where(lg2 == m2, lane, LANES), axis=1, keepdims=True)
    e = jnp.exp(m2 - m1)
    g1 = 1.0 / (1.0 + e)
    g2 = e / (1.0 + e)
    sel1 = lane == i1
    sel2 = lane == i2
    onehot = jnp.where(sel1 | sel2, 1.0, 0.0)
    rr = lax.broadcasted_iota(jnp.int32, (tr, tr), 0)
    cc = lax.broadcasted_iota(jnp.int32, (tr, tr), 1)
    tri = jnp.where(cc < rr, 1.0, 0.0).astype(BF16)
    before = _dot(tri, onehot.astype(BF16)) + carry[0:1, :]
    rank1 = jnp.sum(jnp.where(sel1, before, 0.0), axis=1, keepdims=True)
    rank2 = jnp.sum(jnp.where(sel2, before, 0.0), axis=1, keepdims=True)
    carry[...] = carry[...] + jnp.sum(onehot, axis=0, keepdims=True)
    cols = (i1.astype(F32), i2.astype(F32), g1, g2, rank1, rank2)
    info = jnp.zeros((tr, LANES), F32)
    for k, col in enumerate(cols):
        info = jnp.where(lane == k, col, info)
    info_ref[...] = info
    cnt_ref[...] = carry[...]


def _route(h, g, router_padded, n_exp):
    n, d = h.shape
    tr = min(512, n)
    slabs = d // LANES
    return pl.pallas_call(
        functools.partial(_route_kernel, tr=tr, n_exp=n_exp),
        grid=(n // tr,),
        in_specs=[pl.BlockSpec((tr, d), lambda i: (i, 0)),
                  pl.BlockSpec((1, d), lambda i: (0, 0)),
                  pl.BlockSpec((d, LANES), lambda i: (0, 0))],
        out_specs=[pl.BlockSpec((tr, d), lambda i: (i, 0)),
                   pl.BlockSpec((tr, LANES), lambda i: (i, 0)),
                   pl.BlockSpec((8, LANES), lambda i: (i, 0))],
        out_shape=[jax.ShapeDtypeStruct((n, d), BF16),
                   jax.ShapeDtypeStruct((n, LANES), F32),
                   jax.ShapeDtypeStruct((n // tr * 8, LANES), F32)],
        scratch_shapes=[pltpu.VMEM((8, LANES), F32)],
        compiler_params=_params("arbitrary"),
        name="moe_route",
    )(h, g, router_padded)


def _dispatch_kernel(pad_start_ref, pad_cnt_ref, plan_ref, ls_ref, u_ref, x_ref, zero_sc, stage, sem, wsem,
                     *, tile, slabs, n_exp):
    i = pl.program_id(0)
    slot = i % 2

    def row_copy(src, src_row, dst_row):
        return pltpu.make_async_copy(src.at[pl.ds(pl.multiple_of(src_row * slabs, slabs), slabs)],
                                     x_ref.at[pl.ds(pl.multiple_of(dst_row * slabs, slabs), slabs)], sem)

    @pl.when(pl.program_id(0) == 0)
    def _():
        zero_sc[...] = jnp.zeros(zero_sc.shape, F32)
        for e in range(n_exp):
            def fill(i, c):
                row_copy(zero_sc, 0, pad_start_ref[e] + i).start()
                return c

            def drain(i, c):
                row_copy(zero_sc, 0, pad_start_ref[e] + i).wait()
                return c

            lax.fori_loop(0, pad_cnt_ref[e], fill, 0)
            lax.fori_loop(0, pad_cnt_ref[e], drain, 0)

        def chunk_copy(i):
            dst = pl.multiple_of((pad_start_ref[n_exp] + i * ZERO_CHUNK) * slabs, slabs)
            return pltpu.make_async_copy(zero_sc, x_ref.at[pl.ds(dst, ZERO_CHUNK * slabs)], sem)

        def fill_tail(i, c):
            chunk_copy(i).start()
            return c

        def drain_tail(i, c):
            chunk_copy(i).wait()
            return c

        lax.fori_loop(0, pad_cnt_ref[n_exp], fill_tail, 0)
        lax.fori_loop(0, pad_cnt_ref[n_exp], drain_tail, 0)

    def each_run_copy(block, buf, action):
        for e in range(n_exp):
            base = (block * 3) * n_exp + e
            dst0, length, src0 = plan_ref[base], plan_ref[base + n_exp], plan_ref[base + 2 * n_exp]

            def copy(off, nrows):
                s = pl.multiple_of((src0 + off) * slabs, slabs)
                d = pl.multiple_of((dst0 + off) * slabs, slabs)
                return pltpu.make_async_copy(stage.at[buf, pl.ds(s, nrows * slabs)],
                                             x_ref.at[pl.ds(d, nrows * slabs)], wsem.at[buf])

            def body(c, carry):
                action(copy(c * RUN_CHUNK, RUN_CHUNK))
                return carry

            rem = length & (RUN_CHUNK - 1)
            full = length - rem
            lax.fori_loop(0, lax.shift_right_logical(full, RUN_CHUNK.bit_length() - 1), body, 0)
            bit = RUN_CHUNK // 2
            while bit:
                done = rem & (RUN_CHUNK - 2 * bit)
                pl.when((rem & bit) != 0)(functools.partial(lambda o, nb: action(copy(o, nb)), full + done, bit))
                bit //= 2

    ls = ls_ref[0]
    r_idx = lax.broadcasted_iota(jnp.int32, (TOP_K * tile, tile), 0)
    onehot = jnp.where(r_idx == ls[0:1, :], 1.0, 0.0) + jnp.where(r_idx == ls[1:2, :], 1.0, 0.0)
    x = _dot(onehot.astype(BF16), u_ref[...])
    _store_row_tiles(stage.at[slot], x, TOP_K * tile)
    each_run_copy(i, slot, lambda copy: copy.start())

    @pl.when(i > 0)
    def _():
        each_run_copy(i - 1, 1 - slot, lambda copy: copy.wait())

    @pl.when(i == pl.num_programs(0) - 1)
    def _():
        each_run_copy(i, slot, lambda copy: copy.wait())


def _dispatch(u, ls, plan, pad_start, pad_cnt, sorted_rows, tile, slabs):
    n, d = u.shape
    grid_spec = pltpu.PrefetchScalarGridSpec(
        num_scalar_prefetch=3,
        grid=(n // tile,),
        in_specs=[pl.BlockSpec((1, TOP_K, tile), lambda i, ps, pc, pn: (i, 0, 0)),
                  pl.BlockSpec((tile, d), lambda i, ps, pc, pn: (i, 0))],
        out_specs=pl.BlockSpec(memory_space=pl.ANY),
        scratch_shapes=[pltpu.VMEM((ZERO_CHUNK * slabs, LANES), F32),
                        pltpu.VMEM((2, TOP_K * tile * slabs, LANES), F32),
                        pltpu.SemaphoreType.DMA(()), pltpu.SemaphoreType.DMA((2,))],
    )
    return pl.pallas_call(
        functools.partial(_dispatch_kernel, tile=tile, slabs=slabs, n_exp=pad_start.shape[0] - 1),
        grid_spec=grid_spec,
        out_shape=jax.ShapeDtypeStruct((sorted_rows * slabs, LANES), F32),
        compiler_params=_params("arbitrary"),
        name="moe_dispatch",
    )(pad_start, pad_cnt, plan, ls, u)


def _moe_kernel(te_ref, nu_ref, x_ref, wg_ref, wu_ref, wd_ref, o_ref, x_sc, acc_sc, *, tm, slabs, nj):
    t = pl.program_id(0)
    j = pl.program_id(1)

    @pl.when(t < nu_ref[0])
    def _():
        @pl.when(j == 0)
        def _():
            for s in range(slabs):
                x_sc[:, s * LANES:(s + 1) * LANES] = _load_row_tile_slab(x_ref, s, tm, slabs).astype(BF16)

        def finish(r):
            o_ref[...] = r.astype(BF16)

        _accumulate_steps(j, nj, lambda: _swiglu_step(x_sc[...], wg_ref[0], wu_ref[0], wd_ref[0]), acc_sc, finish)

    @pl.when((t >= nu_ref[0]) & (j == 0))
    def _():
        o_ref[...] = jnp.zeros(o_ref.shape, BF16)


def _moe_experts(x_sorted, tile_expert, n_used, wg, wu, wd, tm, slabs):
    d = slabs * LANES
    rows = x_sorted.shape[0] // slabs
    f = wg.shape[2]
    tf = _ff_chunk(f)
    nj = f // tf
    tmap = lambda t, j, te, nu: (jnp.minimum(t, nu[0] - 1), 0)
    jsel = lambda t, j, nu: jnp.where(t < nu[0], j, nj - 1)
    grid_spec = pltpu.PrefetchScalarGridSpec(
        num_scalar_prefetch=2,
        grid=(rows // tm, nj),
        in_specs=[pl.BlockSpec((tm * slabs, LANES), tmap),
                  pl.BlockSpec((1, d, tf), lambda t, j, te, nu: (te[t], 0, jsel(t, j, nu))),
                  pl.BlockSpec((1, d, tf), lambda t, j, te, nu: (te[t], 0, jsel(t, j, nu))),
                  pl.BlockSpec((1, tf, d), lambda t, j, te, nu: (te[t], jsel(t, j, nu), 0))],
        out_specs=pl.BlockSpec((tm, d), lambda t, j, te, nu: (t, 0)),
        scratch_shapes=[pltpu.VMEM((tm, d), BF16), pltpu.VMEM((tm, d), F32)],
    )
    return pl.pallas_call(
        functools.partial(_moe_kernel, tm=tm, slabs=slabs, nj=nj),
        grid_spec=grid_spec,
        out_shape=jax.ShapeDtypeStruct((rows, d), BF16),
        compiler_params=_params("arbitrary", "arbitrary"),
        name="moe_experts",
    )(tile_expert, n_used, x_sorted, wg, wu, wd)


def _combine_stage_rows(tile, n_exp):
    max_chunks = (TOP_K * tile + n_exp * (15 + COMBINE_CHUNK - 1)) // COMBINE_CHUNK
    return -(-max_chunks * COMBINE_CHUNK // 256) * 256


def _combine_kernel(plan_ref, h_ref, info_ref, loc_ref, gf_ref, y_ref, o_ref, stage, sem, *, tile, n_exp, final_norm):
    i = pl.program_id(0)
    slot = i % 2

    def each_chunk(block, buf, action):
        for e in range(n_exp):
            base = (block * 3) * n_exp + e

            def body(c, carry):
                src = pl.multiple_of(plan_ref[base] + c * COMBINE_CHUNK, 16)
                dst = pl.multiple_of(plan_ref[base + 2 * n_exp] + c * COMBINE_CHUNK, COMBINE_CHUNK)
                action(pltpu.make_async_copy(y_ref.at[pl.ds(src, COMBINE_CHUNK)],
                                             stage.at[buf, pl.ds(dst, COMBINE_CHUNK)], sem.at[buf]))
                return carry

            lax.fori_loop(0, plan_ref[base + n_exp], body, 0)

    @pl.when(i == 0)
    def _():
        stage[...] = jnp.zeros(stage.shape, BF16)
        each_chunk(i, slot, lambda copy: copy.start())

    @pl.when(i + 1 < pl.num_programs(0))
    def _():
        each_chunk(i + 1, 1 - slot, lambda copy: copy.start())

    each_chunk(i, slot, lambda copy: copy.wait())

    info = info_ref[...]
    loc = loc_ref[...]
    col = lax.broadcasted_iota(jnp.int32, (tile, stage.shape[1]), 1)
    p = jnp.where(col == loc[:, 0:1], info[:, 2:3], 0.0) + jnp.where(col == loc[:, 1:2], info[:, 3:4], 0.0)
    out = h_ref[...] + _dot(p.astype(BF16), stage[slot])
    if final_norm:
        out = _rms(out, gf_ref[...])
    o_ref[...] = out


def _combine(h, info, loc, plan, y_sorted, g_final, final_norm, n_exp):
    n, d = h.shape
    tile = n // (plan.shape[0] // (3 * n_exp))
    grid_spec = pltpu.PrefetchScalarGridSpec(
        num_scalar_prefetch=1,
        grid=(n // tile,),
        in_specs=[pl.BlockSpec((tile, d), lambda i, plan: (i, 0)),
                  pl.BlockSpec((tile, LANES), lambda i, plan: (i, 0)),
                  pl.BlockSpec((tile, TOP_K), lambda i, plan: (i, 0)),
                  pl.BlockSpec((1, d), lambda i, plan: (0, 0)),
                  pl.BlockSpec(memory_space=pl.ANY)],
        out_specs=pl.BlockSpec((tile, d), lambda i, plan: (i, 0)),
        scratch_shapes=[pltpu.VMEM((2, _combine_stage_rows(tile, n_exp), d), BF16), pltpu.SemaphoreType.DMA((2,))],
    )
    return pl.pallas_call(
        functools.partial(_combine_kernel, tile=tile, n_exp=n_exp, final_norm=final_norm),
        grid_spec=grid_spec,
        out_shape=jax.ShapeDtypeStruct((n, d), F32),
        compiler_params=_params("arbitrary"),
        name="moe_combine",
    )(plan, h, info, loc, g_final, y_sorted)


def _moe_layer(h, g, router, wg, wu, wd, g_final, final_norm):
    n, d = h.shape
    n_exp = router.shape[1]
    tm = min(512, n)
    router_padded = jnp.pad(router, ((0, 0), (0, LANES - n_exp)))
    u, info, cnt = _route(h, g, router_padded, n_exp)

    tile = min(512, n)
    n_blocks = n // tile
    idx = info[:, 0:TOP_K].astype(jnp.int32)
    rank = info[:, 4:4 + TOP_K].astype(jnp.int32)
    after = cnt.reshape(n_blocks, 8, LANES)[:, 0, :n_exp].astype(jnp.int32)
    before = jnp.concatenate([jnp.zeros((1, n_exp), jnp.int32), after[:-1]])
    counts = after[-1]
    padded = (counts + tm - 1) // tm * tm
    ends = jnp.cumsum(padded)
    starts = ends - padded
    pos = jnp.sum(jnp.where(idx[..., None] == jnp.arange(n_exp), starts, 0), axis=-1) + rank
    rows = TOP_K * n + n_exp * tm
    n_used = (ends[-1] // tm).astype(jnp.int32)
    tile_ids = jnp.minimum(jnp.arange(rows // tm, dtype=jnp.int32), n_used - 1)
    tile_expert = jnp.sum(tile_ids[:, None] * tm >= ends[None, :], axis=1).astype(jnp.int32)

    run_start = starts[None, :] + before
    run_len = after - before
    fetch_start = run_start // 16 * 16
    n_chunks = jnp.where(run_len > 0, (run_start - fetch_start + run_len + COMBINE_CHUNK - 1) // COMBINE_CHUNK, 0)
    stage_off = COMBINE_CHUNK * (jnp.cumsum(n_chunks, axis=1) - n_chunks)
    plan = jnp.stack([fetch_start, n_chunks, stage_off], axis=1).reshape(-1).astype(jnp.int32)
    shift = jnp.repeat(stage_off - fetch_start, tile, axis=0)
    loc = pos + jnp.sum(jnp.where(idx[..., None] == jnp.arange(n_exp), shift[:, None, :], 0), axis=-1)

    slabs = d // LANES
    assert slabs == 8
    local_off = jnp.cumsum(run_len, axis=1) - run_len
    dispatch_plan = jnp.stack([run_start, run_len, local_off], axis=1).reshape(-1).astype(jnp.int32)
    local_shift = jnp.repeat(local_off - run_start, tile, axis=0)
    ls = pos + jnp.sum(jnp.where(idx[..., None] == jnp.arange(n_exp), local_shift[:, None, :], 0), axis=-1)
    ls = ls.T.reshape(TOP_K, n_blocks, tile).transpose(1, 0, 2).astype(jnp.int32)
    assert tm % ZERO_CHUNK == 0
    pad_start = jnp.concatenate([starts + counts, ends[-1:]]).astype(jnp.int32)
    pad_cnt = jnp.concatenate([padded - counts, (rows - ends[-1:]) // ZERO_CHUNK]).astype(jnp.int32)
    x_sorted = _dispatch(u, ls, dispatch_plan, pad_start, pad_cnt, rows, tile, slabs)
    y_sorted = _moe_experts(x_sorted, tile_expert, n_used.reshape(1), wg, wu, wd, tm, slabs)
    return _combine(h, info, loc.astype(jnp.int32), plan, y_sorted, g_final, final_norm, n_exp)


def _mla_weights(wq_b, wkv_a, wkv_b, heads, lora):
    def padded(w):
        return jnp.concatenate([w, jnp.zeros(w.shape[:-1] + (LANES - ROPE,), w.dtype)], axis=-1)

    rq = wq_b.shape[0]
    wq = wq_b.reshape(rq, heads, NOPE + ROPE)
    wqb = jnp.concatenate([wq[..., :NOPE].reshape(rq, -1), padded(wq[..., NOPE:]).reshape(rq, -1)], axis=1)
    wkva = jnp.concatenate([wkv_a[:, :lora], padded(wkv_a[:, lora:])], axis=1)
    wkv = wkv_b.reshape(lora, heads, 2 * NOPE)
    wkvb = jnp.concatenate([wkv[..., :NOPE].reshape(lora, -1), wkv[..., NOPE:].reshape(lora, -1)], axis=1)
    return wqb.astype(BF16), wkva.astype(BF16), wkvb.astype(BF16)


def _cast_kernel(x_ref, o_ref):
    o_ref[...] = x_ref[0].astype(BF16)


def _to_bf16(w, layer):
    cols = w.shape[-1]
    w3 = w.reshape(w.shape[0], -1, cols)
    rows = w3.shape[1]
    cap = 256 if cols > 2048 else 1024
    tr = next(t for t in (1024, 512, 256, 128, 64, 32, 16, 8) if t <= cap and rows % t == 0)
    out = pl.pallas_call(
        _cast_kernel,
        grid=(rows // tr,),
        in_specs=[pl.BlockSpec((1, tr, cols), lambda i: (layer, i, 0))],
        out_specs=pl.BlockSpec((tr, cols), lambda i: (i, 0)),
        out_shape=jax.ShapeDtypeStruct((rows, cols), BF16),
        compiler_params=_params("arbitrary"),
        name="weights_to_bf16",
    )(w3)
    return out.reshape(w.shape[1:])


def _final_norm_kernel(h_ref, g_ref, o_ref):
    o_ref[...] = _rms(h_ref[...], g_ref[...])


def _final_norm(h, g):
    n, d = h.shape
    tm = min(512, n)
    return pl.pallas_call(
        _final_norm_kernel,
        grid=(n // tm,),
        in_specs=[pl.BlockSpec((tm, d), lambda i: (i, 0)), pl.BlockSpec((1, d), lambda i: (0, 0))],
        out_specs=pl.BlockSpec((tm, d), lambda i: (i, 0)),
        out_shape=jax.ShapeDtypeStruct((n, d), F32),
        compiler_params=_params("arbitrary"),
        name="final_norm",
    )(h, g)


@jax.jit
def _trunk(x, positions, norm_mix, norm_ffn, norm_final, pool_w, pool_scale,
           conv_w1, conv_b1, conv_dw, conv_bdw, conv_ln_g, conv_ln_b, conv_w2, conv_b2,
           mla_wq_a, mla_q_norm, mla_wq_b, mla_wkv_a, mla_kv_norm, mla_wkv_b, mla_wo,
           ffn_wg, ffn_wu, ffn_wd, moe_router, moe_wg, moe_wu, moe_wd):
    batch, seq, d = x.shape
    depth = norm_mix.shape[0]
    lora = mla_kv_norm.shape[1]
    heads = mla_wo.shape[1] // NOPE
    assert mla_wq_b.shape[2] == heads * (NOPE + ROPE) and mla_wkv_a.shape[2] == lora + ROPE
    n = batch * seq
    h = x.reshape(n, d)
    pos = positions.reshape(n, 1)
    row = lambda v: v.reshape(1, -1)

    half = jnp.arange(0, ROPE, 2, dtype=F32) / ROPE
    inv_freq = ROPE_THETA ** (-half)
    invf = row(jnp.concatenate([inv_freq, inv_freq, jnp.zeros((LANES - ROPE,), F32)]))
    sgn = row(jnp.concatenate([-jnp.ones((ROPE // 2,), F32), jnp.ones((ROPE // 2,), F32),
                               jnp.zeros((LANES - ROPE,), F32)]))

    moe_w = (moe_wg, moe_wu, moe_wd)
    moe_bf16 = {}
    ffn_bf16 = {}

    def moe_cast_job(i):
        nxt = i if i % 2 == 1 else i + 1
        if nxt < depth and nxt // 2 not in moe_bf16:
            return moe_w, nxt // 2
        return (), 0

    for i in range(depth):
        kind, j = i % 3, i // 3
        g = row(norm_mix[i])
        if kind == 0:
            jobs = [(w, f) for f in range(ffn_wg.shape[0]) for w in (ffn_wg, ffn_wu, ffn_wd)] if not ffn_bf16 else []
            h, cast = _pool_layer(h, g, pool_w[j].astype(BF16), row(pool_scale[j]), [w for w, _ in jobs],
                                  [f for _, f in jobs], batch=batch, seq=seq)
            for k in range(0, len(cast), 3):
                ffn_bf16[jobs[k][1]] = cast[k:k + 3]
        elif kind == 1:
            side_w, side_layer = moe_cast_job(i)
            h, cast = _conv_layer(h, g, conv_w1[j].astype(BF16), row(conv_b1[j]), conv_dw[j], row(conv_bdw[j]),
                                  row(conv_ln_g[j]), row(conv_ln_b[j]), conv_w2[j].astype(BF16), row(conv_b2[j]),
                                  side_w, side_layer, batch=batch, seq=seq)
            if cast:
                moe_bf16[side_layer] = cast
        else:
            wqb, wkva, wkvb = _mla_weights(mla_wq_b[j], mla_wkv_a[j], mla_wkv_b[j], heads, lora)
            side_w, side_layer = moe_cast_job(i)
            h, cast = _mla_layer(h, pos, g, mla_wq_a[j].astype(BF16), row(mla_q_norm[j]), wqb, wkva,
                                 row(mla_kv_norm[j]), wkvb, mla_wo[j].T.astype(BF16), invf, sgn,
                                 side_w, side_layer, batch=batch, seq=seq, heads=heads, lora=lora)
            if cast:
                moe_bf16[side_layer] = cast
        g = row(norm_ffn[i])
        f_idx = i // 2
        last = i == depth - 1
        if i % 2 == 0:
            wg, wu, wd = ffn_bf16.get(f_idx) or [_to_bf16(w, f_idx) for w in (ffn_wg, ffn_wu, ffn_wd)]
            h = _ffn_layer(h, g, wg, wu, wd)
            if last:
                h = _final_norm(h, row(norm_final))
        else:
            wg, wu, wd = moe_bf16.get(f_idx) or [_to_bf16(w, f_idx) for w in moe_w]
            h = _moe_layer(h, g, moe_router[f_idx], wg, wu, wd, row(norm_final), last)
    return h.reshape(batch, seq, d)


def kernel(x, positions, norm_mix, norm_ffn, norm_final, pool_w, pool_scale, conv_w1, conv_b1, conv_dw, conv_bdw, conv_ln_g, conv_ln_b, conv_w2, conv_b2, mla_wq_a, mla_q_norm, mla_wq_b, mla_wkv_a, mla_kv_norm, mla_wkv_b, mla_wo, ffn_wg, ffn_wu, ffn_wd, moe_router, moe_wg, moe_wu, moe_wd):
    return _trunk(x, positions, norm_mix, norm_ffn, norm_final, pool_w, pool_scale,
                  conv_w1, conv_b1, conv_dw, conv_bdw, conv_ln_g, conv_ln_b, conv_w2, conv_b2,
                  mla_wq_a, mla_q_norm, mla_wq_b, mla_wkv_a, mla_kv_norm, mla_wkv_b, mla_wo,
                  ffn_wg, ffn_wu, ffn_wd, moe_router, moe_wg, moe_wu, moe_wd)
```

```python
import functools

import jax
import jax.numpy as jnp
from jax import lax
from jax.experimental import pallas as pl
from jax.experimental.pallas import tpu as pltpu

F32 = jnp.float32
BF16 = jnp.bfloat16

NORM_EPS = 1e-6
LN_EPS = 1e-5
ROPE_THETA = 10000.0
LOG2E = 1.4426950408889634

POOL_WINDOWS = (2, 4, 8, 16)
POOL_HALO = 16
CONV_HALO = 32
LANES = 128
SUBLANES = 8
POOL_TOP = POOL_HALO + SUBLANES
NOPE = 128
ROPE = 64
V_ROWS = NOPE + 16
TOP_K = 2
ZERO_CHUNK = 64
COMBINE_CHUNK = 64
RUN_CHUNK = 64

VMEM_LIMIT = 56 * 1024 * 1024


def _params(*sem):
    return pltpu.CompilerParams(dimension_semantics=sem, vmem_limit_bytes=VMEM_LIMIT)


def _rms(x, g, eps=NORM_EPS):
    return x * lax.rsqrt(jnp.mean(x * x, axis=-1, keepdims=True) + eps) * g


def _dot(a, b):
    return jnp.dot(a, b, preferred_element_type=F32)


def _split_bf16(x):
    hi = x.astype(BF16)
    return hi, (x - hi.astype(F32)).astype(BF16)


def _dot_3pass(a, b):
    a_hi, a_lo = _split_bf16(a)
    b_hi, b_lo = _split_bf16(b)
    return _dot(a_hi, b_hi) + (_dot(a_hi, b_lo) + _dot(a_lo, b_hi))


def _ff_chunk(f):
    for c in (1792, 1024, 512, 256, 128):
        if f % c == 0:
            return c
    return f


def _pool_kernel(h_ref, g_ref, w_ref, sc_ref, *rest, ts, gdim, n_side):
    side_in, o_ref, side_out, (ubuf, lvl_a, lvl_b) = rest[:n_side], rest[n_side], rest[n_side + 1:-3], rest[-3:]
    _side_cast(side_in, side_out)
    j = pl.program_id(1)
    h = h_ref[...]
    u = _rms(h, g_ref[...])
    ext = POOL_TOP + ts

    @pl.when(j == 0)
    def _():
        ubuf[0:POOL_TOP, :] = jnp.zeros((POOL_TOP, ubuf.shape[1]), F32)
        lvl_a[0:SUBLANES, :] = jnp.zeros((SUBLANES, gdim), F32)
        lvl_b[0:SUBLANES, :] = jnp.zeros((SUBLANES, gdim), F32)

    ubuf[POOL_TOP:ext, :] = u
    t1 = lax.broadcasted_iota(jnp.int32, (ts, 1), 0) + (j * ts + 1)
    for g, win in enumerate(POOL_WINDOWS):
        cs = slice(g * gdim, (g + 1) * gdim)
        src, cols, spare, k = ubuf, cs, [lvl_a, lvl_b], 1
        while k < win:
            s = src[SUBLANES:ext, cols] + src[SUBLANES - k:ext - k, cols]
            k *= 2
            if k < win:
                dst = spare.pop(0)
                dst[SUBLANES:ext, :] = s
                spare.append(dst)
                src, cols = dst, slice(None)
        s = s[POOL_HALO:POOL_HALO + ts]
        ug = u[:, cs]
        cnt = jnp.minimum(t1, win).astype(F32)
        pooled = s / cnt - ug
        m = _dot(pooled.astype(BF16), w_ref[g])
        o_ref[:, cs] = h[:, cs] + m * sc_ref[:, cs]
    ubuf[SUBLANES:POOL_TOP, :] = ubuf[ts + SUBLANES:ts + POOL_TOP, :]


def _pool_layer(h, g, w, scale, side_weights, side_layers, *, batch, seq):
    n, d = h.shape
    ts = min(512, seq)
    nt = seq // ts
    gdim = d // len(POOL_WINDOWS)
    row = lambda b, j: (b * nt + j, 0)
    const2 = lambda b, j: (0, 0)
    side, side_in, side_out, side_shapes = _side_cast_plan(side_weights, list(side_layers), batch * nt,
                                                           lambda b, j: b * nt + j)
    outs = pl.pallas_call(
        functools.partial(_pool_kernel, ts=ts, gdim=gdim, n_side=len(side)),
        grid=(batch, nt),
        in_specs=[pl.BlockSpec((ts, d), row),
                  pl.BlockSpec((1, d), const2),
                  pl.BlockSpec(w.shape, lambda b, j: (0, 0, 0)),
                  pl.BlockSpec((1, d), const2)] + side_in,
        out_specs=[pl.BlockSpec((ts, d), row)] + side_out,
        out_shape=[jax.ShapeDtypeStruct((n, d), F32)] + side_shapes,
        scratch_shapes=[pltpu.VMEM((ts + POOL_TOP, d), F32), pltpu.VMEM((ts + POOL_TOP, gdim), F32),
                        pltpu.VMEM((ts + POOL_TOP, gdim), F32)],
        compiler_params=_params("arbitrary", "arbitrary"),
        name="pool_mixer",
    )(h, g, w, scale, *side)
    return outs[0], [o.reshape(w.shape[1:]) for o, w in zip(outs[1:], side_weights)]


def _swiglu_step(u, wg, wu, wd):
    a = _dot(u, wg)
    b = _dot(u, wu)
    hh = a * jax.nn.sigmoid(a) * b
    return _dot(hh.astype(BF16), wd)


def _accumulate_steps(j, nj, partial, acc_sc, finish):
    def run(first, last):
        r = partial()
        if not first:
            r = acc_sc[...] + r
        if last:
            finish(r)
        else:
            acc_sc[...] = r

    if nj == 1:
        run(True, True)
        return
    pl.when(j == 0)(functools.partial(run, True, False))
    if nj > 2:
        pl.when((j > 0) & (j < nj - 1))(functools.partial(run, False, False))
    pl.when(j == nj - 1)(functools.partial(run, False, True))


def _ffn_kernel(h_ref, g_ref, wg_ref, wu_ref, wd_ref, o_ref, u_sc, acc_sc, *, nj):
    j = pl.program_id(1)
    t = pl.program_id(2)

    @pl.when(j == 0)
    def _():
        u_sc[t] = _rms(h_ref[...], g_ref[...]).astype(BF16)

    def finish(r):
        o_ref[...] = h_ref[...] + r

    _accumulate_steps(j, nj, lambda: _swiglu_step(u_sc[t], wg_ref[...], wu_ref[...], wd_ref[...]), acc_sc.at[t],
                      finish)


def _tile_group(n_tiles):
    return next(g for g in (4, 2, 1) if n_tiles % g == 0)


def _ffn_layer(h, g, wg, wu, wd):
    n, d = h.shape
    f = wg.shape[1]
    tm = min(512, n)
    tf = _ff_chunk(f)
    nj = f // tf
    grp = _tile_group(n // tm)
    tile = lambda i, j, t: (i * grp + t, 0)
    out_tile = lambda i, j, t: (i * grp + jnp.where(j == nj - 1, t, 0), 0)
    return pl.pallas_call(
        functools.partial(_ffn_kernel, nj=nj),
        grid=(n // tm // grp, nj, grp),
        in_specs=[pl.BlockSpec((tm, d), tile),
                  pl.BlockSpec((1, d), lambda i, j, t: (0, 0)),
                  pl.BlockSpec((d, tf), lambda i, j, t: (0, j)),
                  pl.BlockSpec((d, tf), lambda i, j, t: (0, j)),
                  pl.BlockSpec((tf, d), lambda i, j, t: (j, 0))],
        out_specs=pl.BlockSpec((tm, d), out_tile),
        out_shape=jax.ShapeDtypeStruct((n, d), F32),
        scratch_shapes=[pltpu.VMEM((grp, tm, d), BF16), pltpu.VMEM((grp, tm, d), F32)],
        compiler_params=_params("arbitrary", "arbitrary", "arbitrary"),
        name="dense_swiglu",
    )(h, g, wg, wu, wd)


def _side_cast_plan(weights, layer, steps, step_index):
    arrays = [w.reshape(w.shape[0], -1, w.shape[-1]) for w in weights]
    if not arrays:
        return [], [], [], []
    nblk = next(n for n in range(steps, 0, -1)
                if all(a.shape[1] % n == 0 and (a.shape[1] // n) % 16 == 0 for a in arrays))
    blk = lambda *g: jnp.minimum(step_index(*g), nblk - 1)
    layers = layer if isinstance(layer, (list, tuple)) else [layer] * len(arrays)
    in_specs, out_specs, out_shapes = [], [], []
    for a, lyr in zip(arrays, layers):
        rows, cols = a.shape[1:]
        in_specs.append(pl.BlockSpec((1, rows // nblk, cols), lambda *g, _l=lyr: (_l, blk(*g), 0)))
        out_specs.append(pl.BlockSpec((rows // nblk, cols), lambda *g: (blk(*g), 0)))
        out_shapes.append(jax.ShapeDtypeStruct((rows, cols), BF16))
    return arrays, in_specs, out_specs, out_shapes


def _side_cast(in_refs, out_refs):
    for src, dst in zip(in_refs, out_refs):
        dst[...] = src[0].astype(BF16)


def _conv_kernel(h_ref, g_ref, w1_ref, b1_ref, dw_ref, bdw_ref, lng_ref, lnb_ref, w2_ref, b2_ref, *rest,
                 ts, width, rb, n_side):
    side_in, o_ref, side_out, (gbuf, cbuf) = rest[:n_side], rest[n_side], rest[n_side + 1:-2], rest[-2:]
    _side_cast(side_in, side_out)
    j = pl.program_id(1)
    d = h_ref.shape[1]
    h = h_ref[...]
    u = _rms(h, g_ref[...]).astype(BF16)
    a = _dot(u, w1_ref[...]) + b1_ref[...]
    glu = a[:, :d] * jax.nn.sigmoid(a[:, d:])

    keep = CONV_HALO + SUBLANES

    @pl.when(j == 0)
    def _():
        gbuf[:, 0:keep, :] = jnp.zeros((SUBLANES, keep, d), F32)

    for b in range(SUBLANES):
        gbuf[b, CONV_HALO + b:CONV_HALO + b + ts, :] = glu
    for cb in range(d // LANES):
        cs = slice(cb * LANES, (cb + 1) * LANES)
        for r0 in range(0, ts, rb):
            acc = jnp.zeros((rb, LANES), F32)
            for k in range(width):
                a, b = divmod(width - 1 - k, SUBLANES)
                off = CONV_HALO + r0 - SUBLANES * a
                acc = acc + gbuf[b, off:off + rb, cs] * dw_ref[k:k + 1, cs]
            cbuf[r0:r0 + rb, cs] = acc + bdw_ref[:, cs]
    gbuf[:, 0:keep, :] = gbuf[:, ts:ts + keep, :]

    c = cbuf[...]
    mu = jnp.mean(c, axis=-1, keepdims=True)
    cc = c - mu
    var = jnp.mean(cc * cc, axis=-1, keepdims=True)
    un = cc * lax.rsqrt(var + LN_EPS) * lng_ref[...] + lnb_ref[...]
    un = un * jax.nn.sigmoid(un)
    o_ref[...] = h + _dot(un.astype(BF16), w2_ref[...]) + b2_ref[...]


def _conv_layer(h, g, w1, b1, dw, bdw, lng, lnb, w2, b2, side_weights, side_layer, *, batch, seq):
    n, d = h.shape
    width = dw.shape[0]
    assert width - 1 <= CONV_HALO
    ts = min(256, seq)
    rb = min(128, ts)
    nt = seq // ts
    row = lambda b, j: (b * nt + j, 0)
    c2 = lambda b, j: (0, 0)
    side, side_in, side_out, side_shapes = _side_cast_plan(side_weights, side_layer, batch * nt,
                                                           lambda b, j: b * nt + j)
    outs = pl.pallas_call(
        functools.partial(_conv_kernel, ts=ts, width=width, rb=rb, n_side=len(side)),
        grid=(batch, nt),
        in_specs=[pl.BlockSpec((ts, d), row),
                  pl.BlockSpec((1, d), c2),
                  pl.BlockSpec(w1.shape, c2),
                  pl.BlockSpec(b1.shape, c2),
                  pl.BlockSpec(dw.shape, c2),
                  pl.BlockSpec((1, d), c2),
                  pl.BlockSpec((1, d), c2),
                  pl.BlockSpec((1, d), c2),
                  pl.BlockSpec(w2.shape, c2),
                  pl.BlockSpec((1, d), c2)] + side_in,
        out_specs=[pl.BlockSpec((ts, d), row)] + side_out,
        out_shape=[jax.ShapeDtypeStruct((n, d), F32)] + side_shapes,
        scratch_shapes=[pltpu.VMEM((SUBLANES, ts + CONV_HALO + SUBLANES, d), F32), pltpu.VMEM((ts, d), F32)],
        compiler_params=_params("arbitrary", "arbitrary"),
        name="conv_module",
    )(h, g, w1, b1, dw, bdw, lng, lnb, w2, b2, *side)
    return outs[0], [o.reshape(w.shape[1:]) for o, w in zip(outs[1:], side_weights)]


def _mla_proj_kernel(h_ref, pos_ref, g_ref, wqa_ref, qnorm_ref, wqb_ref, wkva_ref, kvnorm_ref, wkvb_ref,
                     invf_ref, sgn_ref, qt_o, k_o, vt_o, *, heads, lora):
    hn = heads * NOPE
    rows = h_ref.shape[0]
    u = _rms(h_ref[...], g_ref[...]).astype(BF16)
    qa = _rms(_dot(u, wqa_ref[...]), qnorm_ref[...]).astype(BF16)
    q = _dot(qa, wqb_ref[...])
    kva = _dot(u, wkva_ref[...])
    ckv = _rms(kva[:, :lora], kvnorm_ref[...]).astype(BF16)
    kv = _dot(ckv, wkvb_ref[...])
    ang = pos_ref[...].astype(F32) * invf_ref[...]
    c = jnp.cos(ang)
    s = jnp.sin(ang) * sgn_ref[...]
    lane = lax.broadcasted_iota(jnp.int32, (rows, LANES), 1)

    def rope(x):
        swapped = jnp.where(lane < ROPE // 2, pltpu.roll(x, LANES - ROPE // 2, 1), pltpu.roll(x, ROPE // 2, 1))
        return x * c + swapped * s

    kr = rope(kva[:, lora:lora + LANES]).astype(BF16)
    ones = jnp.ones((V_ROWS - NOPE, rows), BF16)
    for hd in range(heads):
        ns = slice(hd * NOPE, (hd + 1) * NOPE)
        qr = rope(q[:, hn + hd * LANES:hn + (hd + 1) * LANES])
        qt_o[2 * hd * NOPE:(2 * hd + 1) * NOPE, :] = q[:, ns].T.astype(BF16)
        qt_o[(2 * hd + 1) * NOPE:(2 * hd + 2) * NOPE, :] = qr.T.astype(BF16)
        k_o[:, 2 * hd * NOPE:(2 * hd + 1) * NOPE] = kv[:, ns].astype(BF16)
        k_o[:, (2 * hd + 1) * NOPE:(2 * hd + 2) * NOPE] = kr
        vt_o[hd * V_ROWS:hd * V_ROWS + NOPE, :] = kv[:, hn + hd * NOPE:hn + (hd + 1) * NOPE].T.astype(BF16)
        vt_o[hd * V_ROWS + NOPE:(hd + 1) * V_ROWS, :] = ones


def _attn_kernel(qi_ref, kj_ref, qt_ref, k_ref, vt_ref, h_ref, wot_ref, *rest, heads, tq, tk, scale, n_side):
    side_in, o_ref, side_out, (m_sc, acc_sc) = rest[:n_side], rest[n_side], rest[n_side + 1:-2], rest[-2:]
    _side_cast(side_in, side_out)
    i = qi_ref[pl.program_id(1)]
    j = kj_ref[pl.program_id(1)]
    c = scale * LOG2E

    @pl.when(j == 0)
    def _():
        m_sc[...] = jnp.full(m_sc.shape, -jnp.inf, F32)
        acc_sc[...] = jnp.zeros(acc_sc.shape, F32)

    def step(diagonal):
        if diagonal:
            causal = (lax.broadcasted_iota(jnp.int32, (tk, tq), 0)
                      <= lax.broadcasted_iota(jnp.int32, (tk, tq), 1))
        es = [slice(2 * hd * NOPE, (2 * hd + 2) * NOPE) for hd in range(heads)]
        vs = [slice(hd * V_ROWS, (hd + 1) * V_ROWS) for hd in range(heads)]
        ss = [_dot(k_ref[:, e], qt_ref[e, :]) for e in es]
        if diagonal:
            ss = [jnp.where(causal, s, -jnp.inf) for s in ss]
        m_prev = [m_sc[hd] for hd in range(heads)]
        m_new = [jnp.maximum(mp, jnp.max(s, axis=0, keepdims=True)) for mp, s in zip(m_prev, ss)]
        ps = [jnp.exp2((s - mn) * c).astype(BF16) for s, mn in zip(ss, m_new)]
        pv = [_dot(vt_ref[v, :], p) for p, v in zip(ps, vs)]
        for hd in range(heads):
            alpha = jnp.exp2((m_prev[hd] - m_new[hd]) * c)
            acc_sc[vs[hd], :] = alpha * acc_sc[vs[hd], :] + pv[hd]
            m_sc[hd] = m_new[hd]

    @pl.when(j < i)
    def _():
        step(False)

    @pl.when(j == i)
    def _():
        step(True)
        parts = [acc_sc[hd * V_ROWS:hd * V_ROWS + NOPE, :] / acc_sc[hd * V_ROWS + NOPE:hd * V_ROWS + NOPE + 1, :]
                 for hd in range(heads)]
        ot = jnp.concatenate(parts, axis=0).astype(BF16)
        o_ref[...] = h_ref[...] + _dot(wot_ref[...], ot).T


def _mla_layer(h, positions, g, wqa, qnorm, wqb, wkva, kvnorm, wkvb, wot, invf, sgn, side_weights, side_layer,
               *, batch, seq, heads, lora):
    n, d = h.shape
    hn = heads * NOPE
    tp = min(256, n)
    c2 = lambda i: (0, 0)
    row = lambda i: (i, 0)
    col = lambda i: (0, i)
    qt, k, vt = pl.pallas_call(
        functools.partial(_mla_proj_kernel, heads=heads, lora=lora),
        grid=(n // tp,),
        in_specs=[pl.BlockSpec((tp, d), row),
                  pl.BlockSpec((tp, 1), row),
                  pl.BlockSpec((1, d), c2),
                  pl.BlockSpec(wqa.shape, c2),
                  pl.BlockSpec(qnorm.shape, c2),
                  pl.BlockSpec(wqb.shape, c2),
                  pl.BlockSpec(wkva.shape, c2),
                  pl.BlockSpec(kvnorm.shape, c2),
                  pl.BlockSpec(wkvb.shape, c2),
                  pl.BlockSpec((1, LANES), c2),
                  pl.BlockSpec((1, LANES), c2)],
        out_specs=[pl.BlockSpec((2 * hn, tp), col), pl.BlockSpec((tp, 2 * hn), row),
                   pl.BlockSpec((heads * V_ROWS, tp), col)],
        out_shape=[jax.ShapeDtypeStruct((2 * hn, n), BF16), jax.ShapeDtypeStruct((n, 2 * hn), BF16),
                   jax.ShapeDtypeStruct((heads * V_ROWS, n), BF16)],
        compiler_params=_params("arbitrary"),
        name="mla_proj",
    )(h, positions, g, wqa, qnorm, wqb, wkva, kvnorm, wkvb, invf, sgn)

    tq = tk = min(512, seq)
    nq = seq // tq
    scale = float(NOPE + ROPE) ** -0.5
    pairs = [(i, j) for i in range(nq) for j in range(i + 1)]
    qi = jnp.asarray([p[0] for p in pairs], jnp.int32)
    kj = jnp.asarray([p[1] for p in pairs], jnp.int32)
    npairs = len(pairs)
    qrow = lambda b, p, qi, kj: (b * nq + qi[p], 0)
    qcol = lambda b, p, qi, kj: (0, b * nq + qi[p])
    krow = lambda b, p, qi, kj: (b * nq + kj[p], 0)
    kcol = lambda b, p, qi, kj: (0, b * nq + kj[p])
    side, side_in, side_out, side_shapes = _side_cast_plan(side_weights, side_layer, batch * npairs,
                                                           lambda b, p, qi, kj: b * npairs + p)
    grid_spec = pltpu.PrefetchScalarGridSpec(
        num_scalar_prefetch=2,
        grid=(batch, npairs),
        in_specs=[pl.BlockSpec((2 * hn, tq), qcol),
                  pl.BlockSpec((tk, 2 * hn), krow),
                  pl.BlockSpec((heads * V_ROWS, tk), kcol),
                  pl.BlockSpec((tq, d), qrow),
                  pl.BlockSpec(wot.shape, lambda b, p, qi, kj: (0, 0))] + side_in,
        out_specs=[pl.BlockSpec((tq, d), qrow)] + side_out,
        scratch_shapes=[pltpu.VMEM((heads, 1, tq), F32), pltpu.VMEM((heads * V_ROWS, tq), F32)],
    )
    outs = pl.pallas_call(
        functools.partial(_attn_kernel, heads=heads, tq=tq, tk=tk, scale=scale, n_side=len(side)),
        grid_spec=grid_spec,
        out_shape=[jax.ShapeDtypeStruct((n, d), F32)] + side_shapes,
        compiler_params=_params("arbitrary", "arbitrary"),
        name="mla_attention",
    )(qi, kj, qt, k, vt, h, wot, *side)
    return outs[0], [o.reshape(w.shape[1:]) for o, w in zip(outs[1:], side_weights)]


def _store_row_tiles(ref, value, rows):
    slabs = value.shape[1] // LANES
    for s in range(slabs):
        ref[pl.ds(s, rows, stride=slabs), :] = value[:, s * LANES:(s + 1) * LANES]


def _load_row_tile_slab(ref, s, rows, slabs):
    return ref[pl.ds(s, rows, stride=slabs), :]


def _route_kernel(h_ref, g_ref, r_ref, u_ref, info_ref, cnt_ref, carry, *, tr, n_exp):
    i = pl.program_id(0)

    @pl.when(i == 0)
    def _():
        carry[...] = jnp.zeros(carry.shape, F32)

    u = _rms(h_ref[...], g_ref[...])
    u_ref[...] = u.astype(BF16)
    logits = _dot_3pass(u, r_ref[...])
    lane = lax.broadcasted_iota(jnp.int32, (tr, LANES), 1)
    lg = jnp.where(lane < n_exp, logits, -jnp.inf)
    m1 = jnp.max(lg, axis=1, keepdims=True)
    i1 = jnp.min(jnp.where(lg == m1, lane, LANES), axis=1, keepdims=True)
    lg2 = jnp.where(lane == i1, -jnp.inf, lg)
    m2 = jnp.max(lg2, axis=1, keepdims=True)
    i2 = jnp.min(jnp.where(lg2 == m2, lane, LANES), axis=1, keepdims=True)
    e = jnp.exp(m2 - m1)
    g1 = 1.0 / (1.0 + e)
    g2 = e / (1.0 + e)
    sel1 = lane == i1
    sel2 = lane == i2
    onehot = jnp.where(sel1 | sel2, 1.0, 0.0)
    rr = lax.broadcasted_iota(jnp.int32, (tr, tr), 0)
    cc = lax.broadcasted_iota(jnp.int32, (tr, tr), 1)
    tri = jnp.where(cc < rr, 1.0, 0.0).astype(BF16)
    before = _dot(tri, onehot.astype(BF16)) + carry[0:1, :]
    rank1 = jnp.sum(jnp.where(sel1, before, 0.0), axis=1, keepdims=True)
    rank2 = jnp.sum(jnp.where(sel2, before, 0.0), axis=1, keepdims=True)
    carry[...] = carry[...] + jnp.sum(onehot, axis=0, keepdims=True)
    cols = (i1.astype(F32), i2.astype(F32), g1, g2, rank1, rank2)
    info = jnp.zeros((tr, LANES), F32)
    for k, col in enumerate(cols):
        info = jnp.where(lane == k, col, info)
    info_ref[...] = info
    cnt_ref[...] = carry[...]


def _route(h, g, router_padded, n_exp):
    n, d = h.shape
    tr = min(512, n)
    slabs = d // LANES
    return pl.pallas_call(
        functools.partial(_route_kernel, tr=tr, n_exp=n_exp),
        grid=(n // tr,),
        in_specs=[pl.BlockSpec((tr, d), lambda i: (i, 0)),
                  pl.BlockSpec((1, d), lambda i: (0, 0)),
                  pl.BlockSpec((d, LANES), lambda i: (0, 0))],
        out_specs=[pl.BlockSpec((tr, d), lambda i: (i, 0)),
                   pl.BlockSpec((tr, LANES), lambda i: (i, 0)),
                   pl.BlockSpec((8, LANES), lambda i: (i, 0))],
        out_shape=[jax.ShapeDtypeStruct((n, d), BF16),
                   jax.ShapeDtypeStruct((n, LANES), F32),
                   jax.ShapeDtypeStruct((n // tr * 8, LANES), F32)],
        scratch_shapes=[pltpu.VMEM((8, LANES), F32)],
        compiler_params=_params("arbitrary"),
        name="moe_route",
    )(h, g, router_padded)


def _dispatch_kernel(pad_start_ref, pad_cnt_ref, plan_ref, ls_ref, u_ref, x_ref, zero_sc, stage, sem, wsem,
                     *, tile, slabs, n_exp):
    i = pl.program_id(0)
    slot = i % 2

    def row_copy(src, src_row, dst_row):
        return pltpu.make_async_copy(src.at[pl.ds(pl.multiple_of(src_row * slabs, slabs), slabs)],
                                     x_ref.at[pl.ds(pl.multiple_of(dst_row * slabs, slabs), slabs)], sem)

    @pl.when(pl.program_id(0) == 0)
    def _():
        zero_sc[...] = jnp.zeros(zero_sc.shape, F32)
        for e in range(n_exp):
            def fill(i, c):
                row_copy(zero_sc, 0, pad_start_ref[e] + i).start()
                return c

            def drain(i, c):
                row_copy(zero_sc, 0, pad_start_ref[e] + i).wait()
                return c

            lax.fori_loop(0, pad_cnt_ref[e], fill, 0)
            lax.fori_loop(0, pad_cnt_ref[e], drain, 0)

        def chunk_copy(i):
            dst = pl.multiple_of((pad_start_ref[n_exp] + i * ZERO_CHUNK) * slabs, slabs)
            return pltpu.make_async_copy(zero_sc, x_ref.at[pl.ds(dst, ZERO_CHUNK * slabs)], sem)

        def fill_tail(i, c):
            chunk_copy(i).start()
            return c

        def drain_tail(i, c):
            chunk_copy(i).wait()
            return c

        lax.fori_loop(0, pad_cnt_ref[n_exp], fill_tail, 0)
        lax.fori_loop(0, pad_cnt_ref[n_exp], drain_tail, 0)

    def each_run_copy(block, buf, action):
        for e in range(n_exp):
            base = (block * 3) * n_exp + e
            dst0, length, src0 = plan_ref[base], plan_ref[base + n_exp], plan_ref[base + 2 * n_exp]

            def copy(off, nrows):
                s = pl.multiple_of((src0 + off) * slabs, slabs)
                d = pl.multiple_of((dst0 + off) * slabs, slabs)
                return pltpu.make_async_copy(stage.at[buf, pl.ds(s, nrows * slabs)],
                                             x_ref.at[pl.ds(d, nrows * slabs)], wsem.at[buf])

            def body(c, carry):
                action(copy(c * RUN_CHUNK, RUN_CHUNK))
                return carry

            rem = length & (RUN_CHUNK - 1)
            full = length - rem
            lax.fori_loop(0, lax.shift_right_logical(full, RUN_CHUNK.bit_length() - 1), body, 0)
            bit = RUN_CHUNK // 2
            while bit:
                done = rem & (RUN_CHUNK - 2 * bit)
                pl.when((rem & bit) != 0)(functools.partial(lambda o, nb: action(copy(o, nb)), full + done, bit))
                bit //= 2

    ls = ls_ref[0]
    r_idx = lax.broadcasted_iota(jnp.int32, (TOP_K * tile, tile), 0)
    onehot = jnp.where(r_idx == ls[0:1, :], 1.0, 0.0) + jnp.where(r_idx == ls[1:2, :], 1.0, 0.0)
    x = _dot(onehot.astype(BF16), u_ref[...])
    _store_row_tiles(stage.at[slot], x, TOP_K * tile)
    each_run_copy(i, slot, lambda copy: copy.start())

    @pl.when(i > 0)
    def _():
        each_run_copy(i - 1, 1 - slot, lambda copy: copy.wait())

    @pl.when(i == pl.num_programs(0) - 1)
    def _():
        each_run_copy(i, slot, lambda copy: copy.wait())


def _dispatch(u, ls, plan, pad_start, pad_cnt, sorted_rows, tile, slabs):
    n, d = u.shape
    grid_spec = pltpu.PrefetchScalarGridSpec(
        num_scalar_prefetch=3,
        grid=(n // tile,),
        in_specs=[pl.BlockSpec((1, TOP_K, tile), lambda i, ps, pc, pn: (i, 0, 0)),
                  pl.BlockSpec((tile, d), lambda i, ps, pc, pn: (i, 0))],
        out_specs=pl.BlockSpec(memory_space=pl.ANY),
        scratch_shapes=[pltpu.VMEM((ZERO_CHUNK * slabs, LANES), F32),
                        pltpu.VMEM((2, TOP_K * tile * slabs, LANES), F32),
                        pltpu.SemaphoreType.DMA(()), pltpu.SemaphoreType.DMA((2,))],
    )
    return pl.pallas_call(
        functools.partial(_dispatch_kernel, tile=tile, slabs=slabs, n_exp=pad_start.shape[0] - 1),
        grid_spec=grid_spec,
        out_shape=jax.ShapeDtypeStruct((sorted_rows * slabs, LANES), F32),
        compiler_params=_params("arbitrary"),
        name="moe_dispatch",
    )(pad_start, pad_cnt, plan, ls, u)


def _moe_kernel(te_ref, nu_ref, x_ref, wg_ref, wu_ref, wd_ref, o_ref, x_sc, acc_sc, *, tm, slabs, nj, grp):
    j = pl.program_id(1)
    k = pl.program_id(2)
    t = pl.program_id(0) * grp + k

    @pl.when(t < nu_ref[0])
    def _():
        @pl.when(j == 0)
        def _():
            for s in range(slabs):
                x_sc[k, :, s * LANES:(s + 1) * LANES] = _load_row_tile_slab(x_ref, s, tm, slabs).astype(BF16)

        def finish(r):
            o_ref[...] = r.astype(BF16)

        _accumulate_steps(j, nj, lambda: _swiglu_step(x_sc[k], wg_ref[0], wu_ref[0], wd_ref[0]), acc_sc.at[k],
                          finish)

    @pl.when((t >= nu_ref[0]) & (j == nj - 1))
    def _():
        o_ref[...] = jnp.zeros(o_ref.shape, BF16)


def _moe_experts(x_sorted, tile_expert, n_used, wg, wu, wd, tm, slabs):
    d = slabs * LANES
    rows = x_sorted.shape[0] // slabs
    f = wg.shape[2]
    tf = _ff_chunk(f)
    nj = f // tf
    n_tiles = rows // tm
    grp = _tile_group(n_tiles)
    tid = lambda i, k: i * grp + k
    tmap = lambda i, j, k, te, nu: (jnp.minimum(tid(i, k), nu[0] - 1), 0)
    jsel = lambda i, j, k, nu: jnp.where(tid(i, k) < nu[0], j, nj - 1)
    wmap = lambda i, j, k, te, nu: (te[tid(i, k)], 0, jsel(i, j, k, nu))
    omap = lambda i, j, k, te, nu: (i * grp + jnp.where(j == nj - 1, k, 0), 0)
    grid_spec = pltpu.PrefetchScalarGridSpec(
        num_scalar_prefetch=2,
        grid=(n_tiles // grp, nj, grp),
        in_specs=[pl.BlockSpec((tm * slabs, LANES), tmap),
                  pl.BlockSpec((1, d, tf), wmap),
                  pl.BlockSpec((1, d, tf), wmap),
                  pl.BlockSpec((1, tf, d), lambda i, j, k, te, nu: (te[tid(i, k)], jsel(i, j, k, nu), 0))],
        out_specs=pl.BlockSpec((tm, d), omap),
        scratch_shapes=[pltpu.VMEM((grp, tm, d), BF16), pltpu.VMEM((grp, tm, d), F32)],
    )
    return pl.pallas_call(
        functools.partial(_moe_kernel, tm=tm, slabs=slabs, nj=nj, grp=grp),
        grid_spec=grid_spec,
        out_shape=jax.ShapeDtypeStruct((rows, d), BF16),
        compiler_params=_params("arbitrary", "arbitrary", "arbitrary"),
        name="moe_experts",
    )(tile_expert, n_used, x_sorted, wg, wu, wd)


def _combine_stage_rows(tile, n_exp):
    max_chunks = (TOP_K * tile + n_exp * (15 + COMBINE_CHUNK - 1)) // COMBINE_CHUNK
    return -(-max_chunks * COMBINE_CHUNK // 256) * 256


def _combine_kernel(plan_ref, h_ref, info_ref, loc_ref, gf_ref, y_ref, o_ref, stage, sem, *, tile, n_exp, final_norm):
    i = pl.program_id(0)
    slot = i % 2

    def each_chunk(block, buf, action):
        for e in range(n_exp):
            base = (block * 3) * n_exp + e

            def body(c, carry):
                src = pl.multiple_of(plan_ref[base] + c * COMBINE_CHUNK, 16)
                dst = pl.multiple_of(plan_ref[base + 2 * n_exp] + c * COMBINE_CHUNK, COMBINE_CHUNK)
                action(pltpu.make_async_copy(y_ref.at[pl.ds(src, COMBINE_CHUNK)],
                                             stage.at[buf, pl.ds(dst, COMBINE_CHUNK)], sem.at[buf]))
                return carry

            lax.fori_loop(0, plan_ref[base + n_exp], body, 0)

    @pl.when(i == 0)
    def _():
        stage[...] = jnp.zeros(stage.shape, BF16)
        each_chunk(i, slot, lambda copy: copy.start())

    @pl.when(i + 1 < pl.num_programs(0))
    def _():
        each_chunk(i + 1, 1 - slot, lambda copy: copy.start())

    each_chunk(i, slot, lambda copy: copy.wait())

    info = info_ref[...]
    loc = loc_ref[...]
    col = lax.broadcasted_iota(jnp.int32, (tile, stage.shape[1]), 1)
    p = jnp.where(col == loc[:, 0:1], info[:, 2:3], 0.0) + jnp.where(col == loc[:, 1:2], info[:, 3:4], 0.0)
    out = h_ref[...] + _dot(p.astype(BF16), stage[slot])
    if final_norm:
        out = _rms(out, gf_ref[...])
    o_ref[...] = out


def _combine(h, info, loc, plan, y_sorted, g_final, final_norm, n_exp):
    n, d = h.shape
    tile = n // (plan.shape[0] // (3 * n_exp))
    grid_spec = pltpu.PrefetchScalarGridSpec(
        num_scalar_prefetch=1,
        grid=(n // tile,),
        in_specs=[pl.BlockSpec((tile, d), lambda i, plan: (i, 0)),
                  pl.BlockSpec((tile, LANES), lambda i, plan: (i, 0)),
                  pl.BlockSpec((tile, TOP_K), lambda i, plan: (i, 0)),
                  pl.BlockSpec((1, d), lambda i, plan: (0, 0)),
                  pl.BlockSpec(memory_space=pl.ANY)],
        out_specs=pl.BlockSpec((tile, d), lambda i, plan: (i, 0)),
        scratch_shapes=[pltpu.VMEM((2, _combine_stage_rows(tile, n_exp), d), BF16), pltpu.SemaphoreType.DMA((2,))],
    )
    return pl.pallas_call(
        functools.partial(_combine_kernel, tile=tile, n_exp=n_exp, final_norm=final_norm),
        grid_spec=grid_spec,
        out_shape=jax.ShapeDtypeStruct((n, d), F32),
        compiler_params=_params("arbitrary"),
        name="moe_combine",
    )(plan, h, info, loc, g_final, y_sorted)


def _moe_layer(h, g, router, wg, wu, wd, g_final, final_norm):
    n, d = h.shape
    n_exp = router.shape[1]
    tm = min(512, n)
    router_padded = jnp.pad(router, ((0, 0), (0, LANES - n_exp)))
    u, info, cnt = _route(h, g, router_padded, n_exp)

    tile = min(512, n)
    n_blocks = n // tile
    idx = info[:, 0:TOP_K].astype(jnp.int32)
    rank = info[:, 4:4 + TOP_K].astype(jnp.int32)
    after = cnt.reshape(n_blocks, 8, LANES)[:, 0, :n_exp].astype(jnp.int32)
    before = jnp.concatenate([jnp.zeros((1, n_exp), jnp.int32), after[:-1]])
    counts = after[-1]
    padded = (counts + tm - 1) // tm * tm
    ends = jnp.cumsum(padded)
    starts = ends - padded
    pos = jnp.sum(jnp.where(idx[..., None] == jnp.arange(n_exp), starts, 0), axis=-1) + rank
    rows = TOP_K * n + n_exp * tm
    n_used = (ends[-1] // tm).astype(jnp.int32)
    tile_ids = jnp.minimum(jnp.arange(rows // tm, dtype=jnp.int32), n_used - 1)
    tile_expert = jnp.sum(tile_ids[:, None] * tm >= ends[None, :], axis=1).astype(jnp.int32)

    run_start = starts[None, :] + before
    run_len = after - before
    fetch_start = run_start // 16 * 16
    n_chunks = jnp.where(run_len > 0, (run_start - fetch_start + run_len + COMBINE_CHUNK - 1) // COMBINE_CHUNK, 0)
    stage_off = COMBINE_CHUNK * (jnp.cumsum(n_chunks, axis=1) - n_chunks)
    plan = jnp.stack([fetch_start, n_chunks, stage_off], axis=1).reshape(-1).astype(jnp.int32)
    shift = jnp.repeat(stage_off - fetch_start, tile, axis=0)
    loc = pos + jnp.sum(jnp.where(idx[..., None] == jnp.arange(n_exp), shift[:, None, :], 0), axis=-1)

    slabs = d // LANES
    assert slabs == 8
    local_off = jnp.cumsum(run_len, axis=1) - run_len
    dispatch_plan = jnp.stack([run_start, run_len, local_off], axis=1).reshape(-1).astype(jnp.int32)
    local_shift = jnp.repeat(local_off - run_start, tile, axis=0)
    ls = pos + jnp.sum(jnp.where(idx[..., None] == jnp.arange(n_exp), local_shift[:, None, :], 0), axis=-1)
    ls = ls.T.reshape(TOP_K, n_blocks, tile).transpose(1, 0, 2).astype(jnp.int32)
    assert tm % ZERO_CHUNK == 0
    pad_start = jnp.concatenate([starts + counts, ends[-1:]]).astype(jnp.int32)
    pad_cnt = jnp.concatenate([padded - counts, (rows - ends[-1:]) // ZERO_CHUNK]).astype(jnp.int32)
    x_sorted = _dispatch(u, ls, dispatch_plan, pad_start, pad_cnt, rows, tile, slabs)
    y_sorted = _moe_experts(x_sorted, tile_expert, n_used.reshape(1), wg, wu, wd, tm, slabs)
    return _combine(h, info, loc.astype(jnp.int32), plan, y_sorted, g_final, final_norm, n_exp)


def _mla_weights(wq_b, wkv_a, wkv_b, heads, lora):
    def padded(w):
        return jnp.concatenate([w, jnp.zeros(w.shape[:-1] + (LANES - ROPE,), w.dtype)], axis=-1)

    rq = wq_b.shape[0]
    wq = wq_b.reshape(rq, heads, NOPE + ROPE)
    wqb = jnp.concatenate([wq[..., :NOPE].reshape(rq, -1), padded(wq[..., NOPE:]).reshape(rq, -1)], axis=1)
    wkva = jnp.concatenate([wkv_a[:, :lora], padded(wkv_a[:, lora:])], axis=1)
    wkv = wkv_b.reshape(lora, heads, 2 * NOPE)
    wkvb = jnp.concatenate([wkv[..., :NOPE].reshape(lora, -1), wkv[..., NOPE:].reshape(lora, -1)], axis=1)
    return wqb.astype(BF16), wkva.astype(BF16), wkvb.astype(BF16)


def _cast_kernel(x_ref, o_ref):
    o_ref[...] = x_ref[0].astype(BF16)


def _to_bf16(w, layer):
    cols = w.shape[-1]
    w3 = w.reshape(w.shape[0], -1, cols)
    rows = w3.shape[1]
    cap = 256 if cols > 2048 else 1024
    tr = next(t for t in (1024, 512, 256, 128, 64, 32, 16, 8) if t <= cap and rows % t == 0)
    out = pl.pallas_call(
        _cast_kernel,
        grid=(rows // tr,),
        in_specs=[pl.BlockSpec((1, tr, cols), lambda i: (layer, i, 0))],
        out_specs=pl.BlockSpec((tr, cols), lambda i: (i, 0)),
        out_shape=jax.ShapeDtypeStruct((rows, cols), BF16),
        compiler_params=_params("arbitrary"),
        name="weights_to_bf16",
    )(w3)
    return out.reshape(w.shape[1:])


def _final_norm_kernel(h_ref, g_ref, o_ref):
    o_ref[...] = _rms(h_ref[...], g_ref[...])


def _final_norm(h, g):
    n, d = h.shape
    tm = min(512, n)
    return pl.pallas_call(
        _final_norm_kernel,
        grid=(n // tm,),
        in_specs=[pl.BlockSpec((tm, d), lambda i: (i, 0)), pl.BlockSpec((1, d), lambda i: (0, 0))],
        out_specs=pl.BlockSpec((tm, d), lambda i: (i, 0)),
        out_shape=jax.ShapeDtypeStruct((n, d), F32),
        compiler_params=_params("arbitrary"),
        name="final_norm",
    )(h, g)


@jax.jit
def _trunk(x, positions, norm_mix, norm_ffn, norm_final, pool_w, pool_scale,
           conv_w1, conv_b1, conv_dw, conv_bdw, conv_ln_g, conv_ln_b, conv_w2, conv_b2,
           mla_wq_a, mla_q_norm, mla_wq_b, mla_wkv_a, mla_kv_norm, mla_wkv_b, mla_wo,
           ffn_wg, ffn_wu, ffn_wd, moe_router, moe_wg, moe_wu, moe_wd):
    batch, seq, d = x.shape
    depth = norm_mix.shape[0]
    lora = mla_kv_norm.shape[1]
    heads = mla_wo.shape[1] // NOPE
    assert mla_wq_b.shape[2] == heads * (NOPE + ROPE) and mla_wkv_a.shape[2] == lora + ROPE
    n = batch * seq
    h = x.reshape(n, d)
    pos = positions.reshape(n, 1)
    row = lambda v: v.reshape(1, -1)

    half = jnp.arange(0, ROPE, 2, dtype=F32) / ROPE
    inv_freq = ROPE_THETA ** (-half)
    invf = row(jnp.concatenate([inv_freq, inv_freq, jnp.zeros((LANES - ROPE,), F32)]))
    sgn = row(jnp.concatenate([-jnp.ones((ROPE // 2,), F32), jnp.ones((ROPE // 2,), F32),
                               jnp.zeros((LANES - ROPE,), F32)]))

    moe_w = (moe_wg, moe_wu, moe_wd)
    moe_bf16 = {}
    ffn_bf16 = {}

    def moe_cast_job(i):
        nxt = i if i % 2 == 1 else i + 1
        if nxt < depth and nxt // 2 not in moe_bf16:
            return moe_w, nxt // 2
        return (), 0

    for i in range(depth):
        kind, j = i % 3, i // 3
        g = row(norm_mix[i])
        if kind == 0:
            jobs = [(w, f) for f in range(ffn_wg.shape[0]) for w in (ffn_wg, ffn_wu, ffn_wd)] if not ffn_bf16 else []
            h, cast = _pool_layer(h, g, pool_w[j].astype(BF16), row(pool_scale[j]), [w for w, _ in jobs],
                                  [f for _, f in jobs], batch=batch, seq=seq)
            for k in range(0, len(cast), 3):
                ffn_bf16[jobs[k][1]] = cast[k:k + 3]
        elif kind == 1:
            side_w, side_layer = moe_cast_job(i)
            h, cast = _conv_layer(h, g, conv_w1[j].astype(BF16), row(conv_b1[j]), conv_dw[j], row(conv_bdw[j]),
                                  row(conv_ln_g[j]), row(conv_ln_b[j]), conv_w2[j].astype(BF16), row(conv_b2[j]),
                                  side_w, side_layer, batch=batch, seq=seq)
            if cast:
                moe_bf16[side_layer] = cast
        else:
            wqb, wkva, wkvb = _mla_weights(mla_wq_b[j], mla_wkv_a[j], mla_wkv_b[j], heads, lora)
            side_w, side_layer = moe_cast_job(i)
            h, cast = _mla_layer(h, pos, g, mla_wq_a[j].astype(BF16), row(mla_q_norm[j]), wqb, wkva,
                                 row(mla_kv_norm[j]), wkvb, mla_wo[j].T.astype(BF16), invf, sgn,
                                 side_w, side_layer, batch=batch, seq=seq, heads=heads, lora=lora)
            if cast:
                moe_bf16[side_layer] = cast
        g = row(norm_ffn[i])
        f_idx = i // 2
        last = i == depth - 1
        if i % 2 == 0:
            wg, wu, wd = ffn_bf16.get(f_idx) or [_to_bf16(w, f_idx) for w in (ffn_wg, ffn_wu, ffn_wd)]
            h = _ffn_layer(h, g, wg, wu, wd)
            if last:
                h = _final_norm(h, row(norm_final))
        else:
            wg, wu, wd = moe_bf16.get(f_idx) or [_to_bf16(w, f_idx) for w in moe_w]
            h = _moe_layer(h, g, moe_router[f_idx], wg, wu, wd, row(norm_final), last)
    return h.reshape(batch, seq, d)


def kernel(x, positions, norm_mix, norm_ffn, norm_final, pool_w, pool_scale, conv_w1, conv_b1, conv_dw, conv_bdw, conv_ln_g, conv_ln_b, conv_w2, conv_b2, mla_wq_a, mla_q_norm, mla_wq_b, mla_wkv_a, mla_kv_norm, mla_wkv_b, mla_wo, ffn_wg, ffn_wu, ffn_wd, moe_router, moe_wg, moe_wu, moe_wd):
    return _trunk(x, positions, norm_mix, norm_ffn, norm_final, pool_w, pool_scale,
                  conv_w1, conv_b1, conv_dw, conv_bdw, conv_ln_g, conv_ln_b, conv_w2, conv_b2,
                  mla_wq_a, mla_q_norm, mla_wq_b, mla_wkv_a, mla_kv_norm, mla_wkv_b, mla_wo,
                  ffn_wg, ffn_wu, ffn_wd, moe_router, moe_wg, moe_wu, moe_wd)
```

```python
import functools

import jax
import jax.numpy as jnp
from jax import lax
from jax.experimental import pallas as pl
from jax.experimental.pallas import tpu as pltpu

F32 = jnp.float32
BF16 = jnp.bfloat16

NORM_EPS = 1e-6
LN_EPS = 1e-5
ROPE_THETA = 10000.0
LOG2E = 1.4426950408889634

POOL_WINDOWS = (2, 4, 8, 16)
POOL_HALO = 16
CONV_HALO = 32
LANES = 128
SUBLANES = 8
POOL_TOP = POOL_HALO + SUBLANES
NOPE = 128
ROPE = 64
V_ROWS = NOPE + 16
TOP_K = 2
ZERO_CHUNK = 64
COMBINE_CHUNK = 32
RUN_CHUNK = 64

VMEM_LIMIT = 56 * 1024 * 1024


def _params(*sem):
    return pltpu.CompilerParams(dimension_semantics=sem, vmem_limit_bytes=VMEM_LIMIT)


def _rms(x, g, eps=NORM_EPS):
    return x * lax.rsqrt(jnp.mean(x * x, axis=-1, keepdims=True) + eps) * g


def _dot(a, b):
    return jnp.dot(a, b, preferred_element_type=F32)


def _split_bf16(x):
    hi = x.astype(BF16)
    return hi, (x - hi.astype(F32)).astype(BF16)


def _dot_3pass(a, b):
    a_hi, a_lo = _split_bf16(a)
    b_hi, b_lo = _split_bf16(b)
    return _dot(a_hi, b_hi) + (_dot(a_hi, b_lo) + _dot(a_lo, b_hi))


def _ff_chunk(f):
    for c in (1792, 1024, 512, 256, 128):
        if f % c == 0:
            return c
    return f


def _pool_kernel(h_ref, g_ref, w_ref, sc_ref, *rest, ts, gdim, n_side):
    side_in, o_ref, side_out, (ubuf, lvl_a, lvl_b) = rest[:n_side], rest[n_side], rest[n_side + 1:-3], rest[-3:]
    _side_cast(side_in, side_out)
    j = pl.program_id(1)
    h = h_ref[...]
    u = _rms(h, g_ref[...])
    ext = POOL_TOP + ts

    @pl.when(j == 0)
    def _():
        ubuf[0:POOL_TOP, :] = jnp.zeros((POOL_TOP, ubuf.shape[1]), F32)
        lvl_a[0:SUBLANES, :] = jnp.zeros((SUBLANES, gdim), F32)
        lvl_b[0:SUBLANES, :] = jnp.zeros((SUBLANES, gdim), F32)

    ubuf[POOL_TOP:ext, :] = u
    t1 = lax.broadcasted_iota(jnp.int32, (ts, 1), 0) + (j * ts + 1)
    for g, win in enumerate(POOL_WINDOWS):
        cs = slice(g * gdim, (g + 1) * gdim)
        src, cols, spare, k = ubuf, cs, [lvl_a, lvl_b], 1
        while k < win:
            s = src[SUBLANES:ext, cols] + src[SUBLANES - k:ext - k, cols]
            k *= 2
            if k < win:
                dst = spare.pop(0)
                dst[SUBLANES:ext, :] = s
                spare.append(dst)
                src, cols = dst, slice(None)
        s = s[POOL_HALO:POOL_HALO + ts]
        ug = u[:, cs]
        cnt = jnp.minimum(t1, win).astype(F32)
        pooled = s / cnt - ug
        m = _dot(pooled.astype(BF16), w_ref[g])
        o_ref[:, cs] = h[:, cs] + m * sc_ref[:, cs]
    ubuf[SUBLANES:POOL_TOP, :] = ubuf[ts + SUBLANES:ts + POOL_TOP, :]


def _pool_layer(h, g, w, scale, side_weights, side_layers, *, batch, seq):
    n, d = h.shape
    ts = min(512, seq)
    nt = seq // ts
    gdim = d // len(POOL_WINDOWS)
    row = lambda b, j: (b * nt + j, 0)
    const2 = lambda b, j: (0, 0)
    side, side_in, side_out, side_shapes = _side_cast_plan(side_weights, list(side_layers), batch * nt,
                                                           lambda b, j: b * nt + j)
    outs = pl.pallas_call(
        functools.partial(_pool_kernel, ts=ts, gdim=gdim, n_side=len(side)),
        grid=(batch, nt),
        in_specs=[pl.BlockSpec((ts, d), row),
                  pl.BlockSpec((1, d), const2),
                  pl.BlockSpec(w.shape, lambda b, j: (0, 0, 0)),
                  pl.BlockSpec((1, d), const2)] + side_in,
        out_specs=[pl.BlockSpec((ts, d), row)] + side_out,
        out_shape=[jax.ShapeDtypeStruct((n, d), F32)] + side_shapes,
        scratch_shapes=[pltpu.VMEM((ts + POOL_TOP, d), F32), pltpu.VMEM((ts + POOL_TOP, gdim), F32),
                        pltpu.VMEM((ts + POOL_TOP, gdim), F32)],
        compiler_params=_params("arbitrary", "arbitrary"),
        name="pool_mixer",
    )(h, g, w, scale, *side)
    return outs[0], [o.reshape(w.shape[1:]) for o, w in zip(outs[1:], side_weights)]


def _swiglu_step(u, wg, wu, wd):
    a = _dot(u, wg)
    b = _dot(u, wu)
    hh = a * jax.nn.sigmoid(a) * b
    return _dot(hh.astype(BF16), wd)


def _accumulate_steps(j, nj, partial, acc_sc, finish):
    def run(first, last):
        r = partial()
        if not first:
            r = acc_sc[...] + r
        if last:
            finish(r)
        else:
            acc_sc[...] = r

    if nj == 1:
        run(True, True)
        return
    pl.when(j == 0)(functools.partial(run, True, False))
    if nj > 2:
        pl.when((j > 0) & (j < nj - 1))(functools.partial(run, False, False))
    pl.when(j == nj - 1)(functools.partial(run, False, True))


def _ffn_kernel(h_ref, g_ref, wg_ref, wu_ref, wd_ref, o_ref, u_sc, acc_sc, *, nj):
    j = pl.program_id(1)

    @pl.when(j == 0)
    def _():
        u_sc[...] = _rms(h_ref[...], g_ref[...]).astype(BF16)

    def finish(r):
        o_ref[...] = h_ref[...] + r

    _accumulate_steps(j, nj, lambda: _swiglu_step(u_sc[...], wg_ref[...], wu_ref[...], wd_ref[...]), acc_sc, finish)


def _ffn_layer(h, g, wg, wu, wd):
    n, d = h.shape
    f = wg.shape[1]
    tm = min(512, n)
    tf = _ff_chunk(f)
    return pl.pallas_call(
        functools.partial(_ffn_kernel, nj=f // tf),
        grid=(n // tm, f // tf),
        in_specs=[pl.BlockSpec((tm, d), lambda i, j: (i, 0)),
                  pl.BlockSpec((1, d), lambda i, j: (0, 0)),
                  pl.BlockSpec((d, tf), lambda i, j: (0, j)),
                  pl.BlockSpec((d, tf), lambda i, j: (0, j)),
                  pl.BlockSpec((tf, d), lambda i, j: (j, 0))],
        out_specs=pl.BlockSpec((tm, d), lambda i, j: (i, 0)),
        out_shape=jax.ShapeDtypeStruct((n, d), F32),
        scratch_shapes=[pltpu.VMEM((tm, d), BF16), pltpu.VMEM((tm, d), F32)],
        compiler_params=_params("arbitrary", "arbitrary"),
        name="dense_swiglu",
    )(h, g, wg, wu, wd)


def _side_cast_plan(weights, layer, steps, step_index):
    arrays = [w.reshape(w.shape[0], -1, w.shape[-1]) for w in weights]
    if not arrays:
        return [], [], [], []
    nblk = next(n for n in range(steps, 0, -1)
                if all(a.shape[1] % n == 0 and (a.shape[1] // n) % 16 == 0 for a in arrays))
    blk = lambda *g: jnp.minimum(step_index(*g), nblk - 1)
    layers = layer if isinstance(layer, (list, tuple)) else [layer] * len(arrays)
    in_specs, out_specs, out_shapes = [], [], []
    for a, lyr in zip(arrays, layers):
        rows, cols = a.shape[1:]
        in_specs.append(pl.BlockSpec((1, rows // nblk, cols), lambda *g, _l=lyr: (_l, blk(*g), 0)))
        out_specs.append(pl.BlockSpec((rows // nblk, cols), lambda *g: (blk(*g), 0)))
        out_shapes.append(jax.ShapeDtypeStruct((rows, cols), BF16))
    return arrays, in_specs, out_specs, out_shapes


def _side_cast(in_refs, out_refs):
    for src, dst in zip(in_refs, out_refs):
        dst[...] = src[0].astype(BF16)


def _conv_kernel(h_ref, g_ref, w1_ref, b1_ref, dw_ref, bdw_ref, lng_ref, lnb_ref, w2_ref, b2_ref, *rest,
                 ts, width, rb, n_side):
    side_in, o_ref, side_out, (gbuf, cbuf) = rest[:n_side], rest[n_side], rest[n_side + 1:-2], rest[-2:]
    _side_cast(side_in, side_out)
    j = pl.program_id(1)
    d = h_ref.shape[1]
    h = h_ref[...]
    u = _rms(h, g_ref[...]).astype(BF16)
    a = _dot(u, w1_ref[...]) + b1_ref[...]
    glu = a[:, :d] * jax.nn.sigmoid(a[:, d:])

    keep = CONV_HALO + SUBLANES

    @pl.when(j == 0)
    def _():
        gbuf[:, 0:keep, :] = jnp.zeros((SUBLANES, keep, d), F32)

    for b in range(SUBLANES):
        gbuf[b, CONV_HALO + b:CONV_HALO + b + ts, :] = glu
    for cb in range(d // LANES):
        cs = slice(cb * LANES, (cb + 1) * LANES)
        for r0 in range(0, ts, rb):
            acc = jnp.zeros((rb, LANES), F32)
            for k in range(width):
                a, b = divmod(width - 1 - k, SUBLANES)
                off = CONV_HALO + r0 - SUBLANES * a
                acc = acc + gbuf[b, off:off + rb, cs] * dw_ref[k:k + 1, cs]
            cbuf[r0:r0 + rb, cs] = acc + bdw_ref[:, cs]
    gbuf[:, 0:keep, :] = gbuf[:, ts:ts + keep, :]

    c = cbuf[...]
    mu = jnp.mean(c, axis=-1, keepdims=True)
    cc = c - mu
    var = jnp.mean(cc * cc, axis=-1, keepdims=True)
    un = cc * lax.rsqrt(var + LN_EPS) * lng_ref[...] + lnb_ref[...]
    un = un * jax.nn.sigmoid(un)
    o_ref[...] = h + _dot(un.astype(BF16), w2_ref[...]) + b2_ref[...]


def _conv_layer(h, g, w1, b1, dw, bdw, lng, lnb, w2, b2, side_weights, side_layer, *, batch, seq):
    n, d = h.shape
    width = dw.shape[0]
    assert width - 1 <= CONV_HALO
    ts = min(256, seq)
    rb = min(128, ts)
    nt = seq // ts
    row = lambda b, j: (b * nt + j, 0)
    c2 = lambda b, j: (0, 0)
    side, side_in, side_out, side_shapes = _side_cast_plan(side_weights, side_layer, batch * nt,
                                                           lambda b, j: b * nt + j)
    outs = pl.pallas_call(
        functools.partial(_conv_kernel, ts=ts, width=width, rb=rb, n_side=len(side)),
        grid=(batch, nt),
        in_specs=[pl.BlockSpec((ts, d), row),
                  pl.BlockSpec((1, d), c2),
                  pl.BlockSpec(w1.shape, c2),
                  pl.BlockSpec(b1.shape, c2),
                  pl.BlockSpec(dw.shape, c2),
                  pl.BlockSpec((1, d), c2),
                  pl.BlockSpec((1, d), c2),
                  pl.BlockSpec((1, d), c2),
                  pl.BlockSpec(w2.shape, c2),
                  pl.BlockSpec((1, d), c2)] + side_in,
        out_specs=[pl.BlockSpec((ts, d), row)] + side_out,
        out_shape=[jax.ShapeDtypeStruct((n, d), F32)] + side_shapes,
        scratch_shapes=[pltpu.VMEM((SUBLANES, ts + CONV_HALO + SUBLANES, d), F32), pltpu.VMEM((ts, d), F32)],
        compiler_params=_params("arbitrary", "arbitrary"),
        name="conv_module",
    )(h, g, w1, b1, dw, bdw, lng, lnb, w2, b2, *side)
    return outs[0], [o.reshape(w.shape[1:]) for o, w in zip(outs[1:], side_weights)]


def _mla_proj_kernel(h_ref, pos_ref, g_ref, wqa_ref, qnorm_ref, wqb_ref, wkva_ref, kvnorm_ref, wkvb_ref,
                     invf_ref, sgn_ref, qt_o, k_o, vt_o, *, heads, lora):
    hn = heads * NOPE
    rows = h_ref.shape[0]
    u = _rms(h_ref[...], g_ref[...]).astype(BF16)
    qa = _rms(_dot(u, wqa_ref[...]), qnorm_ref[...]).astype(BF16)
    q = _dot(qa, wqb_ref[...])
    kva = _dot(u, wkva_ref[...])
    ckv = _rms(kva[:, :lora], kvnorm_ref[...]).astype(BF16)
    kv = _dot(ckv, wkvb_ref[...])
    ang = pos_ref[...].astype(F32) * invf_ref[...]
    c = jnp.cos(ang)
    s = jnp.sin(ang) * sgn_ref[...]
    lane = lax.broadcasted_iota(jnp.int32, (rows, LANES), 1)

    def rope(x):
        swapped = jnp.where(lane < ROPE // 2, pltpu.roll(x, LANES - ROPE // 2, 1), pltpu.roll(x, ROPE // 2, 1))
        return x * c + swapped * s

    kr = rope(kva[:, lora:lora + LANES]).astype(BF16)
    ones = jnp.ones((V_ROWS - NOPE, rows), BF16)
    for hd in range(heads):
        ns = slice(hd * NOPE, (hd + 1) * NOPE)
        qr = rope(q[:, hn + hd * LANES:hn + (hd + 1) * LANES])
        qt_o[2 * hd * NOPE:(2 * hd + 1) * NOPE, :] = q[:, ns].T.astype(BF16)
        qt_o[(2 * hd + 1) * NOPE:(2 * hd + 2) * NOPE, :] = qr.T.astype(BF16)
        k_o[:, 2 * hd * NOPE:(2 * hd + 1) * NOPE] = kv[:, ns].astype(BF16)
        k_o[:, (2 * hd + 1) * NOPE:(2 * hd + 2) * NOPE] = kr
        vt_o[hd * V_ROWS:hd * V_ROWS + NOPE, :] = kv[:, hn + hd * NOPE:hn + (hd + 1) * NOPE].T.astype(BF16)
        vt_o[hd * V_ROWS + NOPE:(hd + 1) * V_ROWS, :] = ones


def _attn_kernel(qi_ref, kj_ref, qt_ref, k_ref, vt_ref, h_ref, wot_ref, *rest, heads, tq, tk, scale, n_side):
    side_in, o_ref, side_out, (m_sc, acc_sc) = rest[:n_side], rest[n_side], rest[n_side + 1:-2], rest[-2:]
    _side_cast(side_in, side_out)
    i = qi_ref[pl.program_id(1)]
    j = kj_ref[pl.program_id(1)]
    c = scale * LOG2E

    @pl.when(j == 0)
    def _():
        m_sc[...] = jnp.full(m_sc.shape, -jnp.inf, F32)
        acc_sc[...] = jnp.zeros(acc_sc.shape, F32)

    def step(diagonal):
        if diagonal:
            causal = (lax.broadcasted_iota(jnp.int32, (tk, tq), 0)
                      <= lax.broadcasted_iota(jnp.int32, (tk, tq), 1))
        es = [slice(2 * hd * NOPE, (2 * hd + 2) * NOPE) for hd in range(heads)]
        vs = [slice(hd * V_ROWS, (hd + 1) * V_ROWS) for hd in range(heads)]
        ss = [_dot(k_ref[:, e], qt_ref[e, :]) for e in es]
        if diagonal:
            ss = [jnp.where(causal, s, -jnp.inf) for s in ss]
        m_prev = [m_sc[hd] for hd in range(heads)]
        m_new = [jnp.maximum(mp, jnp.max(s, axis=0, keepdims=True)) for mp, s in zip(m_prev, ss)]
        ps = [jnp.exp2((s - mn) * c).astype(BF16) for s, mn in zip(ss, m_new)]
        pv = [_dot(vt_ref[v, :], p) for p, v in zip(ps, vs)]
        for hd in range(heads):
            alpha = jnp.exp2((m_prev[hd] - m_new[hd]) * c)
            acc_sc[vs[hd], :] = alpha * acc_sc[vs[hd], :] + pv[hd]
            m_sc[hd] = m_new[hd]

    @pl.when(j < i)
    def _():
        step(False)

    @pl.when(j == i)
    def _():
        step(True)
        parts = [acc_sc[hd * V_ROWS:hd * V_ROWS + NOPE, :] / acc_sc[hd * V_ROWS + NOPE:hd * V_ROWS + NOPE + 1, :]
                 for hd in range(heads)]
        ot = jnp.concatenate(parts, axis=0).astype(BF16)
        o_ref[...] = h_ref[...] + _dot(wot_ref[...], ot).T


def _mla_layer(h, positions, g, wqa, qnorm, wqb, wkva, kvnorm, wkvb, wot, invf, sgn, side_weights, side_layer,
               *, batch, seq, heads, lora):
    n, d = h.shape
    hn = heads * NOPE
    tp = min(512, n)
    c2 = lambda i: (0, 0)
    row = lambda i: (i, 0)
    col = lambda i: (0, i)
    qt, k, vt = pl.pallas_call(
        functools.partial(_mla_proj_kernel, heads=heads, lora=lora),
        grid=(n // tp,),
        in_specs=[pl.BlockSpec((tp, d), row),
                  pl.BlockSpec((tp, 1), row),
                  pl.BlockSpec((1, d), c2),
                  pl.BlockSpec(wqa.shape, c2),
                  pl.BlockSpec(qnorm.shape, c2),
                  pl.BlockSpec(wqb.shape, c2),
                  pl.BlockSpec(wkva.shape, c2),
                  pl.BlockSpec(kvnorm.shape, c2),
                  pl.BlockSpec(wkvb.shape, c2),
                  pl.BlockSpec((1, LANES), c2),
                  pl.BlockSpec((1, LANES), c2)],
        out_specs=[pl.BlockSpec((2 * hn, tp), col), pl.BlockSpec((tp, 2 * hn), row),
                   pl.BlockSpec((heads * V_ROWS, tp), col)],
        out_shape=[jax.ShapeDtypeStruct((2 * hn, n), BF16), jax.ShapeDtypeStruct((n, 2 * hn), BF16),
                   jax.ShapeDtypeStruct((heads * V_ROWS, n), BF16)],
        compiler_params=_params("arbitrary"),
        name="mla_proj",
    )(h, positions, g, wqa, qnorm, wqb, wkva, kvnorm, wkvb, invf, sgn)

    tq = tk = min(512, seq)
    nq = seq // tq
    scale = float(NOPE + ROPE) ** -0.5
    pairs = [(i, j) for i in range(nq) for j in range(i + 1)]
    qi = jnp.asarray([p[0] for p in pairs], jnp.int32)
    kj = jnp.asarray([p[1] for p in pairs], jnp.int32)
    npairs = len(pairs)
    qrow = lambda b, p, qi, kj: (b * nq + qi[p], 0)
    qcol = lambda b, p, qi, kj: (0, b * nq + qi[p])
    krow = lambda b, p, qi, kj: (b * nq + kj[p], 0)
    kcol = lambda b, p, qi, kj: (0, b * nq + kj[p])
    side, side_in, side_out, side_shapes = _side_cast_plan(side_weights, side_layer, batch * npairs,
                                                           lambda b, p, qi, kj: b * npairs + p)
    grid_spec = pltpu.PrefetchScalarGridSpec(
        num_scalar_prefetch=2,
        grid=(batch, npairs),
        in_specs=[pl.BlockSpec((2 * hn, tq), qcol),
                  pl.BlockSpec((tk, 2 * hn), krow),
                  pl.BlockSpec((heads * V_ROWS, tk), kcol),
                  pl.BlockSpec((tq, d), qrow),
                  pl.BlockSpec(wot.shape, lambda b, p, qi, kj: (0, 0))] + side_in,
        out_specs=[pl.BlockSpec((tq, d), qrow)] + side_out,
        scratch_shapes=[pltpu.VMEM((heads, 1, tq), F32), pltpu.VMEM((heads * V_ROWS, tq), F32)],
    )
    outs = pl.pallas_call(
        functools.partial(_attn_kernel, heads=heads, tq=tq, tk=tk, scale=scale, n_side=len(side)),
        grid_spec=grid_spec,
        out_shape=[jax.ShapeDtypeStruct((n, d), F32)] + side_shapes,
        compiler_params=_params("arbitrary", "arbitrary"),
        name="mla_attention",
    )(qi, kj, qt, k, vt, h, wot, *side)
    return outs[0], [o.reshape(w.shape[1:]) for o, w in zip(outs[1:], side_weights)]


def _store_row_tiles(ref, value, rows):
    slabs = value.shape[1] // LANES
    for s in range(slabs):
        ref[pl.ds(s, rows, stride=slabs), :] = value[:, s * LANES:(s + 1) * LANES]


def _load_row_tile_slab(ref, s, rows, slabs):
    return ref[pl.ds(s, rows, stride=slabs), :]


def _route_kernel(h_ref, g_ref, r_ref, u_ref, info_ref, cnt_ref, carry, *, tr, n_exp):
    i = pl.program_id(0)

    @pl.when(i == 0)
    def _():
        carry[...] = jnp.zeros(carry.shape, F32)

    u = _rms(h_ref[...], g_ref[...])
    u_ref[...] = u.astype(BF16)
    logits = _dot_3pass(u, r_ref[...])
    lane = lax.broadcasted_iota(jnp.int32, (tr, LANES), 1)
    lg = jnp.where(lane < n_exp, logits, -jnp.inf)
    m1 = jnp.max(lg, axis=1, keepdims=True)
    i1 = jnp.min(jnp.where(lg == m1, lane, LANES), axis=1, keepdims=True)
    lg2 = jnp.where(lane == i1, -jnp.inf, lg)
    m2 = jnp.max(lg2, axis=1, keepdims=True)
    i2 = jnp.min(jnp.where(lg2 == m2, lane, LANES), axis=1, keepdims=True)
    e = jnp.exp(m2 - m1)
    g1 = 1.0 / (1.0 + e)
    g2 = e / (1.0 + e)
    sel1 = lane == i1
    sel2 = lane == i2
    onehot = jnp.where(sel1 | sel2, 1.0, 0.0)
    rr = lax.broadcasted_iota(jnp.int32, (tr, tr), 0)
    cc = lax.broadcasted_iota(jnp.int32, (tr, tr), 1)
    tri = jnp.where(cc < rr, 1.0, 0.0).astype(BF16)
    before = _dot(tri, onehot.astype(BF16)) + carry[0:1, :]
    rank1 = jnp.sum(jnp.where(sel1, before, 0.0), axis=1, keepdims=True)
    rank2 = jnp.sum(jnp.where(sel2, before, 0.0), axis=1, keepdims=True)
    carry[...] = carry[...] + jnp.sum(onehot, axis=0, keepdims=True)
    cols = (i1.astype(F32), i2.astype(F32), g1, g2, rank1, rank2)
    info = jnp.zeros((tr, LANES), F32)
    for k, col in enumerate(cols):
        info = jnp.where(lane == k, col, info)
    info_ref[...] = info
    cnt_ref[...] = carry[...]


def _route(h, g, router_padded, n_exp):
    n, d = h.shape
    tr = min(512, n)
    slabs = d // LANES
    return pl.pallas_call(
        functools.partial(_route_kernel, tr=tr, n_exp=n_exp),
        grid=(n // tr,),
        in_specs=[pl.BlockSpec((tr, d), lambda i: (i, 0)),
                  pl.BlockSpec((1, d), lambda i: (0, 0)),
                  pl.BlockSpec((d, LANES), lambda i: (0, 0))],
        out_specs=[pl.BlockSpec((tr, d), lambda i: (i, 0)),
                   pl.BlockSpec((tr, LANES), lambda i: (i, 0)),
                   pl.BlockSpec((8, LANES), lambda i: (i, 0))],
        out_shape=[jax.ShapeDtypeStruct((n, d), BF16),
                   jax.ShapeDtypeStruct((n, LANES), F32),
                   jax.ShapeDtypeStruct((n // tr * 8, LANES), F32)],
        scratch_shapes=[pltpu.VMEM((8, LANES), F32)],
        compiler_params=_params("arbitrary"),
        name="moe_route",
    )(h, g, router_padded)


def _dispatch_kernel(pad_start_ref, pad_cnt_ref, plan_ref, ls_ref, u_ref, x_ref, zero_sc, stage, sem, wsem,
                     *, tile, slabs, n_exp):
    i = pl.program_id(0)
    slot = i % 2

    def row_copy(src, src_row, dst_row):
        return pltpu.make_async_copy(src.at[pl.ds(pl.multiple_of(src_row * slabs, slabs), slabs)],
                                     x_ref.at[pl.ds(pl.multiple_of(dst_row * slabs, slabs), slabs)], sem)

    @pl.when(pl.program_id(0) == 0)
    def _():
        zero_sc[...] = jnp.zeros(zero_sc.shape, F32)
        for e in range(n_exp):
            def fill(i, c):
                row_copy(zero_sc, 0, pad_start_ref[e] + i).start()
                return c

            def drain(i, c):
                row_copy(zero_sc, 0, pad_start_ref[e] + i).wait()
                return c

            lax.fori_loop(0, pad_cnt_ref[e], fill, 0)
            lax.fori_loop(0, pad_cnt_ref[e], drain, 0)

        def chunk_copy(i):
            dst = pl.multiple_of((pad_start_ref[n_exp] + i * ZERO_CHUNK) * slabs, slabs)
            return pltpu.make_async_copy(zero_sc, x_ref.at[pl.ds(dst, ZERO_CHUNK * slabs)], sem)

        def fill_tail(i, c):
            chunk_copy(i).start()
            return c

        def drain_tail(i, c):
            chunk_copy(i).wait()
            return c

        lax.fori_loop(0, pad_cnt_ref[n_exp], fill_tail, 0)
        lax.fori_loop(0, pad_cnt_ref[n_exp], drain_tail, 0)

    def each_run_copy(block, buf, action):
        for e in range(n_exp):
            base = (block * 3) * n_exp + e
            dst0, length, src0 = plan_ref[base], plan_ref[base + n_exp], plan_ref[base + 2 * n_exp]

            def copy(off, nrows):
                s = pl.multiple_of((src0 + off) * slabs, slabs)
                d = pl.multiple_of((dst0 + off) * slabs, slabs)
                return pltpu.make_async_copy(stage.at[buf, pl.ds(s, nrows * slabs)],
                                             x_ref.at[pl.ds(d, nrows * slabs)], wsem.at[buf])

            def body(c, carry):
                action(copy(c * RUN_CHUNK, RUN_CHUNK))
                return carry

            rem = length & (RUN_CHUNK - 1)
            full = length - rem
            lax.fori_loop(0, lax.shift_right_logical(full, RUN_CHUNK.bit_length() - 1), body, 0)
            bit = RUN_CHUNK // 2
            while bit:
                done = rem & (RUN_CHUNK - 2 * bit)
                pl.when((rem & bit) != 0)(functools.partial(lambda o, nb: action(copy(o, nb)), full + done, bit))
                bit //= 2

    ls = ls_ref[0]
    r_idx = lax.broadcasted_iota(jnp.int32, (TOP_K * tile, tile), 0)
    onehot = jnp.where(r_idx == ls[0:1, :], 1.0, 0.0) + jnp.where(r_idx == ls[1:2, :], 1.0, 0.0)
    x = _dot(onehot.astype(BF16), u_ref[...])
    _store_row_tiles(stage.at[slot], x, TOP_K * tile)
    each_run_copy(i, slot, lambda copy: copy.start())

    @pl.when(i > 0)
    def _():
        each_run_copy(i - 1, 1 - slot, lambda copy: copy.wait())

    @pl.when(i == pl.num_programs(0) - 1)
    def _():
        each_run_copy(i, slot, lambda copy: copy.wait())


def _dispatch(u, ls, plan, pad_start, pad_cnt, sorted_rows, tile, slabs):
    n, d = u.shape
    grid_spec = pltpu.PrefetchScalarGridSpec(
        num_scalar_prefetch=3,
        grid=(n // tile,),
        in_specs=[pl.BlockSpec((1, TOP_K, tile), lambda i, ps, pc, pn: (i, 0, 0)),
                  pl.BlockSpec((tile, d), lambda i, ps, pc, pn: (i, 0))],
        out_specs=pl.BlockSpec(memory_space=pl.ANY),
        scratch_shapes=[pltpu.VMEM((ZERO_CHUNK * slabs, LANES), F32),
                        pltpu.VMEM((2, TOP_K * tile * slabs, LANES), F32),
                        pltpu.SemaphoreType.DMA(()), pltpu.SemaphoreType.DMA((2,))],
    )
    return pl.pallas_call(
        functools.partial(_dispatch_kernel, tile=tile, slabs=slabs, n_exp=pad_start.shape[0] - 1),
        grid_spec=grid_spec,
        out_shape=jax.ShapeDtypeStruct((sorted_rows * slabs, LANES), F32),
        compiler_params=_params("arbitrary"),
        name="moe_dispatch",
    )(pad_start, pad_cnt, plan, ls, u)


def _moe_kernel(te_ref, nu_ref, x_ref, wg_ref, wu_ref, wd_ref, o_ref, x_sc, acc_sc, *, tm, slabs, nj):
    t = pl.program_id(0)
    j = pl.program_id(1)

    @pl.when(t < nu_ref[0])
    def _():
        @pl.when(j == 0)
        def _():
            for s in range(slabs):
                x_sc[:, s * LANES:(s + 1) * LANES] = _load_row_tile_slab(x_ref, s, tm, slabs).astype(BF16)

        def finish(r):
            o_ref[...] = r.astype(BF16)

        _accumulate_steps(j, nj, lambda: _swiglu_step(x_sc[...], wg_ref[0], wu_ref[0], wd_ref[0]), acc_sc, finish)

    @pl.when((t >= nu_ref[0]) & (j == 0))
    def _():
        o_ref[...] = jnp.zeros(o_ref.shape, BF16)


def _moe_experts(x_sorted, tile_expert, n_used, wg, wu, wd, tm, slabs):
    d = slabs * LANES
    rows = x_sorted.shape[0] // slabs
    f = wg.shape[2]
    tf = _ff_chunk(f)
    nj = f // tf
    tmap = lambda t, j, te, nu: (jnp.minimum(t, nu[0] - 1), 0)
    jsel = lambda t, j, nu: jnp.where(t < nu[0], j, nj - 1)
    grid_spec = pltpu.PrefetchScalarGridSpec(
        num_scalar_prefetch=2,
        grid=(rows // tm, nj),
        in_specs=[pl.BlockSpec((tm * slabs, LANES), tmap),
                  pl.BlockSpec((1, d, tf), lambda t, j, te, nu: (te[t], 0, jsel(t, j, nu))),
                  pl.BlockSpec((1, d, tf), lambda t, j, te, nu: (te[t], 0, jsel(t, j, nu))),
                  pl.BlockSpec((1, tf, d), lambda t, j, te, nu: (te[t], jsel(t, j, nu), 0))],
        out_specs=pl.BlockSpec((tm, d), lambda t, j, te, nu: (t, 0)),
        scratch_shapes=[pltpu.VMEM((tm, d), BF16), pltpu.VMEM((tm, d), F32)],
    )
    return pl.pallas_call(
        functools.partial(_moe_kernel, tm=tm, slabs=slabs, nj=nj),
        grid_spec=grid_spec,
        out_shape=jax.ShapeDtypeStruct((rows, d), BF16),
        compiler_params=_params("arbitrary", "arbitrary"),
        name="moe_experts",
    )(tile_expert, n_used, x_sorted, wg, wu, wd)


def _combine_stage_rows(tile, n_exp):
    max_chunks = (TOP_K * tile + n_exp * (15 + COMBINE_CHUNK - 1)) // COMBINE_CHUNK
    return -(-max_chunks * COMBINE_CHUNK // 256) * 256


def _combine_kernel(plan_ref, h_ref, info_ref, loc_ref, gf_ref, y_ref, o_ref, stage, sem, *, tile, n_exp, final_norm):
    i = pl.program_id(0)
    slot = i % 2

    def each_chunk(block, buf, action):
        for e in range(n_exp):
            base = (block * 3) * n_exp + e

            def body(c, carry):
                src = pl.multiple_of(plan_ref[base] + c * COMBINE_CHUNK, 16)
                dst = pl.multiple_of(plan_ref[base + 2 * n_exp] + c * COMBINE_CHUNK, COMBINE_CHUNK)
                action(pltpu.make_async_copy(y_ref.at[pl.ds(src, COMBINE_CHUNK)],
                                             stage.at[buf, pl.ds(dst, COMBINE_CHUNK)], sem.at[buf]))
                return carry

            lax.fori_loop(0, plan_ref[base + n_exp], body, 0)

    @pl.when(i == 0)
    def _():
        stage[...] = jnp.zeros(stage.shape, BF16)
        each_chunk(i, slot, lambda copy: copy.start())

    @pl.when(i + 1 < pl.num_programs(0))
    def _():
        each_chunk(i + 1, 1 - slot, lambda copy: copy.start())

    each_chunk(i, slot, lambda copy: copy.wait())

    info = info_ref[...]
    loc = loc_ref[...]
    col = lax.broadcasted_iota(jnp.int32, (tile, stage.shape[1]), 1)
    p = jnp.where(col == loc[:, 0:1], info[:, 2:3], 0.0) + jnp.where(col == loc[:, 1:2], info[:, 3:4], 0.0)
    out = h_ref[...] + _dot(p.astype(BF16), stage[slot])
    if final_norm:
        out = _rms(out, gf_ref[...])
    o_ref[...] = out


def _combine(h, info, loc, plan, y_sorted, g_final, final_norm, n_exp):
    n, d = h.shape
    tile = n // (plan.shape[0] // (3 * n_exp))
    grid_spec = pltpu.PrefetchScalarGridSpec(
        num_scalar_prefetch=1,
        grid=(n // tile,),
        in_specs=[pl.BlockSpec((tile, d), lambda i, plan: (i, 0)),
                  pl.BlockSpec((tile, LANES), lambda i, plan: (i, 0)),
                  pl.BlockSpec((tile, TOP_K), lambda i, plan: (i, 0)),
                  pl.BlockSpec((1, d), lambda i, plan: (0, 0)),
                  pl.BlockSpec(memory_space=pl.ANY)],
        out_specs=pl.BlockSpec((tile, d), lambda i, plan: (i, 0)),
        scratch_shapes=[pltpu.VMEM((2, _combine_stage_rows(tile, n_exp), d), BF16), pltpu.SemaphoreType.DMA((2,))],
    )
    return pl.pallas_call(
        functools.partial(_combine_kernel, tile=tile, n_exp=n_exp, final_norm=final_norm),
        grid_spec=grid_spec,
        out_shape=jax.ShapeDtypeStruct((n, d), F32),
        compiler_params=_params("arbitrary"),
        name="moe_combine",
    )(plan, h, info, loc, g_final, y_sorted)


def _moe_layer(h, g, router, wg, wu, wd, g_final, final_norm):
    n, d = h.shape
    n_exp = router.shape[1]
    tm = min(512, n)
    router_padded = jnp.pad(router, ((0, 0), (0, LANES - n_exp)))
    u, info, cnt = _route(h, g, router_padded, n_exp)

    tile = min(512, n)
    n_blocks = n // tile
    idx = info[:, 0:TOP_K].astype(jnp.int32)
    rank = info[:, 4:4 + TOP_K].astype(jnp.int32)
    after = cnt.reshape(n_blocks, 8, LANES)[:, 0, :n_exp].astype(jnp.int32)
    before = jnp.concatenate([jnp.zeros((1, n_exp), jnp.int32), after[:-1]])
    counts = after[-1]
    padded = (counts + tm - 1) // tm * tm
    ends = jnp.cumsum(padded)
    starts = ends - padded
    pos = jnp.sum(jnp.where(idx[..., None] == jnp.arange(n_exp), starts, 0), axis=-1) + rank
    rows = TOP_K * n + n_exp * tm
    n_used = (ends[-1] // tm).astype(jnp.int32)
    tile_ids = jnp.minimum(jnp.arange(rows // tm, dtype=jnp.int32), n_used - 1)
    tile_expert = jnp.sum(tile_ids[:, None] * tm >= ends[None, :], axis=1).astype(jnp.int32)

    run_start = starts[None, :] + before
    run_len = after - before
    fetch_start = run_start // 16 * 16
    n_chunks = jnp.where(run_len > 0, (run_start - fetch_start + run_len + COMBINE_CHUNK - 1) // COMBINE_CHUNK, 0)
    stage_off = COMBINE_CHUNK * (jnp.cumsum(n_chunks, axis=1) - n_chunks)
    plan = jnp.stack([fetch_start, n_chunks, stage_off], axis=1).reshape(-1).astype(jnp.int32)
    shift = jnp.repeat(stage_off - fetch_start, tile, axis=0)
    loc = pos + jnp.sum(jnp.where(idx[..., None] == jnp.arange(n_exp), shift[:, None, :], 0), axis=-1)

    slabs = d // LANES
    assert slabs == 8
    local_off = jnp.cumsum(run_len, axis=1) - run_len
    dispatch_plan = jnp.stack([run_start, run_len, local_off], axis=1).reshape(-1).astype(jnp.int32)
    local_shift = jnp.repeat(local_off - run_start, tile, axis=0)
    ls = pos + jnp.sum(jnp.where(idx[..., None] == jnp.arange(n_exp), local_shift[:, None, :], 0), axis=-1)
    ls = ls.T.reshape(TOP_K, n_blocks, tile).transpose(1, 0, 2).astype(jnp.int32)
    assert tm % ZERO_CHUNK == 0
    pad_start = jnp.concatenate([starts + counts, ends[-1:]]).astype(jnp.int32)
    pad_cnt = jnp.concatenate([padded - counts, (rows - ends[-1:]) // ZERO_CHUNK]).astype(jnp.int32)
    x_sorted = _dispatch(u, ls, dispatch_plan, pad_start, pad_cnt, rows, tile, slabs)
    y_sorted = _moe_experts(x_sorted, tile_expert, n_used.reshape(1), wg, wu, wd, tm, slabs)
    return _combine(h, info, loc.astype(jnp.int32), plan, y_sorted, g_final, final_norm, n_exp)


def _mla_weights(wq_b, wkv_a, wkv_b, heads, lora):
    def padded(w):
        return jnp.concatenate([w, jnp.zeros(w.shape[:-1] + (LANES - ROPE,), w.dtype)], axis=-1)

    rq = wq_b.shape[0]
    wq = wq_b.reshape(rq, heads, NOPE + ROPE)
    wqb = jnp.concatenate([wq[..., :NOPE].reshape(rq, -1), padded(wq[..., NOPE:]).reshape(rq, -1)], axis=1)
    wkva = jnp.concatenate([wkv_a[:, :lora], padded(wkv_a[:, lora:])], axis=1)
    wkv = wkv_b.reshape(lora, heads, 2 * NOPE)
    wkvb = jnp.concatenate([wkv[..., :NOPE].reshape(lora, -1), wkv[..., NOPE:].reshape(lora, -1)], axis=1)
    return wqb.astype(BF16), wkva.astype(BF16), wkvb.astype(BF16)


def _cast_kernel(x_ref, o_ref):
    o_ref[...] = x_ref[0].astype(BF16)


def _to_bf16(w, layer):
    cols = w.shape[-1]
    w3 = w.reshape(w.shape[0], -1, cols)
    rows = w3.shape[1]
    cap = 256 if cols > 2048 else 1024
    tr = next(t for t in (1024, 512, 256, 128, 64, 32, 16, 8) if t <= cap and rows % t == 0)
    out = pl.pallas_call(
        _cast_kernel,
        grid=(rows // tr,),
        in_specs=[pl.BlockSpec((1, tr, cols), lambda i: (layer, i, 0))],
        out_specs=pl.BlockSpec((tr, cols), lambda i: (i, 0)),
        out_shape=jax.ShapeDtypeStruct((rows, cols), BF16),
        compiler_params=_params("arbitrary"),
        name="weights_to_bf16",
    )(w3)
    return out.reshape(w.shape[1:])


def _final_norm_kernel(h_ref, g_ref, o_ref):
    o_ref[...] = _rms(h_ref[...], g_ref[...])


def _final_norm(h, g):
    n, d = h.shape
    tm = min(512, n)
    return pl.pallas_call(
        _final_norm_kernel,
        grid=(n // tm,),
        in_specs=[pl.BlockSpec((tm, d), lambda i: (i, 0)), pl.BlockSpec((1, d), lambda i: (0, 0))],
        out_specs=pl.BlockSpec((tm, d), lambda i: (i, 0)),
        out_shape=jax.ShapeDtypeStruct((n, d), F32),
        compiler_params=_params("arbitrary"),
        name="final_norm",
    )(h, g)


@jax.jit
def _trunk(x, positions, norm_mix, norm_ffn, norm_final, pool_w, pool_scale,
           conv_w1, conv_b1, conv_dw, conv_bdw, conv_ln_g, conv_ln_b, conv_w2, conv_b2,
           mla_wq_a, mla_q_norm, mla_wq_b, mla_wkv_a, mla_kv_norm, mla_wkv_b, mla_wo,
           ffn_wg, ffn_wu, ffn_wd, moe_router, moe_wg, moe_wu, moe_wd):
    batch, seq, d = x.shape
    depth = norm_mix.shape[0]
    lora = mla_kv_norm.shape[1]
    heads = mla_wo.shape[1] // NOPE
    assert mla_wq_b.shape[2] == heads * (NOPE + ROPE) and mla_wkv_a.shape[2] == lora + ROPE
    n = batch * seq
    h = x.reshape(n, d)
    pos = positions.reshape(n, 1)
    row = lambda v: v.reshape(1, -1)

    half = jnp.arange(0, ROPE, 2, dtype=F32) / ROPE
    inv_freq = ROPE_THETA ** (-half)
    invf = row(jnp.concatenate([inv_freq, inv_freq, jnp.zeros((LANES - ROPE,), F32)]))
    sgn = row(jnp.concatenate([-jnp.ones((ROPE // 2,), F32), jnp.ones((ROPE // 2,), F32),
                               jnp.zeros((LANES - ROPE,), F32)]))

    moe_w = (moe_wg, moe_wu, moe_wd)
    moe_bf16 = {}
    ffn_bf16 = {}

    def moe_cast_job(i):
        nxt = i if i % 2 == 1 else i + 1
        if nxt < depth and nxt // 2 not in moe_bf16:
            return moe_w, nxt // 2
        return (), 0

    for i in range(depth):
        kind, j = i % 3, i // 3
        g = row(norm_mix[i])
        if kind == 0:
            jobs = [(w, f) for f in range(ffn_wg.shape[0]) for w in (ffn_wg, ffn_wu, ffn_wd)] if not ffn_bf16 else []
            h, cast = _pool_layer(h, g, pool_w[j].astype(BF16), row(pool_scale[j]), [w for w, _ in jobs],
                                  [f for _, f in jobs], batch=batch, seq=seq)
            for k in range(0, len(cast), 3):
                ffn_bf16[jobs[k][1]] = cast[k:k + 3]
        elif kind == 1:
            side_w, side_layer = moe_cast_job(i)
            h, cast = _conv_layer(h, g, conv_w1[j].astype(BF16), row(conv_b1[j]), conv_dw[j], row(conv_bdw[j]),
                                  row(conv_ln_g[j]), row(conv_ln_b[j]), conv_w2[j].astype(BF16), row(conv_b2[j]),
                                  side_w, side_layer, batch=batch, seq=seq)
            if cast:
                moe_bf16[side_layer] = cast
        else:
            wqb, wkva, wkvb = _mla_weights(mla_wq_b[j], mla_wkv_a[j], mla_wkv_b[j], heads, lora)
            side_w, side_layer = moe_cast_job(i)
            h, cast = _mla_layer(h, pos, g, mla_wq_a[j].astype(BF16), row(mla_q_norm[j]), wqb, wkva,
                                 row(mla_kv_norm[j]), wkvb, mla_wo[j].T.astype(BF16), invf, sgn,
                                 side_w, side_layer, batch=batch, seq=seq, heads=heads, lora=lora)
            if cast:
                moe_bf16[side_layer] = cast
        g = row(norm_ffn[i])
        f_idx = i // 2
        last = i == depth - 1
        if i % 2 == 0:
            wg, wu, wd = ffn_bf16.get(f_idx) or [_to_bf16(w, f_idx) for w in (ffn_wg, ffn_wu, ffn_wd)]
            h = _ffn_layer(h, g, wg, wu, wd)
            if last:
                h = _final_norm(h, row(norm_final))
        else:
            wg, wu, wd = moe_bf16.get(f_idx) or [_to_bf16(w, f_idx) for w in moe_w]
            h = _moe_layer(h, g, moe_router[f_idx], wg, wu, wd, row(norm_final), last)
    return h.reshape(batch, seq, d)


def kernel(x, positions, norm_mix, norm_ffn, norm_final, pool_w, pool_scale, conv_w1, conv_b1, conv_dw, conv_bdw, conv_ln_g, conv_ln_b, conv_w2, conv_b2, mla_wq_a, mla_q_norm, mla_wq_b, mla_wkv_a, mla_kv_norm, mla_wkv_b, mla_wo, ffn_wg, ffn_wu, ffn_wd, moe_router, moe_wg, moe_wu, moe_wd):
    return _trunk(x, positions, norm_mix, norm_ffn, norm_final, pool_w, pool_scale,
                  conv_w1, conv_b1, conv_dw, conv_bdw, conv_ln_g, conv_ln_b, conv_w2, conv_b2,
                  mla_wq_a, mla_q_norm, mla_wq_b, mla_wkv_a, mla_kv_norm, mla_wkv_b, mla_wo,
                  ffn_wg, ffn_wu, ffn_wd, moe_router, moe_wg, moe_wu, moe_wd)
```

```python
import functools

import jax
import jax.numpy as jnp
from jax import lax
from jax.experimental import pallas as pl
from jax.experimental.pallas import tpu as pltpu

F32 = jnp.float32
BF16 = jnp.bfloat16

NORM_EPS = 1e-6
LN_EPS = 1e-5
ROPE_THETA = 10000.0
LOG2E = 1.4426950408889634

POOL_WINDOWS = (2, 4, 8, 16)
POOL_HALO = 16
CONV_HALO = 32
LANES = 128
SUBLANES = 8
POOL_TOP = POOL_HALO + SUBLANES
NOPE = 128
ROPE = 64
V_ROWS = NOPE + 16
TOP_K = 2
ZERO_CHUNK = 64
COMBINE_CHUNK = 32
RUN_CHUNK = 64
PLAN_ROWS = 4

VMEM_LIMIT = 56 * 1024 * 1024


def _params(*sem):
    return pltpu.CompilerParams(dimension_semantics=sem, vmem_limit_bytes=VMEM_LIMIT)


def _rms(x, g, eps=NORM_EPS):
    return x * lax.rsqrt(jnp.mean(x * x, axis=-1, keepdims=True) + eps) * g


def _dot(a, b):
    return jnp.dot(a, b, preferred_element_type=F32)


def _split_bf16(x):
    hi = x.astype(BF16)
    return hi, (x - hi.astype(F32)).astype(BF16)


def _dot_3pass(a, b):
    a_hi, a_lo = _split_bf16(a)
    b_hi, b_lo = _split_bf16(b)
    return _dot(a_hi, b_hi) + (_dot(a_hi, b_lo) + _dot(a_lo, b_hi))


def _ff_chunk(f):
    for c in (1792, 1024, 512, 256, 128):
        if f % c == 0:
            return c
    return f


def _pool_kernel(h_ref, g_ref, w_ref, sc_ref, *rest, ts, gdim, n_side):
    side_in, o_ref, side_out, (ubuf, lvl_a, lvl_b) = rest[:n_side], rest[n_side], rest[n_side + 1:-3], rest[-3:]
    _side_cast(side_in, side_out)
    j = pl.program_id(1)
    h = h_ref[...]
    u = _rms(h, g_ref[...])
    ext = POOL_TOP + ts

    @pl.when(j == 0)
    def _():
        ubuf[0:POOL_TOP, :] = jnp.zeros((POOL_TOP, ubuf.shape[1]), F32)
        lvl_a[0:SUBLANES, :] = jnp.zeros((SUBLANES, gdim), F32)
        lvl_b[0:SUBLANES, :] = jnp.zeros((SUBLANES, gdim), F32)

    ubuf[POOL_TOP:ext, :] = u
    t1 = lax.broadcasted_iota(jnp.int32, (ts, 1), 0) + (j * ts + 1)
    for g, win in enumerate(POOL_WINDOWS):
        cs = slice(g * gdim, (g + 1) * gdim)
        src, cols, spare, k = ubuf, cs, [lvl_a, lvl_b], 1
        while k < win:
            s = src[SUBLANES:ext, cols] + src[SUBLANES - k:ext - k, cols]
            k *= 2
            if k < win:
                dst = spare.pop(0)
                dst[SUBLANES:ext, :] = s
                spare.append(dst)
                src, cols = dst, slice(None)
        s = s[POOL_HALO:POOL_HALO + ts]
        ug = u[:, cs]
        cnt = jnp.minimum(t1, win).astype(F32)
        pooled = s / cnt - ug
        m = _dot(pooled.astype(BF16), w_ref[g])
        o_ref[:, cs] = h[:, cs] + m * sc_ref[:, cs]
    ubuf[SUBLANES:POOL_TOP, :] = ubuf[ts + SUBLANES:ts + POOL_TOP, :]


def _pool_layer(h, g, w, scale, side_weights, side_layers, *, batch, seq):
    n, d = h.shape
    ts = min(512, seq)
    nt = seq // ts
    gdim = d // len(POOL_WINDOWS)
    row = lambda b, j: (b * nt + j, 0)
    const2 = lambda b, j: (0, 0)
    side, side_in, side_out, side_shapes = _side_cast_plan(side_weights, list(side_layers), batch * nt,
                                                           lambda b, j: b * nt + j)
    outs = pl.pallas_call(
        functools.partial(_pool_kernel, ts=ts, gdim=gdim, n_side=len(side)),
        grid=(batch, nt),
        in_specs=[pl.BlockSpec((ts, d), row),
                  pl.BlockSpec((1, d), const2),
                  pl.BlockSpec(w.shape, lambda b, j: (0, 0, 0)),
                  pl.BlockSpec((1, d), const2)] + side_in,
        out_specs=[pl.BlockSpec((ts, d), row)] + side_out,
        out_shape=[jax.ShapeDtypeStruct((n, d), F32)] + side_shapes,
        scratch_shapes=[pltpu.VMEM((ts + POOL_TOP, d), F32), pltpu.VMEM((ts + POOL_TOP, gdim), F32),
                        pltpu.VMEM((ts + POOL_TOP, gdim), F32)],
        compiler_params=_params("arbitrary", "arbitrary"),
        name="pool_mixer",
    )(h, g, w, scale, *side)
    return outs[0], [o.reshape(w.shape[1:]) for o, w in zip(outs[1:], side_weights)]


def _swiglu_step(u, wg, wu, wd):
    a = _dot(u, wg)
    b = _dot(u, wu)
    hh = a * jax.nn.sigmoid(a) * b
    return _dot(hh.astype(BF16), wd)


def _accumulate_steps(j, nj, partial, acc_sc, finish):
    def run(first, last):
        r = partial()
        if not first:
            r = acc_sc[...] + r
        if last:
            finish(r)
        else:
            acc_sc[...] = r

    if nj == 1:
        run(True, True)
        return
    pl.when(j == 0)(functools.partial(run, True, False))
    if nj > 2:
        pl.when((j > 0) & (j < nj - 1))(functools.partial(run, False, False))
    pl.when(j == nj - 1)(functools.partial(run, False, True))


def _ffn_kernel(h_ref, g_ref, wg_ref, wu_ref, wd_ref, o_ref, u_sc, acc_sc, *, nj):
    j = pl.program_id(1)

    @pl.when(j == 0)
    def _():
        u_sc[...] = _rms(h_ref[...], g_ref[...]).astype(BF16)

    def finish(r):
        o_ref[...] = h_ref[...] + r

    _accumulate_steps(j, nj, lambda: _swiglu_step(u_sc[...], wg_ref[...], wu_ref[...], wd_ref[...]), acc_sc, finish)


def _ffn_layer(h, g, wg, wu, wd):
    n, d = h.shape
    f = wg.shape[1]
    tm = min(512, n)
    tf = _ff_chunk(f)
    return pl.pallas_call(
        functools.partial(_ffn_kernel, nj=f // tf),
        grid=(n // tm, f // tf),
        in_specs=[pl.BlockSpec((tm, d), lambda i, j: (i, 0)),
                  pl.BlockSpec((1, d), lambda i, j: (0, 0)),
                  pl.BlockSpec((d, tf), lambda i, j: (0, j)),
                  pl.BlockSpec((d, tf), lambda i, j: (0, j)),
                  pl.BlockSpec((tf, d), lambda i, j: (j, 0))],
        out_specs=pl.BlockSpec((tm, d), lambda i, j: (i, 0)),
        out_shape=jax.ShapeDtypeStruct((n, d), F32),
        scratch_shapes=[pltpu.VMEM((tm, d), BF16), pltpu.VMEM((tm, d), F32)],
        compiler_params=_params("arbitrary", "arbitrary"),
        name="dense_swiglu",
    )(h, g, wg, wu, wd)


def _side_cast_plan(weights, layer, steps, step_index):
    arrays = [w.reshape(w.shape[0], -1, w.shape[-1]) for w in weights]
    if not arrays:
        return [], [], [], []
    nblk = next(n for n in range(steps, 0, -1)
                if all(a.shape[1] % n == 0 and (a.shape[1] // n) % 16 == 0 for a in arrays))
    blk = lambda *g: jnp.minimum(step_index(*g), nblk - 1)
    layers = layer if isinstance(layer, (list, tuple)) else [layer] * len(arrays)
    in_specs, out_specs, out_shapes = [], [], []
    for a, lyr in zip(arrays, layers):
        rows, cols = a.shape[1:]
        in_specs.append(pl.BlockSpec((1, rows // nblk, cols), lambda *g, _l=lyr: (_l, blk(*g), 0)))
        out_specs.append(pl.BlockSpec((rows // nblk, cols), lambda *g: (blk(*g), 0)))
        out_shapes.append(jax.ShapeDtypeStruct((rows, cols), BF16))
    return arrays, in_specs, out_specs, out_shapes


def _side_cast(in_refs, out_refs):
    for src, dst in zip(in_refs, out_refs):
        dst[...] = src[0].astype(BF16)


def _conv_kernel(h_ref, g_ref, w1_ref, b1_ref, dw_ref, bdw_ref, lng_ref, lnb_ref, w2_ref, b2_ref, *rest,
                 ts, width, rb, n_side):
    side_in, o_ref, side_out, (gbuf, cbuf) = rest[:n_side], rest[n_side], rest[n_side + 1:-2], rest[-2:]
    _side_cast(side_in, side_out)
    j = pl.program_id(1)
    d = h_ref.shape[1]
    h = h_ref[...]
    u = _rms(h, g_ref[...]).astype(BF16)
    a = _dot(u, w1_ref[...]) + b1_ref[...]
    glu = a[:, :d] * jax.nn.sigmoid(a[:, d:])

    keep = CONV_HALO + SUBLANES

    @pl.when(j == 0)
    def _():
        gbuf[:, 0:keep, :] = jnp.zeros((SUBLANES, keep, d), F32)

    for b in range(SUBLANES):
        gbuf[b, CONV_HALO + b:CONV_HALO + b + ts, :] = glu
    for cb in range(d // LANES):
        cs = slice(cb * LANES, (cb + 1) * LANES)
        for r0 in range(0, ts, rb):
            acc = jnp.zeros((rb, LANES), F32)
            for k in range(width):
                a, b = divmod(width - 1 - k, SUBLANES)
                off = CONV_HALO + r0 - SUBLANES * a
                acc = acc + gbuf[b, off:off + rb, cs] * dw_ref[k:k + 1, cs]
            cbuf[r0:r0 + rb, cs] = acc + bdw_ref[:, cs]
    gbuf[:, 0:keep, :] = gbuf[:, ts:ts + keep, :]

    c = cbuf[...]
    mu = jnp.mean(c, axis=-1, keepdims=True)
    cc = c - mu
    var = jnp.mean(cc * cc, axis=-1, keepdims=True)
    un = cc * lax.rsqrt(var + LN_EPS) * lng_ref[...] + lnb_ref[...]
    un = un * jax.nn.sigmoid(un)
    o_ref[...] = h + _dot(un.astype(BF16), w2_ref[...]) + b2_ref[...]


def _conv_layer(h, g, w1, b1, dw, bdw, lng, lnb, w2, b2, side_weights, side_layer, *, batch, seq):
    n, d = h.shape
    width = dw.shape[0]
    assert width - 1 <= CONV_HALO
    ts = min(256, seq)
    rb = min(128, ts)
    nt = seq // ts
    row = lambda b, j: (b * nt + j, 0)
    c2 = lambda b, j: (0, 0)
    side, side_in, side_out, side_shapes = _side_cast_plan(side_weights, side_layer, batch * nt,
                                                           lambda b, j: b * nt + j)
    outs = pl.pallas_call(
        functools.partial(_conv_kernel, ts=ts, width=width, rb=rb, n_side=len(side)),
        grid=(batch, nt),
        in_specs=[pl.BlockSpec((ts, d), row),
                  pl.BlockSpec((1, d), c2),
                  pl.BlockSpec(w1.shape, c2),
                  pl.BlockSpec(b1.shape, c2),
                  pl.BlockSpec(dw.shape, c2),
                  pl.BlockSpec((1, d), c2),
                  pl.BlockSpec((1, d), c2),
                  pl.BlockSpec((1, d), c2),
                  pl.BlockSpec(w2.shape, c2),
                  pl.BlockSpec((1, d), c2)] + side_in,
        out_specs=[pl.BlockSpec((ts, d), row)] + side_out,
        out_shape=[jax.ShapeDtypeStruct((n, d), F32)] + side_shapes,
        scratch_shapes=[pltpu.VMEM((SUBLANES, ts + CONV_HALO + SUBLANES, d), F32), pltpu.VMEM((ts, d), F32)],
        compiler_params=_params("arbitrary", "arbitrary"),
        name="conv_module",
    )(h, g, w1, b1, dw, bdw, lng, lnb, w2, b2, *side)
    return outs[0], [o.reshape(w.shape[1:]) for o, w in zip(outs[1:], side_weights)]


def _mla_proj_kernel(h_ref, pos_ref, g_ref, wqa_ref, qnorm_ref, wqb_ref, wkva_ref, kvnorm_ref, wkvb_ref,
                     invf_ref, sgn_ref, qt_o, k_o, vt_o, *, heads, lora):
    hn = heads * NOPE
    rows = h_ref.shape[0]
    u = _rms(h_ref[...], g_ref[...]).astype(BF16)
    qa = _rms(_dot(u, wqa_ref[...]), qnorm_ref[...]).astype(BF16)
    q = _dot(qa, wqb_ref[...])
    kva = _dot(u, wkva_ref[...])
    ckv = _rms(kva[:, :lora], kvnorm_ref[...]).astype(BF16)
    kv = _dot(ckv, wkvb_ref[...])
    ang = pos_ref[...].astype(F32) * invf_ref[...]
    c = jnp.cos(ang)
    s = jnp.sin(ang) * sgn_ref[...]
    lane = lax.broadcasted_iota(jnp.int32, (rows, LANES), 1)

    def rope(x):
        swapped = jnp.where(lane < ROPE // 2, pltpu.roll(x, LANES - ROPE // 2, 1), pltpu.roll(x, ROPE // 2, 1))
        return x * c + swapped * s

    kr = rope(kva[:, lora:lora + LANES]).astype(BF16)
    ones = jnp.ones((V_ROWS - NOPE, rows), BF16)
    for hd in range(heads):
        ns = slice(hd * NOPE, (hd + 1) * NOPE)
        qr = rope(q[:, hn + hd * LANES:hn + (hd + 1) * LANES])
        qt_o[2 * hd * NOPE:(2 * hd + 1) * NOPE, :] = q[:, ns].T.astype(BF16)
        qt_o[(2 * hd + 1) * NOPE:(2 * hd + 2) * NOPE, :] = qr.T.astype(BF16)
        k_o[:, 2 * hd * NOPE:(2 * hd + 1) * NOPE] = kv[:, ns].astype(BF16)
        k_o[:, (2 * hd + 1) * NOPE:(2 * hd + 2) * NOPE] = kr
        vt_o[hd * V_ROWS:hd * V_ROWS + NOPE, :] = kv[:, hn + hd * NOPE:hn + (hd + 1) * NOPE].T.astype(BF16)
        vt_o[hd * V_ROWS + NOPE:(hd + 1) * V_ROWS, :] = ones


def _attn_kernel(qi_ref, kj_ref, qt_ref, k_ref, vt_ref, h_ref, wot_ref, *rest, heads, tq, tk, scale, n_side):
    side_in, o_ref, side_out, (m_sc, acc_sc) = rest[:n_side], rest[n_side], rest[n_side + 1:-2], rest[-2:]
    _side_cast(side_in, side_out)
    i = qi_ref[pl.program_id(1)]
    j = kj_ref[pl.program_id(1)]
    c = scale * LOG2E

    @pl.when(j == 0)
    def _():
        m_sc[...] = jnp.full(m_sc.shape, -jnp.inf, F32)
        acc_sc[...] = jnp.zeros(acc_sc.shape, F32)

    def step(diagonal):
        if diagonal:
            causal = (lax.broadcasted_iota(jnp.int32, (tk, tq), 0)
                      <= lax.broadcasted_iota(jnp.int32, (tk, tq), 1))
        es = [slice(2 * hd * NOPE, (2 * hd + 2) * NOPE) for hd in range(heads)]
        vs = [slice(hd * V_ROWS, (hd + 1) * V_ROWS) for hd in range(heads)]
        ss = [_dot(k_ref[:, e], qt_ref[e, :]) for e in es]
        if diagonal:
            ss = [jnp.where(causal, s, -jnp.inf) for s in ss]
        m_prev = [m_sc[hd] for hd in range(heads)]
        m_new = [jnp.maximum(mp, jnp.max(s, axis=0, keepdims=True)) for mp, s in zip(m_prev, ss)]
        ps = [jnp.exp2((s - mn) * c).astype(BF16) for s, mn in zip(ss, m_new)]
        pv = [_dot(vt_ref[v, :], p) for p, v in zip(ps, vs)]
        for hd in range(heads):
            alpha = jnp.exp2((m_prev[hd] - m_new[hd]) * c)
            acc_sc[vs[hd], :] = alpha * acc_sc[vs[hd], :] + pv[hd]
            m_sc[hd] = m_new[hd]

    @pl.when(j < i)
    def _():
        step(False)

    @pl.when(j == i)
    def _():
        step(True)
        parts = [acc_sc[hd * V_ROWS:hd * V_ROWS + NOPE, :] / acc_sc[hd * V_ROWS + NOPE:hd * V_ROWS + NOPE + 1, :]
                 for hd in range(heads)]
        ot = jnp.concatenate(parts, axis=0).astype(BF16)
        o_ref[...] = h_ref[...] + _dot(wot_ref[...], ot).T


def _mla_layer(h, positions, g, wqa, qnorm, wqb, wkva, kvnorm, wkvb, wot, invf, sgn, side_weights, side_layer,
               *, batch, seq, heads, lora):
    n, d = h.shape
    hn = heads * NOPE
    tp = min(512, n)
    c2 = lambda i: (0, 0)
    row = lambda i: (i, 0)
    col = lambda i: (0, i)
    qt, k, vt = pl.pallas_call(
        functools.partial(_mla_proj_kernel, heads=heads, lora=lora),
        grid=(n // tp,),
        in_specs=[pl.BlockSpec((tp, d), row),
                  pl.BlockSpec((tp, 1), row),
                  pl.BlockSpec((1, d), c2),
                  pl.BlockSpec(wqa.shape, c2),
                  pl.BlockSpec(qnorm.shape, c2),
                  pl.BlockSpec(wqb.shape, c2),
                  pl.BlockSpec(wkva.shape, c2),
                  pl.BlockSpec(kvnorm.shape, c2),
                  pl.BlockSpec(wkvb.shape, c2),
                  pl.BlockSpec((1, LANES), c2),
                  pl.BlockSpec((1, LANES), c2)],
        out_specs=[pl.BlockSpec((2 * hn, tp), col), pl.BlockSpec((tp, 2 * hn), row),
                   pl.BlockSpec((heads * V_ROWS, tp), col)],
        out_shape=[jax.ShapeDtypeStruct((2 * hn, n), BF16), jax.ShapeDtypeStruct((n, 2 * hn), BF16),
                   jax.ShapeDtypeStruct((heads * V_ROWS, n), BF16)],
        compiler_params=_params("arbitrary"),
        name="mla_proj",
    )(h, positions, g, wqa, qnorm, wqb, wkva, kvnorm, wkvb, invf, sgn)

    tq = tk = min(512, seq)
    nq = seq // tq
    scale = float(NOPE + ROPE) ** -0.5
    pairs = [(i, j) for i in range(nq) for j in range(i + 1)]
    qi = jnp.asarray([p[0] for p in pairs], jnp.int32)
    kj = jnp.asarray([p[1] for p in pairs], jnp.int32)
    npairs = len(pairs)
    qrow = lambda b, p, qi, kj: (b * nq + qi[p], 0)
    qcol = lambda b, p, qi, kj: (0, b * nq + qi[p])
    krow = lambda b, p, qi, kj: (b * nq + kj[p], 0)
    kcol = lambda b, p, qi, kj: (0, b * nq + kj[p])
    side, side_in, side_out, side_shapes = _side_cast_plan(side_weights, side_layer, batch * npairs,
                                                           lambda b, p, qi, kj: b * npairs + p)
    grid_spec = pltpu.PrefetchScalarGridSpec(
        num_scalar_prefetch=2,
        grid=(batch, npairs),
        in_specs=[pl.BlockSpec((2 * hn, tq), qcol),
                  pl.BlockSpec((tk, 2 * hn), krow),
                  pl.BlockSpec((heads * V_ROWS, tk), kcol),
                  pl.BlockSpec((tq, d), qrow),
                  pl.BlockSpec(wot.shape, lambda b, p, qi, kj: (0, 0))] + side_in,
        out_specs=[pl.BlockSpec((tq, d), qrow)] + side_out,
        scratch_shapes=[pltpu.VMEM((heads, 1, tq), F32), pltpu.VMEM((heads * V_ROWS, tq), F32)],
    )
    outs = pl.pallas_call(
        functools.partial(_attn_kernel, heads=heads, tq=tq, tk=tk, scale=scale, n_side=len(side)),
        grid_spec=grid_spec,
        out_shape=[jax.ShapeDtypeStruct((n, d), F32)] + side_shapes,
        compiler_params=_params("arbitrary", "arbitrary"),
        name="mla_attention",
    )(qi, kj, qt, k, vt, h, wot, *side)
    return outs[0], [o.reshape(w.shape[1:]) for o, w in zip(outs[1:], side_weights)]


def _store_row_tiles(ref, value, rows):
    slabs = value.shape[1] // LANES
    for s in range(slabs):
        ref[pl.ds(s, rows, stride=slabs), :] = value[:, s * LANES:(s + 1) * LANES]


def _load_row_tile_slab(ref, s, rows, slabs):
    return ref[pl.ds(s, rows, stride=slabs), :]


def _route_kernel(h_ref, g_ref, r_ref, u_ref, info_ref, info_t_ref, cnt_ref, carry, *, tr, n_exp):
    i = pl.program_id(0)

    @pl.when(i == 0)
    def _():
        carry[...] = jnp.zeros(carry.shape, F32)

    u = _rms(h_ref[...], g_ref[...])
    u_ref[...] = u.astype(BF16)
    logits = _dot_3pass(u, r_ref[...])
    lane = lax.broadcasted_iota(jnp.int32, (tr, LANES), 1)
    lg = jnp.where(lane < n_exp, logits, -jnp.inf)
    m1 = jnp.max(lg, axis=1, keepdims=True)
    i1 = jnp.min(jnp.where(lg == m1, lane, LANES), axis=1, keepdims=True)
    lg2 = jnp.where(lane == i1, -jnp.inf, lg)
    m2 = jnp.max(lg2, axis=1, keepdims=True)
    i2 = jnp.min(jnp.where(lg2 == m2, lane, LANES), axis=1, keepdims=True)
    e = jnp.exp(m2 - m1)
    g1 = 1.0 / (1.0 + e)
    g2 = e / (1.0 + e)
    sel1 = lane == i1
    sel2 = lane == i2
    onehot = jnp.where(sel1 | sel2, 1.0, 0.0)
    rr = lax.broadcasted_iota(jnp.int32, (tr, tr), 0)
    cc = lax.broadcasted_iota(jnp.int32, (tr, tr), 1)
    tri = jnp.where(cc < rr, 1.0, 0.0).astype(BF16)
    before = _dot(tri, onehot.astype(BF16)) + carry[0:1, :]
    rank1 = jnp.sum(jnp.where(sel1, before, 0.0), axis=1, keepdims=True)
    rank2 = jnp.sum(jnp.where(sel2, before, 0.0), axis=1, keepdims=True)
    carry[...] = carry[...] + jnp.sum(onehot, axis=0, keepdims=True)
    cols = (i1.astype(F32), i2.astype(F32), g1, g2, rank1, rank2)
    info = jnp.zeros((tr, LANES), F32)
    for k, col in enumerate(cols):
        info = jnp.where(lane == k, col, info)
    info_ref[...] = info
    info_t_ref[...] = info.T[0:SUBLANES, :]
    cnt_ref[...] = carry[...]


def _route(h, g, router_padded, n_exp):
    n, d = h.shape
    tr = min(512, n)
    slabs = d // LANES
    return pl.pallas_call(
        functools.partial(_route_kernel, tr=tr, n_exp=n_exp),
        grid=(n // tr,),
        in_specs=[pl.BlockSpec((tr, d), lambda i: (i, 0)),
                  pl.BlockSpec((1, d), lambda i: (0, 0)),
                  pl.BlockSpec((d, LANES), lambda i: (0, 0))],
        out_specs=[pl.BlockSpec((tr, d), lambda i: (i, 0)),
                   pl.BlockSpec((tr, LANES), lambda i: (i, 0)),
                   pl.BlockSpec((SUBLANES, tr), lambda i: (0, i)),
                   pl.BlockSpec((8, LANES), lambda i: (i, 0))],
        out_shape=[jax.ShapeDtypeStruct((n, d), BF16),
                   jax.ShapeDtypeStruct((n, LANES), F32),
                   jax.ShapeDtypeStruct((SUBLANES, n), F32),
                   jax.ShapeDtypeStruct((n // tr * 8, LANES), F32)],
        scratch_shapes=[pltpu.VMEM((8, LANES), F32)],
        compiler_params=_params("arbitrary"),
        name="moe_route",
    )(h, g, router_padded)


def _dispatch_kernel(pad_start_ref, pad_cnt_ref, plan_ref, info_t_ref, u_ref, x_ref, zero_sc, stage, sem, wsem,
                     *, tile, slabs, n_exp):
    i = pl.program_id(0)
    slot = i % 2

    def row_copy(src, src_row, dst_row):
        return pltpu.make_async_copy(src.at[pl.ds(pl.multiple_of(src_row * slabs, slabs), slabs)],
                                     x_ref.at[pl.ds(pl.multiple_of(dst_row * slabs, slabs), slabs)], sem)

    @pl.when(pl.program_id(0) == 0)
    def _():
        zero_sc[...] = jnp.zeros(zero_sc.shape, F32)
        for e in range(n_exp):
            def fill(i, c):
                row_copy(zero_sc, 0, pad_start_ref[e] + i).start()
                return c

            def drain(i, c):
                row_copy(zero_sc, 0, pad_start_ref[e] + i).wait()
                return c

            lax.fori_loop(0, pad_cnt_ref[e], fill, 0)
            lax.fori_loop(0, pad_cnt_ref[e], drain, 0)

        def chunk_copy(i):
            dst = pl.multiple_of((pad_start_ref[n_exp] + i * ZERO_CHUNK) * slabs, slabs)
            return pltpu.make_async_copy(zero_sc, x_ref.at[pl.ds(dst, ZERO_CHUNK * slabs)], sem)

        def fill_tail(i, c):
            chunk_copy(i).start()
            return c

        def drain_tail(i, c):
            chunk_copy(i).wait()
            return c

        lax.fori_loop(0, pad_cnt_ref[n_exp], fill_tail, 0)
        lax.fori_loop(0, pad_cnt_ref[n_exp], drain_tail, 0)

    def each_run_copy(block, buf, action):
        for e in range(n_exp):
            base = (block * PLAN_ROWS) * n_exp + e
            dst0, length, src0 = plan_ref[base], plan_ref[base + n_exp], plan_ref[base + 2 * n_exp]

            def copy(off, nrows):
                s = pl.multiple_of((src0 + off) * slabs, slabs)
                d = pl.multiple_of((dst0 + off) * slabs, slabs)
                return pltpu.make_async_copy(stage.at[buf, pl.ds(s, nrows * slabs)],
                                             x_ref.at[pl.ds(d, nrows * slabs)], wsem.at[buf])

            def body(c, carry):
                action(copy(c * RUN_CHUNK, RUN_CHUNK))
                return carry

            rem = length & (RUN_CHUNK - 1)
            full = length - rem
            lax.fori_loop(0, lax.shift_right_logical(full, RUN_CHUNK.bit_length() - 1), body, 0)
            bit = RUN_CHUNK // 2
            while bit:
                done = rem & (RUN_CHUNK - 2 * bit)
                pl.when((rem & bit) != 0)(functools.partial(lambda o, nb: action(copy(o, nb)), full + done, bit))
                bit //= 2

    info_t = info_t_ref[...]
    shift_base = (i * PLAN_ROWS + 3) * n_exp
    r_idx = lax.broadcasted_iota(jnp.int32, (TOP_K * tile, tile), 0)
    onehot = jnp.zeros(r_idx.shape, F32)
    for k in range(TOP_K):
        ls = info_t[4 + k:5 + k, :] + _select_by_expert(info_t[k:k + 1, :], plan_ref, shift_base, n_exp)
        onehot = onehot + jnp.where(r_idx == ls.astype(jnp.int32), 1.0, 0.0)
    x = _dot(onehot.astype(BF16), u_ref[...])
    _store_row_tiles(stage.at[slot], x, TOP_K * tile)
    each_run_copy(i, slot, lambda copy: copy.start())

    @pl.when(i > 0)
    def _():
        each_run_copy(i - 1, 1 - slot, lambda copy: copy.wait())

    @pl.when(i == pl.num_programs(0) - 1)
    def _():
        each_run_copy(i, slot, lambda copy: copy.wait())


def _dispatch(u, info_t, plan, pad_start, pad_cnt, sorted_rows, tile, slabs):
    n, d = u.shape
    grid_spec = pltpu.PrefetchScalarGridSpec(
        num_scalar_prefetch=3,
        grid=(n // tile,),
        in_specs=[pl.BlockSpec((SUBLANES, tile), lambda i, ps, pc, pn: (0, i)),
                  pl.BlockSpec((tile, d), lambda i, ps, pc, pn: (i, 0))],
        out_specs=pl.BlockSpec(memory_space=pl.ANY),
        scratch_shapes=[pltpu.VMEM((ZERO_CHUNK * slabs, LANES), F32),
                        pltpu.VMEM((2, TOP_K * tile * slabs, LANES), F32),
                        pltpu.SemaphoreType.DMA(()), pltpu.SemaphoreType.DMA((2,))],
    )
    return pl.pallas_call(
        functools.partial(_dispatch_kernel, tile=tile, slabs=slabs, n_exp=pad_start.shape[0] - 1),
        grid_spec=grid_spec,
        out_shape=jax.ShapeDtypeStruct((sorted_rows * slabs, LANES), F32),
        compiler_params=_params("arbitrary"),
        name="moe_dispatch",
    )(pad_start, pad_cnt, plan, info_t, u)


def _moe_kernel(te_ref, nu_ref, x_ref, wg_ref, wu_ref, wd_ref, o_ref, x_sc, acc_sc, *, tm, slabs, nj):
    t = pl.program_id(0)
    j = pl.program_id(1)

    @pl.when(t < nu_ref[0])
    def _():
        @pl.when(j == 0)
        def _():
            for s in range(slabs):
                x_sc[:, s * LANES:(s + 1) * LANES] = _load_row_tile_slab(x_ref, s, tm, slabs).astype(BF16)

        def finish(r):
            o_ref[...] = r.astype(BF16)

        _accumulate_steps(j, nj, lambda: _swiglu_step(x_sc[...], wg_ref[0], wu_ref[0], wd_ref[0]), acc_sc, finish)

    @pl.when((t >= nu_ref[0]) & (j == 0))
    def _():
        o_ref[...] = jnp.zeros(o_ref.shape, BF16)


def _moe_experts(x_sorted, tile_expert, n_used, wg, wu, wd, tm, slabs):
    d = slabs * LANES
    rows = x_sorted.shape[0] // slabs
    f = wg.shape[2]
    tf = _ff_chunk(f)
    nj = f // tf
    tmap = lambda t, j, te, nu: (jnp.minimum(t, nu[0] - 1), 0)
    jsel = lambda t, j, nu: jnp.where(t < nu[0], j, nj - 1)
    grid_spec = pltpu.PrefetchScalarGridSpec(
        num_scalar_prefetch=2,
        grid=(rows // tm, nj),
        in_specs=[pl.BlockSpec((tm * slabs, LANES), tmap),
                  pl.BlockSpec((1, d, tf), lambda t, j, te, nu: (te[t], 0, jsel(t, j, nu))),
                  pl.BlockSpec((1, d, tf), lambda t, j, te, nu: (te[t], 0, jsel(t, j, nu))),
                  pl.BlockSpec((1, tf, d), lambda t, j, te, nu: (te[t], jsel(t, j, nu), 0))],
        out_specs=pl.BlockSpec((tm, d), lambda t, j, te, nu: (t, 0)),
        scratch_shapes=[pltpu.VMEM((tm, d), BF16), pltpu.VMEM((tm, d), F32)],
    )
    return pl.pallas_call(
        functools.partial(_moe_kernel, tm=tm, slabs=slabs, nj=nj),
        grid_spec=grid_spec,
        out_shape=jax.ShapeDtypeStruct((rows, d), BF16),
        compiler_params=_params("arbitrary", "arbitrary"),
        name="moe_experts",
    )(tile_expert, n_used, x_sorted, wg, wu, wd)


def _select_by_expert(expert, plan_ref, base, n_exp):
    out = jnp.zeros(expert.shape, F32)
    for e in range(n_exp):
        out = jnp.where(expert == float(e), plan_ref[base + e].astype(F32), out)
    return out


def _combine_stage_rows(tile, n_exp):
    max_chunks = (TOP_K * tile + n_exp * (15 + COMBINE_CHUNK - 1)) // COMBINE_CHUNK
    return -(-max_chunks * COMBINE_CHUNK // 256) * 256


def _combine_kernel(plan_ref, h_ref, info_ref, gf_ref, y_ref, o_ref, stage, sem, *, tile, n_exp, final_norm):
    i = pl.program_id(0)
    slot = i % 2

    def each_chunk(block, buf, action):
        for e in range(n_exp):
            base = (block * PLAN_ROWS) * n_exp + e

            def body(c, carry):
                src = pl.multiple_of(plan_ref[base] + c * COMBINE_CHUNK, 16)
                dst = pl.multiple_of(plan_ref[base + 2 * n_exp] + c * COMBINE_CHUNK, COMBINE_CHUNK)
                action(pltpu.make_async_copy(y_ref.at[pl.ds(src, COMBINE_CHUNK)],
                                             stage.at[buf, pl.ds(dst, COMBINE_CHUNK)], sem.at[buf]))
                return carry

            lax.fori_loop(0, plan_ref[base + n_exp], body, 0)

    @pl.when(i == 0)
    def _():
        stage[...] = jnp.zeros(stage.shape, BF16)
        each_chunk(i, slot, lambda copy: copy.start())

    @pl.when(i + 1 < pl.num_programs(0))
    def _():
        each_chunk(i + 1, 1 - slot, lambda copy: copy.start())

    each_chunk(i, slot, lambda copy: copy.wait())

    info = info_ref[...]
    shift_base = (i * PLAN_ROWS + 3) * n_exp
    col = lax.broadcasted_iota(jnp.int32, (tile, stage.shape[1]), 1)
    p = jnp.zeros(col.shape, F32)
    for k in range(TOP_K):
        loc = info[:, 4 + k:5 + k] + _select_by_expert(info[:, k:k + 1], plan_ref, shift_base, n_exp)
        p = p + jnp.where(col == loc.astype(jnp.int32), info[:, 2 + k:3 + k], 0.0)
    out = h_ref[...] + _dot(p.astype(BF16), stage[slot])
    if final_norm:
        out = _rms(out, gf_ref[...])
    o_ref[...] = out


def _combine(h, info, plan, y_sorted, g_final, final_norm, n_exp):
    n, d = h.shape
    tile = n // (plan.shape[0] // (PLAN_ROWS * n_exp))
    grid_spec = pltpu.PrefetchScalarGridSpec(
        num_scalar_prefetch=1,
        grid=(n // tile,),
        in_specs=[pl.BlockSpec((tile, d), lambda i, plan: (i, 0)),
                  pl.BlockSpec((tile, LANES), lambda i, plan: (i, 0)),
                  pl.BlockSpec((1, d), lambda i, plan: (0, 0)),
                  pl.BlockSpec(memory_space=pl.ANY)],
        out_specs=pl.BlockSpec((tile, d), lambda i, plan: (i, 0)),
        scratch_shapes=[pltpu.VMEM((2, _combine_stage_rows(tile, n_exp), d), BF16), pltpu.SemaphoreType.DMA((2,))],
    )
    return pl.pallas_call(
        functools.partial(_combine_kernel, tile=tile, n_exp=n_exp, final_norm=final_norm),
        grid_spec=grid_spec,
        out_shape=jax.ShapeDtypeStruct((n, d), F32),
        compiler_params=_params("arbitrary"),
        name="moe_combine",
    )(plan, h, info, g_final, y_sorted)


def _moe_layer(h, g, router, wg, wu, wd, g_final, final_norm):
    n, d = h.shape
    n_exp = router.shape[1]
    tm = min(512, n)
    router_padded = jnp.pad(router, ((0, 0), (0, LANES - n_exp)))
    u, info, info_t, cnt = _route(h, g, router_padded, n_exp)

    tile = min(512, n)
    n_blocks = n // tile
    after = cnt.reshape(n_blocks, 8, LANES)[:, 0, :n_exp].astype(jnp.int32)
    before = jnp.concatenate([jnp.zeros((1, n_exp), jnp.int32), after[:-1]])
    counts = after[-1]
    padded = (counts + tm - 1) // tm * tm
    ends = jnp.cumsum(padded)
    starts = ends - padded
    rows = TOP_K * n + n_exp * tm
    n_used = (ends[-1] // tm).astype(jnp.int32)
    tile_ids = jnp.minimum(jnp.arange(rows // tm, dtype=jnp.int32), n_used - 1)
    tile_expert = jnp.sum(tile_ids[:, None] * tm >= ends[None, :], axis=1).astype(jnp.int32)

    run_start = starts[None, :] + before
    run_len = after - before
    fetch_start = run_start // 16 * 16
    n_chunks = jnp.where(run_len > 0, (run_start - fetch_start + run_len + COMBINE_CHUNK - 1) // COMBINE_CHUNK, 0)
    stage_off = COMBINE_CHUNK * (jnp.cumsum(n_chunks, axis=1) - n_chunks)
    plan = jnp.stack([fetch_start, n_chunks, stage_off, starts[None, :] + stage_off - fetch_start],
                     axis=1).reshape(-1).astype(jnp.int32)

    slabs = d // LANES
    assert slabs == 8
    local_off = jnp.cumsum(run_len, axis=1) - run_len
    dispatch_plan = jnp.stack([run_start, run_len, local_off, local_off - before],
                              axis=1).reshape(-1).astype(jnp.int32)
    assert tm % ZERO_CHUNK == 0
    pad_start = jnp.concatenate([starts + counts, ends[-1:]]).astype(jnp.int32)
    pad_cnt = jnp.concatenate([padded - counts, (rows - ends[-1:]) // ZERO_CHUNK]).astype(jnp.int32)
    x_sorted = _dispatch(u, info_t, dispatch_plan, pad_start, pad_cnt, rows, tile, slabs)
    y_sorted = _moe_experts(x_sorted, tile_expert, n_used.reshape(1), wg, wu, wd, tm, slabs)
    return _combine(h, info, plan, y_sorted, g_final, final_norm, n_exp)


def _mla_weights(wq_b, wkv_a, wkv_b, heads, lora):
    def padded(w):
        return jnp.concatenate([w, jnp.zeros(w.shape[:-1] + (LANES - ROPE,), w.dtype)], axis=-1)

    rq = wq_b.shape[0]
    wq = wq_b.reshape(rq, heads, NOPE + ROPE)
    wqb = jnp.concatenate([wq[..., :NOPE].reshape(rq, -1), padded(wq[..., NOPE:]).reshape(rq, -1)], axis=1)
    wkva = jnp.concatenate([wkv_a[:, :lora], padded(wkv_a[:, lora:])], axis=1)
    wkv = wkv_b.reshape(lora, heads, 2 * NOPE)
    wkvb = jnp.concatenate([wkv[..., :NOPE].reshape(lora, -1), wkv[..., NOPE:].reshape(lora, -1)], axis=1)
    return wqb.astype(BF16), wkva.astype(BF16), wkvb.astype(BF16)


def _cast_kernel(x_ref, o_ref):
    o_ref[...] = x_ref[0].astype(BF16)


def _to_bf16(w, layer):
    cols = w.shape[-1]
    w3 = w.reshape(w.shape[0], -1, cols)
    rows = w3.shape[1]
    cap = 256 if cols > 2048 else 1024
    tr = next(t for t in (1024, 512, 256, 128, 64, 32, 16, 8) if t <= cap and rows % t == 0)
    out = pl.pallas_call(
        _cast_kernel,
        grid=(rows // tr,),
        in_specs=[pl.BlockSpec((1, tr, cols), lambda i: (layer, i, 0))],
        out_specs=pl.BlockSpec((tr, cols), lambda i: (i, 0)),
        out_shape=jax.ShapeDtypeStruct((rows, cols), BF16),
        compiler_params=_params("arbitrary"),
        name="weights_to_bf16",
    )(w3)
    return out.reshape(w.shape[1:])


def _final_norm_kernel(h_ref, g_ref, o_ref):
    o_ref[...] = _rms(h_ref[...], g_ref[...])


def _final_norm(h, g):
    n, d = h.shape
    tm = min(512, n)
    return pl.pallas_call(
        _final_norm_kernel,
        grid=(n // tm,),
        in_specs=[pl.BlockSpec((tm, d), lambda i: (i, 0)), pl.BlockSpec((1, d), lambda i: (0, 0))],
        out_specs=pl.BlockSpec((tm, d), lambda i: (i, 0)),
        out_shape=jax.ShapeDtypeStruct((n, d), F32),
        compiler_params=_params("arbitrary"),
        name="final_norm",
    )(h, g)


@jax.jit
def _trunk(x, positions, norm_mix, norm_ffn, norm_final, pool_w, pool_scale,
           conv_w1, conv_b1, conv_dw, conv_bdw, conv_ln_g, conv_ln_b, conv_w2, conv_b2,
           mla_wq_a, mla_q_norm, mla_wq_b, mla_wkv_a, mla_kv_norm, mla_wkv_b, mla_wo,
           ffn_wg, ffn_wu, ffn_wd, moe_router, moe_wg, moe_wu, moe_wd):
    batch, seq, d = x.shape
    depth = norm_mix.shape[0]
    lora = mla_kv_norm.shape[1]
    heads = mla_wo.shape[1] // NOPE
    assert mla_wq_b.shape[2] == heads * (NOPE + ROPE) and mla_wkv_a.shape[2] == lora + ROPE
    n = batch * seq
    h = x.reshape(n, d)
    pos = positions.reshape(n, 1)
    row = lambda v: v.reshape(1, -1)

    half = jnp.arange(0, ROPE, 2, dtype=F32) / ROPE
    inv_freq = ROPE_THETA ** (-half)
    invf = row(jnp.concatenate([inv_freq, inv_freq, jnp.zeros((LANES - ROPE,), F32)]))
    sgn = row(jnp.concatenate([-jnp.ones((ROPE // 2,), F32), jnp.ones((ROPE // 2,), F32),
                               jnp.zeros((LANES - ROPE,), F32)]))

    moe_w = (moe_wg, moe_wu, moe_wd)
    moe_bf16 = {}
    ffn_bf16 = {}

    def moe_cast_job(i):
        nxt = i if i % 2 == 1 else i + 1
        if nxt < depth and nxt // 2 not in moe_bf16:
            return moe_w, nxt // 2
        return (), 0

    for i in range(depth):
        kind, j = i % 3, i // 3
        g = row(norm_mix[i])
        if kind == 0:
            jobs = [(w, f) for f in range(ffn_wg.shape[0]) for w in (ffn_wg, ffn_wu, ffn_wd)] if not ffn_bf16 else []
            h, cast = _pool_layer(h, g, pool_w[j].astype(BF16), row(pool_scale[j]), [w for w, _ in jobs],
                                  [f for _, f in jobs], batch=batch, seq=seq)
            for k in range(0, len(cast), 3):
                ffn_bf16[jobs[k][1]] = cast[k:k + 3]
        elif kind == 1:
            side_w, side_layer = moe_cast_job(i)
            h, cast = _conv_layer(h, g, conv_w1[j].astype(BF16), row(conv_b1[j]), conv_dw[j], row(conv_bdw[j]),
                                  row(conv_ln_g[j]), row(conv_ln_b[j]), conv_w2[j].astype(BF16), row(conv_b2[j]),
                                  side_w, side_layer, batch=batch, seq=seq)
            if cast:
                moe_bf16[side_layer] = cast
        else:
            wqb, wkva, wkvb = _mla_weights(mla_wq_b[j], mla_wkv_a[j], mla_wkv_b[j], heads, lora)
            side_w, side_layer = moe_cast_job(i)
            h, cast = _mla_layer(h, pos, g, mla_wq_a[j].astype(BF16), row(mla_q_norm[j]), wqb, wkva,
                                 row(mla_kv_norm[j]), wkvb, mla_wo[j].T.astype(BF16), invf, sgn,
                                 side_w, side_layer, batch=batch, seq=seq, heads=heads, lora=lora)
            if cast:
                moe_bf16[side_layer] = cast
        g = row(norm_ffn[i])
        f_idx = i // 2
        last = i == depth - 1
        if i % 2 == 0:
            wg, wu, wd = ffn_bf16.get(f_idx) or [_to_bf16(w, f_idx) for w in (ffn_wg, ffn_wu, ffn_wd)]
            h = _ffn_layer(h, g, wg, wu, wd)
            if last:
                h = _final_norm(h, row(norm_final))
        else:
            wg, wu, wd = moe_bf16.get(f_idx) or [_to_bf16(w, f_idx) for w in moe_w]
            h = _moe_layer(h, g, moe_router[f_idx], wg, wu, wd, row(norm_final), last)
    return h.reshape(batch, seq, d)


def kernel(x, positions, norm_mix, norm_ffn, norm_final, pool_w, pool_scale, conv_w1, conv_b1, conv_dw, conv_bdw, conv_ln_g, conv_ln_b, conv_w2, conv_b2, mla_wq_a, mla_q_norm, mla_wq_b, mla_wkv_a, mla_kv_norm, mla_wkv_b, mla_wo, ffn_wg, ffn_wu, ffn_wd, moe_router, moe_wg, moe_wu, moe_wd):
    return _trunk(x, positions, norm_mix, norm_ffn, norm_final, pool_w, pool_scale,
                  conv_w1, conv_b1, conv_dw, conv_bdw, conv_ln_g, conv_ln_b, conv_w2, conv_b2,
                  mla_wq_a, mla_q_norm, mla_wq_b, mla_wkv_a, mla_kv_norm, mla_wkv_b, mla_wo,
                  ffn_wg, ffn_wu, ffn_wd, moe_router, moe_wg, moe_wu, moe_wd)
```

```python
import functools

import jax
import jax.numpy as jnp
from jax import lax
from jax.experimental import pallas as pl
from jax.experimental.pallas import tpu as pltpu

F32 = jnp.float32
BF16 = jnp.bfloat16

NORM_EPS = 1e-6
LN_EPS = 1e-5
ROPE_THETA = 10000.0
LOG2E = 1.4426950408889634

POOL_WINDOWS = (2, 4, 8, 16)
POOL_HALO = 16
CONV_HALO = 32
LANES = 128
SUBLANES = 8
POOL_TOP = POOL_HALO + SUBLANES
NOPE = 128
ROPE = 64
V_ROWS = NOPE + 16
TOP_K = 2
ZERO_CHUNK = 64
COMBINE_CHUNK = 32
RUN_CHUNK = 64
PLAN_ROWS = 4

VMEM_LIMIT = 56 * 1024 * 1024


def _params(*sem):
    return pltpu.CompilerParams(dimension_semantics=sem, vmem_limit_bytes=VMEM_LIMIT)


def _rms(x, g, eps=NORM_EPS):
    return x * lax.rsqrt(jnp.mean(x * x, axis=-1, keepdims=True) + eps) * g


def _dot(a, b):
    return jnp.dot(a, b, preferred_element_type=F32)


def _split_bf16(x):
    hi = x.astype(BF16)
    return hi, (x - hi.astype(F32)).astype(BF16)


def _dot_3pass(a, b):
    a_hi, a_lo = _split_bf16(a)
    b_hi, b_lo = _split_bf16(b)
    return _dot(a_hi, b_hi) + (_dot(a_hi, b_lo) + _dot(a_lo, b_hi))


def _ff_chunk(f):
    for c in (1792, 1024, 512, 256, 128):
        if f % c == 0:
            return c
    return f


def _pool_kernel(h_ref, g_ref, w_ref, sc_ref, *rest, ts, gdim, n_side):
    side_in, o_ref, side_out, (ubuf, lvl_a, lvl_b) = rest[:n_side], rest[n_side], rest[n_side + 1:-3], rest[-3:]
    _side_cast(side_in, side_out)
    j = pl.program_id(1)
    h = h_ref[...]
    u = _rms(h, g_ref[...])
    ext = POOL_TOP + ts

    @pl.when(j == 0)
    def _():
        ubuf[0:POOL_TOP, :] = jnp.zeros((POOL_TOP, ubuf.shape[1]), F32)
        lvl_a[0:SUBLANES, :] = jnp.zeros((SUBLANES, gdim), F32)
        lvl_b[0:SUBLANES, :] = jnp.zeros((SUBLANES, gdim), F32)

    ubuf[POOL_TOP:ext, :] = u
    t1 = lax.broadcasted_iota(jnp.int32, (ts, 1), 0) + (j * ts + 1)
    for g, win in enumerate(POOL_WINDOWS):
        cs = slice(g * gdim, (g + 1) * gdim)
        src, cols, spare, k = ubuf, cs, [lvl_a, lvl_b], 1
        while k < win:
            s = src[SUBLANES:ext, cols] + src[SUBLANES - k:ext - k, cols]
            k *= 2
            if k < win:
                dst = spare.pop(0)
                dst[SUBLANES:ext, :] = s
                spare.append(dst)
                src, cols = dst, slice(None)
        s = s[POOL_HALO:POOL_HALO + ts]
        ug = u[:, cs]
        cnt = jnp.minimum(t1, win).astype(F32)
        pooled = s / cnt - ug
        m = _dot(pooled.astype(BF16), w_ref[g])
        o_ref[:, cs] = h[:, cs] + m * sc_ref[:, cs]
    ubuf[SUBLANES:POOL_TOP, :] = ubuf[ts + SUBLANES:ts + POOL_TOP, :]


def _pool_layer(h, g, w, scale, side_weights, side_layers, *, batch, seq):
    n, d = h.shape
    ts = min(512, seq)
    nt = seq // ts
    gdim = d // len(POOL_WINDOWS)
    row = lambda b, j: (b * nt + j, 0)
    const2 = lambda b, j: (0, 0)
    side, side_in, side_out, side_shapes = _side_cast_plan(side_weights, list(side_layers), batch * nt,
                                                           lambda b, j: b * nt + j)
    outs = pl.pallas_call(
        functools.partial(_pool_kernel, ts=ts, gdim=gdim, n_side=len(side)),
        grid=(batch, nt),
        in_specs=[pl.BlockSpec((ts, d), row),
                  pl.BlockSpec((1, d), const2),
                  pl.BlockSpec(w.shape, lambda b, j: (0, 0, 0)),
                  pl.BlockSpec((1, d), const2)] + side_in,
        out_specs=[pl.BlockSpec((ts, d), row)] + side_out,
        out_shape=[jax.ShapeDtypeStruct((n, d), F32)] + side_shapes,
        scratch_shapes=[pltpu.VMEM((ts + POOL_TOP, d), F32), pltpu.VMEM((ts + POOL_TOP, gdim), F32),
                        pltpu.VMEM((ts + POOL_TOP, gdim), F32)],
        compiler_params=_params("arbitrary", "arbitrary"),
        name="pool_mixer",
    )(h, g, w, scale, *side)
    return outs[0], [o.reshape(w.shape[1:]) for o, w in zip(outs[1:], side_weights)]


def _swiglu_step(u, wg, wu, wd):
    a = _dot(u, wg)
    b = _dot(u, wu)
    hh = a * jax.nn.sigmoid(a) * b
    return _dot(hh.astype(BF16), wd)


def _accumulate_steps(j, nj, partial, acc_sc, finish):
    def run(first, last):
        r = partial()
        if not first:
            r = acc_sc[...] + r
        if last:
            finish(r)
        else:
            acc_sc[...] = r

    if nj == 1:
        run(True, True)
        return
    pl.when(j == 0)(functools.partial(run, True, False))
    if nj > 2:
        pl.when((j > 0) & (j < nj - 1))(functools.partial(run, False, False))
    pl.when(j == nj - 1)(functools.partial(run, False, True))


def _ffn_kernel(h_ref, g_ref, wg_ref, wu_ref, wd_ref, o_ref, u_sc, acc_sc, *, nj):
    j = pl.program_id(1)

    @pl.when(j == 0)
    def _():
        u_sc[...] = _rms(h_ref[...], g_ref[...]).astype(BF16)

    def finish(r):
        o_ref[...] = h_ref[...] + r

    _accumulate_steps(j, nj, lambda: _swiglu_step(u_sc[...], wg_ref[...], wu_ref[...], wd_ref[...]), acc_sc, finish)


def _ffn_layer(h, g, wg, wu, wd):
    n, d = h.shape
    f = wg.shape[1]
    tm = min(512, n)
    tf = _ff_chunk(f)
    return pl.pallas_call(
        functools.partial(_ffn_kernel, nj=f // tf),
        grid=(n // tm, f // tf),
        in_specs=[pl.BlockSpec((tm, d), lambda i, j: (i, 0)),
                  pl.BlockSpec((1, d), lambda i, j: (0, 0)),
                  pl.BlockSpec((d, tf), lambda i, j: (0, j)),
                  pl.BlockSpec((d, tf), lambda i, j: (0, j)),
                  pl.BlockSpec((tf, d), lambda i, j: (j, 0))],
        out_specs=pl.BlockSpec((tm, d), lambda i, j: (i, 0)),
        out_shape=jax.ShapeDtypeStruct((n, d), F32),
        scratch_shapes=[pltpu.VMEM((tm, d), BF16), pltpu.VMEM((tm, d), F32)],
        compiler_params=_params("arbitrary", "arbitrary"),
        name="dense_swiglu",
    )(h, g, wg, wu, wd)


def _side_cast_plan(weights, layer, steps, step_index):
    arrays = [w.reshape(w.shape[0], -1, w.shape[-1]) for w in weights]
    if not arrays:
        return [], [], [], []
    nblk = next(n for n in range(steps, 0, -1)
                if all(a.shape[1] % n == 0 and (a.shape[1] // n) % 16 == 0 for a in arrays))
    blk = lambda *g: jnp.minimum(step_index(*g), nblk - 1)
    layers = layer if isinstance(layer, (list, tuple)) else [layer] * len(arrays)
    in_specs, out_specs, out_shapes = [], [], []
    for a, lyr in zip(arrays, layers):
        rows, cols = a.shape[1:]
        in_specs.append(pl.BlockSpec((1, rows // nblk, cols), lambda *g, _l=lyr: (_l, blk(*g), 0)))
        out_specs.append(pl.BlockSpec((rows // nblk, cols), lambda *g: (blk(*g), 0)))
        out_shapes.append(jax.ShapeDtypeStruct((rows, cols), BF16))
    return arrays, in_specs, out_specs, out_shapes


def _side_cast(in_refs, out_refs):
    for src, dst in zip(in_refs, out_refs):
        dst[...] = src[0].astype(BF16)


def _conv_kernel(h_ref, g_ref, w1_ref, b1_ref, dw_ref, bdw_ref, lng_ref, lnb_ref, w2_ref, b2_ref, *rest,
                 ts, width, rb, n_side):
    side_in, o_ref, side_out, (gbuf, cbuf) = rest[:n_side], rest[n_side], rest[n_side + 1:-2], rest[-2:]
    _side_cast(side_in, side_out)
    j = pl.program_id(1)
    d = h_ref.shape[1]
    h = h_ref[...]
    u = _rms(h, g_ref[...]).astype(BF16)
    a = _dot(u, w1_ref[...]) + b1_ref[...]
    glu = a[:, :d] * jax.nn.sigmoid(a[:, d:])

    keep = CONV_HALO + SUBLANES

    @pl.when(j == 0)
    def _():
        gbuf[:, 0:keep, :] = jnp.zeros((SUBLANES, keep, d), F32)

    for b in range(SUBLANES):
        gbuf[b, CONV_HALO + b:CONV_HALO + b + ts, :] = glu
    for cb in range(d // LANES):
        cs = slice(cb * LANES, (cb + 1) * LANES)
        for r0 in range(0, ts, rb):
            acc = jnp.zeros((rb, LANES), F32)
            for k in range(width):
                a, b = divmod(width - 1 - k, SUBLANES)
                off = CONV_HALO + r0 - SUBLANES * a
                acc = acc + gbuf[b, off:off + rb, cs] * dw_ref[k:k + 1, cs]
            cbuf[r0:r0 + rb, cs] = acc + bdw_ref[:, cs]
    gbuf[:, 0:keep, :] = gbuf[:, ts:ts + keep, :]

    c = cbuf[...]
    mu = jnp.mean(c, axis=-1, keepdims=True)
    cc = c - mu
    var = jnp.mean(cc * cc, axis=-1, keepdims=True)
    un = cc * lax.rsqrt(var + LN_EPS) * lng_ref[...] + lnb_ref[...]
    un = un * jax.nn.sigmoid(un)
    o_ref[...] = h + _dot(un.astype(BF16), w2_ref[...]) + b2_ref[...]


def _conv_layer(h, g, w1, b1, dw, bdw, lng, lnb, w2, b2, side_weights, side_layer, *, batch, seq):
    n, d = h.shape
    width = dw.shape[0]
    assert width - 1 <= CONV_HALO
    ts = min(256, seq)
    rb = min(128, ts)
    nt = seq // ts
    row = lambda b, j: (b * nt + j, 0)
    c2 = lambda b, j: (0, 0)
    side, side_in, side_out, side_shapes = _side_cast_plan(side_weights, side_layer, batch * nt,
                                                           lambda b, j: b * nt + j)
    outs = pl.pallas_call(
        functools.partial(_conv_kernel, ts=ts, width=width, rb=rb, n_side=len(side)),
        grid=(batch, nt),
        in_specs=[pl.BlockSpec((ts, d), row),
                  pl.BlockSpec((1, d), c2),
                  pl.BlockSpec(w1.shape, c2),
                  pl.BlockSpec(b1.shape, c2),
                  pl.BlockSpec(dw.shape, c2),
                  pl.BlockSpec((1, d), c2),
                  pl.BlockSpec((1, d), c2),
                  pl.BlockSpec((1, d), c2),
                  pl.BlockSpec(w2.shape, c2),
                  pl.BlockSpec((1, d), c2)] + side_in,
        out_specs=[pl.BlockSpec((ts, d), row)] + side_out,
        out_shape=[jax.ShapeDtypeStruct((n, d), F32)] + side_shapes,
        scratch_shapes=[pltpu.VMEM((SUBLANES, ts + CONV_HALO + SUBLANES, d), F32), pltpu.VMEM((ts, d), F32)],
        compiler_params=_params("arbitrary", "arbitrary"),
        name="conv_module",
    )(h, g, w1, b1, dw, bdw, lng, lnb, w2, b2, *side)
    return outs[0], [o.reshape(w.shape[1:]) for o, w in zip(outs[1:], side_weights)]


def _mla_proj_kernel(h_ref, pos_ref, g_ref, wqa_ref, qnorm_ref, wqb_ref, wkva_ref, kvnorm_ref, wkvb_ref,
                     invf_ref, sgn_ref, qt_o, k_o, vt_o, *, heads, lora):
    hn = heads * NOPE
    rows = h_ref.shape[0]
    u = _rms(h_ref[...], g_ref[...]).astype(BF16)
    qa = _rms(_dot(u, wqa_ref[...]), qnorm_ref[...]).astype(BF16)
    q = _dot(qa, wqb_ref[...])
    kva = _dot(u, wkva_ref[...])
    ckv = _rms(kva[:, :lora], kvnorm_ref[...]).astype(BF16)
    kv = _dot(ckv, wkvb_ref[...])
    ang = pos_ref[...].astype(F32) * invf_ref[...]
    c = jnp.cos(ang)
    s = jnp.sin(ang) * sgn_ref[...]
    lane = lax.broadcasted_iota(jnp.int32, (rows, LANES), 1)

    def rope(x):
        swapped = jnp.where(lane < ROPE // 2, pltpu.roll(x, LANES - ROPE // 2, 1), pltpu.roll(x, ROPE // 2, 1))
        return x * c + swapped * s

    kr = rope(kva[:, lora:lora + LANES]).astype(BF16)
    ones = jnp.ones((V_ROWS - NOPE, rows), BF16)
    for hd in range(heads):
        ns = slice(hd * NOPE, (hd + 1) * NOPE)
        qr = rope(q[:, hn + hd * LANES:hn + (hd + 1) * LANES])
        qt_o[2 * hd * NOPE:(2 * hd + 1) * NOPE, :] = q[:, ns].T.astype(BF16)
        qt_o[(2 * hd + 1) * NOPE:(2 * hd + 2) * NOPE, :] = qr.T.astype(BF16)
        k_o[:, 2 * hd * NOPE:(2 * hd + 1) * NOPE] = kv[:, ns].astype(BF16)
        k_o[:, (2 * hd + 1) * NOPE:(2 * hd + 2) * NOPE] = kr
        vt_o[hd * V_ROWS:hd * V_ROWS + NOPE, :] = kv[:, hn + hd * NOPE:hn + (hd + 1) * NOPE].T.astype(BF16)
        vt_o[hd * V_ROWS + NOPE:(hd + 1) * V_ROWS, :] = ones


def _attn_kernel(qi_ref, kj_ref, qt_ref, k_ref, vt_ref, h_ref, wot_ref, *rest, heads, tq, tk, scale, n_side):
    side_in, o_ref, side_out, (m_sc, acc_sc) = rest[:n_side], rest[n_side], rest[n_side + 1:-2], rest[-2:]
    _side_cast(side_in, side_out)
    i = qi_ref[pl.program_id(1)]
    j = kj_ref[pl.program_id(1)]
    c = scale * LOG2E

    @pl.when(j == 0)
    def _():
        m_sc[...] = jnp.full(m_sc.shape, -jnp.inf, F32)
        acc_sc[...] = jnp.zeros(acc_sc.shape, F32)

    def step(diagonal):
        if diagonal:
            causal = (lax.broadcasted_iota(jnp.int32, (tk, tq), 0)
                      <= lax.broadcasted_iota(jnp.int32, (tk, tq), 1))
        es = [slice(2 * hd * NOPE, (2 * hd + 2) * NOPE) for hd in range(heads)]
        vs = [slice(hd * V_ROWS, (hd + 1) * V_ROWS) for hd in range(heads)]
        ss = [_dot(k_ref[:, e], qt_ref[e, :]) for e in es]
        if diagonal:
            ss = [jnp.where(causal, s, -jnp.inf) for s in ss]
        m_prev = [m_sc[hd] for hd in range(heads)]
        m_new = [jnp.maximum(mp, jnp.max(s, axis=0, keepdims=True)) for mp, s in zip(m_prev, ss)]
        ps = [jnp.exp2((s - mn) * c).astype(BF16) for s, mn in zip(ss, m_new)]
        pv = [_dot(vt_ref[v, :], p) for p, v in zip(ps, vs)]
        for hd in range(heads):
            alpha = jnp.exp2((m_prev[hd] - m_new[hd]) * c)
            acc_sc[vs[hd], :] = alpha * acc_sc[vs[hd], :] + pv[hd]
            m_sc[hd] = m_new[hd]

    @pl.when(j < i)
    def _():
        step(False)

    @pl.when(j == i)
    def _():
        step(True)
        parts = [acc_sc[hd * V_ROWS:hd * V_ROWS + NOPE, :] / acc_sc[hd * V_ROWS + NOPE:hd * V_ROWS + NOPE + 1, :]
                 for hd in range(heads)]
        ot = jnp.concatenate(parts, axis=0).astype(BF16)
        o_ref[...] = h_ref[...] + _dot(wot_ref[...], ot).T


def _mla_layer(h, positions, g, wqa, qnorm, wqb, wkva, kvnorm, wkvb, wot, invf, sgn, side_weights, side_layer,
               *, batch, seq, heads, lora):
    n, d = h.shape
    hn = heads * NOPE
    tp = min(512, n)
    c2 = lambda i: (0, 0)
    row = lambda i: (i, 0)
    col = lambda i: (0, i)
    qt, k, vt = pl.pallas_call(
        functools.partial(_mla_proj_kernel, heads=heads, lora=lora),
        grid=(n // tp,),
        in_specs=[pl.BlockSpec((tp, d), row),
                  pl.BlockSpec((tp, 1), row),
                  pl.BlockSpec((1, d), c2),
                  pl.BlockSpec(wqa.shape, c2),
                  pl.BlockSpec(qnorm.shape, c2),
                  pl.BlockSpec(wqb.shape, c2),
                  pl.BlockSpec(wkva.shape, c2),
                  pl.BlockSpec(kvnorm.shape, c2),
                  pl.BlockSpec(wkvb.shape, c2),
                  pl.BlockSpec((1, LANES), c2),
                  pl.BlockSpec((1, LANES), c2)],
        out_specs=[pl.BlockSpec((2 * hn, tp), col), pl.BlockSpec((tp, 2 * hn), row),
                   pl.BlockSpec((heads * V_ROWS, tp), col)],
        out_shape=[jax.ShapeDtypeStruct((2 * hn, n), BF16), jax.ShapeDtypeStruct((n, 2 * hn), BF16),
                   jax.ShapeDtypeStruct((heads * V_ROWS, n), BF16)],
        compiler_params=_params("arbitrary"),
        name="mla_proj",
    )(h, positions, g, wqa, qnorm, wqb, wkva, kvnorm, wkvb, invf, sgn)

    tq = tk = min(512, seq)
    nq = seq // tq
    scale = float(NOPE + ROPE) ** -0.5
    pairs = [(i, j) for i in range(nq) for j in range(i + 1)]
    qi = jnp.asarray([p[0] for p in pairs], jnp.int32)
    kj = jnp.asarray([p[1] for p in pairs], jnp.int32)
    npairs = len(pairs)
    qrow = lambda b, p, qi, kj: (b * nq + qi[p], 0)
    qcol = lambda b, p, qi, kj: (0, b * nq + qi[p])
    krow = lambda b, p, qi, kj: (b * nq + kj[p], 0)
    kcol = lambda b, p, qi, kj: (0, b * nq + kj[p])
    side, side_in, side_out, side_shapes = _side_cast_plan(side_weights, side_layer, batch * npairs,
                                                           lambda b, p, qi, kj: b * npairs + p)
    grid_spec = pltpu.PrefetchScalarGridSpec(
        num_scalar_prefetch=2,
        grid=(batch, npairs),
        in_specs=[pl.BlockSpec((2 * hn, tq), qcol),
                  pl.BlockSpec((tk, 2 * hn), krow),
                  pl.BlockSpec((heads * V_ROWS, tk), kcol),
                  pl.BlockSpec((tq, d), qrow),
                  pl.BlockSpec(wot.shape, lambda b, p, qi, kj: (0, 0))] + side_in,
        out_specs=[pl.BlockSpec((tq, d), qrow)] + side_out,
        scratch_shapes=[pltpu.VMEM((heads, 1, tq), F32), pltpu.VMEM((heads * V_ROWS, tq), F32)],
    )
    outs = pl.pallas_call(
        functools.partial(_attn_kernel, heads=heads, tq=tq, tk=tk, scale=scale, n_side=len(side)),
        grid_spec=grid_spec,
        out_shape=[jax.ShapeDtypeStruct((n, d), F32)] + side_shapes,
        compiler_params=_params("arbitrary", "arbitrary"),
        name="mla_attention",
    )(qi, kj, qt, k, vt, h, wot, *side)
    return outs[0], [o.reshape(w.shape[1:]) for o, w in zip(outs[1:], side_weights)]


def _store_row_tiles(ref, value, rows):
    slabs = value.shape[1] // LANES
    for s in range(slabs):
        ref[pl.ds(s, rows, stride=slabs), :] = value[:, s * LANES:(s + 1) * LANES]


def _load_row_tile_slab(ref, s, rows, slabs):
    return ref[pl.ds(s, rows, stride=slabs), :]


def _route_kernel(h_ref, g_ref, r_ref, u_ref, info_ref, info_t_ref, cnt_ref, carry, *, tr, n_exp):
    i = pl.program_id(0)

    @pl.when(i == 0)
    def _():
        carry[...] = jnp.zeros(carry.shape, F32)

    u = _rms(h_ref[...], g_ref[...])
    u_ref[...] = u.astype(BF16)
    logits = _dot_3pass(u, r_ref[...])
    lane = lax.broadcasted_iota(jnp.int32, (tr, LANES), 1)
    lg = jnp.where(lane < n_exp, logits, -jnp.inf)
    m1 = jnp.max(lg, axis=1, keepdims=True)
    i1 = jnp.min(jnp.where(lg == m1, lane, LANES), axis=1, keepdims=True)
    lg2 = jnp.where(lane == i1, -jnp.inf, lg)
    m2 = jnp.max(lg2, axis=1, keepdims=True)
    i2 = jnp.min(jnp.where(lg2 == m2, lane, LANES), axis=1, keepdims=True)
    e = jnp.exp(m2 - m1)
    g1 = 1.0 / (1.0 + e)
    g2 = e / (1.0 + e)
    sel1 = lane == i1
    sel2 = lane == i2
    onehot = jnp.where(sel1 | sel2, 1.0, 0.0)
    rr = lax.broadcasted_iota(jnp.int32, (tr, tr), 0)
    cc = lax.broadcasted_iota(jnp.int32, (tr, tr), 1)
    tri = jnp.where(cc < rr, 1.0, 0.0).astype(BF16)
    before = _dot(tri, onehot.astype(BF16)) + carry[0:1, :]
    rank1 = jnp.sum(jnp.where(sel1, before, 0.0), axis=1, keepdims=True)
    rank2 = jnp.sum(jnp.where(sel2, before, 0.0), axis=1, keepdims=True)
    carry[...] = carry[...] + jnp.sum(onehot, axis=0, keepdims=True)
    cols = (i1.astype(F32), i2.astype(F32), g1, g2, rank1, rank2)
    info = jnp.zeros((tr, LANES), F32)
    for k, col in enumerate(cols):
        info = jnp.where(lane == k, col, info)
    info_ref[...] = info
    info_t_ref[...] = info.T[0:SUBLANES, :]
    cnt_ref[...] = carry[...]


def _route(h, g, router_padded, n_exp):
    n, d = h.shape
    tr = min(512, n)
    slabs = d // LANES
    return pl.pallas_call(
        functools.partial(_route_kernel, tr=tr, n_exp=n_exp),
        grid=(n // tr,),
        in_specs=[pl.BlockSpec((tr, d), lambda i: (i, 0)),
                  pl.BlockSpec((1, d), lambda i: (0, 0)),
                  pl.BlockSpec((d, LANES), lambda i: (0, 0))],
        out_specs=[pl.BlockSpec((tr, d), lambda i: (i, 0)),
                   pl.BlockSpec((tr, LANES), lambda i: (i, 0)),
                   pl.BlockSpec((SUBLANES, tr), lambda i: (0, i)),
                   pl.BlockSpec((8, LANES), lambda i: (i, 0))],
        out_shape=[jax.ShapeDtypeStruct((n, d), BF16),
                   jax.ShapeDtypeStruct((n, LANES), F32),
                   jax.ShapeDtypeStruct((SUBLANES, n), F32),
                   jax.ShapeDtypeStruct((n // tr * 8, LANES), F32)],
        scratch_shapes=[pltpu.VMEM((8, LANES), F32)],
        compiler_params=_params("arbitrary"),
        name="moe_route",
    )(h, g, router_padded)


def _run_pieces(length, piece):
    rem = length & (RUN_CHUNK - 1)
    full = length - rem

    def body(c, carry):
        piece(c * RUN_CHUNK, RUN_CHUNK)
        return carry

    lax.fori_loop(0, lax.shift_right_logical(full, RUN_CHUNK.bit_length() - 1), body, 0)
    bit = RUN_CHUNK // 2
    while bit:
        done = rem & (RUN_CHUNK - 2 * bit)
        pl.when((rem & bit) != 0)(functools.partial(piece, full + done, bit))
        bit //= 2


def _dispatch_kernel(pad_start_ref, pad_cnt_ref, plan_ref, info_t_ref, u_ref, x_ref, zero_sc, stage, sem, wsem,
                     *, tile, slabs, n_exp):
    i = pl.program_id(0)
    slot = i % 2

    def each_pad_copy(action):
        for e in range(n_exp):
            def piece(off, nrows):
                d = pl.multiple_of((pad_start_ref[e] + off) * slabs, slabs)
                action(pltpu.make_async_copy(zero_sc.at[pl.ds(0, nrows * slabs)],
                                             x_ref.at[pl.ds(d, nrows * slabs)], sem))

            _run_pieces(pad_cnt_ref[e], piece)

    @pl.when(pl.program_id(0) == 0)
    def _():
        zero_sc[...] = jnp.zeros(zero_sc.shape, F32)
        each_pad_copy(lambda copy: copy.start())
        each_pad_copy(lambda copy: copy.wait())

        def chunk_copy(i):
            dst = pl.multiple_of((pad_start_ref[n_exp] + i * ZERO_CHUNK) * slabs, slabs)
            return pltpu.make_async_copy(zero_sc, x_ref.at[pl.ds(dst, ZERO_CHUNK * slabs)], sem)

        def fill_tail(i, c):
            chunk_copy(i).start()
            return c

        def drain_tail(i, c):
            chunk_copy(i).wait()
            return c

        lax.fori_loop(0, pad_cnt_ref[n_exp], fill_tail, 0)
        lax.fori_loop(0, pad_cnt_ref[n_exp], drain_tail, 0)

    def each_run_copy(block, buf, action):
        for e in range(n_exp):
            base = (block * PLAN_ROWS) * n_exp + e
            dst0, length, src0 = plan_ref[base], plan_ref[base + n_exp], plan_ref[base + 2 * n_exp]

            def piece(off, nrows):
                s = pl.multiple_of((src0 + off) * slabs, slabs)
                d = pl.multiple_of((dst0 + off) * slabs, slabs)
                action(pltpu.make_async_copy(stage.at[buf, pl.ds(s, nrows * slabs)],
                                             x_ref.at[pl.ds(d, nrows * slabs)], wsem.at[buf]))

            _run_pieces(length, piece)

    info_t = info_t_ref[...]
    shift_base = (i * PLAN_ROWS + 3) * n_exp
    r_idx = lax.broadcasted_iota(jnp.int32, (TOP_K * tile, tile), 0)
    onehot = jnp.zeros(r_idx.shape, F32)
    for k in range(TOP_K):
        ls = info_t[4 + k:5 + k, :] + _select_by_expert(info_t[k:k + 1, :], plan_ref, shift_base, n_exp)
        onehot = onehot + jnp.where(r_idx == ls.astype(jnp.int32), 1.0, 0.0)
    x = _dot(onehot.astype(BF16), u_ref[...])
    _store_row_tiles(stage.at[slot], x, TOP_K * tile)
    each_run_copy(i, slot, lambda copy: copy.start())

    @pl.when(i > 0)
    def _():
        each_run_copy(i - 1, 1 - slot, lambda copy: copy.wait())

    @pl.when(i == pl.num_programs(0) - 1)
    def _():
        each_run_copy(i, slot, lambda copy: copy.wait())


def _dispatch(u, info_t, plan, pad_start, pad_cnt, sorted_rows, tile, slabs):
    n, d = u.shape
    grid_spec = pltpu.PrefetchScalarGridSpec(
        num_scalar_prefetch=3,
        grid=(n // tile,),
        in_specs=[pl.BlockSpec((SUBLANES, tile), lambda i, ps, pc, pn: (0, i)),
                  pl.BlockSpec((tile, d), lambda i, ps, pc, pn: (i, 0))],
        out_specs=pl.BlockSpec(memory_space=pl.ANY),
        scratch_shapes=[pltpu.VMEM((ZERO_CHUNK * slabs, LANES), F32),
                        pltpu.VMEM((2, TOP_K * tile * slabs, LANES), F32),
                        pltpu.SemaphoreType.DMA(()), pltpu.SemaphoreType.DMA((2,))],
    )
    return pl.pallas_call(
        functools.partial(_dispatch_kernel, tile=tile, slabs=slabs, n_exp=pad_start.shape[0] - 1),
        grid_spec=grid_spec,
        out_shape=jax.ShapeDtypeStruct((sorted_rows * slabs, LANES), F32),
        compiler_params=_params("arbitrary"),
        name="moe_dispatch",
    )(pad_start, pad_cnt, plan, info_t, u)


def _moe_kernel(te_ref, nu_ref, x_ref, wg_ref, wu_ref, wd_ref, o_ref, x_sc, acc_sc, *, tm, slabs, nj):
    t = pl.program_id(0)
    j = pl.program_id(1)

    @pl.when(t < nu_ref[0])
    def _():
        @pl.when(j == 0)
        def _():
            for s in range(slabs):
                x_sc[:, s * LANES:(s + 1) * LANES] = _load_row_tile_slab(x_ref, s, tm, slabs).astype(BF16)

        def finish(r):
            o_ref[...] = r.astype(BF16)

        _accumulate_steps(j, nj, lambda: _swiglu_step(x_sc[...], wg_ref[0], wu_ref[0], wd_ref[0]), acc_sc, finish)

    @pl.when((t >= nu_ref[0]) & (j == 0))
    def _():
        o_ref[...] = jnp.zeros(o_ref.shape, BF16)


def _moe_experts(x_sorted, tile_expert, n_used, wg, wu, wd, tm, slabs):
    d = slabs * LANES
    rows = x_sorted.shape[0] // slabs
    f = wg.shape[2]
    tf = _ff_chunk(f)
    nj = f // tf
    tmap = lambda t, j, te, nu: (jnp.minimum(t, nu[0] - 1), 0)
    jsel = lambda t, j, nu: jnp.where(t < nu[0], j, nj - 1)
    grid_spec = pltpu.PrefetchScalarGridSpec(
        num_scalar_prefetch=2,
        grid=(rows // tm, nj),
        in_specs=[pl.BlockSpec((tm * slabs, LANES), tmap),
                  pl.BlockSpec((1, d, tf), lambda t, j, te, nu: (te[t], 0, jsel(t, j, nu))),
                  pl.BlockSpec((1, d, tf), lambda t, j, te, nu: (te[t], 0, jsel(t, j, nu))),
                  pl.BlockSpec((1, tf, d), lambda t, j, te, nu: (te[t], jsel(t, j, nu), 0))],
        out_specs=pl.BlockSpec((tm, d), lambda t, j, te, nu: (t, 0)),
        scratch_shapes=[pltpu.VMEM((tm, d), BF16), pltpu.VMEM((tm, d), F32)],
    )
    return pl.pallas_call(
        functools.partial(_moe_kernel, tm=tm, slabs=slabs, nj=nj),
        grid_spec=grid_spec,
        out_shape=jax.ShapeDtypeStruct((rows, d), BF16),
        compiler_params=_params("arbitrary", "arbitrary"),
        name="moe_experts",
    )(tile_expert, n_used, x_sorted, wg, wu, wd)


def _select_by_expert(expert, plan_ref, base, n_exp):
    out = jnp.zeros(expert.shape, F32)
    for e in range(n_exp):
        out = jnp.where(expert == float(e), plan_ref[base + e].astype(F32), out)
    return out


def _combine_stage_rows(tile, n_exp):
    max_chunks = (TOP_K * tile + n_exp * (15 + COMBINE_CHUNK - 1)) // COMBINE_CHUNK
    return -(-max_chunks * COMBINE_CHUNK // 256) * 256


def _combine_kernel(plan_ref, h_ref, info_ref, gf_ref, y_ref, o_ref, stage, sem, *, tile, n_exp, final_norm):
    i = pl.program_id(0)
    slot = i % 2

    def each_chunk(block, buf, action):
        for e in range(n_exp):
            base = (block * PLAN_ROWS) * n_exp + e

            def body(c, carry):
                src = pl.multiple_of(plan_ref[base] + c * COMBINE_CHUNK, 16)
                dst = pl.multiple_of(plan_ref[base + 2 * n_exp] + c * COMBINE_CHUNK, COMBINE_CHUNK)
                action(pltpu.make_async_copy(y_ref.at[pl.ds(src, COMBINE_CHUNK)],
                                             stage.at[buf, pl.ds(dst, COMBINE_CHUNK)], sem.at[buf]))
                return carry

            lax.fori_loop(0, plan_ref[base + n_exp], body, 0)

    @pl.when(i == 0)
    def _():
        stage[...] = jnp.zeros(stage.shape, BF16)
        each_chunk(i, slot, lambda copy: copy.start())

    @pl.when(i + 1 < pl.num_programs(0))
    def _():
        each_chunk(i + 1, 1 - slot, lambda copy: copy.start())

    each_chunk(i, slot, lambda copy: copy.wait())

    info = info_ref[...]
    shift_base = (i * PLAN_ROWS + 3) * n_exp
    col = lax.broadcasted_iota(jnp.int32, (tile, stage.shape[1]), 1)
    p = jnp.zeros(col.shape, F32)
    for k in range(TOP_K):
        loc = info[:, 4 + k:5 + k] + _select_by_expert(info[:, k:k + 1], plan_ref, shift_base, n_exp)
        p = p + jnp.where(col == loc.astype(jnp.int32), info[:, 2 + k:3 + k], 0.0)
    out = h_ref[...] + _dot(p.astype(BF16), stage[slot])
    if final_norm:
        out = _rms(out, gf_ref[...])
    o_ref[...] = out


def _combine(h, info, plan, y_sorted, g_final, final_norm, n_exp):
    n, d = h.shape
    tile = n // (plan.shape[0] // (PLAN_ROWS * n_exp))
    grid_spec = pltpu.PrefetchScalarGridSpec(
        num_scalar_prefetch=1,
        grid=(n // tile,),
        in_specs=[pl.BlockSpec((tile, d), lambda i, plan: (i, 0)),
                  pl.BlockSpec((tile, LANES), lambda i, plan: (i, 0)),
                  pl.BlockSpec((1, d), lambda i, plan: (0, 0)),
                  pl.BlockSpec(memory_space=pl.ANY)],
        out_specs=pl.BlockSpec((tile, d), lambda i, plan: (i, 0)),
        scratch_shapes=[pltpu.VMEM((2, _combine_stage_rows(tile, n_exp), d), BF16), pltpu.SemaphoreType.DMA((2,))],
    )
    return pl.pallas_call(
        functools.partial(_combine_kernel, tile=tile, n_exp=n_exp, final_norm=final_norm),
        grid_spec=grid_spec,
        out_shape=jax.ShapeDtypeStruct((n, d), F32),
        compiler_params=_params("arbitrary"),
        name="moe_combine",
    )(plan, h, info, g_final, y_sorted)


def _moe_layer(h, g, router, wg, wu, wd, g_final, final_norm):
    n, d = h.shape
    n_exp = router.shape[1]
    tm = min(512, n)
    router_padded = jnp.pad(router, ((0, 0), (0, LANES - n_exp)))
    u, info, info_t, cnt = _route(h, g, router_padded, n_exp)

    tile = min(512, n)
    n_blocks = n // tile
    after = cnt.reshape(n_blocks, 8, LANES)[:, 0, :n_exp].astype(jnp.int32)
    before = jnp.concatenate([jnp.zeros((1, n_exp), jnp.int32), after[:-1]])
    counts = after[-1]
    padded = (counts + tm - 1) // tm * tm
    ends = jnp.cumsum(padded)
    starts = ends - padded
    rows = TOP_K * n + n_exp * tm
    n_used = (ends[-1] // tm).astype(jnp.int32)
    tile_ids = jnp.minimum(jnp.arange(rows // tm, dtype=jnp.int32), n_used - 1)
    tile_expert = jnp.sum(tile_ids[:, None] * tm >= ends[None, :], axis=1).astype(jnp.int32)

    run_start = starts[None, :] + before
    run_len = after - before
    fetch_start = run_start // 16 * 16
    n_chunks = jnp.where(run_len > 0, (run_start - fetch_start + run_len + COMBINE_CHUNK - 1) // COMBINE_CHUNK, 0)
    stage_off = COMBINE_CHUNK * (jnp.cumsum(n_chunks, axis=1) - n_chunks)
    plan = jnp.stack([fetch_start, n_chunks, stage_off, starts[None, :] + stage_off - fetch_start],
                     axis=1).reshape(-1).astype(jnp.int32)

    slabs = d // LANES
    assert slabs == 8
    local_off = jnp.cumsum(run_len, axis=1) - run_len
    dispatch_plan = jnp.stack([run_start, run_len, local_off, local_off - before],
                              axis=1).reshape(-1).astype(jnp.int32)
    assert tm % ZERO_CHUNK == 0
    pad_start = jnp.concatenate([starts + counts, ends[-1:]]).astype(jnp.int32)
    pad_cnt = jnp.concatenate([padded - counts, (rows - ends[-1:]) // ZERO_CHUNK]).astype(jnp.int32)
    x_sorted = _dispatch(u, info_t, dispatch_plan, pad_start, pad_cnt, rows, tile, slabs)
    y_sorted = _moe_experts(x_sorted, tile_expert, n_used.reshape(1), wg, wu, wd, tm, slabs)
    return _combine(h, info, plan, y_sorted, g_final, final_norm, n_exp)


def _mla_weights(wq_b, wkv_a, wkv_b, heads, lora):
    def padded(w):
        return jnp.concatenate([w, jnp.zeros(w.shape[:-1] + (LANES - ROPE,), w.dtype)], axis=-1)

    rq = wq_b.shape[0]
    wq = wq_b.reshape(rq, heads, NOPE + ROPE)
    wqb = jnp.concatenate([wq[..., :NOPE].reshape(rq, -1), padded(wq[..., NOPE:]).reshape(rq, -1)], axis=1)
    wkva = jnp.concatenate([wkv_a[:, :lora], padded(wkv_a[:, lora:])], axis=1)
    wkv = wkv_b.reshape(lora, heads, 2 * NOPE)
    wkvb = jnp.concatenate([wkv[..., :NOPE].reshape(lora, -1), wkv[..., NOPE:].reshape(lora, -1)], axis=1)
    return wqb.astype(BF16), wkva.astype(BF16), wkvb.astype(BF16)


def _cast_kernel(x_ref, o_ref):
    o_ref[...] = x_ref[0].astype(BF16)


def _to_bf16(w, layer):
    cols = w.shape[-1]
    w3 = w.reshape(w.shape[0], -1, cols)
    rows = w3.shape[1]
    cap = 256 if cols > 2048 else 1024
    tr = next(t for t in (1024, 512, 256, 128, 64, 32, 16, 8) if t <= cap and rows % t == 0)
    out = pl.pallas_call(
        _cast_kernel,
        grid=(rows // tr,),
        in_specs=[pl.BlockSpec((1, tr, cols), lambda i: (layer, i, 0))],
        out_specs=pl.BlockSpec((tr, cols), lambda i: (i, 0)),
        out_shape=jax.ShapeDtypeStruct((rows, cols), BF16),
        compiler_params=_params("arbitrary"),
        name="weights_to_bf16",
    )(w3)
    return out.reshape(w.shape[1:])


def _final_norm_kernel(h_ref, g_ref, o_ref):
    o_ref[...] = _rms(h_ref[...], g_ref[...])


def _final_norm(h, g):
    n, d = h.shape
    tm = min(512, n)
    return pl.pallas_call(
        _final_norm_kernel,
        grid=(n // tm,),
        in_specs=[pl.BlockSpec((tm, d), lambda i: (i, 0)), pl.BlockSpec((1, d), lambda i: (0, 0))],
        out_specs=pl.BlockSpec((tm, d), lambda i: (i, 0)),
        out_shape=jax.ShapeDtypeStruct((n, d), F32),
        compiler_params=_params("arbitrary"),
        name="final_norm",
    )(h, g)


@jax.jit
def _trunk(x, positions, norm_mix, norm_ffn, norm_final, pool_w, pool_scale,
           conv_w1, conv_b1, conv_dw, conv_bdw, conv_ln_g, conv_ln_b, conv_w2, conv_b2,
           mla_wq_a, mla_q_norm, mla_wq_b, mla_wkv_a, mla_kv_norm, mla_wkv_b, mla_wo,
           ffn_wg, ffn_wu, ffn_wd, moe_router, moe_wg, moe_wu, moe_wd):
    batch, seq, d = x.shape
    depth = norm_mix.shape[0]
    lora = mla_kv_norm.shape[1]
    heads = mla_wo.shape[1] // NOPE
    assert mla_wq_b.shape[2] == heads * (NOPE + ROPE) and mla_wkv_a.shape[2] == lora + ROPE
    n = batch * seq
    h = x.reshape(n, d)
    pos = positions.reshape(n, 1)
    row = lambda v: v.reshape(1, -1)

    half = jnp.arange(0, ROPE, 2, dtype=F32) / ROPE
    inv_freq = ROPE_THETA ** (-half)
    invf = row(jnp.concatenate([inv_freq, inv_freq, jnp.zeros((LANES - ROPE,), F32)]))
    sgn = row(jnp.concatenate([-jnp.ones((ROPE // 2,), F32), jnp.ones((ROPE // 2,), F32),
                               jnp.zeros((LANES - ROPE,), F32)]))

    moe_w = (moe_wg, moe_wu, moe_wd)
    moe_bf16 = {}
    ffn_bf16 = {}

    def moe_cast_job(i):
        nxt = i if i % 2 == 1 else i + 1
        if nxt < depth and nxt // 2 not in moe_bf16:
            return moe_w, nxt // 2
        return (), 0

    for i in range(depth):
        kind, j = i % 3, i // 3
        g = row(norm_mix[i])
        if kind == 0:
            jobs = [(w, f) for f in range(ffn_wg.shape[0]) for w in (ffn_wg, ffn_wu, ffn_wd)] if not ffn_bf16 else []
            h, cast = _pool_layer(h, g, pool_w[j].astype(BF16), row(pool_scale[j]), [w for w, _ in jobs],
                                  [f for _, f in jobs], batch=batch, seq=seq)
            for k in range(0, len(cast), 3):
                ffn_bf16[jobs[k][1]] = cast[k:k + 3]
        elif kind == 1:
            side_w, side_layer = moe_cast_job(i)
            h, cast = _conv_layer(h, g, conv_w1[j].astype(BF16), row(conv_b1[j]), conv_dw[j], row(conv_bdw[j]),
                                  row(conv_ln_g[j]), row(conv_ln_b[j]), conv_w2[j].astype(BF16), row(conv_b2[j]),
                                  side_w, side_layer, batch=batch, seq=seq)
            if cast:
                moe_bf16[side_layer] = cast
        else:
            wqb, wkva, wkvb = _mla_weights(mla_wq_b[j], mla_wkv_a[j], mla_wkv_b[j], heads, lora)
            side_w, side_layer = moe_cast_job(i)
            h, cast = _mla_layer(h, pos, g, mla_wq_a[j].astype(BF16), row(mla_q_norm[j]), wqb, wkva,
                                 row(mla_kv_norm[j]), wkvb, mla_wo[j].T.astype(BF16), invf, sgn,
                                 side_w, side_layer, batch=batch, seq=seq, heads=heads, lora=lora)
            if cast:
                moe_bf16[side_layer] = cast
        g = row(norm_ffn[i])
        f_idx = i // 2
        last = i == depth - 1
        if i % 2 == 0:
            wg, wu, wd = ffn_bf16.get(f_idx) or [_to_bf16(w, f_idx) for w in (ffn_wg, ffn_wu, ffn_wd)]
            h = _ffn_layer(h, g, wg, wu, wd)
            if last:
                h = _final_norm(h, row(norm_final))
        else:
            wg, wu, wd = moe_bf16.get(f_idx) or [_to_bf16(w, f_idx) for w in moe_w]
            h = _moe_layer(h, g, moe_router[f_idx], wg, wu, wd, row(norm_final), last)
    return h.reshape(batch, seq, d)


def kernel(x, positions, norm_mix, norm_ffn, norm_final, pool_w, pool_scale, conv_w1, conv_b1, conv_dw, conv_bdw, conv_ln_g, conv_ln_b, conv_w2, conv_b2, mla_wq_a, mla_q_norm, mla_wq_b, mla_wkv_a, mla_kv_norm, mla_wkv_b, mla_wo, ffn_wg, ffn_wu, ffn_wd, moe_router, moe_wg, moe_wu, moe_wd):
    return _trunk(x, positions, norm_mix, norm_ffn, norm_final, pool_w, pool_scale,
                  conv_w1, conv_b1, conv_dw, conv_bdw, conv_ln_g, conv_ln_b, conv_w2, conv_b2,
                  mla_wq_a, mla_q_norm, mla_wq_b, mla_wkv_a, mla_kv_norm, mla_wkv_b, mla_wo,
                  ffn_wg, ffn_wu, ffn_wd, moe_router, moe_wg, moe_wu, moe_wd)
```

```python
import functools

import jax
import jax.numpy as jnp
from jax import lax
from jax.experimental import pallas as pl
from jax.experimental.pallas import tpu as pltpu

F32 = jnp.float32
BF16 = jnp.bfloat16

NORM_EPS = 1e-6
LN_EPS = 1e-5
ROPE_THETA = 10000.0
LOG2E = 1.4426950408889634

POOL_WINDOWS = (2, 4, 8, 16)
POOL_HALO = 16
CONV_HALO = 32
LANES = 128
SUBLANES = 8
POOL_TOP = POOL_HALO + SUBLANES
NOPE = 128
ROPE = 64
V_ROWS = NOPE + 16
TOP_K = 2
ZERO_CHUNK = 64
COMBINE_CHUNK = 32
RUN_CHUNK = 64
PLAN_ROWS = 4

VMEM_LIMIT = 56 * 1024 * 1024


def _params(*sem):
    return pltpu.CompilerParams(dimension_semantics=sem, vmem_limit_bytes=VMEM_LIMIT)


def _rms(x, g, eps=NORM_EPS):
    return x * lax.rsqrt(jnp.mean(x * x, axis=-1, keepdims=True) + eps) * g


def _dot(a, b):
    return jnp.dot(a, b, preferred_element_type=F32)


def _split_bf16(x):
    hi = x.astype(BF16)
    return hi, (x - hi.astype(F32)).astype(BF16)


def _dot_3pass(a, b):
    a_hi, a_lo = _split_bf16(a)
    b_hi, b_lo = _split_bf16(b)
    return _dot(a_hi, b_hi) + (_dot(a_hi, b_lo) + _dot(a_lo, b_hi))


def _ff_chunk(f):
    for c in (1792, 1024, 512, 256, 128):
        if f % c == 0:
            return c
    return f


def _pool_kernel(h_ref, g_ref, w_ref, sc_ref, *rest, ts, gdim, n_side):
    side_in, o_ref, side_out, (ubuf, lvl_a, lvl_b) = rest[:n_side], rest[n_side], rest[n_side + 1:-3], rest[-3:]
    _side_cast(side_in, side_out)
    j = pl.program_id(1)
    h = h_ref[...]
    u = _rms(h, g_ref[...])
    ext = POOL_TOP + ts

    @pl.when(j == 0)
    def _():
        ubuf[0:POOL_TOP, :] = jnp.zeros((POOL_TOP, ubuf.shape[1]), F32)
        lvl_a[0:SUBLANES, :] = jnp.zeros((SUBLANES, gdim), F32)
        lvl_b[0:SUBLANES, :] = jnp.zeros((SUBLANES, gdim), F32)

    ubuf[POOL_TOP:ext, :] = u
    t1 = lax.broadcasted_iota(jnp.int32, (ts, 1), 0) + (j * ts + 1)
    for g, win in enumerate(POOL_WINDOWS):
        cs = slice(g * gdim, (g + 1) * gdim)
        src, cols, spare, k = ubuf, cs, [lvl_a, lvl_b], 1
        while k < win:
            s = src[SUBLANES:ext, cols] + src[SUBLANES - k:ext - k, cols]
            k *= 2
            if k < win:
                dst = spare.pop(0)
                dst[SUBLANES:ext, :] = s
                spare.append(dst)
                src, cols = dst, slice(None)
        s = s[POOL_HALO:POOL_HALO + ts]
        ug = u[:, cs]
        cnt = jnp.minimum(t1, win).astype(F32)
        pooled = s / cnt - ug
        m = _dot(pooled.astype(BF16), w_ref[g])
        o_ref[:, cs] = h[:, cs] + m * sc_ref[:, cs]
    ubuf[SUBLANES:POOL_TOP, :] = ubuf[ts + SUBLANES:ts + POOL_TOP, :]


def _pool_layer(h, g, w, scale, side_weights, side_layers, *, batch, seq):
    n, d = h.shape
    ts = min(512, seq)
    nt = seq // ts
    gdim = d // len(POOL_WINDOWS)
    row = lambda b, j: (b * nt + j, 0)
    const2 = lambda b, j: (0, 0)
    side, side_in, side_out, side_shapes = _side_cast_plan(side_weights, list(side_layers), batch * nt,
                                                           lambda b, j: b * nt + j)
    outs = pl.pallas_call(
        functools.partial(_pool_kernel, ts=ts, gdim=gdim, n_side=len(side)),
        grid=(batch, nt),
        in_specs=[pl.BlockSpec((ts, d), row),
                  pl.BlockSpec((1, d), const2),
                  pl.BlockSpec(w.shape, lambda b, j: (0, 0, 0)),
                  pl.BlockSpec((1, d), const2)] + side_in,
        out_specs=[pl.BlockSpec((ts, d), row)] + side_out,
        out_shape=[jax.ShapeDtypeStruct((n, d), F32)] + side_shapes,
        scratch_shapes=[pltpu.VMEM((ts + POOL_TOP, d), F32), pltpu.VMEM((ts + POOL_TOP, gdim), F32),
                        pltpu.VMEM((ts + POOL_TOP, gdim), F32)],
        compiler_params=_params("arbitrary", "arbitrary"),
        name="pool_mixer",
    )(h, g, w, scale, *side)
    return outs[0], [o.reshape(w.shape[1:]) for o, w in zip(outs[1:], side_weights)]


def _swiglu_step(u, wg, wu, wd):
    a = _dot(u, wg)
    b = _dot(u, wu)
    hh = a * jax.nn.sigmoid(a) * b
    return _dot(hh.astype(BF16), wd)


def _accumulate_steps(j, nj, partial, acc_sc, finish):
    def run(first, last):
        r = partial()
        if not first:
            r = acc_sc[...] + r
        if last:
            finish(r)
        else:
            acc_sc[...] = r

    if nj == 1:
        run(True, True)
        return
    pl.when(j == 0)(functools.partial(run, True, False))
    if nj > 2:
        pl.when((j > 0) & (j < nj - 1))(functools.partial(run, False, False))
    pl.when(j == nj - 1)(functools.partial(run, False, True))


def _ffn_kernel(h_ref, g_ref, wg_ref, wu_ref, wd_ref, o_ref, u_sc, acc_sc, *, nj):
    j = pl.program_id(1)

    @pl.when(j == 0)
    def _():
        u_sc[...] = _rms(h_ref[...], g_ref[...]).astype(BF16)

    def finish(r):
        o_ref[...] = h_ref[...] + r

    _accumulate_steps(j, nj, lambda: _swiglu_step(u_sc[...], wg_ref[...], wu_ref[...], wd_ref[...]), acc_sc, finish)


def _ffn_layer(h, g, wg, wu, wd):
    n, d = h.shape
    f = wg.shape[1]
    tm = min(512, n)
    tf = _ff_chunk(f)
    return pl.pallas_call(
        functools.partial(_ffn_kernel, nj=f // tf),
        grid=(n // tm, f // tf),
        in_specs=[pl.BlockSpec((tm, d), lambda i, j: (i, 0)),
                  pl.BlockSpec((1, d), lambda i, j: (0, 0)),
                  pl.BlockSpec((d, tf), lambda i, j: (0, j)),
                  pl.BlockSpec((d, tf), lambda i, j: (0, j)),
                  pl.BlockSpec((tf, d), lambda i, j: (j, 0))],
        out_specs=pl.BlockSpec((tm, d), lambda i, j: (i, 0)),
        out_shape=jax.ShapeDtypeStruct((n, d), F32),
        scratch_shapes=[pltpu.VMEM((tm, d), BF16), pltpu.VMEM((tm, d), F32)],
        compiler_params=_params("arbitrary", "arbitrary"),
        name="dense_swiglu",
    )(h, g, wg, wu, wd)


def _side_cast_plan(weights, layer, steps, step_index):
    arrays = [w.reshape(w.shape[0], -1, w.shape[-1]) for w in weights]
    if not arrays:
        return [], [], [], []
    nblk = next(n for n in range(steps, 0, -1)
                if all(a.shape[1] % n == 0 and (a.shape[1] // n) % 16 == 0 for a in arrays))
    blk = lambda *g: jnp.minimum(step_index(*g), nblk - 1)
    layers = layer if isinstance(layer, (list, tuple)) else [layer] * len(arrays)
    in_specs, out_specs, out_shapes = [], [], []
    for a, lyr in zip(arrays, layers):
        rows, cols = a.shape[1:]
        in_specs.append(pl.BlockSpec((1, rows // nblk, cols), lambda *g, _l=lyr: (_l, blk(*g), 0)))
        out_specs.append(pl.BlockSpec((rows // nblk, cols), lambda *g: (blk(*g), 0)))
        out_shapes.append(jax.ShapeDtypeStruct((rows, cols), BF16))
    return arrays, in_specs, out_specs, out_shapes


def _side_cast(in_refs, out_refs):
    for src, dst in zip(in_refs, out_refs):
        dst[...] = src[0].astype(BF16)


def _conv_kernel(h_ref, g_ref, w1_ref, b1_ref, dw_ref, bdw_ref, lng_ref, lnb_ref, w2_ref, b2_ref, *rest,
                 ts, width, rb, n_side):
    side_in, o_ref, side_out, (gbuf, cbuf) = rest[:n_side], rest[n_side], rest[n_side + 1:-2], rest[-2:]
    _side_cast(side_in, side_out)
    j = pl.program_id(1)
    d = h_ref.shape[1]
    h = h_ref[...]
    u = _rms(h, g_ref[...]).astype(BF16)
    a = _dot(u, w1_ref[...]) + b1_ref[...]
    glu = a[:, :d] * jax.nn.sigmoid(a[:, d:])

    keep = CONV_HALO + SUBLANES

    @pl.when(j == 0)
    def _():
        gbuf[:, 0:keep, :] = jnp.zeros((SUBLANES, keep, d), F32)

    for b in range(SUBLANES):
        gbuf[b, CONV_HALO + b:CONV_HALO + b + ts, :] = glu
    for cb in range(d // LANES):
        cs = slice(cb * LANES, (cb + 1) * LANES)
        for r0 in range(0, ts, rb):
            acc = jnp.zeros((rb, LANES), F32)
            for k in range(width):
                a, b = divmod(width - 1 - k, SUBLANES)
                off = CONV_HALO + r0 - SUBLANES * a
                acc = acc + gbuf[b, off:off + rb, cs] * dw_ref[k:k + 1, cs]
            cbuf[r0:r0 + rb, cs] = acc + bdw_ref[:, cs]
    gbuf[:, 0:keep, :] = gbuf[:, ts:ts + keep, :]

    c = cbuf[...]
    mu = jnp.mean(c, axis=-1, keepdims=True)
    cc = c - mu
    var = jnp.mean(cc * cc, axis=-1, keepdims=True)
    un = cc * lax.rsqrt(var + LN_EPS) * lng_ref[...] + lnb_ref[...]
    un = un * jax.nn.sigmoid(un)
    o_ref[...] = h + _dot(un.astype(BF16), w2_ref[...]) + b2_ref[...]


def _conv_layer(h, g, w1, b1, dw, bdw, lng, lnb, w2, b2, side_weights, side_layer, *, batch, seq):
    n, d = h.shape
    width = dw.shape[0]
    assert width - 1 <= CONV_HALO
    ts = min(256, seq)
    rb = min(128, ts)
    nt = seq // ts
    row = lambda b, j: (b * nt + j, 0)
    c2 = lambda b, j: (0, 0)
    side, side_in, side_out, side_shapes = _side_cast_plan(side_weights, side_layer, batch * nt,
                                                           lambda b, j: b * nt + j)
    outs = pl.pallas_call(
        functools.partial(_conv_kernel, ts=ts, width=width, rb=rb, n_side=len(side)),
        grid=(batch, nt),
        in_specs=[pl.BlockSpec((ts, d), row),
                  pl.BlockSpec((1, d), c2),
                  pl.BlockSpec(w1.shape, c2),
                  pl.BlockSpec(b1.shape, c2),
                  pl.BlockSpec(dw.shape, c2),
                  pl.BlockSpec((1, d), c2),
                  pl.BlockSpec((1, d), c2),
                  pl.BlockSpec((1, d), c2),
                  pl.BlockSpec(w2.shape, c2),
                  pl.BlockSpec((1, d), c2)] + side_in,
        out_specs=[pl.BlockSpec((ts, d), row)] + side_out,
        out_shape=[jax.ShapeDtypeStruct((n, d), F32)] + side_shapes,
        scratch_shapes=[pltpu.VMEM((SUBLANES, ts + CONV_HALO + SUBLANES, d), F32), pltpu.VMEM((ts, d), F32)],
        compiler_params=_params("arbitrary", "arbitrary"),
        name="conv_module",
    )(h, g, w1, b1, dw, bdw, lng, lnb, w2, b2, *side)
    return outs[0], [o.reshape(w.shape[1:]) for o, w in zip(outs[1:], side_weights)]


def _mla_proj_kernel(h_ref, pos_ref, g_ref, wqa_ref, qnorm_ref, wqb_ref, wkva_ref, kvnorm_ref, wkvb_ref,
                     invf_ref, sgn_ref, qt_o, k_o, vt_o, *, heads, lora):
    hn = heads * NOPE
    rows = h_ref.shape[0]
    u = _rms(h_ref[...], g_ref[...]).astype(BF16)
    qa = _rms(_dot(u, wqa_ref[...]), qnorm_ref[...]).astype(BF16)
    q = _dot(qa, wqb_ref[...])
    kva = _dot(u, wkva_ref[...])
    ckv = _rms(kva[:, :lora], kvnorm_ref[...]).astype(BF16)
    kv = _dot(ckv, wkvb_ref[...])
    ang = pos_ref[...].astype(F32) * invf_ref[...]
    c = jnp.cos(ang)
    s = jnp.sin(ang) * sgn_ref[...]
    lane = lax.broadcasted_iota(jnp.int32, (rows, LANES), 1)

    def rope(x):
        swapped = jnp.where(lane < ROPE // 2, pltpu.roll(x, LANES - ROPE // 2, 1), pltpu.roll(x, ROPE // 2, 1))
        return x * c + swapped * s

    kr = rope(kva[:, lora:lora + LANES]).astype(BF16)
    ones = jnp.ones((V_ROWS - NOPE, rows), BF16)
    for hd in range(heads):
        ns = slice(hd * NOPE, (hd + 1) * NOPE)
        qr = rope(q[:, hn + hd * LANES:hn + (hd + 1) * LANES])
        qt_o[2 * hd * NOPE:(2 * hd + 1) * NOPE, :] = q[:, ns].T.astype(BF16)
        qt_o[(2 * hd + 1) * NOPE:(2 * hd + 2) * NOPE, :] = qr.T.astype(BF16)
        k_o[:, 2 * hd * NOPE:(2 * hd + 1) * NOPE] = kv[:, ns].astype(BF16)
        k_o[:, (2 * hd + 1) * NOPE:(2 * hd + 2) * NOPE] = kr
        vt_o[hd * V_ROWS:hd * V_ROWS + NOPE, :] = kv[:, hn + hd * NOPE:hn + (hd + 1) * NOPE].T.astype(BF16)
        vt_o[hd * V_ROWS + NOPE:(hd + 1) * V_ROWS, :] = ones


def _attn_kernel(qi_ref, kj_ref, qt_ref, k_ref, vt_ref, h_ref, wot_ref, *rest, heads, tq, tk, scale, n_side):
    side_in, o_ref, side_out, (m_sc, acc_sc) = rest[:n_side], rest[n_side], rest[n_side + 1:-2], rest[-2:]
    _side_cast(side_in, side_out)
    i = qi_ref[pl.program_id(1)]
    j = kj_ref[pl.program_id(1)]
    c = scale * LOG2E

    @pl.when(j == 0)
    def _():
        m_sc[...] = jnp.full(m_sc.shape, -jnp.inf, F32)
        acc_sc[...] = jnp.zeros(acc_sc.shape, F32)

    def step(keys, queries, causal_offset):
        nk, nq_ = keys.stop - keys.start, queries.stop - queries.start
        if causal_offset is not None:
            causal = (lax.broadcasted_iota(jnp.int32, (nk, nq_), 0)
                      <= lax.broadcasted_iota(jnp.int32, (nk, nq_), 1) + causal_offset)
        es = [slice(2 * hd * NOPE, (2 * hd + 2) * NOPE) for hd in range(heads)]
        vs = [slice(hd * V_ROWS, (hd + 1) * V_ROWS) for hd in range(heads)]
        ss = [_dot(k_ref[keys, e], qt_ref[e, queries]) for e in es]
        if causal_offset is not None:
            ss = [jnp.where(causal, s, -jnp.inf) for s in ss]
        m_prev = [m_sc[hd, :, queries] for hd in range(heads)]
        m_new = [jnp.maximum(mp, jnp.max(s, axis=0, keepdims=True)) for mp, s in zip(m_prev, ss)]
        ps = [jnp.exp2((s - mn) * c).astype(BF16) for s, mn in zip(ss, m_new)]
        pv = [_dot(vt_ref[v, keys], p) for p, v in zip(ps, vs)]
        for hd in range(heads):
            alpha = jnp.exp2((m_prev[hd] - m_new[hd]) * c)
            acc_sc[vs[hd], queries] = alpha * acc_sc[vs[hd], queries] + pv[hd]
            m_sc[hd, :, queries] = m_new[hd]

    @pl.when(j < i)
    def _():
        step(slice(0, tk), slice(0, tq), None)

    @pl.when(j == i)
    def _():
        half = tq // 2
        step(slice(0, half), slice(0, half), 0)
        step(slice(0, tk), slice(half, tq), half)
        parts = [acc_sc[hd * V_ROWS:hd * V_ROWS + NOPE, :] / acc_sc[hd * V_ROWS + NOPE:hd * V_ROWS + NOPE + 1, :]
                 for hd in range(heads)]
        ot = jnp.concatenate(parts, axis=0).astype(BF16)
        o_ref[...] = h_ref[...] + _dot(wot_ref[...], ot).T


def _mla_layer(h, positions, g, wqa, qnorm, wqb, wkva, kvnorm, wkvb, wot, invf, sgn, side_weights, side_layer,
               *, batch, seq, heads, lora):
    n, d = h.shape
    hn = heads * NOPE
    tp = min(512, n)
    c2 = lambda i: (0, 0)
    row = lambda i: (i, 0)
    col = lambda i: (0, i)
    qt, k, vt = pl.pallas_call(
        functools.partial(_mla_proj_kernel, heads=heads, lora=lora),
        grid=(n // tp,),
        in_specs=[pl.BlockSpec((tp, d), row),
                  pl.BlockSpec((tp, 1), row),
                  pl.BlockSpec((1, d), c2),
                  pl.BlockSpec(wqa.shape, c2),
                  pl.BlockSpec(qnorm.shape, c2),
                  pl.BlockSpec(wqb.shape, c2),
                  pl.BlockSpec(wkva.shape, c2),
                  pl.BlockSpec(kvnorm.shape, c2),
                  pl.BlockSpec(wkvb.shape, c2),
                  pl.BlockSpec((1, LANES), c2),
                  pl.BlockSpec((1, LANES), c2)],
        out_specs=[pl.BlockSpec((2 * hn, tp), col), pl.BlockSpec((tp, 2 * hn), row),
                   pl.BlockSpec((heads * V_ROWS, tp), col)],
        out_shape=[jax.ShapeDtypeStruct((2 * hn, n), BF16), jax.ShapeDtypeStruct((n, 2 * hn), BF16),
                   jax.ShapeDtypeStruct((heads * V_ROWS, n), BF16)],
        compiler_params=_params("arbitrary"),
        name="mla_proj",
    )(h, positions, g, wqa, qnorm, wqb, wkva, kvnorm, wkvb, invf, sgn)

    tq = tk = min(512, seq)
    nq = seq // tq
    scale = float(NOPE + ROPE) ** -0.5
    pairs = [(i, j) for i in range(nq) for j in range(i + 1)]
    qi = jnp.asarray([p[0] for p in pairs], jnp.int32)
    kj = jnp.asarray([p[1] for p in pairs], jnp.int32)
    npairs = len(pairs)
    qrow = lambda b, p, qi, kj: (b * nq + qi[p], 0)
    qcol = lambda b, p, qi, kj: (0, b * nq + qi[p])
    krow = lambda b, p, qi, kj: (b * nq + kj[p], 0)
    kcol = lambda b, p, qi, kj: (0, b * nq + kj[p])
    side, side_in, side_out, side_shapes = _side_cast_plan(side_weights, side_layer, batch * npairs,
                                                           lambda b, p, qi, kj: b * npairs + p)
    grid_spec = pltpu.PrefetchScalarGridSpec(
        num_scalar_prefetch=2,
        grid=(batch, npairs),
        in_specs=[pl.BlockSpec((2 * hn, tq), qcol),
                  pl.BlockSpec((tk, 2 * hn), krow),
                  pl.BlockSpec((heads * V_ROWS, tk), kcol),
                  pl.BlockSpec((tq, d), qrow),
                  pl.BlockSpec(wot.shape, lambda b, p, qi, kj: (0, 0))] + side_in,
        out_specs=[pl.BlockSpec((tq, d), qrow)] + side_out,
        scratch_shapes=[pltpu.VMEM((heads, 1, tq), F32), pltpu.VMEM((heads * V_ROWS, tq), F32)],
    )
    outs = pl.pallas_call(
        functools.partial(_attn_kernel, heads=heads, tq=tq, tk=tk, scale=scale, n_side=len(side)),
        grid_spec=grid_spec,
        out_shape=[jax.ShapeDtypeStruct((n, d), F32)] + side_shapes,
        compiler_params=_params("arbitrary", "arbitrary"),
        name="mla_attention",
    )(qi, kj, qt, k, vt, h, wot, *side)
    return outs[0], [o.reshape(w.shape[1:]) for o, w in zip(outs[1:], side_weights)]


def _store_row_tiles(ref, value, rows):
    slabs = value.shape[1] // LANES
    for s in range(slabs):
        ref[pl.ds(s, rows, stride=slabs), :] = value[:, s * LANES:(s + 1) * LANES]


def _load_row_tile_slab(ref, s, rows, slabs):
    return ref[pl.ds(s, rows, stride=slabs), :]


def _route_kernel(h_ref, g_ref, r_ref, u_ref, info_ref, info_t_ref, cnt_ref, carry, *, tr, n_exp):
    i = pl.program_id(0)

    @pl.when(i == 0)
    def _():
        carry[...] = jnp.zeros(carry.shape, F32)

    u = _rms(h_ref[...], g_ref[...])
    u_ref[...] = u.astype(BF16)
    logits = _dot_3pass(u, r_ref[...])
    lane = lax.broadcasted_iota(jnp.int32, (tr, LANES), 1)
    lg = jnp.where(lane < n_exp, logits, -jnp.inf)
    m1 = jnp.max(lg, axis=1, keepdims=True)
    i1 = jnp.min(jnp.where(lg == m1, lane, LANES), axis=1, keepdims=True)
    lg2 = jnp.where(lane == i1, -jnp.inf, lg)
    m2 = jnp.max(lg2, axis=1, keepdims=True)
    i2 = jnp.min(jnp.where(lg2 == m2, lane, LANES), axis=1, keepdims=True)
    e = jnp.exp(m2 - m1)
    g1 = 1.0 / (1.0 + e)
    g2 = e / (1.0 + e)
    sel1 = lane == i1
    sel2 = lane == i2
    onehot = jnp.where(sel1 | sel2, 1.0, 0.0)
    rr = lax.broadcasted_iota(jnp.int32, (tr, tr), 0)
    cc = lax.broadcasted_iota(jnp.int32, (tr, tr), 1)
    tri = jnp.where(cc < rr, 1.0, 0.0).astype(BF16)
    before = _dot(tri, onehot.astype(BF16)) + carry[0:1, :]
    rank1 = jnp.sum(jnp.where(sel1, before, 0.0), axis=1, keepdims=True)
    rank2 = jnp.sum(jnp.where(sel2, before, 0.0), axis=1, keepdims=True)
    carry[...] = carry[...] + jnp.sum(onehot, axis=0, keepdims=True)
    cols = (i1.astype(F32), i2.astype(F32), g1, g2, rank1, rank2)
    info = jnp.zeros((tr, LANES), F32)
    for k, col in enumerate(cols):
        info = jnp.where(lane == k, col, info)
    info_ref[...] = info
    info_t_ref[...] = info.T[0:SUBLANES, :]
    cnt_ref[...] = carry[...]


def _route(h, g, router_padded, n_exp):
    n, d = h.shape
    tr = min(512, n)
    slabs = d // LANES
    return pl.pallas_call(
        functools.partial(_route_kernel, tr=tr, n_exp=n_exp),
        grid=(n // tr,),
        in_specs=[pl.BlockSpec((tr, d), lambda i: (i, 0)),
                  pl.BlockSpec((1, d), lambda i: (0, 0)),
                  pl.BlockSpec((d, LANES), lambda i: (0, 0))],
        out_specs=[pl.BlockSpec((tr, d), lambda i: (i, 0)),
                   pl.BlockSpec((tr, LANES), lambda i: (i, 0)),
                   pl.BlockSpec((SUBLANES, tr), lambda i: (0, i)),
                   pl.BlockSpec((8, LANES), lambda i: (i, 0))],
        out_shape=[jax.ShapeDtypeStruct((n, d), BF16),
                   jax.ShapeDtypeStruct((n, LANES), F32),
                   jax.ShapeDtypeStruct((SUBLANES, n), F32),
                   jax.ShapeDtypeStruct((n // tr * 8, LANES), F32)],
        scratch_shapes=[pltpu.VMEM((8, LANES), F32)],
        compiler_params=_params("arbitrary"),
        name="moe_route",
    )(h, g, router_padded)


def _run_pieces(length, piece):
    rem = length & (RUN_CHUNK - 1)
    full = length - rem

    def body(c, carry):
        piece(c * RUN_CHUNK, RUN_CHUNK)
        return carry

    lax.fori_loop(0, lax.shift_right_logical(full, RUN_CHUNK.bit_length() - 1), body, 0)
    bit = RUN_CHUNK // 2
    while bit:
        done = rem & (RUN_CHUNK - 2 * bit)
        pl.when((rem & bit) != 0)(functools.partial(piece, full + done, bit))
        bit //= 2


def _dispatch_kernel(pad_start_ref, pad_cnt_ref, plan_ref, info_t_ref, u_ref, x_ref, zero_sc, stage, sem, wsem,
                     *, tile, slabs, n_exp):
    i = pl.program_id(0)
    slot = i % 2

    def each_pad_copy(action):
        for e in range(n_exp):
            def piece(off, nrows):
                d = pl.multiple_of((pad_start_ref[e] + off) * slabs, slabs)
                action(pltpu.make_async_copy(zero_sc.at[pl.ds(0, nrows * slabs)],
                                             x_ref.at[pl.ds(d, nrows * slabs)], sem))

            _run_pieces(pad_cnt_ref[e], piece)

    @pl.when(pl.program_id(0) == 0)
    def _():
        zero_sc[...] = jnp.zeros(zero_sc.shape, F32)
        each_pad_copy(lambda copy: copy.start())
        each_pad_copy(lambda copy: copy.wait())

        def chunk_copy(i):
            dst = pl.multiple_of((pad_start_ref[n_exp] + i * ZERO_CHUNK) * slabs, slabs)
            return pltpu.make_async_copy(zero_sc, x_ref.at[pl.ds(dst, ZERO_CHUNK * slabs)], sem)

        def fill_tail(i, c):
            chunk_copy(i).start()
            return c

        def drain_tail(i, c):
            chunk_copy(i).wait()
            return c

        lax.fori_loop(0, pad_cnt_ref[n_exp], fill_tail, 0)
        lax.fori_loop(0, pad_cnt_ref[n_exp], drain_tail, 0)

    def each_run_copy(block, buf, action):
        for e in range(n_exp):
            base = (block * PLAN_ROWS) * n_exp + e
            dst0, length, src0 = plan_ref[base], plan_ref[base + n_exp], plan_ref[base + 2 * n_exp]

            def piece(off, nrows):
                s = pl.multiple_of((src0 + off) * slabs, slabs)
                d = pl.multiple_of((dst0 + off) * slabs, slabs)
                action(pltpu.make_async_copy(stage.at[buf, pl.ds(s, nrows * slabs)],
                                             x_ref.at[pl.ds(d, nrows * slabs)], wsem.at[buf]))

            _run_pieces(length, piece)

    info_t = info_t_ref[...]
    shift_base = (i * PLAN_ROWS + 3) * n_exp
    r_idx = lax.broadcasted_iota(jnp.int32, (TOP_K * tile, tile), 0)
    onehot = jnp.zeros(r_idx.shape, F32)
    for k in range(TOP_K):
        ls = info_t[4 + k:5 + k, :] + _select_by_expert(info_t[k:k + 1, :], plan_ref, shift_base, n_exp)
        onehot = onehot + jnp.where(r_idx == ls.astype(jnp.int32), 1.0, 0.0)
    x = _dot(onehot.astype(BF16), u_ref[...])
    _store_row_tiles(stage.at[slot], x, TOP_K * tile)
    each_run_copy(i, slot, lambda copy: copy.start())

    @pl.when(i > 0)
    def _():
        each_run_copy(i - 1, 1 - slot, lambda copy: copy.wait())

    @pl.when(i == pl.num_programs(0) - 1)
    def _():
        each_run_copy(i, slot, lambda copy: copy.wait())


def _dispatch(u, info_t, plan, pad_start, pad_cnt, sorted_rows, tile, slabs):
    n, d = u.shape
    grid_spec = pltpu.PrefetchScalarGridSpec(
        num_scalar_prefetch=3,
        grid=(n // tile,),
        in_specs=[pl.BlockSpec((SUBLANES, tile), lambda i, ps, pc, pn: (0, i)),
                  pl.BlockSpec((tile, d), lambda i, ps, pc, pn: (i, 0))],
        out_specs=pl.BlockSpec(memory_space=pl.ANY),
        scratch_shapes=[pltpu.VMEM((ZERO_CHUNK * slabs, LANES), F32),
                        pltpu.VMEM((2, TOP_K * tile * slabs, LANES), F32),
                        pltpu.SemaphoreType.DMA(()), pltpu.SemaphoreType.DMA((2,))],
    )
    return pl.pallas_call(
        functools.partial(_dispatch_kernel, tile=tile, slabs=slabs, n_exp=pad_start.shape[0] - 1),
        grid_spec=grid_spec,
        out_shape=jax.ShapeDtypeStruct((sorted_rows * slabs, LANES), F32),
        compiler_params=_params("arbitrary"),
        name="moe_dispatch",
    )(pad_start, pad_cnt, plan, info_t, u)


def _moe_kernel(te_ref, nu_ref, x_ref, wg_ref, wu_ref, wd_ref, o_ref, x_sc, acc_sc, *, tm, slabs, nj):
    t = pl.program_id(0)
    j = pl.program_id(1)

    @pl.when(t < nu_ref[0])
    def _():
        @pl.when(j == 0)
        def _():
            for s in range(slabs):
                x_sc[:, s * LANES:(s + 1) * LANES] = _load_row_tile_slab(x_ref, s, tm, slabs).astype(BF16)

        def finish(r):
            o_ref[...] = r.astype(BF16)

        _accumulate_steps(j, nj, lambda: _swiglu_step(x_sc[...], wg_ref[0], wu_ref[0], wd_ref[0]), acc_sc, finish)

    @pl.when((t >= nu_ref[0]) & (j == 0))
    def _():
        o_ref[...] = jnp.zeros(o_ref.shape, BF16)


def _moe_experts(x_sorted, tile_expert, n_used, wg, wu, wd, tm, slabs):
    d = slabs * LANES
    rows = x_sorted.shape[0] // slabs
    f = wg.shape[2]
    tf = _ff_chunk(f)
    nj = f // tf
    tmap = lambda t, j, te, nu: (jnp.minimum(t, nu[0] - 1), 0)
    jsel = lambda t, j, nu: jnp.where(t < nu[0], j, nj - 1)
    grid_spec = pltpu.PrefetchScalarGridSpec(
        num_scalar_prefetch=2,
        grid=(rows // tm, nj),
        in_specs=[pl.BlockSpec((tm * slabs, LANES), tmap),
                  pl.BlockSpec((1, d, tf), lambda t, j, te, nu: (te[t], 0, jsel(t, j, nu))),
                  pl.BlockSpec((1, d, tf), lambda t, j, te, nu: (te[t], 0, jsel(t, j, nu))),
                  pl.BlockSpec((1, tf, d), lambda t, j, te, nu: (te[t], jsel(t, j, nu), 0))],
        out_specs=pl.BlockSpec((tm, d), lambda t, j, te, nu: (t, 0)),
        scratch_shapes=[pltpu.VMEM((tm, d), BF16), pltpu.VMEM((tm, d), F32)],
    )
    return pl.pallas_call(
        functools.partial(_moe_kernel, tm=tm, slabs=slabs, nj=nj),
        grid_spec=grid_spec,
        out_shape=jax.ShapeDtypeStruct((rows, d), BF16),
        compiler_params=_params("arbitrary", "arbitrary"),
        name="moe_experts",
    )(tile_expert, n_used, x_sorted, wg, wu, wd)


def _select_by_expert(expert, plan_ref, base, n_exp):
    out = jnp.zeros(expert.shape, F32)
    for e in range(n_exp):
        out = jnp.where(expert == float(e), plan_ref[base + e].astype(F32), out)
    return out


def _combine_stage_rows(tile, n_exp):
    max_chunks = (TOP_K * tile + n_exp * (15 + COMBINE_CHUNK - 1)) // COMBINE_CHUNK
    return -(-max_chunks * COMBINE_CHUNK // 256) * 256


def _combine_kernel(plan_ref, h_ref, info_ref, gf_ref, y_ref, o_ref, stage, sem, *, tile, n_exp, final_norm):
    i = pl.program_id(0)
    slot = i % 2

    def each_chunk(block, buf, action):
        for e in range(n_exp):
            base = (block * PLAN_ROWS) * n_exp + e

            def body(c, carry):
                src = pl.multiple_of(plan_ref[base] + c * COMBINE_CHUNK, 16)
                dst = pl.multiple_of(plan_ref[base + 2 * n_exp] + c * COMBINE_CHUNK, COMBINE_CHUNK)
                action(pltpu.make_async_copy(y_ref.at[pl.ds(src, COMBINE_CHUNK)],
                                             stage.at[buf, pl.ds(dst, COMBINE_CHUNK)], sem.at[buf]))
                return carry

            lax.fori_loop(0, plan_ref[base + n_exp], body, 0)

    @pl.when(i == 0)
    def _():
        stage[...] = jnp.zeros(stage.shape, BF16)
        each_chunk(i, slot, lambda copy: copy.start())

    @pl.when(i + 1 < pl.num_programs(0))
    def _():
        each_chunk(i + 1, 1 - slot, lambda copy: copy.start())

    each_chunk(i, slot, lambda copy: copy.wait())

    info = info_ref[...]
    shift_base = (i * PLAN_ROWS + 3) * n_exp
    col = lax.broadcasted_iota(jnp.int32, (tile, stage.shape[1]), 1)
    p = jnp.zeros(col.shape, F32)
    for k in range(TOP_K):
        loc = info[:, 4 + k:5 + k] + _select_by_expert(info[:, k:k + 1], plan_ref, shift_base, n_exp)
        p = p + jnp.where(col == loc.astype(jnp.int32), info[:, 2 + k:3 + k], 0.0)
    out = h_ref[...] + _dot(p.astype(BF16), stage[slot])
    if final_norm:
        out = _rms(out, gf_ref[...])
    o_ref[...] = out


def _combine(h, info, plan, y_sorted, g_final, final_norm, n_exp):
    n, d = h.shape
    tile = n // (plan.shape[0] // (PLAN_ROWS * n_exp))
    grid_spec = pltpu.PrefetchScalarGridSpec(
        num_scalar_prefetch=1,
        grid=(n // tile,),
        in_specs=[pl.BlockSpec((tile, d), lambda i, plan: (i, 0)),
                  pl.BlockSpec((tile, LANES), lambda i, plan: (i, 0)),
                  pl.BlockSpec((1, d), lambda i, plan: (0, 0)),
                  pl.BlockSpec(memory_space=pl.ANY)],
        out_specs=pl.BlockSpec((tile, d), lambda i, plan: (i, 0)),
        scratch_shapes=[pltpu.VMEM((2, _combine_stage_rows(tile, n_exp), d), BF16), pltpu.SemaphoreType.DMA((2,))],
    )
    return pl.pallas_call(
        functools.partial(_combine_kernel, tile=tile, n_exp=n_exp, final_norm=final_norm),
        grid_spec=grid_spec,
        out_shape=jax.ShapeDtypeStruct((n, d), F32),
        compiler_params=_params("arbitrary"),
        name="moe_combine",
    )(plan, h, info, g_final, y_sorted)


def _moe_layer(h, g, router, wg, wu, wd, g_final, final_norm):
    n, d = h.shape
    n_exp = router.shape[1]
    tm = min(512, n)
    router_padded = jnp.pad(router, ((0, 0), (0, LANES - n_exp)))
    u, info, info_t, cnt = _route(h, g, router_padded, n_exp)

    tile = min(512, n)
    n_blocks = n // tile
    after = cnt.reshape(n_blocks, 8, LANES)[:, 0, :n_exp].astype(jnp.int32)
    before = jnp.concatenate([jnp.zeros((1, n_exp), jnp.int32), after[:-1]])
    counts = after[-1]
    padded = (counts + tm - 1) // tm * tm
    ends = jnp.cumsum(padded)
    starts = ends - padded
    rows = TOP_K * n + n_exp * tm
    n_used = (ends[-1] // tm).astype(jnp.int32)
    tile_ids = jnp.minimum(jnp.arange(rows // tm, dtype=jnp.int32), n_used - 1)
    tile_expert = jnp.sum(tile_ids[:, None] * tm >= ends[None, :], axis=1).astype(jnp.int32)

    run_start = starts[None, :] + before
    run_len = after - before
    fetch_start = run_start // 16 * 16
    n_chunks = jnp.where(run_len > 0, (run_start - fetch_start + run_len + COMBINE_CHUNK - 1) // COMBINE_CHUNK, 0)
    stage_off = COMBINE_CHUNK * (jnp.cumsum(n_chunks, axis=1) - n_chunks)
    plan = jnp.stack([fetch_start, n_chunks, stage_off, starts[None, :] + stage_off - fetch_start],
                     axis=1).reshape(-1).astype(jnp.int32)

    slabs = d // LANES
    assert slabs == 8
    local_off = jnp.cumsum(run_len, axis=1) - run_len
    dispatch_plan = jnp.stack([run_start, run_len, local_off, local_off - before],
                              axis=1).reshape(-1).astype(jnp.int32)
    assert tm % ZERO_CHUNK == 0
    pad_start = jnp.concatenate([starts + counts, ends[-1:]]).astype(jnp.int32)
    pad_cnt = jnp.concatenate([padded - counts, (rows - ends[-1:]) // ZERO_CHUNK]).astype(jnp.int32)
    x_sorted = _dispatch(u, info_t, dispatch_plan, pad_start, pad_cnt, rows, tile, slabs)
    y_sorted = _moe_experts(x_sorted, tile_expert, n_used.reshape(1), wg, wu, wd, tm, slabs)
    return _combine(h, info, plan, y_sorted, g_final, final_norm, n_exp)


def _mla_weights(wq_b, wkv_a, wkv_b, heads, lora):
    def padded(w):
        return jnp.concatenate([w, jnp.zeros(w.shape[:-1] + (LANES - ROPE,), w.dtype)], axis=-1)

    rq = wq_b.shape[0]
    wq = wq_b.reshape(rq, heads, NOPE + ROPE)
    wqb = jnp.concatenate([wq[..., :NOPE].reshape(rq, -1), padded(wq[..., NOPE:]).reshape(rq, -1)], axis=1)
    wkva = jnp.concatenate([wkv_a[:, :lora], padded(wkv_a[:, lora:])], axis=1)
    wkv = wkv_b.reshape(lora, heads, 2 * NOPE)
    wkvb = jnp.concatenate([wkv[..., :NOPE].reshape(lora, -1), wkv[..., NOPE:].reshape(lora, -1)], axis=1)
    return wqb.astype(BF16), wkva.astype(BF16), wkvb.astype(BF16)


def _cast_kernel(x_ref, o_ref):
    o_ref[...] = x_ref[0].astype(BF16)


def _to_bf16(w, layer):
    cols = w.shape[-1]
    w3 = w.reshape(w.shape[0], -1, cols)
    rows = w3.shape[1]
    cap = 256 if cols > 2048 else 1024
    tr = next(t for t in (1024, 512, 256, 128, 64, 32, 16, 8) if t <= cap and rows % t == 0)
    out = pl.pallas_call(
        _cast_kernel,
        grid=(rows // tr,),
        in_specs=[pl.BlockSpec((1, tr, cols), lambda i: (layer, i, 0))],
        out_specs=pl.BlockSpec((tr, cols), lambda i: (i, 0)),
        out_shape=jax.ShapeDtypeStruct((rows, cols), BF16),
        compiler_params=_params("arbitrary"),
        name="weights_to_bf16",
    )(w3)
    return out.reshape(w.shape[1:])


def _final_norm_kernel(h_ref, g_ref, o_ref):
    o_ref[...] = _rms(h_ref[...], g_ref[...])


def _final_norm(h, g):
    n, d = h.shape
    tm = min(512, n)
    return pl.pallas_call(
        _final_norm_kernel,
        grid=(n // tm,),
        in_specs=[pl.BlockSpec((tm, d), lambda i: (i, 0)), pl.BlockSpec((1, d), lambda i: (0, 0))],
        out_specs=pl.BlockSpec((tm, d), lambda i: (i, 0)),
        out_shape=jax.ShapeDtypeStruct((n, d), F32),
        compiler_params=_params("arbitrary"),
        name="final_norm",
    )(h, g)


@jax.jit
def _trunk(x, positions, norm_mix, norm_ffn, norm_final, pool_w, pool_scale,
           conv_w1, conv_b1, conv_dw, conv_bdw, conv_ln_g, conv_ln_b, conv_w2, conv_b2,
           mla_wq_a, mla_q_norm, mla_wq_b, mla_wkv_a, mla_kv_norm, mla_wkv_b, mla_wo,
           ffn_wg, ffn_wu, ffn_wd, moe_router, moe_wg, moe_wu, moe_wd):
    batch, seq, d = x.shape
    depth = norm_mix.shape[0]
    lora = mla_kv_norm.shape[1]
    heads = mla_wo.shape[1] // NOPE
    assert mla_wq_b.shape[2] == heads * (NOPE + ROPE) and mla_wkv_a.shape[2] == lora + ROPE
    n = batch * seq
    h = x.reshape(n, d)
    pos = positions.reshape(n, 1)
    row = lambda v: v.reshape(1, -1)

    half = jnp.arange(0, ROPE, 2, dtype=F32) / ROPE
    inv_freq = ROPE_THETA ** (-half)
    invf = row(jnp.concatenate([inv_freq, inv_freq, jnp.zeros((LANES - ROPE,), F32)]))
    sgn = row(jnp.concatenate([-jnp.ones((ROPE // 2,), F32), jnp.ones((ROPE // 2,), F32),
                               jnp.zeros((LANES - ROPE,), F32)]))

    moe_w = (moe_wg, moe_wu, moe_wd)
    moe_bf16 = {}
    ffn_bf16 = {}

    def moe_cast_job(i):
        nxt = i if i % 2 == 1 else i + 1
        if nxt < depth and nxt // 2 not in moe_bf16:
            return moe_w, nxt // 2
        return (), 0

    for i in range(depth):
        kind, j = i % 3, i // 3
        g = row(norm_mix[i])
        if kind == 0:
            jobs = [(w, f) for f in range(ffn_wg.shape[0]) for w in (ffn_wg, ffn_wu, ffn_wd)] if not ffn_bf16 else []
            h, cast = _pool_layer(h, g, pool_w[j].astype(BF16), row(pool_scale[j]), [w for w, _ in jobs],
                                  [f for _, f in jobs], batch=batch, seq=seq)
            for k in range(0, len(cast), 3):
                ffn_bf16[jobs[k][1]] = cast[k:k + 3]
        elif kind == 1:
            side_w, side_layer = moe_cast_job(i)
            h, cast = _conv_layer(h, g, conv_w1[j].astype(BF16), row(conv_b1[j]), conv_dw[j], row(conv_bdw[j]),
                                  row(conv_ln_g[j]), row(conv_ln_b[j]), conv_w2[j].astype(BF16), row(conv_b2[j]),
                                  side_w, side_layer, batch=batch, seq=seq)
            if cast:
                moe_bf16[side_layer] = cast
        else:
            wqb, wkva, wkvb = _mla_weights(mla_wq_b[j], mla_wkv_a[j], mla_wkv_b[j], heads, lora)
            side_w, side_layer = moe_cast_job(i)
            h, cast = _mla_layer(h, pos, g, mla_wq_a[j].astype(BF16), row(mla_q_norm[j]), wqb, wkva,
                                 row(mla_kv_norm[j]), wkvb, mla_wo[j].T.astype(BF16), invf, sgn,
                                 side_w, side_layer, batch=batch, seq=seq, heads=heads, lora=lora)
            if cast:
                moe_bf16[side_layer] = cast
        g = row(norm_ffn[i])
        f_idx = i // 2
        last = i == depth - 1
        if i % 2 == 0:
            wg, wu, wd = ffn_bf16.get(f_idx) or [_to_bf16(w, f_idx) for w in (ffn_wg, ffn_wu, ffn_wd)]
            h = _ffn_layer(h, g, wg, wu, wd)
            if last:
                h = _final_norm(h, row(norm_final))
        else:
            wg, wu, wd = moe_bf16.get(f_idx) or [_to_bf16(w, f_idx) for w in moe_w]
            h = _moe_layer(h, g, moe_router[f_idx], wg, wu, wd, row(norm_final), last)
    return h.reshape(batch, seq, d)


def kernel(x, positions, norm_mix, norm_ffn, norm_final, pool_w, pool_scale, conv_w1, conv_b1, conv_dw, conv_bdw, conv_ln_g, conv_ln_b, conv_w2, conv_b2, mla_wq_a, mla_q_norm, mla_wq_b, mla_wkv_a, mla_kv_norm, mla_wkv_b, mla_wo, ffn_wg, ffn_wu, ffn_wd, moe_router, moe_wg, moe_wu, moe_wd):
    return _trunk(x, positions, norm_mix, norm_ffn, norm_final, pool_w, pool_scale,
                  conv_w1, conv_b1, conv_dw, conv_bdw, conv_ln_g, conv_ln_b, conv_w2, conv_b2,
                  mla_wq_a, mla_q_norm, mla_wq_b, mla_wkv_a, mla_kv_norm, mla_wkv_b, mla_wo,
                  ffn_wg, ffn_wu, ffn_wd, moe_router, moe_wg, moe_wu, moe_wd)
```

```python
import functools

import jax
import jax.numpy as jnp
from jax import lax
from jax.experimental import pallas as pl
from jax.experimental.pallas import tpu as pltpu

F32 = jnp.float32
BF16 = jnp.bfloat16

NORM_EPS = 1e-6
LN_EPS = 1e-5
ROPE_THETA = 10000.0
LOG2E = 1.4426950408889634

POOL_WINDOWS = (2, 4, 8, 16)
POOL_HALO = 16
CONV_HALO = 32
LANES = 128
BF16_ROWS = 16
TOKEN_TILE = 512
CONV_TILE = 256
SUBLANES = 8
POOL_TOP = POOL_HALO + SUBLANES
NOPE = 128
ROPE = 64
V_ROWS = NOPE + 16
TOP_K = 2
ZERO_CHUNK = 64
COMBINE_CHUNK = 16
RUN_CHUNK = 64
PLAN_ROWS = 4

VMEM_LIMIT = 56 * 1024 * 1024


def _params(*sem):
    return pltpu.CompilerParams(dimension_semantics=sem, vmem_limit_bytes=VMEM_LIMIT)


def _rms(x, g, eps=NORM_EPS):
    return x * lax.rsqrt(jnp.mean(x * x, axis=-1, keepdims=True) + eps) * g


def _dot(a, b):
    return jnp.dot(a, b, preferred_element_type=F32)


def _split_bf16(x):
    hi = x.astype(BF16)
    return hi, (x - hi.astype(F32)).astype(BF16)


def _dot_3pass(a, b):
    a_hi, a_lo = _split_bf16(a)
    b_hi, b_lo = _split_bf16(b)
    return _dot(a_hi, b_hi) + (_dot(a_hi, b_lo) + _dot(a_lo, b_hi))


def _ff_chunk(f):
    for c in (1792, 1024, 512, 256, 128):
        if f % c == 0:
            return c
    return f


def _pool_kernel(h_ref, g_ref, w_ref, sc_ref, *rest, ts, gdim, n_side):
    side_in, o_ref, side_out, (ubuf, lvl_a, lvl_b) = rest[:n_side], rest[n_side], rest[n_side + 1:-3], rest[-3:]
    _side_cast(side_in, side_out)
    j = pl.program_id(1)
    h = h_ref[...]
    u = _rms(h, g_ref[...])
    ext = POOL_TOP + ts

    @pl.when(j == 0)
    def _():
        ubuf[0:POOL_TOP, :] = jnp.zeros((POOL_TOP, ubuf.shape[1]), F32)
        lvl_a[0:SUBLANES, :] = jnp.zeros((SUBLANES, gdim), F32)
        lvl_b[0:SUBLANES, :] = jnp.zeros((SUBLANES, gdim), F32)

    ubuf[POOL_TOP:ext, :] = u
    t1 = lax.broadcasted_iota(jnp.int32, (ts, 1), 0) + (j * ts + 1)
    for g, win in enumerate(POOL_WINDOWS):
        cs = slice(g * gdim, (g + 1) * gdim)
        src, cols, spare, k = ubuf, cs, [lvl_a, lvl_b], 1
        while k < win:
            s = src[SUBLANES:ext, cols] + src[SUBLANES - k:ext - k, cols]
            k *= 2
            if k < win:
                dst = spare.pop(0)
                dst[SUBLANES:ext, :] = s
                spare.append(dst)
                src, cols = dst, slice(None)
        s = s[POOL_HALO:POOL_HALO + ts]
        ug = u[:, cs]
        cnt = jnp.minimum(t1, win).astype(F32)
        pooled = s / cnt - ug
        m = _dot(pooled.astype(BF16), w_ref[g])
        o_ref[:, cs] = h[:, cs] + m * sc_ref[:, cs]
    ubuf[SUBLANES:POOL_TOP, :] = ubuf[ts + SUBLANES:ts + POOL_TOP, :]


def _pool_layer(h, g, w, scale, side_weights, side_layers, *, batch, seq):
    n, d = h.shape
    ts = min(TOKEN_TILE, seq)
    nt = seq // ts
    gdim = d // len(POOL_WINDOWS)
    row = lambda b, j: (b * nt + j, 0)
    const2 = lambda b, j: (0, 0)
    side, side_in, side_out, side_shapes = _side_cast_plan(side_weights, list(side_layers), batch * nt,
                                                           lambda b, j: b * nt + j)
    outs = pl.pallas_call(
        functools.partial(_pool_kernel, ts=ts, gdim=gdim, n_side=len(side)),
        grid=(batch, nt),
        in_specs=[pl.BlockSpec((ts, d), row),
                  pl.BlockSpec((1, d), const2),
                  pl.BlockSpec(w.shape, lambda b, j: (0, 0, 0)),
                  pl.BlockSpec((1, d), const2)] + side_in,
        out_specs=[pl.BlockSpec((ts, d), row)] + side_out,
        out_shape=[jax.ShapeDtypeStruct((n, d), F32)] + side_shapes,
        scratch_shapes=[pltpu.VMEM((ts + POOL_TOP, d), F32), pltpu.VMEM((ts + POOL_TOP, gdim), F32),
                        pltpu.VMEM((ts + POOL_TOP, gdim), F32)],
        compiler_params=_params("arbitrary", "arbitrary"),
        name="pool_mixer",
    )(h, g, w, scale, *side)
    return outs[0], [o.reshape(w.shape[1:]) for o, w in zip(outs[1:], side_weights)]


def _swiglu_step(u, wg, wu, wd):
    a = _dot(u, wg)
    b = _dot(u, wu)
    hh = a * jax.nn.sigmoid(a) * b
    return _dot(hh.astype(BF16), wd)


def _accumulate_steps(j, nj, partial, acc_sc, finish):
    def run(first, last):
        r = partial()
        if not first:
            r = acc_sc[...] + r
        if last:
            finish(r)
        else:
            acc_sc[...] = r

    if nj == 1:
        run(True, True)
        return
    pl.when(j == 0)(functools.partial(run, True, False))
    if nj > 2:
        pl.when((j > 0) & (j < nj - 1))(functools.partial(run, False, False))
    pl.when(j == nj - 1)(functools.partial(run, False, True))


def _ffn_kernel(h_ref, g_ref, wg_ref, wu_ref, wd_ref, o_ref, u_sc, acc_sc, *, nj):
    j = pl.program_id(1)

    @pl.when(j == 0)
    def _():
        u_sc[...] = _rms(h_ref[...], g_ref[...]).astype(BF16)

    def finish(r):
        o_ref[...] = h_ref[...] + r

    _accumulate_steps(j, nj, lambda: _swiglu_step(u_sc[...], wg_ref[...], wu_ref[...], wd_ref[...]), acc_sc, finish)


def _ffn_layer(h, g, wg, wu, wd):
    n, d = h.shape
    f = wg.shape[1]
    tm = min(TOKEN_TILE, n)
    tf = _ff_chunk(f)
    return pl.pallas_call(
        functools.partial(_ffn_kernel, nj=f // tf),
        grid=(n // tm, f // tf),
        in_specs=[pl.BlockSpec((tm, d), lambda i, j: (i, 0)),
                  pl.BlockSpec((1, d), lambda i, j: (0, 0)),
                  pl.BlockSpec((d, tf), lambda i, j: (0, j)),
                  pl.BlockSpec((d, tf), lambda i, j: (0, j)),
                  pl.BlockSpec((tf, d), lambda i, j: (j, 0))],
        out_specs=pl.BlockSpec((tm, d), lambda i, j: (i, 0)),
        out_shape=jax.ShapeDtypeStruct((n, d), F32),
        scratch_shapes=[pltpu.VMEM((tm, d), BF16), pltpu.VMEM((tm, d), F32)],
        compiler_params=_params("arbitrary", "arbitrary"),
        name="dense_swiglu",
    )(h, g, wg, wu, wd)


def _side_cast_plan(weights, layer, steps, step_index):
    arrays = [w.reshape(w.shape[0], -1, w.shape[-1]) for w in weights]
    if not arrays:
        return [], [], [], []
    nblk = next(n for n in range(steps, 0, -1)
                if all(a.shape[1] % n == 0 and (a.shape[1] // n) % BF16_ROWS == 0 for a in arrays))
    blk = lambda *g: jnp.minimum(step_index(*g), nblk - 1)
    layers = layer if isinstance(layer, (list, tuple)) else [layer] * len(arrays)
    in_specs, out_specs, out_shapes = [], [], []
    for a, lyr in zip(arrays, layers):
        rows, cols = a.shape[1:]
        in_specs.append(pl.BlockSpec((1, rows // nblk, cols), lambda *g, _l=lyr: (_l, blk(*g), 0)))
        out_specs.append(pl.BlockSpec((rows // nblk, cols), lambda *g: (blk(*g), 0)))
        out_shapes.append(jax.ShapeDtypeStruct((rows, cols), BF16))
    return arrays, in_specs, out_specs, out_shapes


def _side_cast(in_refs, out_refs):
    for src, dst in zip(in_refs, out_refs):
        dst[...] = src[0].astype(BF16)


def _conv_kernel(h_ref, g_ref, w1_ref, b1_ref, dw_ref, bdw_ref, lng_ref, lnb_ref, w2_ref, b2_ref, *rest,
                 ts, width, rb, n_side):
    side_in, o_ref, side_out, (gbuf, cbuf) = rest[:n_side], rest[n_side], rest[n_side + 1:-2], rest[-2:]
    _side_cast(side_in, side_out)
    j = pl.program_id(1)
    d = h_ref.shape[1]
    h = h_ref[...]
    u = _rms(h, g_ref[...]).astype(BF16)
    a = _dot(u, w1_ref[...]) + b1_ref[...]
    glu = a[:, :d] * jax.nn.sigmoid(a[:, d:])

    keep = CONV_HALO + SUBLANES

    @pl.when(j == 0)
    def _():
        gbuf[:, 0:keep, :] = jnp.zeros((SUBLANES, keep, d), F32)

    for b in range(SUBLANES):
        gbuf[b, CONV_HALO + b:CONV_HALO + b + ts, :] = glu
    for cb in range(d // LANES):
        cs = slice(cb * LANES, (cb + 1) * LANES)
        for r0 in range(0, ts, rb):
            acc = jnp.zeros((rb, LANES), F32)
            for k in range(width):
                a, b = divmod(width - 1 - k, SUBLANES)
                off = CONV_HALO + r0 - SUBLANES * a
                acc = acc + gbuf[b, off:off + rb, cs] * dw_ref[k:k + 1, cs]
            cbuf[r0:r0 + rb, cs] = acc + bdw_ref[:, cs]
    gbuf[:, 0:keep, :] = gbuf[:, ts:ts + keep, :]

    c = cbuf[...]
    mu = jnp.mean(c, axis=-1, keepdims=True)
    cc = c - mu
    var = jnp.mean(cc * cc, axis=-1, keepdims=True)
    un = cc * lax.rsqrt(var + LN_EPS) * lng_ref[...] + lnb_ref[...]
    un = un * jax.nn.sigmoid(un)
    o_ref[...] = h + _dot(un.astype(BF16), w2_ref[...]) + b2_ref[...]


def _conv_layer(h, g, w1, b1, dw, bdw, lng, lnb, w2, b2, side_weights, side_layer, *, batch, seq):
    n, d = h.shape
    width = dw.shape[0]
    assert width - 1 <= CONV_HALO
    ts = min(CONV_TILE, seq)
    rb = min(128, ts)
    nt = seq // ts
    row = lambda b, j: (b * nt + j, 0)
    c2 = lambda b, j: (0, 0)
    side, side_in, side_out, side_shapes = _side_cast_plan(side_weights, side_layer, batch * nt,
                                                           lambda b, j: b * nt + j)
    outs = pl.pallas_call(
        functools.partial(_conv_kernel, ts=ts, width=width, rb=rb, n_side=len(side)),
        grid=(batch, nt),
        in_specs=[pl.BlockSpec((ts, d), row),
                  pl.BlockSpec((1, d), c2),
                  pl.BlockSpec(w1.shape, c2),
                  pl.BlockSpec(b1.shape, c2),
                  pl.BlockSpec(dw.shape, c2),
                  pl.BlockSpec((1, d), c2),
                  pl.BlockSpec((1, d), c2),
                  pl.BlockSpec((1, d), c2),
                  pl.BlockSpec(w2.shape, c2),
                  pl.BlockSpec((1, d), c2)] + side_in,
        out_specs=[pl.BlockSpec((ts, d), row)] + side_out,
        out_shape=[jax.ShapeDtypeStruct((n, d), F32)] + side_shapes,
        scratch_shapes=[pltpu.VMEM((SUBLANES, ts + CONV_HALO + SUBLANES, d), F32), pltpu.VMEM((ts, d), F32)],
        compiler_params=_params("arbitrary", "arbitrary"),
        name="conv_module",
    )(h, g, w1, b1, dw, bdw, lng, lnb, w2, b2, *side)
    return outs[0], [o.reshape(w.shape[1:]) for o, w in zip(outs[1:], side_weights)]


def _mla_proj_kernel(h_ref, pos_ref, g_ref, wqa_ref, qnorm_ref, wqb_ref, wkva_ref, kvnorm_ref, wkvb_ref,
                     invf_ref, sgn_ref, qt_o, k_o, vt_o, *, heads, lora):
    hn = heads * NOPE
    rows = h_ref.shape[0]
    u = _rms(h_ref[...], g_ref[...]).astype(BF16)
    qa = _rms(_dot(u, wqa_ref[...]), qnorm_ref[...]).astype(BF16)
    q = _dot(qa, wqb_ref[...])
    kva = _dot(u, wkva_ref[...])
    ckv = _rms(kva[:, :lora], kvnorm_ref[...]).astype(BF16)
    kv = _dot(ckv, wkvb_ref[...])
    ang = pos_ref[...].astype(F32) * invf_ref[...]
    c = jnp.cos(ang)
    s = jnp.sin(ang) * sgn_ref[...]
    lane = lax.broadcasted_iota(jnp.int32, (rows, LANES), 1)

    def rope(x):
        swapped = jnp.where(lane < ROPE // 2, pltpu.roll(x, LANES - ROPE // 2, 1), pltpu.roll(x, ROPE // 2, 1))
        return x * c + swapped * s

    kr = rope(kva[:, lora:lora + LANES]).astype(BF16)
    ones = jnp.ones((V_ROWS - NOPE, rows), BF16)
    for hd in range(heads):
        ns = slice(hd * NOPE, (hd + 1) * NOPE)
        qr = rope(q[:, hn + hd * LANES:hn + (hd + 1) * LANES])
        qt_o[2 * hd * NOPE:(2 * hd + 1) * NOPE, :] = q[:, ns].T.astype(BF16)
        qt_o[(2 * hd + 1) * NOPE:(2 * hd + 2) * NOPE, :] = qr.T.astype(BF16)
        k_o[:, 2 * hd * NOPE:(2 * hd + 1) * NOPE] = kv[:, ns].astype(BF16)
        k_o[:, (2 * hd + 1) * NOPE:(2 * hd + 2) * NOPE] = kr
        vt_o[hd * V_ROWS:hd * V_ROWS + NOPE, :] = kv[:, hn + hd * NOPE:hn + (hd + 1) * NOPE].T.astype(BF16)
        vt_o[hd * V_ROWS + NOPE:(hd + 1) * V_ROWS, :] = ones


def _attn_kernel(qi_ref, kj_ref, qt_ref, k_ref, vt_ref, h_ref, wot_ref, *rest, heads, tq, tk, scale, n_side):
    side_in, o_ref, side_out, (m_sc, acc_sc) = rest[:n_side], rest[n_side], rest[n_side + 1:-2], rest[-2:]
    _side_cast(side_in, side_out)
    i = qi_ref[pl.program_id(1)]
    j = kj_ref[pl.program_id(1)]
    c = scale * LOG2E

    @pl.when(j == 0)
    def _():
        m_sc[...] = jnp.full(m_sc.shape, -jnp.inf, F32)
        acc_sc[...] = jnp.zeros(acc_sc.shape, F32)

    def step(diagonal):
        if diagonal:
            causal = (lax.broadcasted_iota(jnp.int32, (tk, tq), 0)
                      <= lax.broadcasted_iota(jnp.int32, (tk, tq), 1))
        es = [slice(2 * hd * NOPE, (2 * hd + 2) * NOPE) for hd in range(heads)]
        vs = [slice(hd * V_ROWS, (hd + 1) * V_ROWS) for hd in range(heads)]
        ss = [_dot(k_ref[:, e], qt_ref[e, :]) for e in es]
        if diagonal:
            ss = [jnp.where(causal, s, -jnp.inf) for s in ss]
        m_prev = [m_sc[hd] for hd in range(heads)]
        m_new = [jnp.maximum(mp, jnp.max(s, axis=0, keepdims=True)) for mp, s in zip(m_prev, ss)]
        ps = [jnp.exp2((s - mn) * c).astype(BF16) for s, mn in zip(ss, m_new)]
        pv = [_dot(vt_ref[v, :], p) for p, v in zip(ps, vs)]
        for hd in range(heads):
            alpha = jnp.exp2((m_prev[hd] - m_new[hd]) * c)
            acc_sc[vs[hd], :] = alpha * acc_sc[vs[hd], :] + pv[hd]
            m_sc[hd] = m_new[hd]

    @pl.when(j < i)
    def _():
        step(False)

    @pl.when(j == i)
    def _():
        step(True)
        parts = [acc_sc[hd * V_ROWS:hd * V_ROWS + NOPE, :] / acc_sc[hd * V_ROWS + NOPE:hd * V_ROWS + NOPE + 1, :]
                 for hd in range(heads)]
        ot = jnp.concatenate(parts, axis=0).astype(BF16)
        o_ref[...] = h_ref[...] + _dot(wot_ref[...], ot).T


def _mla_layer(h, positions, g, wqa, qnorm, wqb, wkva, kvnorm, wkvb, wot, invf, sgn, side_weights, side_layer,
               *, batch, seq, heads, lora):
    n, d = h.shape
    hn = heads * NOPE
    tp = min(TOKEN_TILE, n)
    c2 = lambda i: (0, 0)
    row = lambda i: (i, 0)
    col = lambda i: (0, i)
    qt, k, vt = pl.pallas_call(
        functools.partial(_mla_proj_kernel, heads=heads, lora=lora),
        grid=(n // tp,),
        in_specs=[pl.BlockSpec((tp, d), row),
                  pl.BlockSpec((tp, 1), row),
                  pl.BlockSpec((1, d), c2),
                  pl.BlockSpec(wqa.shape, c2),
                  pl.BlockSpec(qnorm.shape, c2),
                  pl.BlockSpec(wqb.shape, c2),
                  pl.BlockSpec(wkva.shape, c2),
                  pl.BlockSpec(kvnorm.shape, c2),
                  pl.BlockSpec(wkvb.shape, c2),
                  pl.BlockSpec((1, LANES), c2),
                  pl.BlockSpec((1, LANES), c2)],
        out_specs=[pl.BlockSpec((2 * hn, tp), col), pl.BlockSpec((tp, 2 * hn), row),
                   pl.BlockSpec((heads * V_ROWS, tp), col)],
        out_shape=[jax.ShapeDtypeStruct((2 * hn, n), BF16), jax.ShapeDtypeStruct((n, 2 * hn), BF16),
                   jax.ShapeDtypeStruct((heads * V_ROWS, n), BF16)],
        compiler_params=_params("arbitrary"),
        name="mla_proj",
    )(h, positions, g, wqa, qnorm, wqb, wkva, kvnorm, wkvb, invf, sgn)

    tq = tk = min(TOKEN_TILE, seq)
    nq = seq // tq
    scale = float(NOPE + ROPE) ** -0.5
    pairs = [(i, j) for i in range(nq) for j in range(i + 1)]
    qi = jnp.asarray([p[0] for p in pairs], jnp.int32)
    kj = jnp.asarray([p[1] for p in pairs], jnp.int32)
    npairs = len(pairs)
    qrow = lambda b, p, qi, kj: (b * nq + qi[p], 0)
    qcol = lambda b, p, qi, kj: (0, b * nq + qi[p])
    krow = lambda b, p, qi, kj: (b * nq + kj[p], 0)
    kcol = lambda b, p, qi, kj: (0, b * nq + kj[p])
    side, side_in, side_out, side_shapes = _side_cast_plan(side_weights, side_layer, batch * npairs,
                                                           lambda b, p, qi, kj: b * npairs + p)
    grid_spec = pltpu.PrefetchScalarGridSpec(
        num_scalar_prefetch=2,
        grid=(batch, npairs),
        in_specs=[pl.BlockSpec((2 * hn, tq), qcol),
                  pl.BlockSpec((tk, 2 * hn), krow),
                  pl.BlockSpec((heads * V_ROWS, tk), kcol),
                  pl.BlockSpec((tq, d), qrow),
                  pl.BlockSpec(wot.shape, lambda b, p, qi, kj: (0, 0))] + side_in,
        out_specs=[pl.BlockSpec((tq, d), qrow)] + side_out,
        scratch_shapes=[pltpu.VMEM((heads, 1, tq), F32), pltpu.VMEM((heads * V_ROWS, tq), F32)],
    )
    outs = pl.pallas_call(
        functools.partial(_attn_kernel, heads=heads, tq=tq, tk=tk, scale=scale, n_side=len(side)),
        grid_spec=grid_spec,
        out_shape=[jax.ShapeDtypeStruct((n, d), F32)] + side_shapes,
        compiler_params=_params("arbitrary", "arbitrary"),
        name="mla_attention",
    )(qi, kj, qt, k, vt, h, wot, *side)
    return outs[0], [o.reshape(w.shape[1:]) for o, w in zip(outs[1:], side_weights)]


def _store_row_tiles(ref, value, rows):
    slabs = value.shape[1] // LANES
    for s in range(slabs):
        ref[pl.ds(s, rows, stride=slabs), :] = value[:, s * LANES:(s + 1) * LANES]


def _load_row_tile_slab(ref, s, rows, slabs):
    return ref[pl.ds(s, rows, stride=slabs), :]


def _route_kernel(h_ref, g_ref, r_ref, u_ref, info_ref, info_t_ref, cnt_ref, carry, *, tr, n_exp):
    i = pl.program_id(0)

    @pl.when(i == 0)
    def _():
        carry[...] = jnp.zeros(carry.shape, F32)

    u = _rms(h_ref[...], g_ref[...])
    u_ref[...] = u.astype(BF16)
    logits = _dot_3pass(u, r_ref[...])
    lane = lax.broadcasted_iota(jnp.int32, (tr, LANES), 1)
    lg = jnp.where(lane < n_exp, logits, -jnp.inf)
    m1 = jnp.max(lg, axis=1, keepdims=True)
    i1 = jnp.min(jnp.where(lg == m1, lane, LANES), axis=1, keepdims=True)
    lg2 = jnp.where(lane == i1, -jnp.inf, lg)
    m2 = jnp.max(lg2, axis=1, keepdims=True)
    i2 = jnp.min(jnp.where(lg2 == m2, lane, LANES), axis=1, keepdims=True)
    e = jnp.exp(m2 - m1)
    g1 = 1.0 / (1.0 + e)
    g2 = e / (1.0 + e)
    sel1 = lane == i1
    sel2 = lane == i2
    onehot = jnp.where(sel1 | sel2, 1.0, 0.0)
    rr = lax.broadcasted_iota(jnp.int32, (tr, tr), 0)
    cc = lax.broadcasted_iota(jnp.int32, (tr, tr), 1)
    tri = jnp.where(cc < rr, 1.0, 0.0).astype(BF16)
    before = _dot(tri, onehot.astype(BF16)) + carry[0:1, :]
    rank1 = jnp.sum(jnp.where(sel1, before, 0.0), axis=1, keepdims=True)
    rank2 = jnp.sum(jnp.where(sel2, before, 0.0), axis=1, keepdims=True)
    carry[...] = carry[...] + jnp.sum(onehot, axis=0, keepdims=True)
    cols = (i1.astype(F32), i2.astype(F32), g1, g2, rank1, rank2)
    info = jnp.zeros((tr, LANES), F32)
    for k, col in enumerate(cols):
        info = jnp.where(lane == k, col, info)
    info_ref[...] = info
    info_t_ref[...] = info.T[0:SUBLANES, :]
    cnt_ref[...] = carry[...]


def _route(h, g, router_padded, n_exp):
    n, d = h.shape
    tr = min(TOKEN_TILE, n)
    slabs = d // LANES
    return pl.pallas_call(
        functools.partial(_route_kernel, tr=tr, n_exp=n_exp),
        grid=(n // tr,),
        in_specs=[pl.BlockSpec((tr, d), lambda i: (i, 0)),
                  pl.BlockSpec((1, d), lambda i: (0, 0)),
                  pl.BlockSpec((d, LANES), lambda i: (0, 0))],
        out_specs=[pl.BlockSpec((tr, d), lambda i: (i, 0)),
                   pl.BlockSpec((tr, LANES), lambda i: (i, 0)),
                   pl.BlockSpec((SUBLANES, tr), lambda i: (0, i)),
                   pl.BlockSpec((8, LANES), lambda i: (i, 0))],
        out_shape=[jax.ShapeDtypeStruct((n, d), BF16),
                   jax.ShapeDtypeStruct((n, LANES), F32),
                   jax.ShapeDtypeStruct((SUBLANES, n), F32),
                   jax.ShapeDtypeStruct((n // tr * 8, LANES), F32)],
        scratch_shapes=[pltpu.VMEM((8, LANES), F32)],
        compiler_params=_params("arbitrary"),
        name="moe_route",
    )(h, g, router_padded)


def _run_pieces(length, piece):
    rem = length & (RUN_CHUNK - 1)
    full = length - rem

    def body(c, carry):
        piece(c * RUN_CHUNK, RUN_CHUNK)
        return carry

    lax.fori_loop(0, lax.shift_right_logical(full, RUN_CHUNK.bit_length() - 1), body, 0)
    bit = RUN_CHUNK // 2
    while bit:
        done = rem & (RUN_CHUNK - 2 * bit)
        pl.when((rem & bit) != 0)(functools.partial(piece, full + done, bit))
        bit //= 2


def _dispatch_kernel(pad_start_ref, pad_cnt_ref, plan_ref, info_t_ref, u_ref, x_ref, zero_sc, stage, sem, wsem,
                     *, tile, slabs, n_exp):
    i = pl.program_id(0)
    slot = i % 2

    def each_pad_copy(action):
        for e in range(n_exp):
            def piece(off, nrows):
                d = pl.multiple_of((pad_start_ref[e] + off) * slabs, slabs)
                action(pltpu.make_async_copy(zero_sc.at[pl.ds(0, nrows * slabs)],
                                             x_ref.at[pl.ds(d, nrows * slabs)], sem))

            _run_pieces(pad_cnt_ref[e], piece)

    @pl.when(pl.program_id(0) == 0)
    def _():
        zero_sc[...] = jnp.zeros(zero_sc.shape, F32)
        each_pad_copy(lambda copy: copy.start())
        each_pad_copy(lambda copy: copy.wait())

        def chunk_copy(i):
            dst = pl.multiple_of((pad_start_ref[n_exp] + i * ZERO_CHUNK) * slabs, slabs)
            return pltpu.make_async_copy(zero_sc, x_ref.at[pl.ds(dst, ZERO_CHUNK * slabs)], sem)

        def fill_tail(i, c):
            chunk_copy(i).start()
            return c

        def drain_tail(i, c):
            chunk_copy(i).wait()
            return c

        lax.fori_loop(0, pad_cnt_ref[n_exp], fill_tail, 0)
        lax.fori_loop(0, pad_cnt_ref[n_exp], drain_tail, 0)

    def each_run_copy(block, buf, action):
        for e in range(n_exp):
            base = (block * PLAN_ROWS) * n_exp + e
            dst0, length, src0 = plan_ref[base], plan_ref[base + n_exp], plan_ref[base + 2 * n_exp]

            def piece(off, nrows):
                s = pl.multiple_of((src0 + off) * slabs, slabs)
                d = pl.multiple_of((dst0 + off) * slabs, slabs)
                action(pltpu.make_async_copy(stage.at[buf, pl.ds(s, nrows * slabs)],
                                             x_ref.at[pl.ds(d, nrows * slabs)], wsem.at[buf]))

            _run_pieces(length, piece)

    info_t = info_t_ref[...]
    shift_base = (i * PLAN_ROWS + 3) * n_exp
    r_idx = lax.broadcasted_iota(jnp.int32, (TOP_K * tile, tile), 0)
    onehot = jnp.zeros(r_idx.shape, F32)
    for k in range(TOP_K):
        ls = info_t[4 + k:5 + k, :] + _select_by_expert(info_t[k:k + 1, :], plan_ref, shift_base, n_exp)
        onehot = onehot + jnp.where(r_idx == ls.astype(jnp.int32), 1.0, 0.0)
    x = _dot(onehot.astype(BF16), u_ref[...])
    _store_row_tiles(stage.at[slot], x, TOP_K * tile)
    each_run_copy(i, slot, lambda copy: copy.start())

    @pl.when(i > 0)
    def _():
        each_run_copy(i - 1, 1 - slot, lambda copy: copy.wait())

    @pl.when(i == pl.num_programs(0) - 1)
    def _():
        each_run_copy(i, slot, lambda copy: copy.wait())


def _dispatch(u, info_t, plan, pad_start, pad_cnt, sorted_rows, tile, slabs):
    n, d = u.shape
    grid_spec = pltpu.PrefetchScalarGridSpec(
        num_scalar_prefetch=3,
        grid=(n // tile,),
        in_specs=[pl.BlockSpec((SUBLANES, tile), lambda i, ps, pc, pn: (0, i)),
                  pl.BlockSpec((tile, d), lambda i, ps, pc, pn: (i, 0))],
        out_specs=pl.BlockSpec(memory_space=pl.ANY),
        scratch_shapes=[pltpu.VMEM((ZERO_CHUNK * slabs, LANES), F32),
                        pltpu.VMEM((2, TOP_K * tile * slabs, LANES), F32),
                        pltpu.SemaphoreType.DMA(()), pltpu.SemaphoreType.DMA((2,))],
    )
    return pl.pallas_call(
        functools.partial(_dispatch_kernel, tile=tile, slabs=slabs, n_exp=pad_start.shape[0] - 1),
        grid_spec=grid_spec,
        out_shape=jax.ShapeDtypeStruct((sorted_rows * slabs, LANES), F32),
        compiler_params=_params("arbitrary"),
        name="moe_dispatch",
    )(pad_start, pad_cnt, plan, info_t, u)


def _moe_kernel(te_ref, nu_ref, x_ref, wg_ref, wu_ref, wd_ref, o_ref, x_sc, acc_sc, *, tm, slabs, nj):
    t = pl.program_id(0)
    j = pl.program_id(1)

    @pl.when(t < nu_ref[0])
    def _():
        @pl.when(j == 0)
        def _():
            for s in range(slabs):
                x_sc[:, s * LANES:(s + 1) * LANES] = _load_row_tile_slab(x_ref, s, tm, slabs).astype(BF16)

        def finish(r):
            o_ref[...] = r.astype(BF16)

        _accumulate_steps(j, nj, lambda: _swiglu_step(x_sc[...], wg_ref[0], wu_ref[0], wd_ref[0]), acc_sc, finish)

    @pl.when((t >= nu_ref[0]) & (j == 0))
    def _():
        o_ref[...] = jnp.zeros(o_ref.shape, BF16)


def _moe_experts(x_sorted, tile_expert, n_used, wg, wu, wd, tm, slabs):
    d = slabs * LANES
    rows = x_sorted.shape[0] // slabs
    f = wg.shape[2]
    tf = _ff_chunk(f)
    nj = f // tf
    tmap = lambda t, j, te, nu: (jnp.minimum(t, nu[0] - 1), 0)
    jsel = lambda t, j, nu: jnp.where(t < nu[0], j, nj - 1)
    grid_spec = pltpu.PrefetchScalarGridSpec(
        num_scalar_prefetch=2,
        grid=(rows // tm, nj),
        in_specs=[pl.BlockSpec((tm * slabs, LANES), tmap),
                  pl.BlockSpec((1, d, tf), lambda t, j, te, nu: (te[t], 0, jsel(t, j, nu))),
                  pl.BlockSpec((1, d, tf), lambda t, j, te, nu: (te[t], 0, jsel(t, j, nu))),
                  pl.BlockSpec((1, tf, d), lambda t, j, te, nu: (te[t], jsel(t, j, nu), 0))],
        out_specs=pl.BlockSpec((tm, d), lambda t, j, te, nu: (t, 0)),
        scratch_shapes=[pltpu.VMEM((tm, d), BF16), pltpu.VMEM((tm, d), F32)],
    )
    return pl.pallas_call(
        functools.partial(_moe_kernel, tm=tm, slabs=slabs, nj=nj),
        grid_spec=grid_spec,
        out_shape=jax.ShapeDtypeStruct((rows, d), BF16),
        compiler_params=_params("arbitrary", "arbitrary"),
        name="moe_experts",
    )(tile_expert, n_used, x_sorted, wg, wu, wd)


def _select_by_expert(expert, plan_ref, base, n_exp):
    out = jnp.zeros(expert.shape, F32)
    for e in range(n_exp):
        out = jnp.where(expert == float(e), plan_ref[base + e].astype(F32), out)
    return out


def _combine_stage_rows(tile, n_exp):
    max_chunks = (TOP_K * tile + n_exp * (BF16_ROWS - 1 + COMBINE_CHUNK - 1)) // COMBINE_CHUNK
    return -(-max_chunks * COMBINE_CHUNK // 256) * 256


def _combine_kernel(plan_ref, h_ref, info_ref, gf_ref, y_ref, o_ref, stage, sem, *, tile, n_exp, final_norm):
    i = pl.program_id(0)
    slot = i % 2

    def each_chunk(block, buf, action):
        for e in range(n_exp):
            base = (block * PLAN_ROWS) * n_exp + e

            def body(c, carry):
                src = pl.multiple_of(plan_ref[base] + c * COMBINE_CHUNK, BF16_ROWS)
                dst = pl.multiple_of(plan_ref[base + 2 * n_exp] + c * COMBINE_CHUNK, COMBINE_CHUNK)
                action(pltpu.make_async_copy(y_ref.at[pl.ds(src, COMBINE_CHUNK)],
                                             stage.at[buf, pl.ds(dst, COMBINE_CHUNK)], sem.at[buf]))
                return carry

            lax.fori_loop(0, plan_ref[base + n_exp], body, 0)

    @pl.when(i == 0)
    def _():
        stage[...] = jnp.zeros(stage.shape, BF16)
        each_chunk(i, slot, lambda copy: copy.start())

    @pl.when(i + 1 < pl.num_programs(0))
    def _():
        each_chunk(i + 1, 1 - slot, lambda copy: copy.start())

    each_chunk(i, slot, lambda copy: copy.wait())

    info = info_ref[...]
    shift_base = (i * PLAN_ROWS + 3) * n_exp
    col = lax.broadcasted_iota(jnp.int32, (tile, stage.shape[1]), 1)
    p = jnp.zeros(col.shape, F32)
    for k in range(TOP_K):
        loc = info[:, 4 + k:5 + k] + _select_by_expert(info[:, k:k + 1], plan_ref, shift_base, n_exp)
        p = p + jnp.where(col == loc.astype(jnp.int32), info[:, 2 + k:3 + k], 0.0)
    out = h_ref[...] + _dot(p.astype(BF16), stage[slot])
    if final_norm:
        out = _rms(out, gf_ref[...])
    o_ref[...] = out


def _combine(h, info, plan, y_sorted, g_final, final_norm, n_exp):
    n, d = h.shape
    tile = n // (plan.shape[0] // (PLAN_ROWS * n_exp))
    grid_spec = pltpu.PrefetchScalarGridSpec(
        num_scalar_prefetch=1,
        grid=(n // tile,),
        in_specs=[pl.BlockSpec((tile, d), lambda i, plan: (i, 0)),
                  pl.BlockSpec((tile, LANES), lambda i, plan: (i, 0)),
                  pl.BlockSpec((1, d), lambda i, plan: (0, 0)),
                  pl.BlockSpec(memory_space=pl.ANY)],
        out_specs=pl.BlockSpec((tile, d), lambda i, plan: (i, 0)),
        scratch_shapes=[pltpu.VMEM((2, _combine_stage_rows(tile, n_exp), d), BF16), pltpu.SemaphoreType.DMA((2,))],
    )
    return pl.pallas_call(
        functools.partial(_combine_kernel, tile=tile, n_exp=n_exp, final_norm=final_norm),
        grid_spec=grid_spec,
        out_shape=jax.ShapeDtypeStruct((n, d), F32),
        compiler_params=_params("arbitrary"),
        name="moe_combine",
    )(plan, h, info, g_final, y_sorted)


def _moe_layer(h, g, router, wg, wu, wd, g_final, final_norm):
    n, d = h.shape
    n_exp = router.shape[1]
    tm = min(TOKEN_TILE, n)
    router_padded = jnp.pad(router, ((0, 0), (0, LANES - n_exp)))
    u, info, info_t, cnt = _route(h, g, router_padded, n_exp)

    tile = min(TOKEN_TILE, n)
    n_blocks = n // tile
    after = cnt.reshape(n_blocks, 8, LANES)[:, 0, :n_exp].astype(jnp.int32)
    before = jnp.concatenate([jnp.zeros((1, n_exp), jnp.int32), after[:-1]])
    counts = after[-1]
    padded = (counts + tm - 1) // tm * tm
    ends = jnp.cumsum(padded)
    starts = ends - padded
    rows = TOP_K * n + n_exp * tm
    n_used = (ends[-1] // tm).astype(jnp.int32)
    tile_ids = jnp.minimum(jnp.arange(rows // tm, dtype=jnp.int32), n_used - 1)
    tile_expert = jnp.sum(tile_ids[:, None] * tm >= ends[None, :], axis=1).astype(jnp.int32)

    run_start = starts[None, :] + before
    run_len = after - before
    fetch_start = run_start // BF16_ROWS * BF16_ROWS
    n_chunks = jnp.where(run_len > 0, (run_start - fetch_start + run_len + COMBINE_CHUNK - 1) // COMBINE_CHUNK, 0)
    stage_off = COMBINE_CHUNK * (jnp.cumsum(n_chunks, axis=1) - n_chunks)
    plan = jnp.stack([fetch_start, n_chunks, stage_off, starts[None, :] + stage_off - fetch_start],
                     axis=1).reshape(-1).astype(jnp.int32)

    slabs = d // LANES
    assert slabs == 8
    local_off = jnp.cumsum(run_len, axis=1) - run_len
    dispatch_plan = jnp.stack([run_start, run_len, local_off, local_off - before],
                              axis=1).reshape(-1).astype(jnp.int32)
    assert tm % ZERO_CHUNK == 0
    pad_start = jnp.concatenate([starts + counts, ends[-1:]]).astype(jnp.int32)
    pad_cnt = jnp.concatenate([padded - counts, (rows - ends[-1:]) // ZERO_CHUNK]).astype(jnp.int32)
    x_sorted = _dispatch(u, info_t, dispatch_plan, pad_start, pad_cnt, rows, tile, slabs)
    y_sorted = _moe_experts(x_sorted, tile_expert, n_used.reshape(1), wg, wu, wd, tm, slabs)
    return _combine(h, info, plan, y_sorted, g_final, final_norm, n_exp)


def _mla_weights(wq_b, wkv_a, wkv_b, heads, lora):
    def padded(w):
        return jnp.concatenate([w, jnp.zeros(w.shape[:-1] + (LANES - ROPE,), w.dtype)], axis=-1)

    rq = wq_b.shape[0]
    wq = wq_b.reshape(rq, heads, NOPE + ROPE)
    wqb = jnp.concatenate([wq[..., :NOPE].reshape(rq, -1), padded(wq[..., NOPE:]).reshape(rq, -1)], axis=1)
    wkva = jnp.concatenate([wkv_a[:, :lora], padded(wkv_a[:, lora:])], axis=1)
    wkv = wkv_b.reshape(lora, heads, 2 * NOPE)
    wkvb = jnp.concatenate([wkv[..., :NOPE].reshape(lora, -1), wkv[..., NOPE:].reshape(lora, -1)], axis=1)
    return wqb.astype(BF16), wkva.astype(BF16), wkvb.astype(BF16)


def _cast_kernel(x_ref, o_ref):
    o_ref[...] = x_ref[0].astype(BF16)


def _to_bf16(w, layer):
    cols = w.shape[-1]
    w3 = w.reshape(w.shape[0], -1, cols)
    rows = w3.shape[1]
    cap = 256 if cols > 2048 else 1024
    tr = next(t for t in (1024, 512, 256, 128, 64, 32, 16, 8) if t <= cap and rows % t == 0)
    out = pl.pallas_call(
        _cast_kernel,
        grid=(rows // tr,),
        in_specs=[pl.BlockSpec((1, tr, cols), lambda i: (layer, i, 0))],
        out_specs=pl.BlockSpec((tr, cols), lambda i: (i, 0)),
        out_shape=jax.ShapeDtypeStruct((rows, cols), BF16),
        compiler_params=_params("arbitrary"),
        name="weights_to_bf16",
    )(w3)
    return out.reshape(w.shape[1:])


def _final_norm_kernel(h_ref, g_ref, o_ref):
    o_ref[...] = _rms(h_ref[...], g_ref[...])


def _final_norm(h, g):
    n, d = h.shape
    tm = min(TOKEN_TILE, n)
    return pl.pallas_call(
        _final_norm_kernel,
        grid=(n // tm,),
        in_specs=[pl.BlockSpec((tm, d), lambda i: (i, 0)), pl.BlockSpec((1, d), lambda i: (0, 0))],
        out_specs=pl.BlockSpec((tm, d), lambda i: (i, 0)),
        out_shape=jax.ShapeDtypeStruct((n, d), F32),
        compiler_params=_params("arbitrary"),
        name="final_norm",
    )(h, g)


@jax.jit
def _trunk(x, positions, norm_mix, norm_ffn, norm_final, pool_w, pool_scale,
           conv_w1, conv_b1, conv_dw, conv_bdw, conv_ln_g, conv_ln_b, conv_w2, conv_b2,
           mla_wq_a, mla_q_norm, mla_wq_b, mla_wkv_a, mla_kv_norm, mla_wkv_b, mla_wo,
           ffn_wg, ffn_wu, ffn_wd, moe_router, moe_wg, moe_wu, moe_wd):
    batch, seq, d = x.shape
    depth = norm_mix.shape[0]
    lora = mla_kv_norm.shape[1]
    heads = mla_wo.shape[1] // NOPE
    assert mla_wq_b.shape[2] == heads * (NOPE + ROPE) and mla_wkv_a.shape[2] == lora + ROPE
    n = batch * seq
    h = x.reshape(n, d)
    pos = positions.reshape(n, 1)
    row = lambda v: v.reshape(1, -1)

    half = jnp.arange(0, ROPE, 2, dtype=F32) / ROPE
    inv_freq = ROPE_THETA ** (-half)
    invf = row(jnp.concatenate([inv_freq, inv_freq, jnp.zeros((LANES - ROPE,), F32)]))
    sgn = row(jnp.concatenate([-jnp.ones((ROPE // 2,), F32), jnp.ones((ROPE // 2,), F32),
                               jnp.zeros((LANES - ROPE,), F32)]))

    moe_w = (moe_wg, moe_wu, moe_wd)
    moe_bf16 = {}
    ffn_bf16 = {}

    def moe_cast_job(i):
        nxt = i if i % 2 == 1 else i + 1
        if nxt < depth and nxt // 2 not in moe_bf16:
            return moe_w, nxt // 2
        return (), 0

    for i in range(depth):
        kind, j = i % 3, i // 3
        g = row(norm_mix[i])
        if kind == 0:
            jobs = [(w, f) for f in range(ffn_wg.shape[0]) for w in (ffn_wg, ffn_wu, ffn_wd)] if not ffn_bf16 else []
            h, cast = _pool_layer(h, g, pool_w[j].astype(BF16), row(pool_scale[j]), [w for w, _ in jobs],
                                  [f for _, f in jobs], batch=batch, seq=seq)
            for k in range(0, len(cast), 3):
                ffn_bf16[jobs[k][1]] = cast[k:k + 3]
        elif kind == 1:
            side_w, side_layer = moe_cast_job(i)
            h, cast = _conv_layer(h, g, conv_w1[j].astype(BF16), row(conv_b1[j]), conv_dw[j], row(conv_bdw[j]),
                                  row(conv_ln_g[j]), row(conv_ln_b[j]), conv_w2[j].astype(BF16), row(conv_b2[j]),
                                  side_w, side_layer, batch=batch, seq=seq)
            if cast:
                moe_bf16[side_layer] = cast
        else:
            wqb, wkva, wkvb = _mla_weights(mla_wq_b[j], mla_wkv_a[j], mla_wkv_b[j], heads, lora)
            side_w, side_layer = moe_cast_job(i)
            h, cast = _mla_layer(h, pos, g, mla_wq_a[j].astype(BF16), row(mla_q_norm[j]), wqb, wkva,
                                 row(mla_kv_norm[j]), wkvb, mla_wo[j].T.astype(BF16), invf, sgn,
                                 side_w, side_layer, batch=batch, seq=seq, heads=heads, lora=lora)
            if cast:
                moe_bf16[side_layer] = cast
        g = row(norm_ffn[i])
        f_idx = i // 2
        last = i == depth - 1
        if i % 2 == 0:
            wg, wu, wd = ffn_bf16.get(f_idx) or [_to_bf16(w, f_idx) for w in (ffn_wg, ffn_wu, ffn_wd)]
            h = _ffn_layer(h, g, wg, wu, wd)
            if last:
                h = _final_norm(h, row(norm_final))
        else:
            wg, wu, wd = moe_bf16.get(f_idx) or [_to_bf16(w, f_idx) for w in moe_w]
            h = _moe_layer(h, g, moe_router[f_idx], wg, wu, wd, row(norm_final), last)
    return h.reshape(batch, seq, d)


def kernel(x, positions, norm_mix, norm_ffn, norm_final, pool_w, pool_scale, conv_w1, conv_b1, conv_dw, conv_bdw, conv_ln_g, conv_ln_b, conv_w2, conv_b2, mla_wq_a, mla_q_norm, mla_wq_b, mla_wkv_a, mla_kv_norm, mla_wkv_b, mla_wo, ffn_wg, ffn_wu, ffn_wd, moe_router, moe_wg, moe_wu, moe_wd):
    return _trunk(x, positions, norm_mix, norm_ffn, norm_final, pool_w, pool_scale,
                  conv_w1, conv_b1, conv_dw, conv_bdw, conv_ln_g, conv_ln_b, conv_w2, conv_b2,
                  mla_wq_a, mla_q_norm, mla_wq_b, mla_wkv_a, mla_kv_norm, mla_wkv_b, mla_wo,
                  ffn_wg, ffn_wu, ffn_wd, moe_router, moe_wg, moe_wu, moe_wd)
```

```python
import functools

import jax
import jax.numpy as jnp
from jax import lax
from jax.experimental import pallas as pl
from jax.experimental.pallas import tpu as pltpu

F32 = jnp.float32
BF16 = jnp.bfloat16

NORM_EPS = 1e-6
LN_EPS = 1e-5
ROPE_THETA = 10000.0
LOG2E = 1.4426950408889634

POOL_WINDOWS = (2, 4, 8, 16)
POOL_HALO = 16
CONV_HALO = 32
LANES = 128
BF16_ROWS = 16
TOKEN_TILE = 512
CONV_TILE = 256
SUBLANES = 8
POOL_TOP = POOL_HALO + SUBLANES
NOPE = 128
ROPE = 64
V_ROWS = NOPE + 16
TOP_K = 2
ZERO_CHUNK = 64
COMBINE_CHUNK = 32
RUN_CHUNK = 64
PLAN_ROWS = 4

VMEM_LIMIT = 56 * 1024 * 1024


def _params(*sem):
    return pltpu.CompilerParams(dimension_semantics=sem, vmem_limit_bytes=VMEM_LIMIT)


def _rms(x, g, eps=NORM_EPS):
    return x * lax.rsqrt(jnp.mean(x * x, axis=-1, keepdims=True) + eps) * g


def _dot(a, b):
    return jnp.dot(a, b, preferred_element_type=F32)


def _split_bf16(x):
    hi = x.astype(BF16)
    return hi, (x - hi.astype(F32)).astype(BF16)


def _dot_3pass(a, b):
    a_hi, a_lo = _split_bf16(a)
    b_hi, b_lo = _split_bf16(b)
    return _dot(a_hi, b_hi) + (_dot(a_hi, b_lo) + _dot(a_lo, b_hi))


def _ff_chunk(f):
    for c in (1792, 1024, 512, 256, 128):
        if f % c == 0:
            return c
    return f


def _pool_kernel(h_ref, g_ref, w_ref, sc_ref, *rest, ts, gdim, n_side):
    side_in, o_ref, side_out, (ubuf, lvl_a, lvl_b) = rest[:n_side], rest[n_side], rest[n_side + 1:-3], rest[-3:]
    _side_cast(side_in, side_out)
    j = pl.program_id(1)
    h = h_ref[...]
    u = _rms(h, g_ref[...])
    ext = POOL_TOP + ts

    @pl.when(j == 0)
    def _():
        ubuf[0:POOL_TOP, :] = jnp.zeros((POOL_TOP, ubuf.shape[1]), F32)
        lvl_a[0:SUBLANES, :] = jnp.zeros((SUBLANES, gdim), F32)
        lvl_b[0:SUBLANES, :] = jnp.zeros((SUBLANES, gdim), F32)

    ubuf[POOL_TOP:ext, :] = u
    t1 = lax.broadcasted_iota(jnp.int32, (ts, 1), 0) + (j * ts + 1)
    for g, win in enumerate(POOL_WINDOWS):
        cs = slice(g * gdim, (g + 1) * gdim)
        src, cols, spare, k = ubuf, cs, [lvl_a, lvl_b], 1
        while k < win:
            s = src[SUBLANES:ext, cols] + src[SUBLANES - k:ext - k, cols]
            k *= 2
            if k < win:
                dst = spare.pop(0)
                dst[SUBLANES:ext, :] = s
                spare.append(dst)
                src, cols = dst, slice(None)
        s = s[POOL_HALO:POOL_HALO + ts]
        ug = u[:, cs]
        cnt = jnp.minimum(t1, win).astype(F32)
        pooled = s / cnt - ug
        m = _dot(pooled.astype(BF16), w_ref[g])
        o_ref[:, cs] = h[:, cs] + m * sc_ref[:, cs]
    ubuf[SUBLANES:POOL_TOP, :] = ubuf[ts + SUBLANES:ts + POOL_TOP, :]


def _pool_layer(h, g, w, scale, side_weights, side_layers, *, batch, seq):
    n, d = h.shape
    ts = min(TOKEN_TILE, seq)
    nt = seq // ts
    gdim = d // len(POOL_WINDOWS)
    row = lambda b, j: (b * nt + j, 0)
    const2 = lambda b, j: (0, 0)
    side, side_in, side_out, side_shapes = _side_cast_plan(side_weights, list(side_layers), batch * nt,
                                                           lambda b, j: b * nt + j)
    outs = pl.pallas_call(
        functools.partial(_pool_kernel, ts=ts, gdim=gdim, n_side=len(side)),
        grid=(batch, nt),
        in_specs=[pl.BlockSpec((ts, d), row),
                  pl.BlockSpec((1, d), const2),
                  pl.BlockSpec(w.shape, lambda b, j: (0, 0, 0)),
                  pl.BlockSpec((1, d), const2)] + side_in,
        out_specs=[pl.BlockSpec((ts, d), row)] + side_out,
        out_shape=[jax.ShapeDtypeStruct((n, d), F32)] + side_shapes,
        scratch_shapes=[pltpu.VMEM((ts + POOL_TOP, d), F32), pltpu.VMEM((ts + POOL_TOP, gdim), F32),
                        pltpu.VMEM((ts + POOL_TOP, gdim), F32)],
        compiler_params=_params("arbitrary", "arbitrary"),
        name="pool_mixer",
    )(h, g, w, scale, *side)
    return outs[0], [o.reshape(w.shape[1:]) for o, w in zip(outs[1:], side_weights)]


def _swiglu_step(u, wg, wu, wd):
    a = _dot(u, wg)
    b = _dot(u, wu)
    hh = a * jax.nn.sigmoid(a) * b
    return _dot(hh.astype(BF16), wd)


def _accumulate_steps(j, nj, partial, acc_sc, finish):
    def run(first, last):
        r = partial()
        if not first:
            r = acc_sc[...] + r
        if last:
            finish(r)
        else:
            acc_sc[...] = r

    if nj == 1:
        run(True, True)
        return
    pl.when(j == 0)(functools.partial(run, True, False))
    if nj > 2:
        pl.when((j > 0) & (j < nj - 1))(functools.partial(run, False, False))
    pl.when(j == nj - 1)(functools.partial(run, False, True))


def _ffn_kernel(h_ref, g_ref, wg_ref, wu_ref, wd_ref, o_ref, u_sc, acc_sc, *, nj):
    j = pl.program_id(1)

    @pl.when(j == 0)
    def _():
        u_sc[...] = _rms(h_ref[...], g_ref[...]).astype(BF16)

    def finish(r):
        o_ref[...] = h_ref[...] + r

    _accumulate_steps(j, nj, lambda: _swiglu_step(u_sc[...], wg_ref[...], wu_ref[...], wd_ref[...]), acc_sc, finish)


def _ffn_layer(h, g, wg, wu, wd):
    n, d = h.shape
    f = wg.shape[1]
    tm = min(TOKEN_TILE, n)
    tf = _ff_chunk(f)
    return pl.pallas_call(
        functools.partial(_ffn_kernel, nj=f // tf),
        grid=(n // tm, f // tf),
        in_specs=[pl.BlockSpec((tm, d), lambda i, j: (i, 0)),
                  pl.BlockSpec((1, d), lambda i, j: (0, 0)),
                  pl.BlockSpec((d, tf), lambda i, j: (0, j)),
                  pl.BlockSpec((d, tf), lambda i, j: (0, j)),
                  pl.BlockSpec((tf, d), lambda i, j: (j, 0))],
        out_specs=pl.BlockSpec((tm, d), lambda i, j: (i, 0)),
        out_shape=jax.ShapeDtypeStruct((n, d), F32),
        scratch_shapes=[pltpu.VMEM((tm, d), BF16), pltpu.VMEM((tm, d), F32)],
        compiler_params=_params("arbitrary", "arbitrary"),
        name="dense_swiglu",
    )(h, g, wg, wu, wd)


def _side_cast_plan(weights, layer, steps, step_index):
    arrays = [w.reshape(w.shape[0], -1, w.shape[-1]) for w in weights]
    if not arrays:
        return [], [], [], []
    nblk = next(n for n in range(steps, 0, -1)
                if all(a.shape[1] % n == 0 and (a.shape[1] // n) % BF16_ROWS == 0 for a in arrays))
    blk = lambda *g: jnp.minimum(step_index(*g), nblk - 1)
    layers = layer if isinstance(layer, (list, tuple)) else [layer] * len(arrays)
    in_specs, out_specs, out_shapes = [], [], []
    for a, lyr in zip(arrays, layers):
        rows, cols = a.shape[1:]
        in_specs.append(pl.BlockSpec((1, rows // nblk, cols), lambda *g, _l=lyr: (_l, blk(*g), 0)))
        out_specs.append(pl.BlockSpec((rows // nblk, cols), lambda *g: (blk(*g), 0)))
        out_shapes.append(jax.ShapeDtypeStruct((rows, cols), BF16))
    return arrays, in_specs, out_specs, out_shapes


def _side_cast(in_refs, out_refs):
    for src, dst in zip(in_refs, out_refs):
        dst[...] = src[0].astype(BF16)


def _conv_kernel(h_ref, g_ref, w1_ref, b1_ref, dw_ref, bdw_ref, lng_ref, lnb_ref, w2_ref, b2_ref, *rest,
                 ts, width, rb, n_side):
    side_in, o_ref, side_out, (gbuf, cbuf) = rest[:n_side], rest[n_side], rest[n_side + 1:-2], rest[-2:]
    _side_cast(side_in, side_out)
    j = pl.program_id(1)
    d = h_ref.shape[1]
    h = h_ref[...]
    u = _rms(h, g_ref[...]).astype(BF16)
    a = _dot(u, w1_ref[...]) + b1_ref[...]
    glu = a[:, :d] * jax.nn.sigmoid(a[:, d:])

    keep = CONV_HALO + SUBLANES

    @pl.when(j == 0)
    def _():
        gbuf[:, 0:keep, :] = jnp.zeros((SUBLANES, keep, d), F32)

    for b in range(SUBLANES):
        gbuf[b, CONV_HALO + b:CONV_HALO + b + ts, :] = glu
    for cb in range(d // LANES):
        cs = slice(cb * LANES, (cb + 1) * LANES)
        for r0 in range(0, ts, rb):
            acc = jnp.zeros((rb, LANES), F32)
            for k in range(width):
                a, b = divmod(width - 1 - k, SUBLANES)
                off = CONV_HALO + r0 - SUBLANES * a
                acc = acc + gbuf[b, off:off + rb, cs] * dw_ref[k:k + 1, cs]
            cbuf[r0:r0 + rb, cs] = acc + bdw_ref[:, cs]
    gbuf[:, 0:keep, :] = gbuf[:, ts:ts + keep, :]

    c = cbuf[...]
    mu = jnp.mean(c, axis=-1, keepdims=True)
    cc = c - mu
    var = jnp.mean(cc * cc, axis=-1, keepdims=True)
    un = cc * lax.rsqrt(var + LN_EPS) * lng_ref[...] + lnb_ref[...]
    un = un * jax.nn.sigmoid(un)
    o_ref[...] = h + _dot(un.astype(BF16), w2_ref[...]) + b2_ref[...]


def _conv_layer(h, g, w1, b1, dw, bdw, lng, lnb, w2, b2, side_weights, side_layer, *, batch, seq):
    n, d = h.shape
    width = dw.shape[0]
    assert width - 1 <= CONV_HALO
    ts = min(CONV_TILE, seq)
    rb = min(128, ts)
    nt = seq // ts
    row = lambda b, j: (b * nt + j, 0)
    c2 = lambda b, j: (0, 0)
    side, side_in, side_out, side_shapes = _side_cast_plan(side_weights, side_layer, batch * nt,
                                                           lambda b, j: b * nt + j)
    outs = pl.pallas_call(
        functools.partial(_conv_kernel, ts=ts, width=width, rb=rb, n_side=len(side)),
        grid=(batch, nt),
        in_specs=[pl.BlockSpec((ts, d), row),
                  pl.BlockSpec((1, d), c2),
                  pl.BlockSpec(w1.shape, c2),
                  pl.BlockSpec(b1.shape, c2),
                  pl.BlockSpec(dw.shape, c2),
                  pl.BlockSpec((1, d), c2),
                  pl.BlockSpec((1, d), c2),
                  pl.BlockSpec((1, d), c2),
                  pl.BlockSpec(w2.shape, c2),
                  pl.BlockSpec((1, d), c2)] + side_in,
        out_specs=[pl.BlockSpec((ts, d), row)] + side_out,
        out_shape=[jax.ShapeDtypeStruct((n, d), F32)] + side_shapes,
        scratch_shapes=[pltpu.VMEM((SUBLANES, ts + CONV_HALO + SUBLANES, d), F32), pltpu.VMEM((ts, d), F32)],
        compiler_params=_params("arbitrary", "arbitrary"),
        name="conv_module",
    )(h, g, w1, b1, dw, bdw, lng, lnb, w2, b2, *side)
    return outs[0], [o.reshape(w.shape[1:]) for o, w in zip(outs[1:], side_weights)]


def _mla_proj_kernel(h_ref, pos_ref, g_ref, wqa_ref, qnorm_ref, wqb_ref, wkva_ref, kvnorm_ref, wkvb_ref,
                     invf_ref, sgn_ref, qt_o, k_o, vt_o, *, heads, lora):
    hn = heads * NOPE
    rows = h_ref.shape[0]
    u = _rms(h_ref[...], g_ref[...]).astype(BF16)
    qa = _rms(_dot(u, wqa_ref[...]), qnorm_ref[...]).astype(BF16)
    q = _dot(qa, wqb_ref[...])
    kva = _dot(u, wkva_ref[...])
    ckv = _rms(kva[:, :lora], kvnorm_ref[...]).astype(BF16)
    kv = _dot(ckv, wkvb_ref[...])
    ang = pos_ref[...].astype(F32) * invf_ref[...]
    c = jnp.cos(ang)
    s = jnp.sin(ang) * sgn_ref[...]
    lane = lax.broadcasted_iota(jnp.int32, (rows, LANES), 1)

    def rope(x):
        swapped = jnp.where(lane < ROPE // 2, pltpu.roll(x, LANES - ROPE // 2, 1), pltpu.roll(x, ROPE // 2, 1))
        return x * c + swapped * s

    kr = rope(kva[:, lora:lora + LANES]).astype(BF16)
    ones = jnp.ones((V_ROWS - NOPE, rows), BF16)
    for hd in range(heads):
        ns = slice(hd * NOPE, (hd + 1) * NOPE)
        qr = rope(q[:, hn + hd * LANES:hn + (hd + 1) * LANES])
        qt_o[2 * hd * NOPE:(2 * hd + 1) * NOPE, :] = q[:, ns].T.astype(BF16)
        qt_o[(2 * hd + 1) * NOPE:(2 * hd + 2) * NOPE, :] = qr.T.astype(BF16)
        k_o[:, 2 * hd * NOPE:(2 * hd + 1) * NOPE] = kv[:, ns].astype(BF16)
        k_o[:, (2 * hd + 1) * NOPE:(2 * hd + 2) * NOPE] = kr
        vt_o[hd * V_ROWS:hd * V_ROWS + NOPE, :] = kv[:, hn + hd * NOPE:hn + (hd + 1) * NOPE].T.astype(BF16)
        vt_o[hd * V_ROWS + NOPE:(hd + 1) * V_ROWS, :] = ones


def _attn_kernel(qi_ref, kj_ref, qt_ref, k_ref, vt_ref, h_ref, wot_ref, *rest, heads, tq, tk, scale, n_side):
    side_in, o_ref, side_out, (m_sc, acc_sc) = rest[:n_side], rest[n_side], rest[n_side + 1:-2], rest[-2:]
    _side_cast(side_in, side_out)
    i = qi_ref[pl.program_id(1)]
    j = kj_ref[pl.program_id(1)]
    c = scale * LOG2E

    @pl.when(j == 0)
    def _():
        m_sc[...] = jnp.full(m_sc.shape, -jnp.inf, F32)
        acc_sc[...] = jnp.zeros(acc_sc.shape, F32)

    def step(diagonal):
        if diagonal:
            causal = (lax.broadcasted_iota(jnp.int32, (tk, tq), 0)
                      <= lax.broadcasted_iota(jnp.int32, (tk, tq), 1))
        es = [slice(2 * hd * NOPE, (2 * hd + 2) * NOPE) for hd in range(heads)]
        vs = [slice(hd * V_ROWS, (hd + 1) * V_ROWS) for hd in range(heads)]
        ss = [_dot(k_ref[:, e], qt_ref[e, :]) for e in es]
        if diagonal:
            ss = [jnp.where(causal, s, -jnp.inf) for s in ss]
        m_prev = [m_sc[hd] for hd in range(heads)]
        m_new = [jnp.maximum(mp, jnp.max(s, axis=0, keepdims=True)) for mp, s in zip(m_prev, ss)]
        ps = [jnp.exp2((s - mn) * c).astype(BF16) for s, mn in zip(ss, m_new)]
        pv = [_dot(vt_ref[v, :], p) for p, v in zip(ps, vs)]
        for hd in range(heads):
            alpha = jnp.exp2((m_prev[hd] - m_new[hd]) * c)
            acc_sc[vs[hd], :] = alpha * acc_sc[vs[hd], :] + pv[hd]
            m_sc[hd] = m_new[hd]

    @pl.when(j < i)
    def _():
        step(False)

    @pl.when(j == i)
    def _():
        step(True)
        parts = [acc_sc[hd * V_ROWS:hd * V_ROWS + NOPE, :] / acc_sc[hd * V_ROWS + NOPE:hd * V_ROWS + NOPE + 1, :]
                 for hd in range(heads)]
        ot = jnp.concatenate(parts, axis=0).astype(BF16)
        o_ref[...] = h_ref[...] + _dot(wot_ref[...], ot).T


def _mla_layer(h, positions, g, wqa, qnorm, wqb, wkva, kvnorm, wkvb, wot, invf, sgn, side_weights, side_layer,
               *, batch, seq, heads, lora):
    n, d = h.shape
    hn = heads * NOPE
    tp = min(TOKEN_TILE, n)
    c2 = lambda i: (0, 0)
    row = lambda i: (i, 0)
    col = lambda i: (0, i)
    qt, k, vt = pl.pallas_call(
        functools.partial(_mla_proj_kernel, heads=heads, lora=lora),
        grid=(n // tp,),
        in_specs=[pl.BlockSpec((tp, d), row),
                  pl.BlockSpec((tp, 1), row),
                  pl.BlockSpec((1, d), c2),
                  pl.BlockSpec(wqa.shape, c2),
                  pl.BlockSpec(qnorm.shape, c2),
                  pl.BlockSpec(wqb.shape, c2),
                  pl.BlockSpec(wkva.shape, c2),
                  pl.BlockSpec(kvnorm.shape, c2),
                  pl.BlockSpec(wkvb.shape, c2),
                  pl.BlockSpec((1, LANES), c2),
                  pl.BlockSpec((1, LANES), c2)],
        out_specs=[pl.BlockSpec((2 * hn, tp), col), pl.BlockSpec((tp, 2 * hn), row),
                   pl.BlockSpec((heads * V_ROWS, tp), col)],
        out_shape=[jax.ShapeDtypeStruct((2 * hn, n), BF16), jax.ShapeDtypeStruct((n, 2 * hn), BF16),
                   jax.ShapeDtypeStruct((heads * V_ROWS, n), BF16)],
        compiler_params=_params("arbitrary"),
        name="mla_proj",
    )(h, positions, g, wqa, qnorm, wqb, wkva, kvnorm, wkvb, invf, sgn)

    tq = tk = min(TOKEN_TILE, seq)
    nq = seq // tq
    scale = float(NOPE + ROPE) ** -0.5
    pairs = [(i, j) for i in range(nq) for j in range(i + 1)]
    qi = jnp.asarray([p[0] for p in pairs], jnp.int32)
    kj = jnp.asarray([p[1] for p in pairs], jnp.int32)
    npairs = len(pairs)
    qrow = lambda b, p, qi, kj: (b * nq + qi[p], 0)
    qcol = lambda b, p, qi, kj: (0, b * nq + qi[p])
    krow = lambda b, p, qi, kj: (b * nq + kj[p], 0)
    kcol = lambda b, p, qi, kj: (0, b * nq + kj[p])
    side, side_in, side_out, side_shapes = _side_cast_plan(side_weights, side_layer, batch * npairs,
                                                           lambda b, p, qi, kj: b * npairs + p)
    grid_spec = pltpu.PrefetchScalarGridSpec(
        num_scalar_prefetch=2,
        grid=(batch, npairs),
        in_specs=[pl.BlockSpec((2 * hn, tq), qcol),
                  pl.BlockSpec((tk, 2 * hn), krow),
                  pl.BlockSpec((heads * V_ROWS, tk), kcol),
                  pl.BlockSpec((tq, d), qrow),
                  pl.BlockSpec(wot.shape, lambda b, p, qi, kj: (0, 0))] + side_in,
        out_specs=[pl.BlockSpec((tq, d), qrow)] + side_out,
        scratch_shapes=[pltpu.VMEM((heads, 1, tq), F32), pltpu.VMEM((heads * V_ROWS, tq), F32)],
    )
    outs = pl.pallas_call(
        functools.partial(_attn_kernel, heads=heads, tq=tq, tk=tk, scale=scale, n_side=len(side)),
        grid_spec=grid_spec,
        out_shape=[jax.ShapeDtypeStruct((n, d), F32)] + side_shapes,
        compiler_params=_params("arbitrary", "arbitrary"),
        name="mla_attention",
    )(qi, kj, qt, k, vt, h, wot, *side)
    return outs[0], [o.reshape(w.shape[1:]) for o, w in zip(outs[1:], side_weights)]


def _store_row_tiles(ref, value, rows):
    slabs = value.shape[1] // LANES
    for s in range(slabs):
        ref[pl.ds(s, rows, stride=slabs), :] = value[:, s * LANES:(s + 1) * LANES]


def _load_row_tile_slab(ref, s, rows, slabs):
    return ref[pl.ds(s, rows, stride=slabs), :]


def _route_kernel(h_ref, g_ref, r_ref, u_ref, info_ref, info_t_ref, cnt_ref, carry, *, tr, n_exp):
    i = pl.program_id(0)

    @pl.when(i == 0)
    def _():
        carry[...] = jnp.zeros(carry.shape, F32)

    u = _rms(h_ref[...], g_ref[...])
    u_ref[...] = u.astype(BF16)
    logits = _dot_3pass(u, r_ref[...])
    lane = lax.broadcasted_iota(jnp.int32, (tr, LANES), 1)
    lg = jnp.where(lane < n_exp, logits, -jnp.inf)
    m1 = jnp.max(lg, axis=1, keepdims=True)
    i1 = jnp.min(jnp.where(lg == m1, lane, LANES), axis=1, keepdims=True)
    lg2 = jnp.where(lane == i1, -jnp.inf, lg)
    m2 = jnp.max(lg2, axis=1, keepdims=True)
    i2 = jnp.min(jnp.where(lg2 == m2, lane, LANES), axis=1, keepdims=True)
    e = jnp.exp(m2 - m1)
    g1 = 1.0 / (1.0 + e)
    g2 = e / (1.0 + e)
    sel1 = lane == i1
    sel2 = lane == i2
    onehot = jnp.where(sel1 | sel2, 1.0, 0.0)
    rr = lax.broadcasted_iota(jnp.int32, (tr, tr), 0)
    cc = lax.broadcasted_iota(jnp.int32, (tr, tr), 1)
    tri = jnp.where(cc < rr, 1.0, 0.0).astype(BF16)
    before = _dot(tri, onehot.astype(BF16)) + carry[0:1, :]
    rank1 = jnp.sum(jnp.where(sel1, before, 0.0), axis=1, keepdims=True)
    rank2 = jnp.sum(jnp.where(sel2, before, 0.0), axis=1, keepdims=True)
    carry[...] = carry[...] + jnp.sum(onehot, axis=0, keepdims=True)
    cols = (i1.astype(F32), i2.astype(F32), g1, g2, rank1, rank2)
    info = jnp.zeros((tr, LANES), F32)
    for k, col in enumerate(cols):
        info = jnp.where(lane == k, col, info)
    info_ref[...] = info
    info_t_ref[...] = info.T[0:SUBLANES, :]
    cnt_ref[...] = carry[...]


def _route(h, g, router_padded, n_exp):
    n, d = h.shape
    tr = min(TOKEN_TILE, n)
    slabs = d // LANES
    return pl.pallas_call(
        functools.partial(_route_kernel, tr=tr, n_exp=n_exp),
        grid=(n // tr,),
        in_specs=[pl.BlockSpec((tr, d), lambda i: (i, 0)),
                  pl.BlockSpec((1, d), lambda i: (0, 0)),
                  pl.BlockSpec((d, LANES), lambda i: (0, 0))],
        out_specs=[pl.BlockSpec((tr, d), lambda i: (i, 0)),
                   pl.BlockSpec((tr, LANES), lambda i: (i, 0)),
                   pl.BlockSpec((SUBLANES, tr), lambda i: (0, i)),
                   pl.BlockSpec((8, LANES), lambda i: (i, 0))],
        out_shape=[jax.ShapeDtypeStruct((n, d), BF16),
                   jax.ShapeDtypeStruct((n, LANES), F32),
                   jax.ShapeDtypeStruct((SUBLANES, n), F32),
                   jax.ShapeDtypeStruct((n // tr * 8, LANES), F32)],
        scratch_shapes=[pltpu.VMEM((8, LANES), F32)],
        compiler_params=_params("arbitrary"),
        name="moe_route",
    )(h, g, router_padded)


def _run_pieces(length, piece):
    rem = length & (RUN_CHUNK - 1)
    full = length - rem

    def body(c, carry):
        piece(c * RUN_CHUNK, RUN_CHUNK)
        return carry

    lax.fori_loop(0, lax.shift_right_logical(full, RUN_CHUNK.bit_length() - 1), body, 0)
    bit = RUN_CHUNK // 2
    while bit:
        done = rem & (RUN_CHUNK - 2 * bit)
        pl.when((rem & bit) != 0)(functools.partial(piece, full + done, bit))
        bit //= 2


def _dispatch_kernel(pad_start_ref, pad_cnt_ref, plan_ref, info_t_ref, u_ref, x_ref, zero_sc, stage, sem, wsem,
                     *, tile, slabs, n_exp):
    i = pl.program_id(0)
    slot = i % 2

    def each_pad_copy(action):
        for e in range(n_exp):
            def piece(off, nrows):
                d = pl.multiple_of((pad_start_ref[e] + off) * slabs, slabs)
                action(pltpu.make_async_copy(zero_sc.at[pl.ds(0, nrows * slabs)],
                                             x_ref.at[pl.ds(d, nrows * slabs)], sem))

            _run_pieces(pad_cnt_ref[e], piece)

    @pl.when(pl.program_id(0) == 0)
    def _():
        zero_sc[...] = jnp.zeros(zero_sc.shape, F32)
        each_pad_copy(lambda copy: copy.start())
        each_pad_copy(lambda copy: copy.wait())

        def chunk_copy(i):
            dst = pl.multiple_of((pad_start_ref[n_exp] + i * ZERO_CHUNK) * slabs, slabs)
            return pltpu.make_async_copy(zero_sc, x_ref.at[pl.ds(dst, ZERO_CHUNK * slabs)], sem)

        def fill_tail(i, c):
            chunk_copy(i).start()
            return c

        def drain_tail(i, c):
            chunk_copy(i).wait()
            return c

        lax.fori_loop(0, pad_cnt_ref[n_exp], fill_tail, 0)
        lax.fori_loop(0, pad_cnt_ref[n_exp], drain_tail, 0)

    def each_run_copy(block, buf, action):
        for e in range(n_exp):
            base = (block * PLAN_ROWS) * n_exp + e
            dst0, length, src0 = plan_ref[base], plan_ref[base + n_exp], plan_ref[base + 2 * n_exp]

            def piece(off, nrows):
                s = pl.multiple_of((src0 + off) * slabs, slabs)
                d = pl.multiple_of((dst0 + off) * slabs, slabs)
                action(pltpu.make_async_copy(stage.at[buf, pl.ds(s, nrows * slabs)],
                                             x_ref.at[pl.ds(d, nrows * slabs)], wsem.at[buf]))

            _run_pieces(length, piece)

    info_t = info_t_ref[...]
    shift_base = (i * PLAN_ROWS + 3) * n_exp
    r_idx = lax.broadcasted_iota(jnp.int32, (TOP_K * tile, tile), 0)
    onehot = jnp.zeros(r_idx.shape, F32)
    for k in range(TOP_K):
        ls = info_t[4 + k:5 + k, :] + _select_by_expert(info_t[k:k + 1, :], plan_ref, shift_base, n_exp)
        onehot = onehot + jnp.where(r_idx == ls.astype(jnp.int32), 1.0, 0.0)
    x = _dot(onehot.astype(BF16), u_ref[...])
    _store_row_tiles(stage.at[slot], x, TOP_K * tile)
    each_run_copy(i, slot, lambda copy: copy.start())

    @pl.when(i > 0)
    def _():
        each_run_copy(i - 1, 1 - slot, lambda copy: copy.wait())

    @pl.when(i == pl.num_programs(0) - 1)
    def _():
        each_run_copy(i, slot, lambda copy: copy.wait())


def _dispatch(u, info_t, plan, pad_start, pad_cnt, sorted_rows, tile, slabs):
    n, d = u.shape
    grid_spec = pltpu.PrefetchScalarGridSpec(
        num_scalar_prefetch=3,
        grid=(n // tile,),
        in_specs=[pl.BlockSpec((SUBLANES, tile), lambda i, ps, pc, pn: (0, i)),
                  pl.BlockSpec((tile, d), lambda i, ps, pc, pn: (i, 0))],
        out_specs=pl.BlockSpec(memory_space=pl.ANY),
        scratch_shapes=[pltpu.VMEM((ZERO_CHUNK * slabs, LANES), F32),
                        pltpu.VMEM((2, TOP_K * tile * slabs, LANES), F32),
                        pltpu.SemaphoreType.DMA(()), pltpu.SemaphoreType.DMA((2,))],
    )
    return pl.pallas_call(
        functools.partial(_dispatch_kernel, tile=tile, slabs=slabs, n_exp=pad_start.shape[0] - 1),
        grid_spec=grid_spec,
        out_shape=jax.ShapeDtypeStruct((sorted_rows * slabs, LANES), F32),
        compiler_params=_params("arbitrary"),
        name="moe_dispatch",
    )(pad_start, pad_cnt, plan, info_t, u)


def _moe_kernel(te_ref, nu_ref, x_ref, wg_ref, wu_ref, wd_ref, o_ref, x_sc, acc_sc, *, tm, slabs, nj):
    t = pl.program_id(0)
    j = pl.program_id(1)

    @pl.when(t < nu_ref[0])
    def _():
        @pl.when(j == 0)
        def _():
            for s in range(slabs):
                x_sc[:, s * LANES:(s + 1) * LANES] = _load_row_tile_slab(x_ref, s, tm, slabs).astype(BF16)

        def finish(r):
            o_ref[...] = r.astype(BF16)

        _accumulate_steps(j, nj, lambda: _swiglu_step(x_sc[...], wg_ref[0], wu_ref[0], wd_ref[0]), acc_sc, finish)

    @pl.when((t >= nu_ref[0]) & (j == 0))
    def _():
        o_ref[...] = jnp.zeros(o_ref.shape, BF16)


def _moe_experts(x_sorted, tile_expert, n_used, wg, wu, wd, tm, slabs):
    d = slabs * LANES
    rows = x_sorted.shape[0] // slabs
    f = wg.shape[2]
    tf = _ff_chunk(f)
    nj = f // tf
    tmap = lambda t, j, te, nu: (jnp.minimum(t, nu[0] - 1), 0)
    jsel = lambda t, j, nu: jnp.where(t < nu[0], j, nj - 1)
    grid_spec = pltpu.PrefetchScalarGridSpec(
        num_scalar_prefetch=2,
        grid=(rows // tm, nj),
        in_specs=[pl.BlockSpec((tm * slabs, LANES), tmap),
                  pl.BlockSpec((1, d, tf), lambda t, j, te, nu: (te[t], 0, jsel(t, j, nu))),
                  pl.BlockSpec((1, d, tf), lambda t, j, te, nu: (te[t], 0, jsel(t, j, nu))),
                  pl.BlockSpec((1, tf, d), lambda t, j, te, nu: (te[t], jsel(t, j, nu), 0))],
        out_specs=pl.BlockSpec((tm, d), lambda t, j, te, nu: (t, 0)),
        scratch_shapes=[pltpu.VMEM((tm, d), BF16), pltpu.VMEM((tm, d), F32)],
    )
    return pl.pallas_call(
        functools.partial(_moe_kernel, tm=tm, slabs=slabs, nj=nj),
        grid_spec=grid_spec,
        out_shape=jax.ShapeDtypeStruct((rows, d), BF16),
        compiler_params=_params("arbitrary", "arbitrary"),
        name="moe_experts",
    )(tile_expert, n_used, x_sorted, wg, wu, wd)


def _select_by_expert(expert, plan_ref, base, n_exp):
    out = jnp.zeros(expert.shape, F32)
    for e in range(n_exp):
        out = jnp.where(expert == float(e), plan_ref[base + e].astype(F32), out)
    return out


def _combine_stage_rows(tile, n_exp):
    max_chunks = (TOP_K * tile + n_exp * (BF16_ROWS - 1 + COMBINE_CHUNK - 1)) // COMBINE_CHUNK
    return -(-max_chunks * COMBINE_CHUNK // 256) * 256


def _combine_kernel(plan_ref, h_ref, info_ref, gf_ref, y_ref, o_ref, stage, sem, *, tile, n_exp, final_norm):
    i = pl.program_id(0)
    slot = i % 2

    def each_chunk(block, buf, action):
        for e in range(n_exp):
            base = (block * PLAN_ROWS) * n_exp + e

            def body(c, carry):
                src = pl.multiple_of(plan_ref[base] + c * COMBINE_CHUNK, BF16_ROWS)
                dst = pl.multiple_of(plan_ref[base + 2 * n_exp] + c * COMBINE_CHUNK, COMBINE_CHUNK)
                action(pltpu.make_async_copy(y_ref.at[pl.ds(src, COMBINE_CHUNK)],
                                             stage.at[buf, pl.ds(dst, COMBINE_CHUNK)], sem.at[buf]))
                return carry

            lax.fori_loop(0, plan_ref[base + n_exp], body, 0)

    @pl.when(i == 0)
    def _():
        stage[...] = jnp.zeros(stage.shape, BF16)
        each_chunk(i, slot, lambda copy: copy.start())

    @pl.when(i + 1 < pl.num_programs(0))
    def _():
        each_chunk(i + 1, 1 - slot, lambda copy: copy.start())

    each_chunk(i, slot, lambda copy: copy.wait())

    info = info_ref[...]
    shift_base = (i * PLAN_ROWS + 3) * n_exp
    col = lax.broadcasted_iota(jnp.int32, (tile, stage.shape[1]), 1)
    p = jnp.zeros(col.shape, F32)
    for k in range(TOP_K):
        loc = info[:, 4 + k:5 + k] + _select_by_expert(info[:, k:k + 1], plan_ref, shift_base, n_exp)
        p = p + jnp.where(col == loc.astype(jnp.int32), info[:, 2 + k:3 + k], 0.0)
    out = h_ref[...] + _dot(p.astype(BF16), stage[slot])
    if final_norm:
        out = _rms(out, gf_ref[...])
    o_ref[...] = out


def _combine(h, info, plan, y_sorted, g_final, final_norm, n_exp):
    n, d = h.shape
    tile = n // (plan.shape[0] // (PLAN_ROWS * n_exp))
    grid_spec = pltpu.PrefetchScalarGridSpec(
        num_scalar_prefetch=1,
        grid=(n // tile,),
        in_specs=[pl.BlockSpec((tile, d), lambda i, plan: (i, 0)),
                  pl.BlockSpec((tile, LANES), lambda i, plan: (i, 0)),
                  pl.BlockSpec((1, d), lambda i, plan: (0, 0)),
                  pl.BlockSpec(memory_space=pl.ANY)],
        out_specs=pl.BlockSpec((tile, d), lambda i, plan: (i, 0)),
        scratch_shapes=[pltpu.VMEM((2, _combine_stage_rows(tile, n_exp), d), BF16), pltpu.SemaphoreType.DMA((2,))],
    )
    return pl.pallas_call(
        functools.partial(_combine_kernel, tile=tile, n_exp=n_exp, final_norm=final_norm),
        grid_spec=grid_spec,
        out_shape=jax.ShapeDtypeStruct((n, d), F32),
        compiler_params=_params("arbitrary"),
        name="moe_combine",
    )(plan, h, info, g_final, y_sorted)


def _moe_layer(h, g, router, wg, wu, wd, g_final, final_norm):
    n, d = h.shape
    n_exp = router.shape[1]
    tm = min(TOKEN_TILE, n)
    router_padded = jnp.pad(router, ((0, 0), (0, LANES - n_exp)))
    u, info, info_t, cnt = _route(h, g, router_padded, n_exp)

    tile = min(TOKEN_TILE, n)
    n_blocks = n // tile
    after = cnt.reshape(n_blocks, 8, LANES)[:, 0, :n_exp].astype(jnp.int32)
    before = jnp.concatenate([jnp.zeros((1, n_exp), jnp.int32), after[:-1]])
    counts = after[-1]
    padded = (counts + tm - 1) // tm * tm
    ends = jnp.cumsum(padded)
    starts = ends - padded
    rows = TOP_K * n + n_exp * tm
    n_used = (ends[-1] // tm).astype(jnp.int32)
    tile_ids = jnp.minimum(jnp.arange(rows // tm, dtype=jnp.int32), n_used - 1)
    tile_expert = jnp.sum(tile_ids[:, None] * tm >= ends[None, :], axis=1).astype(jnp.int32)

    run_start = starts[None, :] + before
    run_len = after - before
    fetch_start = run_start // BF16_ROWS * BF16_ROWS
    n_chunks = jnp.where(run_len > 0, (run_start - fetch_start + run_len + COMBINE_CHUNK - 1) // COMBINE_CHUNK, 0)
    stage_off = COMBINE_CHUNK * (jnp.cumsum(n_chunks, axis=1) - n_chunks)
    plan = jnp.stack([fetch_start, n_chunks, stage_off, starts[None, :] + stage_off - fetch_start],
                     axis=1).reshape(-1).astype(jnp.int32)

    slabs = d // LANES
    assert slabs == 8
    local_off = jnp.cumsum(run_len, axis=1) - run_len
    dispatch_plan = jnp.stack([run_start, run_len, local_off, local_off - before],
                              axis=1).reshape(-1).astype(jnp.int32)
    assert tm % ZERO_CHUNK == 0
    pad_start = jnp.concatenate([starts + counts, ends[-1:]]).astype(jnp.int32)
    pad_cnt = jnp.concatenate([padded - counts, (rows - ends[-1:]) // ZERO_CHUNK]).astype(jnp.int32)
    x_sorted = _dispatch(u, info_t, dispatch_plan, pad_start, pad_cnt, rows, tile, slabs)
    y_sorted = _moe_experts(x_sorted, tile_expert, n_used.reshape(1), wg, wu, wd, tm, slabs)
    return _combine(h, info, plan, y_sorted, g_final, final_norm, n_exp)


def _mla_weights(wq_b, wkv_a, wkv_b, heads, lora):
    def padded(w):
        return jnp.concatenate([w, jnp.zeros(w.shape[:-1] + (LANES - ROPE,), w.dtype)], axis=-1)

    rq = wq_b.shape[0]
    wq = wq_b.reshape(rq, heads, NOPE + ROPE)
    wqb = jnp.concatenate([wq[..., :NOPE].reshape(rq, -1), padded(wq[..., NOPE:]).reshape(rq, -1)], axis=1)
    wkva = jnp.concatenate([wkv_a[:, :lora], padded(wkv_a[:, lora:])], axis=1)
    wkv = wkv_b.reshape(lora, heads, 2 * NOPE)
    wkvb = jnp.concatenate([wkv[..., :NOPE].reshape(lora, -1), wkv[..., NOPE:].reshape(lora, -1)], axis=1)
    return wqb.astype(BF16), wkva.astype(BF16), wkvb.astype(BF16)


def _cast_kernel(x_ref, o_ref):
    o_ref[...] = x_ref[0].astype(BF16)


def _to_bf16(w, layer):
    cols = w.shape[-1]
    w3 = w.reshape(w.shape[0], -1, cols)
    rows = w3.shape[1]
    cap = 256 if cols > 2048 else 1024
    tr = next(t for t in (1024, 512, 256, 128, 64, 32, 16, 8) if t <= cap and rows % t == 0)
    out = pl.pallas_call(
        _cast_kernel,
        grid=(rows // tr,),
        in_specs=[pl.BlockSpec((1, tr, cols), lambda i: (layer, i, 0))],
        out_specs=pl.BlockSpec((tr, cols), lambda i: (i, 0)),
        out_shape=jax.ShapeDtypeStruct((rows, cols), BF16),
        compiler_params=_params("arbitrary"),
        name="weights_to_bf16",
    )(w3)
    return out.reshape(w.shape[1:])


def _final_norm_kernel(h_ref, g_ref, o_ref):
    o_ref[...] = _rms(h_ref[...], g_ref[...])


def _final_norm(h, g):
    n, d = h.shape
    tm = min(TOKEN_TILE, n)
    return pl.pallas_call(
        _final_norm_kernel,
        grid=(n // tm,),
        in_specs=[pl.BlockSpec((tm, d), lambda i: (i, 0)), pl.BlockSpec((1, d), lambda i: (0, 0))],
        out_specs=pl.BlockSpec((tm, d), lambda i: (i, 0)),
        out_shape=jax.ShapeDtypeStruct((n, d), F32),
        compiler_params=_params("arbitrary"),
        name="final_norm",
    )(h, g)


@jax.jit
def _trunk(x, positions, norm_mix, norm_ffn, norm_final, pool_w, pool_scale,
           conv_w1, conv_b1, conv_dw, conv_bdw, conv_ln_g, conv_ln_b, conv_w2, conv_b2,
           mla_wq_a, mla_q_norm, mla_wq_b, mla_wkv_a, mla_kv_norm, mla_wkv_b, mla_wo,
           ffn_wg, ffn_wu, ffn_wd, moe_router, moe_wg, moe_wu, moe_wd):
    batch, seq, d = x.shape
    depth = norm_mix.shape[0]
    lora = mla_kv_norm.shape[1]
    heads = mla_wo.shape[1] // NOPE
    assert mla_wq_b.shape[2] == heads * (NOPE + ROPE) and mla_wkv_a.shape[2] == lora + ROPE
    n = batch * seq
    h = x.reshape(n, d)
    pos = positions.reshape(n, 1)
    row = lambda v: v.reshape(1, -1)

    half = jnp.arange(0, ROPE, 2, dtype=F32) / ROPE
    inv_freq = ROPE_THETA ** (-half)
    invf = row(jnp.concatenate([inv_freq, inv_freq, jnp.zeros((LANES - ROPE,), F32)]))
    sgn = row(jnp.concatenate([-jnp.ones((ROPE // 2,), F32), jnp.ones((ROPE // 2,), F32),
                               jnp.zeros((LANES - ROPE,), F32)]))

    moe_w = (moe_wg, moe_wu, moe_wd)
    moe_bf16 = {}
    ffn_bf16 = {}

    def moe_cast_job(i):
        nxt = i if i % 2 == 1 else i + 1
        if nxt < depth and nxt // 2 not in moe_bf16:
            return moe_w, nxt // 2
        return (), 0

    for i in range(depth):
        kind, j = i % 3, i // 3
        g = row(norm_mix[i])
        if kind == 0:
            jobs = [(w, f) for f in range(ffn_wg.shape[0]) for w in (ffn_wg, ffn_wu, ffn_wd)] if not ffn_bf16 else []
            h, cast = _pool_layer(h, g, pool_w[j].astype(BF16), row(pool_scale[j]), [w for w, _ in jobs],
                                  [f for _, f in jobs], batch=batch, seq=seq)
            for k in range(0, len(cast), 3):
                ffn_bf16[jobs[k][1]] = cast[k:k + 3]
        elif kind == 1:
            side_w, side_layer = moe_cast_job(i)
            h, cast = _conv_layer(h, g, conv_w1[j].astype(BF16), row(conv_b1[j]), conv_dw[j], row(conv_bdw[j]),
                                  row(conv_ln_g[j]), row(conv_ln_b[j]), conv_w2[j].astype(BF16), row(conv_b2[j]),
                                  side_w, side_layer, batch=batch, seq=seq)
            if cast:
                moe_bf16[side_layer] = cast
        else:
            wqb, wkva, wkvb = _mla_weights(mla_wq_b[j], mla_wkv_a[j], mla_wkv_b[j], heads, lora)
            side_w, side_layer = moe_cast_job(i)
            h, cast = _mla_layer(h, pos, g, mla_wq_a[j].astype(BF16), row(mla_q_norm[j]), wqb, wkva,
                                 row(mla_kv_norm[j]), wkvb, mla_wo[j].T.astype(BF16), invf, sgn,
                                 side_w, side_layer, batch=batch, seq=seq, heads=heads, lora=lora)
            if cast:
                moe_bf16[side_layer] = cast
        g = row(norm_ffn[i])
        f_idx = i // 2
        last = i == depth - 1
        if i % 2 == 0:
            wg, wu, wd = ffn_bf16.get(f_idx) or [_to_bf16(w, f_idx) for w in (ffn_wg, ffn_wu, ffn_wd)]
            h = _ffn_layer(h, g, wg, wu, wd)
            if last:
                h = _final_norm(h, row(norm_final))
        else:
            wg, wu, wd = moe_bf16.get(f_idx) or [_to_bf16(w, f_idx) for w in moe_w]
            h = _moe_layer(h, g, moe_router[f_idx], wg, wu, wd, row(norm_final), last)
    return h.reshape(batch, seq, d)


def kernel(x, positions, norm_mix, norm_ffn, norm_final, pool_w, pool_scale, conv_w1, conv_b1, conv_dw, conv_bdw, conv_ln_g, conv_ln_b, conv_w2, conv_b2, mla_wq_a, mla_q_norm, mla_wq_b, mla_wkv_a, mla_kv_norm, mla_wkv_b, mla_wo, ffn_wg, ffn_wu, ffn_wd, moe_router, moe_wg, moe_wu, moe_wd):
    return _trunk(x, positions, norm_mix, norm_ffn, norm_final, pool_w, pool_scale,
                  conv_w1, conv_b1, conv_dw, conv_bdw, conv_ln_g, conv_ln_b, conv_w2, conv_b2,
                  mla_wq_a, mla_q_norm, mla_wq_b, mla_wkv_a, mla_kv_norm, mla_wkv_b, mla_wo,
                  ffn_wg, ffn_wu, ffn_wd, moe_router, moe_wg, moe_wu, moe_wd)
```

```python
import functools

import jax
import jax.numpy as jnp
from jax import lax
from jax.experimental import pallas as pl
from jax.experimental.pallas import tpu as pltpu

F32 = jnp.float32
BF16 = jnp.bfloat16

NORM_EPS = 1e-6
LN_EPS = 1e-5
ROPE_THETA = 10000.0
LOG2E = 1.4426950408889634

POOL_WINDOWS = (2, 4, 8, 16)
POOL_HALO = 16
CONV_HALO = 32
LANES = 128
BF16_ROWS = 16
TOKEN_TILE = 512
CONV_TILE = 256
WIDE_TILE = 1024
SUBLANES = 8
POOL_TOP = POOL_HALO + SUBLANES
NOPE = 128
ROPE = 64
V_ROWS = NOPE + 16
TOP_K = 2
ZERO_CHUNK = 64
COMBINE_CHUNK = 32
RUN_CHUNK = 64
PLAN_ROWS = 4

VMEM_LIMIT = 56 * 1024 * 1024


def _params(*sem):
    return pltpu.CompilerParams(dimension_semantics=sem, vmem_limit_bytes=VMEM_LIMIT)


def _rms(x, g, eps=NORM_EPS):
    return x * lax.rsqrt(jnp.mean(x * x, axis=-1, keepdims=True) + eps) * g


def _dot(a, b):
    return jnp.dot(a, b, preferred_element_type=F32)


def _split_bf16(x):
    hi = x.astype(BF16)
    return hi, (x - hi.astype(F32)).astype(BF16)


def _dot_3pass(a, b):
    a_hi, a_lo = _split_bf16(a)
    b_hi, b_lo = _split_bf16(b)
    return _dot(a_hi, b_hi) + (_dot(a_hi, b_lo) + _dot(a_lo, b_hi))


def _ff_chunk(f):
    for c in (1792, 1024, 512, 256, 128):
        if f % c == 0:
            return c
    return f


def _pool_kernel(h_ref, g_ref, w_ref, sc_ref, *rest, ts, gdim, n_side):
    side_in, o_ref, side_out, (ubuf, lvl_a, lvl_b) = rest[:n_side], rest[n_side], rest[n_side + 1:-3], rest[-3:]
    _side_cast(side_in, side_out)
    j = pl.program_id(1)
    h = h_ref[...]
    u = _rms(h, g_ref[...])
    ext = POOL_TOP + ts

    @pl.when(j == 0)
    def _():
        ubuf[0:POOL_TOP, :] = jnp.zeros((POOL_TOP, ubuf.shape[1]), F32)
        lvl_a[0:SUBLANES, :] = jnp.zeros((SUBLANES, gdim), F32)
        lvl_b[0:SUBLANES, :] = jnp.zeros((SUBLANES, gdim), F32)

    ubuf[POOL_TOP:ext, :] = u
    t1 = lax.broadcasted_iota(jnp.int32, (ts, 1), 0) + (j * ts + 1)
    for g, win in enumerate(POOL_WINDOWS):
        cs = slice(g * gdim, (g + 1) * gdim)
        src, cols, spare, k = ubuf, cs, [lvl_a, lvl_b], 1
        while k < win:
            s = src[SUBLANES:ext, cols] + src[SUBLANES - k:ext - k, cols]
            k *= 2
            if k < win:
                dst = spare.pop(0)
                dst[SUBLANES:ext, :] = s
                spare.append(dst)
                src, cols = dst, slice(None)
        s = s[POOL_HALO:POOL_HALO + ts]
        ug = u[:, cs]
        cnt = jnp.minimum(t1, win).astype(F32)
        pooled = s / cnt - ug
        m = _dot(pooled.astype(BF16), w_ref[g])
        o_ref[:, cs] = h[:, cs] + m * sc_ref[:, cs]
    ubuf[SUBLANES:POOL_TOP, :] = ubuf[ts + SUBLANES:ts + POOL_TOP, :]


def _pool_layer(h, g, w, scale, side_weights, side_layers, *, batch, seq):
    n, d = h.shape
    ts = min(WIDE_TILE, seq)
    nt = seq // ts
    gdim = d // len(POOL_WINDOWS)
    row = lambda b, j: (b * nt + j, 0)
    const2 = lambda b, j: (0, 0)
    side, side_in, side_out, side_shapes = _side_cast_plan(side_weights, list(side_layers), batch * nt,
                                                           lambda b, j: b * nt + j)
    outs = pl.pallas_call(
        functools.partial(_pool_kernel, ts=ts, gdim=gdim, n_side=len(side)),
        grid=(batch, nt),
        in_specs=[pl.BlockSpec((ts, d), row),
                  pl.BlockSpec((1, d), const2),
                  pl.BlockSpec(w.shape, lambda b, j: (0, 0, 0)),
                  pl.BlockSpec((1, d), const2)] + side_in,
        out_specs=[pl.BlockSpec((ts, d), row)] + side_out,
        out_shape=[jax.ShapeDtypeStruct((n, d), F32)] + side_shapes,
        scratch_shapes=[pltpu.VMEM((ts + POOL_TOP, d), F32), pltpu.VMEM((ts + POOL_TOP, gdim), F32),
                        pltpu.VMEM((ts + POOL_TOP, gdim), F32)],
        compiler_params=_params("arbitrary", "arbitrary"),
        name="pool_mixer",
    )(h, g, w, scale, *side)
    return outs[0], [o.reshape(w.shape[1:]) for o, w in zip(outs[1:], side_weights)]


def _swiglu_step(u, wg, wu, wd):
    a = _dot(u, wg)
    b = _dot(u, wu)
    hh = a * jax.nn.sigmoid(a) * b
    return _dot(hh.astype(BF16), wd)


def _accumulate_steps(j, nj, partial, acc_sc, finish):
    def run(first, last):
        r = partial()
        if not first:
            r = acc_sc[...] + r
        if last:
            finish(r)
        else:
            acc_sc[...] = r

    if nj == 1:
        run(True, True)
        return
    pl.when(j == 0)(functools.partial(run, True, False))
    if nj > 2:
        pl.when((j > 0) & (j < nj - 1))(functools.partial(run, False, False))
    pl.when(j == nj - 1)(functools.partial(run, False, True))


def _ffn_kernel(h_ref, g_ref, wg_ref, wu_ref, wd_ref, o_ref, u_sc, acc_sc, *, nj):
    j = pl.program_id(1)

    @pl.when(j == 0)
    def _():
        u_sc[...] = _rms(h_ref[...], g_ref[...]).astype(BF16)

    def finish(r):
        o_ref[...] = h_ref[...] + r

    _accumulate_steps(j, nj, lambda: _swiglu_step(u_sc[...], wg_ref[...], wu_ref[...], wd_ref[...]), acc_sc, finish)


def _ffn_layer(h, g, wg, wu, wd):
    n, d = h.shape
    f = wg.shape[1]
    tm = min(TOKEN_TILE, n)
    tf = _ff_chunk(f)
    return pl.pallas_call(
        functools.partial(_ffn_kernel, nj=f // tf),
        grid=(n // tm, f // tf),
        in_specs=[pl.BlockSpec((tm, d), lambda i, j: (i, 0)),
                  pl.BlockSpec((1, d), lambda i, j: (0, 0)),
                  pl.BlockSpec((d, tf), lambda i, j: (0, j)),
                  pl.BlockSpec((d, tf), lambda i, j: (0, j)),
                  pl.BlockSpec((tf, d), lambda i, j: (j, 0))],
        out_specs=pl.BlockSpec((tm, d), lambda i, j: (i, 0)),
        out_shape=jax.ShapeDtypeStruct((n, d), F32),
        scratch_shapes=[pltpu.VMEM((tm, d), BF16), pltpu.VMEM((tm, d), F32)],
        compiler_params=_params("arbitrary", "arbitrary"),
        name="dense_swiglu",
    )(h, g, wg, wu, wd)


def _side_cast_plan(weights, layer, steps, step_index):
    arrays = [w.reshape(w.shape[0], -1, w.shape[-1]) for w in weights]
    if not arrays:
        return [], [], [], []
    nblk = next(n for n in range(steps, 0, -1)
                if all(a.shape[1] % n == 0 and (a.shape[1] // n) % BF16_ROWS == 0 for a in arrays))
    blk = lambda *g: jnp.minimum(step_index(*g), nblk - 1)
    layers = layer if isinstance(layer, (list, tuple)) else [layer] * len(arrays)
    in_specs, out_specs, out_shapes = [], [], []
    for a, lyr in zip(arrays, layers):
        rows, cols = a.shape[1:]
        in_specs.append(pl.BlockSpec((1, rows // nblk, cols), lambda *g, _l=lyr: (_l, blk(*g), 0)))
        out_specs.append(pl.BlockSpec((rows // nblk, cols), lambda *g: (blk(*g), 0)))
        out_shapes.append(jax.ShapeDtypeStruct((rows, cols), BF16))
    return arrays, in_specs, out_specs, out_shapes


def _side_cast(in_refs, out_refs):
    for src, dst in zip(in_refs, out_refs):
        dst[...] = src[0].astype(BF16)


def _conv_kernel(h_ref, g_ref, w1_ref, b1_ref, dw_ref, bdw_ref, lng_ref, lnb_ref, w2_ref, b2_ref, *rest,
                 ts, width, rb, n_side):
    side_in, o_ref, side_out, (gbuf, cbuf) = rest[:n_side], rest[n_side], rest[n_side + 1:-2], rest[-2:]
    _side_cast(side_in, side_out)
    j = pl.program_id(1)
    d = h_ref.shape[1]
    h = h_ref[...]
    u = _rms(h, g_ref[...]).astype(BF16)
    a = _dot(u, w1_ref[...]) + b1_ref[...]
    glu = a[:, :d] * jax.nn.sigmoid(a[:, d:])

    keep = CONV_HALO + SUBLANES

    @pl.when(j == 0)
    def _():
        gbuf[:, 0:keep, :] = jnp.zeros((SUBLANES, keep, d), F32)

    for b in range(SUBLANES):
        gbuf[b, CONV_HALO + b:CONV_HALO + b + ts, :] = glu
    for cb in range(d // LANES):
        cs = slice(cb * LANES, (cb + 1) * LANES)
        for r0 in range(0, ts, rb):
            acc = jnp.zeros((rb, LANES), F32)
            for k in range(width):
                a, b = divmod(width - 1 - k, SUBLANES)
                off = CONV_HALO + r0 - SUBLANES * a
                acc = acc + gbuf[b, off:off + rb, cs] * dw_ref[k:k + 1, cs]
            cbuf[r0:r0 + rb, cs] = acc + bdw_ref[:, cs]
    gbuf[:, 0:keep, :] = gbuf[:, ts:ts + keep, :]

    c = cbuf[...]
    mu = jnp.mean(c, axis=-1, keepdims=True)
    cc = c - mu
    var = jnp.mean(cc * cc, axis=-1, keepdims=True)
    un = cc * lax.rsqrt(var + LN_EPS) * lng_ref[...] + lnb_ref[...]
    un = un * jax.nn.sigmoid(un)
    o_ref[...] = h + _dot(un.astype(BF16), w2_ref[...]) + b2_ref[...]


def _conv_layer(h, g, w1, b1, dw, bdw, lng, lnb, w2, b2, side_weights, side_layer, *, batch, seq):
    n, d = h.shape
    width = dw.shape[0]
    assert width - 1 <= CONV_HALO
    ts = min(CONV_TILE, seq)
    rb = min(128, ts)
    nt = seq // ts
    row = lambda b, j: (b * nt + j, 0)
    c2 = lambda b, j: (0, 0)
    side, side_in, side_out, side_shapes = _side_cast_plan(side_weights, side_layer, batch * nt,
                                                           lambda b, j: b * nt + j)
    outs = pl.pallas_call(
        functools.partial(_conv_kernel, ts=ts, width=width, rb=rb, n_side=len(side)),
        grid=(batch, nt),
        in_specs=[pl.BlockSpec((ts, d), row),
                  pl.BlockSpec((1, d), c2),
                  pl.BlockSpec(w1.shape, c2),
                  pl.BlockSpec(b1.shape, c2),
                  pl.BlockSpec(dw.shape, c2),
                  pl.BlockSpec((1, d), c2),
                  pl.BlockSpec((1, d), c2),
                  pl.BlockSpec((1, d), c2),
                  pl.BlockSpec(w2.shape, c2),
                  pl.BlockSpec((1, d), c2)] + side_in,
        out_specs=[pl.BlockSpec((ts, d), row)] + side_out,
        out_shape=[jax.ShapeDtypeStruct((n, d), F32)] + side_shapes,
        scratch_shapes=[pltpu.VMEM((SUBLANES, ts + CONV_HALO + SUBLANES, d), F32), pltpu.VMEM((ts, d), F32)],
        compiler_params=_params("arbitrary", "arbitrary"),
        name="conv_module",
    )(h, g, w1, b1, dw, bdw, lng, lnb, w2, b2, *side)
    return outs[0], [o.reshape(w.shape[1:]) for o, w in zip(outs[1:], side_weights)]


def _mla_proj_kernel(h_ref, pos_ref, g_ref, wqa_ref, qnorm_ref, wqb_ref, wkva_ref, kvnorm_ref, wkvb_ref,
                     invf_ref, sgn_ref, qt_o, k_o, vt_o, *, heads, lora):
    hn = heads * NOPE
    rows = h_ref.shape[0]
    u = _rms(h_ref[...], g_ref[...]).astype(BF16)
    qa = _rms(_dot(u, wqa_ref[...]), qnorm_ref[...]).astype(BF16)
    q = _dot(qa, wqb_ref[...])
    kva = _dot(u, wkva_ref[...])
    ckv = _rms(kva[:, :lora], kvnorm_ref[...]).astype(BF16)
    kv = _dot(ckv, wkvb_ref[...])
    ang = pos_ref[...].astype(F32) * invf_ref[...]
    c = jnp.cos(ang)
    s = jnp.sin(ang) * sgn_ref[...]
    lane = lax.broadcasted_iota(jnp.int32, (rows, LANES), 1)

    def rope(x):
        swapped = jnp.where(lane < ROPE // 2, pltpu.roll(x, LANES - ROPE // 2, 1), pltpu.roll(x, ROPE // 2, 1))
        return x * c + swapped * s

    kr = rope(kva[:, lora:lora + LANES]).astype(BF16)
    ones = jnp.ones((V_ROWS - NOPE, rows), BF16)
    for hd in range(heads):
        ns = slice(hd * NOPE, (hd + 1) * NOPE)
        qr = rope(q[:, hn + hd * LANES:hn + (hd + 1) * LANES])
        qt_o[2 * hd * NOPE:(2 * hd + 1) * NOPE, :] = q[:, ns].T.astype(BF16)
        qt_o[(2 * hd + 1) * NOPE:(2 * hd + 2) * NOPE, :] = qr.T.astype(BF16)
        k_o[:, 2 * hd * NOPE:(2 * hd + 1) * NOPE] = kv[:, ns].astype(BF16)
        k_o[:, (2 * hd + 1) * NOPE:(2 * hd + 2) * NOPE] = kr
        vt_o[hd * V_ROWS:hd * V_ROWS + NOPE, :] = kv[:, hn + hd * NOPE:hn + (hd + 1) * NOPE].T.astype(BF16)
        vt_o[hd * V_ROWS + NOPE:(hd + 1) * V_ROWS, :] = ones


def _attn_kernel(qi_ref, kj_ref, qt_ref, k_ref, vt_ref, h_ref, wot_ref, *rest, heads, tq, tk, scale, n_side):
    side_in, o_ref, side_out, (m_sc, acc_sc) = rest[:n_side], rest[n_side], rest[n_side + 1:-2], rest[-2:]
    _side_cast(side_in, side_out)
    i = qi_ref[pl.program_id(1)]
    j = kj_ref[pl.program_id(1)]
    c = scale * LOG2E

    @pl.when(j == 0)
    def _():
        m_sc[...] = jnp.full(m_sc.shape, -jnp.inf, F32)
        acc_sc[...] = jnp.zeros(acc_sc.shape, F32)

    def step(diagonal):
        if diagonal:
            causal = (lax.broadcasted_iota(jnp.int32, (tk, tq), 0)
                      <= lax.broadcasted_iota(jnp.int32, (tk, tq), 1))
        es = [slice(2 * hd * NOPE, (2 * hd + 2) * NOPE) for hd in range(heads)]
        vs = [slice(hd * V_ROWS, (hd + 1) * V_ROWS) for hd in range(heads)]
        ss = [_dot(k_ref[:, e], qt_ref[e, :]) for e in es]
        if diagonal:
            ss = [jnp.where(causal, s, -jnp.inf) for s in ss]
        m_prev = [m_sc[hd] for hd in range(heads)]
        m_new = [jnp.maximum(mp, jnp.max(s, axis=0, keepdims=True)) for mp, s in zip(m_prev, ss)]
        ps = [jnp.exp2((s - mn) * c).astype(BF16) for s, mn in zip(ss, m_new)]
        pv = [_dot(vt_ref[v, :], p) for p, v in zip(ps, vs)]
        for hd in range(heads):
            alpha = jnp.exp2((m_prev[hd] - m_new[hd]) * c)
            acc_sc[vs[hd], :] = alpha * acc_sc[vs[hd], :] + pv[hd]
            m_sc[hd] = m_new[hd]

    @pl.when(j < i)
    def _():
        step(False)

    @pl.when(j == i)
    def _():
        step(True)
        parts = [acc_sc[hd * V_ROWS:hd * V_ROWS + NOPE, :] / acc_sc[hd * V_ROWS + NOPE:hd * V_ROWS + NOPE + 1, :]
                 for hd in range(heads)]
        ot = jnp.concatenate(parts, axis=0).astype(BF16)
        o_ref[...] = h_ref[...] + _dot(wot_ref[...], ot).T


def _mla_layer(h, positions, g, wqa, qnorm, wqb, wkva, kvnorm, wkvb, wot, invf, sgn, side_weights, side_layer,
               *, batch, seq, heads, lora):
    n, d = h.shape
    hn = heads * NOPE
    tp = min(WIDE_TILE, n)
    c2 = lambda i: (0, 0)
    row = lambda i: (i, 0)
    col = lambda i: (0, i)
    qt, k, vt = pl.pallas_call(
        functools.partial(_mla_proj_kernel, heads=heads, lora=lora),
        grid=(n // tp,),
        in_specs=[pl.BlockSpec((tp, d), row),
                  pl.BlockSpec((tp, 1), row),
                  pl.BlockSpec((1, d), c2),
                  pl.BlockSpec(wqa.shape, c2),
                  pl.BlockSpec(qnorm.shape, c2),
                  pl.BlockSpec(wqb.shape, c2),
                  pl.BlockSpec(wkva.shape, c2),
                  pl.BlockSpec(kvnorm.shape, c2),
                  pl.BlockSpec(wkvb.shape, c2),
                  pl.BlockSpec((1, LANES), c2),
                  pl.BlockSpec((1, LANES), c2)],
        out_specs=[pl.BlockSpec((2 * hn, tp), col), pl.BlockSpec((tp, 2 * hn), row),
                   pl.BlockSpec((heads * V_ROWS, tp), col)],
        out_shape=[jax.ShapeDtypeStruct((2 * hn, n), BF16), jax.ShapeDtypeStruct((n, 2 * hn), BF16),
                   jax.ShapeDtypeStruct((heads * V_ROWS, n), BF16)],
        compiler_params=_params("arbitrary"),
        name="mla_proj",
    )(h, positions, g, wqa, qnorm, wqb, wkva, kvnorm, wkvb, invf, sgn)

    tq = tk = min(TOKEN_TILE, seq)
    nq = seq // tq
    scale = float(NOPE + ROPE) ** -0.5
    pairs = [(i, j) for i in range(nq) for j in range(i + 1)]
    qi = jnp.asarray([p[0] for p in pairs], jnp.int32)
    kj = jnp.asarray([p[1] for p in pairs], jnp.int32)
    npairs = len(pairs)
    qrow = lambda b, p, qi, kj: (b * nq + qi[p], 0)
    qcol = lambda b, p, qi, kj: (0, b * nq + qi[p])
    krow = lambda b, p, qi, kj: (b * nq + kj[p], 0)
    kcol = lambda b, p, qi, kj: (0, b * nq + kj[p])
    side, side_in, side_out, side_shapes = _side_cast_plan(side_weights, side_layer, batch * npairs,
                                                           lambda b, p, qi, kj: b * npairs + p)
    grid_spec = pltpu.PrefetchScalarGridSpec(
        num_scalar_prefetch=2,
        grid=(batch, npairs),
        in_specs=[pl.BlockSpec((2 * hn, tq), qcol),
                  pl.BlockSpec((tk, 2 * hn), krow),
                  pl.BlockSpec((heads * V_ROWS, tk), kcol),
                  pl.BlockSpec((tq, d), qrow),
                  pl.BlockSpec(wot.shape, lambda b, p, qi, kj: (0, 0))] + side_in,
        out_specs=[pl.BlockSpec((tq, d), qrow)] + side_out,
        scratch_shapes=[pltpu.VMEM((heads, 1, tq), F32), pltpu.VMEM((heads * V_ROWS, tq), F32)],
    )
    outs = pl.pallas_call(
        functools.partial(_attn_kernel, heads=heads, tq=tq, tk=tk, scale=scale, n_side=len(side)),
        grid_spec=grid_spec,
        out_shape=[jax.ShapeDtypeStruct((n, d), F32)] + side_shapes,
        compiler_params=_params("arbitrary", "arbitrary"),
        name="mla_attention",
    )(qi, kj, qt, k, vt, h, wot, *side)
    return outs[0], [o.reshape(w.shape[1:]) for o, w in zip(outs[1:], side_weights)]


def _store_row_tiles(ref, value, rows):
    slabs = value.shape[1] // LANES
    for s in range(slabs):
        ref[pl.ds(s, rows, stride=slabs), :] = value[:, s * LANES:(s + 1) * LANES]


def _load_row_tile_slab(ref, s, rows, slabs):
    return ref[pl.ds(s, rows, stride=slabs), :]


def _route_kernel(h_ref, g_ref, r_ref, u_ref, info_ref, info_t_ref, cnt_ref, carry, *, tr, n_exp):
    i = pl.program_id(0)

    @pl.when(i == 0)
    def _():
        carry[...] = jnp.zeros(carry.shape, F32)

    u = _rms(h_ref[...], g_ref[...])
    u_ref[...] = u.astype(BF16)
    logits = _dot_3pass(u, r_ref[...])
    lane = lax.broadcasted_iota(jnp.int32, (tr, LANES), 1)
    lg = jnp.where(lane < n_exp, logits, -jnp.inf)
    m1 = jnp.max(lg, axis=1, keepdims=True)
    i1 = jnp.min(jnp.where(lg == m1, lane, LANES), axis=1, keepdims=True)
    lg2 = jnp.where(lane == i1, -jnp.inf, lg)
    m2 = jnp.max(lg2, axis=1, keepdims=True)
    i2 = jnp.min(jnp.where(lg2 == m2, lane, LANES), axis=1, keepdims=True)
    e = jnp.exp(m2 - m1)
    g1 = 1.0 / (1.0 + e)
    g2 = e / (1.0 + e)
    sel1 = lane == i1
    sel2 = lane == i2
    onehot = jnp.where(sel1 | sel2, 1.0, 0.0)
    rr = lax.broadcasted_iota(jnp.int32, (tr, tr), 0)
    cc = lax.broadcasted_iota(jnp.int32, (tr, tr), 1)
    tri = jnp.where(cc < rr, 1.0, 0.0).astype(BF16)
    before = _dot(tri, onehot.astype(BF16)) + carry[0:1, :]
    rank1 = jnp.sum(jnp.where(sel1, before, 0.0), axis=1, keepdims=True)
    rank2 = jnp.sum(jnp.where(sel2, before, 0.0), axis=1, keepdims=True)
    carry[...] = carry[...] + jnp.sum(onehot, axis=0, keepdims=True)
    cols = (i1.astype(F32), i2.astype(F32), g1, g2, rank1, rank2)
    info = jnp.zeros((tr, LANES), F32)
    for k, col in enumerate(cols):
        info = jnp.where(lane == k, col, info)
    info_ref[...] = info
    info_t_ref[...] = info.T[0:SUBLANES, :]
    cnt_ref[...] = carry[...]


def _route(h, g, router_padded, n_exp):
    n, d = h.shape
    tr = min(TOKEN_TILE, n)
    slabs = d // LANES
    return pl.pallas_call(
        functools.partial(_route_kernel, tr=tr, n_exp=n_exp),
        grid=(n // tr,),
        in_specs=[pl.BlockSpec((tr, d), lambda i: (i, 0)),
                  pl.BlockSpec((1, d), lambda i: (0, 0)),
                  pl.BlockSpec((d, LANES), lambda i: (0, 0))],
        out_specs=[pl.BlockSpec((tr, d), lambda i: (i, 0)),
                   pl.BlockSpec((tr, LANES), lambda i: (i, 0)),
                   pl.BlockSpec((SUBLANES, tr), lambda i: (0, i)),
                   pl.BlockSpec((8, LANES), lambda i: (i, 0))],
        out_shape=[jax.ShapeDtypeStruct((n, d), BF16),
                   jax.ShapeDtypeStruct((n, LANES), F32),
                   jax.ShapeDtypeStruct((SUBLANES, n), F32),
                   jax.ShapeDtypeStruct((n // tr * 8, LANES), F32)],
        scratch_shapes=[pltpu.VMEM((8, LANES), F32)],
        compiler_params=_params("arbitrary"),
        name="moe_route",
    )(h, g, router_padded)


def _run_pieces(length, piece):
    rem = length & (RUN_CHUNK - 1)
    full = length - rem

    def body(c, carry):
        piece(c * RUN_CHUNK, RUN_CHUNK)
        return carry

    lax.fori_loop(0, lax.shift_right_logical(full, RUN_CHUNK.bit_length() - 1), body, 0)
    bit = RUN_CHUNK // 2
    while bit:
        done = rem & (RUN_CHUNK - 2 * bit)
        pl.when((rem & bit) != 0)(functools.partial(piece, full + done, bit))
        bit //= 2


def _dispatch_kernel(pad_start_ref, pad_cnt_ref, plan_ref, info_t_ref, u_ref, x_ref, zero_sc, stage, sem, wsem,
                     *, tile, slabs, n_exp):
    i = pl.program_id(0)
    slot = i % 2

    def each_pad_copy(action):
        for e in range(n_exp):
            def piece(off, nrows):
                d = pl.multiple_of((pad_start_ref[e] + off) * slabs, slabs)
                action(pltpu.make_async_copy(zero_sc.at[pl.ds(0, nrows * slabs)],
                                             x_ref.at[pl.ds(d, nrows * slabs)], sem))

            _run_pieces(pad_cnt_ref[e], piece)

    @pl.when(pl.program_id(0) == 0)
    def _():
        zero_sc[...] = jnp.zeros(zero_sc.shape, F32)
        each_pad_copy(lambda copy: copy.start())
        each_pad_copy(lambda copy: copy.wait())

        def chunk_copy(i):
            dst = pl.multiple_of((pad_start_ref[n_exp] + i * ZERO_CHUNK) * slabs, slabs)
            return pltpu.make_async_copy(zero_sc, x_ref.at[pl.ds(dst, ZERO_CHUNK * slabs)], sem)

        def fill_tail(i, c):
            chunk_copy(i).start()
            return c

        def drain_tail(i, c):
            chunk_copy(i).wait()
            return c

        lax.fori_loop(0, pad_cnt_ref[n_exp], fill_tail, 0)
        lax.fori_loop(0, pad_cnt_ref[n_exp], drain_tail, 0)

    def each_run_copy(block, buf, action):
        for e in range(n_exp):
            base = (block * PLAN_ROWS) * n_exp + e
            dst0, length, src0 = plan_ref[base], plan_ref[base + n_exp], plan_ref[base + 2 * n_exp]

            def piece(off, nrows):
                s = pl.multiple_of((src0 + off) * slabs, slabs)
                d = pl.multiple_of((dst0 + off) * slabs, slabs)
                action(pltpu.make_async_copy(stage.at[buf, pl.ds(s, nrows * slabs)],
                                             x_ref.at[pl.ds(d, nrows * slabs)], wsem.at[buf]))

            _run_pieces(length, piece)

    info_t = info_t_ref[...]
    shift_base = (i * PLAN_ROWS + 3) * n_exp
    r_idx = lax.broadcasted_iota(jnp.int32, (TOP_K * tile, tile), 0)
    onehot = jnp.zeros(r_idx.shape, F32)
    for k in range(TOP_K):
        ls = info_t[4 + k:5 + k, :] + _select_by_expert(info_t[k:k + 1, :], plan_ref, shift_base, n_exp)
        onehot = onehot + jnp.where(r_idx == ls.astype(jnp.int32), 1.0, 0.0)
    x = _dot(onehot.astype(BF16), u_ref[...])
    _store_row_tiles(stage.at[slot], x, TOP_K * tile)
    each_run_copy(i, slot, lambda copy: copy.start())

    @pl.when(i > 0)
    def _():
        each_run_copy(i - 1, 1 - slot, lambda copy: copy.wait())

    @pl.when(i == pl.num_programs(0) - 1)
    def _():
        each_run_copy(i, slot, lambda copy: copy.wait())


def _dispatch(u, info_t, plan, pad_start, pad_cnt, sorted_rows, tile, slabs):
    n, d = u.shape
    grid_spec = pltpu.PrefetchScalarGridSpec(
        num_scalar_prefetch=3,
        grid=(n // tile,),
        in_specs=[pl.BlockSpec((SUBLANES, tile), lambda i, ps, pc, pn: (0, i)),
                  pl.BlockSpec((tile, d), lambda i, ps, pc, pn: (i, 0))],
        out_specs=pl.BlockSpec(memory_space=pl.ANY),
        scratch_shapes=[pltpu.VMEM((ZERO_CHUNK * slabs, LANES), F32),
                        pltpu.VMEM((2, TOP_K * tile * slabs, LANES), F32),
                        pltpu.SemaphoreType.DMA(()), pltpu.SemaphoreType.DMA((2,))],
    )
    return pl.pallas_call(
        functools.partial(_dispatch_kernel, tile=tile, slabs=slabs, n_exp=pad_start.shape[0] - 1),
        grid_spec=grid_spec,
        out_shape=jax.ShapeDtypeStruct((sorted_rows * slabs, LANES), F32),
        compiler_params=_params("arbitrary"),
        name="moe_dispatch",
    )(pad_start, pad_cnt, plan, info_t, u)


def _moe_kernel(te_ref, nu_ref, x_ref, wg_ref, wu_ref, wd_ref, o_ref, x_sc, acc_sc, *, tm, slabs, nj):
    t = pl.program_id(0)
    j = pl.program_id(1)

    @pl.when(t < nu_ref[0])
    def _():
        @pl.when(j == 0)
        def _():
            for s in range(slabs):
                x_sc[:, s * LANES:(s + 1) * LANES] = _load_row_tile_slab(x_ref, s, tm, slabs).astype(BF16)

        def finish(r):
            o_ref[...] = r.astype(BF16)

        _accumulate_steps(j, nj, lambda: _swiglu_step(x_sc[...], wg_ref[0], wu_ref[0], wd_ref[0]), acc_sc, finish)

    @pl.when((t >= nu_ref[0]) & (j == 0))
    def _():
        o_ref[...] = jnp.zeros(o_ref.shape, BF16)


def _moe_experts(x_sorted, tile_expert, n_used, wg, wu, wd, tm, slabs):
    d = slabs * LANES
    rows = x_sorted.shape[0] // slabs
    f = wg.shape[2]
    tf = _ff_chunk(f)
    nj = f // tf
    tmap = lambda t, j, te, nu: (jnp.minimum(t, nu[0] - 1), 0)
    jsel = lambda t, j, nu: jnp.where(t < nu[0], j, nj - 1)
    grid_spec = pltpu.PrefetchScalarGridSpec(
        num_scalar_prefetch=2,
        grid=(rows // tm, nj),
        in_specs=[pl.BlockSpec((tm * slabs, LANES), tmap),
                  pl.BlockSpec((1, d, tf), lambda t, j, te, nu: (te[t], 0, jsel(t, j, nu))),
                  pl.BlockSpec((1, d, tf), lambda t, j, te, nu: (te[t], 0, jsel(t, j, nu))),
                  pl.BlockSpec((1, tf, d), lambda t, j, te, nu: (te[t], jsel(t, j, nu), 0))],
        out_specs=pl.BlockSpec((tm, d), lambda t, j, te, nu: (t, 0)),
        scratch_shapes=[pltpu.VMEM((tm, d), BF16), pltpu.VMEM((tm, d), F32)],
    )
    return pl.pallas_call(
        functools.partial(_moe_kernel, tm=tm, slabs=slabs, nj=nj),
        grid_spec=grid_spec,
        out_shape=jax.ShapeDtypeStruct((rows, d), BF16),
        compiler_params=_params("arbitrary", "arbitrary"),
        name="moe_experts",
    )(tile_expert, n_used, x_sorted, wg, wu, wd)


def _select_by_expert(expert, plan_ref, base, n_exp):
    out = jnp.zeros(expert.shape, F32)
    for e in range(n_exp):
        out = jnp.where(expert == float(e), plan_ref[base + e].astype(F32), out)
    return out


def _combine_stage_rows(tile, n_exp):
    max_chunks = (TOP_K * tile + n_exp * (BF16_ROWS - 1 + COMBINE_CHUNK - 1)) // COMBINE_CHUNK
    return -(-max_chunks * COMBINE_CHUNK // 256) * 256


def _combine_kernel(plan_ref, h_ref, info_ref, gf_ref, y_ref, o_ref, stage, sem, *, tile, n_exp, final_norm):
    i = pl.program_id(0)
    slot = i % 2

    def each_chunk(block, buf, action):
        for e in range(n_exp):
            base = (block * PLAN_ROWS) * n_exp + e

            def body(c, carry):
                src = pl.multiple_of(plan_ref[base] + c * COMBINE_CHUNK, BF16_ROWS)
                dst = pl.multiple_of(plan_ref[base + 2 * n_exp] + c * COMBINE_CHUNK, COMBINE_CHUNK)
                action(pltpu.make_async_copy(y_ref.at[pl.ds(src, COMBINE_CHUNK)],
                                             stage.at[buf, pl.ds(dst, COMBINE_CHUNK)], sem.at[buf]))
                return carry

            lax.fori_loop(0, plan_ref[base + n_exp], body, 0)

    @pl.when(i == 0)
    def _():
        stage[...] = jnp.zeros(stage.shape, BF16)
        each_chunk(i, slot, lambda copy: copy.start())

    @pl.when(i + 1 < pl.num_programs(0))
    def _():
        each_chunk(i + 1, 1 - slot, lambda copy: copy.start())

    each_chunk(i, slot, lambda copy: copy.wait())

    info = info_ref[...]
    shift_base = (i * PLAN_ROWS + 3) * n_exp
    col = lax.broadcasted_iota(jnp.int32, (tile, stage.shape[1]), 1)
    p = jnp.zeros(col.shape, F32)
    for k in range(TOP_K):
        loc = info[:, 4 + k:5 + k] + _select_by_expert(info[:, k:k + 1], plan_ref, shift_base, n_exp)
        p = p + jnp.where(col == loc.astype(jnp.int32), info[:, 2 + k:3 + k], 0.0)
    out = h_ref[...] + _dot(p.astype(BF16), stage[slot])
    if final_norm:
        out = _rms(out, gf_ref[...])
    o_ref[...] = out


def _combine(h, info, plan, y_sorted, g_final, final_norm, n_exp):
    n, d = h.shape
    tile = n // (plan.shape[0] // (PLAN_ROWS * n_exp))
    grid_spec = pltpu.PrefetchScalarGridSpec(
        num_scalar_prefetch=1,
        grid=(n // tile,),
        in_specs=[pl.BlockSpec((tile, d), lambda i, plan: (i, 0)),
                  pl.BlockSpec((tile, LANES), lambda i, plan: (i, 0)),
                  pl.BlockSpec((1, d), lambda i, plan: (0, 0)),
                  pl.BlockSpec(memory_space=pl.ANY)],
        out_specs=pl.BlockSpec((tile, d), lambda i, plan: (i, 0)),
        scratch_shapes=[pltpu.VMEM((2, _combine_stage_rows(tile, n_exp), d), BF16), pltpu.SemaphoreType.DMA((2,))],
    )
    return pl.pallas_call(
        functools.partial(_combine_kernel, tile=tile, n_exp=n_exp, final_norm=final_norm),
        grid_spec=grid_spec,
        out_shape=jax.ShapeDtypeStruct((n, d), F32),
        compiler_params=_params("arbitrary"),
        name="moe_combine",
    )(plan, h, info, g_final, y_sorted)


def _moe_layer(h, g, router, wg, wu, wd, g_final, final_norm):
    n, d = h.shape
    n_exp = router.shape[1]
    tm = min(TOKEN_TILE, n)
    router_padded = jnp.pad(router, ((0, 0), (0, LANES - n_exp)))
    u, info, info_t, cnt = _route(h, g, router_padded, n_exp)

    tile = min(TOKEN_TILE, n)
    n_blocks = n // tile
    after = cnt.reshape(n_blocks, 8, LANES)[:, 0, :n_exp].astype(jnp.int32)
    before = jnp.concatenate([jnp.zeros((1, n_exp), jnp.int32), after[:-1]])
    counts = after[-1]
    padded = (counts + tm - 1) // tm * tm
    ends = jnp.cumsum(padded)
    starts = ends - padded
    rows = TOP_K * n + n_exp * tm
    n_used = (ends[-1] // tm).astype(jnp.int32)
    tile_ids = jnp.minimum(jnp.arange(rows // tm, dtype=jnp.int32), n_used - 1)
    tile_expert = jnp.sum(tile_ids[:, None] * tm >= ends[None, :], axis=1).astype(jnp.int32)

    run_start = starts[None, :] + before
    run_len = after - before
    fetch_start = run_start // BF16_ROWS * BF16_ROWS
    n_chunks = jnp.where(run_len > 0, (run_start - fetch_start + run_len + COMBINE_CHUNK - 1) // COMBINE_CHUNK, 0)
    stage_off = COMBINE_CHUNK * (jnp.cumsum(n_chunks, axis=1) - n_chunks)
    plan = jnp.stack([fetch_start, n_chunks, stage_off, starts[None, :] + stage_off - fetch_start],
                     axis=1).reshape(-1).astype(jnp.int32)

    slabs = d // LANES
    assert slabs == 8
    local_off = jnp.cumsum(run_len, axis=1) - run_len
    dispatch_plan = jnp.stack([run_start, run_len, local_off, local_off - before],
                              axis=1).reshape(-1).astype(jnp.int32)
    assert tm % ZERO_CHUNK == 0
    pad_start = jnp.concatenate([starts + counts, ends[-1:]]).astype(jnp.int32)
    pad_cnt = jnp.concatenate([padded - counts, (rows - ends[-1:]) // ZERO_CHUNK]).astype(jnp.int32)
    x_sorted = _dispatch(u, info_t, dispatch_plan, pad_start, pad_cnt, rows, tile, slabs)
    y_sorted = _moe_experts(x_sorted, tile_expert, n_used.reshape(1), wg, wu, wd, tm, slabs)
    return _combine(h, info, plan, y_sorted, g_final, final_norm, n_exp)


def _mla_weights(wq_b, wkv_a, wkv_b, heads, lora):
    def padded(w):
        return jnp.concatenate([w, jnp.zeros(w.shape[:-1] + (LANES - ROPE,), w.dtype)], axis=-1)

    rq = wq_b.shape[0]
    wq = wq_b.reshape(rq, heads, NOPE + ROPE)
    wqb = jnp.concatenate([wq[..., :NOPE].reshape(rq, -1), padded(wq[..., NOPE:]).reshape(rq, -1)], axis=1)
    wkva = jnp.concatenate([wkv_a[:, :lora], padded(wkv_a[:, lora:])], axis=1)
    wkv = wkv_b.reshape(lora, heads, 2 * NOPE)
    wkvb = jnp.concatenate([wkv[..., :NOPE].reshape(lora, -1), wkv[..., NOPE:].reshape(lora, -1)], axis=1)
    return wqb.astype(BF16), wkva.astype(BF16), wkvb.astype(BF16)


def _cast_kernel(x_ref, o_ref):
    o_ref[...] = x_ref[0].astype(BF16)


def _to_bf16(w, layer):
    cols = w.shape[-1]
    w3 = w.reshape(w.shape[0], -1, cols)
    rows = w3.shape[1]
    cap = 256 if cols > 2048 else 1024
    tr = next(t for t in (1024, 512, 256, 128, 64, 32, 16, 8) if t <= cap and rows % t == 0)
    out = pl.pallas_call(
        _cast_kernel,
        grid=(rows // tr,),
        in_specs=[pl.BlockSpec((1, tr, cols), lambda i: (layer, i, 0))],
        out_specs=pl.BlockSpec((tr, cols), lambda i: (i, 0)),
        out_shape=jax.ShapeDtypeStruct((rows, cols), BF16),
        compiler_params=_params("arbitrary"),
        name="weights_to_bf16",
    )(w3)
    return out.reshape(w.shape[1:])


def _final_norm_kernel(h_ref, g_ref, o_ref):
    o_ref[...] = _rms(h_ref[...], g_ref[...])


def _final_norm(h, g):
    n, d = h.shape
    tm = min(TOKEN_TILE, n)
    return pl.pallas_call(
        _final_norm_kernel,
        grid=(n // tm,),
        in_specs=[pl.BlockSpec((tm, d), lambda i: (i, 0)), pl.BlockSpec((1, d), lambda i: (0, 0))],
        out_specs=pl.BlockSpec((tm, d), lambda i: (i, 0)),
        out_shape=jax.ShapeDtypeStruct((n, d), F32),
        compiler_params=_params("arbitrary"),
        name="final_norm",
    )(h, g)


@jax.jit
def _trunk(x, positions, norm_mix, norm_ffn, norm_final, pool_w, pool_scale,
           conv_w1, conv_b1, conv_dw, conv_bdw, conv_ln_g, conv_ln_b, conv_w2, conv_b2,
           mla_wq_a, mla_q_norm, mla_wq_b, mla_wkv_a, mla_kv_norm, mla_wkv_b, mla_wo,
           ffn_wg, ffn_wu, ffn_wd, moe_router, moe_wg, moe_wu, moe_wd):
    batch, seq, d = x.shape
    depth = norm_mix.shape[0]
    lora = mla_kv_norm.shape[1]
    heads = mla_wo.shape[1] // NOPE
    assert mla_wq_b.shape[2] == heads * (NOPE + ROPE) and mla_wkv_a.shape[2] == lora + ROPE
    n = batch * seq
    h = x.reshape(n, d)
    pos = positions.reshape(n, 1)
    row = lambda v: v.reshape(1, -1)

    half = jnp.arange(0, ROPE, 2, dtype=F32) / ROPE
    inv_freq = ROPE_THETA ** (-half)
    invf = row(jnp.concatenate([inv_freq, inv_freq, jnp.zeros((LANES - ROPE,), F32)]))
    sgn = row(jnp.concatenate([-jnp.ones((ROPE // 2,), F32), jnp.ones((ROPE // 2,), F32),
                               jnp.zeros((LANES - ROPE,), F32)]))

    moe_w = (moe_wg, moe_wu, moe_wd)
    moe_bf16 = {}
    ffn_bf16 = {}

    def moe_cast_job(i):
        nxt = i if i % 2 == 1 else i + 1
        if nxt < depth and nxt // 2 not in moe_bf16:
            return moe_w, nxt // 2
        return (), 0

    for i in range(depth):
        kind, j = i % 3, i // 3
        g = row(norm_mix[i])
        if kind == 0:
            jobs = [(w, f) for f in range(ffn_wg.shape[0]) for w in (ffn_wg, ffn_wu, ffn_wd)] if not ffn_bf16 else []
            h, cast = _pool_layer(h, g, pool_w[j].astype(BF16), row(pool_scale[j]), [w for w, _ in jobs],
                                  [f for _, f in jobs], batch=batch, seq=seq)
            for k in range(0, len(cast), 3):
                ffn_bf16[jobs[k][1]] = cast[k:k + 3]
        elif kind == 1:
            side_w, side_layer = moe_cast_job(i)
            h, cast = _conv_layer(h, g, conv_w1[j].astype(BF16), row(conv_b1[j]), conv_dw[j], row(conv_bdw[j]),
                                  row(conv_ln_g[j]), row(conv_ln_b[j]), conv_w2[j].astype(BF16), row(conv_b2[j]),
                                  side_w, side_layer, batch=batch, seq=seq)
            if cast:
                moe_bf16[side_layer] = cast
        else:
            wqb, wkva, wkvb = _mla_weights(mla_wq_b[j], mla_wkv_a[j], mla_wkv_b[j], heads, lora)
            side_w, side_layer = moe_cast_job(i)
            h, cast = _mla_layer(h, pos, g, mla_wq_a[j].astype(BF16), row(mla_q_norm[j]), wqb, wkva,
                                 row(mla_kv_norm[j]), wkvb, mla_wo[j].T.astype(BF16), invf, sgn,
                                 side_w, side_layer, batch=batch, seq=seq, heads=heads, lora=lora)
            if cast:
                moe_bf16[side_layer] = cast
        g = row(norm_ffn[i])
        f_idx = i // 2
        last = i == depth - 1
        if i % 2 == 0:
            wg, wu, wd = ffn_bf16.get(f_idx) or [_to_bf16(w, f_idx) for w in (ffn_wg, ffn_wu, ffn_wd)]
            h = _ffn_layer(h, g, wg, wu, wd)
            if last:
                h = _final_norm(h, row(norm_final))
        else:
            wg, wu, wd = moe_bf16.get(f_idx) or [_to_bf16(w, f_idx) for w in moe_w]
            h = _moe_layer(h, g, moe_router[f_idx], wg, wu, wd, row(norm_final), last)
    return h.reshape(batch, seq, d)


def kernel(x, positions, norm_mix, norm_ffn, norm_final, pool_w, pool_scale, conv_w1, conv_b1, conv_dw, conv_bdw, conv_ln_g, conv_ln_b, conv_w2, conv_b2, mla_wq_a, mla_q_norm, mla_wq_b, mla_wkv_a, mla_kv_norm, mla_wkv_b, mla_wo, ffn_wg, ffn_wu, ffn_wd, moe_router, moe_wg, moe_wu, moe_wd):
    return _trunk(x, positions, norm_mix, norm_ffn, norm_final, pool_w, pool_scale,
                  conv_w1, conv_b1, conv_dw, conv_bdw, conv_ln_g, conv_ln_b, conv_w2, conv_b2,
                  mla_wq_a, mla_q_norm, mla_wq_b, mla_wkv_a, mla_kv_norm, mla_wkv_b, mla_wo,
                  ffn_wg, ffn_wu, ffn_wd, moe_router, moe_wg, moe_wu, moe_wd)
```

```python
import functools

import jax
import jax.numpy as jnp
from jax import lax
from jax.experimental import pallas as pl
from jax.experimental.pallas import tpu as pltpu

F32 = jnp.float32
BF16 = jnp.bfloat16

NORM_EPS = 1e-6
LN_EPS = 1e-5
ROPE_THETA = 10000.0
LOG2E = 1.4426950408889634

POOL_WINDOWS = (2, 4, 8, 16)
POOL_HALO = 16
CONV_HALO = 32
LANES = 128
BF16_ROWS = 16
TOKEN_TILE = 512
CONV_TILE = 256
WIDE_TILE = 1024
SUBLANES = 8
POOL_TOP = POOL_HALO + SUBLANES
NOPE = 128
ROPE = 64
V_ROWS = NOPE + 16
TOP_K = 2
ZERO_CHUNK = 64
COMBINE_CHUNK = 32
RUN_CHUNK = 64
PLAN_ROWS = 4
SIDE_JOBS_PER_HOST = 4

VMEM_LIMIT = 56 * 1024 * 1024


def _params(*sem):
    return pltpu.CompilerParams(dimension_semantics=sem, vmem_limit_bytes=VMEM_LIMIT)


def _rms(x, g, eps=NORM_EPS):
    return x * lax.rsqrt(jnp.mean(x * x, axis=-1, keepdims=True) + eps) * g


def _dot(a, b):
    return jnp.dot(a, b, preferred_element_type=F32)


def _split_bf16(x):
    hi = x.astype(BF16)
    return hi, (x - hi.astype(F32)).astype(BF16)


def _dot_3pass(a, b):
    a_hi, a_lo = _split_bf16(a)
    b_hi, b_lo = _split_bf16(b)
    return _dot(a_hi, b_hi) + (_dot(a_hi, b_lo) + _dot(a_lo, b_hi))


def _ff_chunk(f):
    for c in (1792, 1024, 512, 256, 128):
        if f % c == 0:
            return c
    return f


def _pool_kernel(h_ref, g_ref, w_ref, sc_ref, *rest, ts, gdim, n_side):
    side_in, o_ref, side_out, (ubuf, lvl_a, lvl_b) = rest[:n_side], rest[n_side], rest[n_side + 1:-3], rest[-3:]
    _side_cast(side_in, side_out)
    j = pl.program_id(1)
    h = h_ref[...]
    u = _rms(h, g_ref[...])
    ext = POOL_TOP + ts

    @pl.when(j == 0)
    def _():
        ubuf[0:POOL_TOP, :] = jnp.zeros((POOL_TOP, ubuf.shape[1]), F32)
        lvl_a[0:SUBLANES, :] = jnp.zeros((SUBLANES, gdim), F32)
        lvl_b[0:SUBLANES, :] = jnp.zeros((SUBLANES, gdim), F32)

    ubuf[POOL_TOP:ext, :] = u
    t1 = lax.broadcasted_iota(jnp.int32, (ts, 1), 0) + (j * ts + 1)
    for g, win in enumerate(POOL_WINDOWS):
        cs = slice(g * gdim, (g + 1) * gdim)
        src, cols, spare, k = ubuf, cs, [lvl_a, lvl_b], 1
        while k < win:
            s = src[SUBLANES:ext, cols] + src[SUBLANES - k:ext - k, cols]
            k *= 2
            if k < win:
                dst = spare.pop(0)
                dst[SUBLANES:ext, :] = s
                spare.append(dst)
                src, cols = dst, slice(None)
        s = s[POOL_HALO:POOL_HALO + ts]
        ug = u[:, cs]
        cnt = jnp.minimum(t1, win).astype(F32)
        pooled = s / cnt - ug
        m = _dot(pooled.astype(BF16), w_ref[g])
        o_ref[:, cs] = h[:, cs] + m * sc_ref[:, cs]
    ubuf[SUBLANES:POOL_TOP, :] = ubuf[ts + SUBLANES:ts + POOL_TOP, :]


def _pool_layer(h, g, w, scale, side_weights, side_layers, *, batch, seq):
    n, d = h.shape
    ts = min(WIDE_TILE, seq)
    nt = seq // ts
    gdim = d // len(POOL_WINDOWS)
    row = lambda b, j: (b * nt + j, 0)
    const2 = lambda b, j: (0, 0)
    side, side_in, side_out, side_shapes = _side_cast_plan(side_weights, list(side_layers), batch * nt,
                                                           lambda b, j: b * nt + j)
    outs = pl.pallas_call(
        functools.partial(_pool_kernel, ts=ts, gdim=gdim, n_side=len(side)),
        grid=(batch, nt),
        in_specs=[pl.BlockSpec((ts, d), row),
                  pl.BlockSpec((1, d), const2),
                  pl.BlockSpec(w.shape, lambda b, j: (0, 0, 0)),
                  pl.BlockSpec((1, d), const2)] + side_in,
        out_specs=[pl.BlockSpec((ts, d), row)] + side_out,
        out_shape=[jax.ShapeDtypeStruct((n, d), F32)] + side_shapes,
        scratch_shapes=[pltpu.VMEM((ts + POOL_TOP, d), F32), pltpu.VMEM((ts + POOL_TOP, gdim), F32),
                        pltpu.VMEM((ts + POOL_TOP, gdim), F32)],
        compiler_params=_params("arbitrary", "arbitrary"),
        name="pool_mixer",
    )(h, g, w, scale, *side)
    return outs[0], [o.reshape(w.shape[1:]) for o, w in zip(outs[1:], side_weights)]


def _swiglu_step(u, wg, wu, wd):
    a = _dot(u, wg)
    b = _dot(u, wu)
    hh = a * jax.nn.sigmoid(a) * b
    return _dot(hh.astype(BF16), wd)


def _accumulate_steps(j, nj, partial, acc_sc, finish):
    def run(first, last):
        r = partial()
        if not first:
            r = acc_sc[...] + r
        if last:
            finish(r)
        else:
            acc_sc[...] = r

    if nj == 1:
        run(True, True)
        return
    pl.when(j == 0)(functools.partial(run, True, False))
    if nj > 2:
        pl.when((j > 0) & (j < nj - 1))(functools.partial(run, False, False))
    pl.when(j == nj - 1)(functools.partial(run, False, True))


def _ffn_kernel(h_ref, g_ref, wg_ref, wu_ref, wd_ref, o_ref, u_sc, acc_sc, *, nj):
    j = pl.program_id(1)

    @pl.when(j == 0)
    def _():
        u_sc[...] = _rms(h_ref[...], g_ref[...]).astype(BF16)

    def finish(r):
        o_ref[...] = h_ref[...] + r

    _accumulate_steps(j, nj, lambda: _swiglu_step(u_sc[...], wg_ref[...], wu_ref[...], wd_ref[...]), acc_sc, finish)


def _ffn_layer(h, g, wg, wu, wd):
    n, d = h.shape
    f = wg.shape[1]
    tm = min(TOKEN_TILE, n)
    tf = _ff_chunk(f)
    return pl.pallas_call(
        functools.partial(_ffn_kernel, nj=f // tf),
        grid=(n // tm, f // tf),
        in_specs=[pl.BlockSpec((tm, d), lambda i, j: (i, 0)),
                  pl.BlockSpec((1, d), lambda i, j: (0, 0)),
                  pl.BlockSpec((d, tf), lambda i, j: (0, j)),
                  pl.BlockSpec((d, tf), lambda i, j: (0, j)),
                  pl.BlockSpec((tf, d), lambda i, j: (j, 0))],
        out_specs=pl.BlockSpec((tm, d), lambda i, j: (i, 0)),
        out_shape=jax.ShapeDtypeStruct((n, d), F32),
        scratch_shapes=[pltpu.VMEM((tm, d), BF16), pltpu.VMEM((tm, d), F32)],
        compiler_params=_params("arbitrary", "arbitrary"),
        name="dense_swiglu",
    )(h, g, wg, wu, wd)


def _side_cast_plan(weights, layer, steps, step_index):
    arrays = [w.reshape(w.shape[0], -1, w.shape[-1]) for w in weights]
    if not arrays:
        return [], [], [], []
    nblk = next(n for n in range(steps, 0, -1)
                if all(a.shape[1] % n == 0 and (a.shape[1] // n) % BF16_ROWS == 0 for a in arrays))
    blk = lambda *g: jnp.minimum(step_index(*g), nblk - 1)
    layers = layer if isinstance(layer, (list, tuple)) else [layer] * len(arrays)
    in_specs, out_specs, out_shapes = [], [], []
    for a, lyr in zip(arrays, layers):
        rows, cols = a.shape[1:]
        in_specs.append(pl.BlockSpec((1, rows // nblk, cols), lambda *g, _l=lyr: (_l, blk(*g), 0)))
        out_specs.append(pl.BlockSpec((rows // nblk, cols), lambda *g: (blk(*g), 0)))
        out_shapes.append(jax.ShapeDtypeStruct((rows, cols), BF16))
    return arrays, in_specs, out_specs, out_shapes


def _side_cast(in_refs, out_refs):
    for src, dst in zip(in_refs, out_refs):
        dst[...] = src[0].astype(BF16)


def _conv_kernel(h_ref, g_ref, w1_ref, b1_ref, dw_ref, bdw_ref, lng_ref, lnb_ref, w2_ref, b2_ref, *rest,
                 ts, width, rb, n_side):
    side_in, o_ref, side_out, (gbuf, cbuf) = rest[:n_side], rest[n_side], rest[n_side + 1:-2], rest[-2:]
    _side_cast(side_in, side_out)
    j = pl.program_id(1)
    d = h_ref.shape[1]
    h = h_ref[...]
    u = _rms(h, g_ref[...]).astype(BF16)
    a = _dot(u, w1_ref[...]) + b1_ref[...]
    glu = a[:, :d] * jax.nn.sigmoid(a[:, d:])

    keep = CONV_HALO + SUBLANES

    @pl.when(j == 0)
    def _():
        gbuf[:, 0:keep, :] = jnp.zeros((SUBLANES, keep, d), F32)

    for b in range(SUBLANES):
        gbuf[b, CONV_HALO + b:CONV_HALO + b + ts, :] = glu
    for cb in range(d // LANES):
        cs = slice(cb * LANES, (cb + 1) * LANES)
        for r0 in range(0, ts, rb):
            acc = jnp.zeros((rb, LANES), F32)
            for k in range(width):
                a, b = divmod(width - 1 - k, SUBLANES)
                off = CONV_HALO + r0 - SUBLANES * a
                acc = acc + gbuf[b, off:off + rb, cs] * dw_ref[k:k + 1, cs]
            cbuf[r0:r0 + rb, cs] = acc + bdw_ref[:, cs]
    gbuf[:, 0:keep, :] = gbuf[:, ts:ts + keep, :]

    c = cbuf[...]
    mu = jnp.mean(c, axis=-1, keepdims=True)
    cc = c - mu
    var = jnp.mean(cc * cc, axis=-1, keepdims=True)
    un = cc * lax.rsqrt(var + LN_EPS) * lng_ref[...] + lnb_ref[...]
    un = un * jax.nn.sigmoid(un)
    o_ref[...] = h + _dot(un.astype(BF16), w2_ref[...]) + b2_ref[...]


def _conv_layer(h, g, w1, b1, dw, bdw, lng, lnb, w2, b2, side_weights, side_layer, *, batch, seq):
    n, d = h.shape
    width = dw.shape[0]
    assert width - 1 <= CONV_HALO
    ts = min(CONV_TILE, seq)
    rb = min(128, ts)
    nt = seq // ts
    row = lambda b, j: (b * nt + j, 0)
    c2 = lambda b, j: (0, 0)
    side, side_in, side_out, side_shapes = _side_cast_plan(side_weights, side_layer, batch * nt,
                                                           lambda b, j: b * nt + j)
    outs = pl.pallas_call(
        functools.partial(_conv_kernel, ts=ts, width=width, rb=rb, n_side=len(side)),
        grid=(batch, nt),
        in_specs=[pl.BlockSpec((ts, d), row),
                  pl.BlockSpec((1, d), c2),
                  pl.BlockSpec(w1.shape, c2),
                  pl.BlockSpec(b1.shape, c2),
                  pl.BlockSpec(dw.shape, c2),
                  pl.BlockSpec((1, d), c2),
                  pl.BlockSpec((1, d), c2),
                  pl.BlockSpec((1, d), c2),
                  pl.BlockSpec(w2.shape, c2),
                  pl.BlockSpec((1, d), c2)] + side_in,
        out_specs=[pl.BlockSpec((ts, d), row)] + side_out,
        out_shape=[jax.ShapeDtypeStruct((n, d), F32)] + side_shapes,
        scratch_shapes=[pltpu.VMEM((SUBLANES, ts + CONV_HALO + SUBLANES, d), F32), pltpu.VMEM((ts, d), F32)],
        compiler_params=_params("arbitrary", "arbitrary"),
        name="conv_module",
    )(h, g, w1, b1, dw, bdw, lng, lnb, w2, b2, *side)
    return outs[0], [o.reshape(w.shape[1:]) for o, w in zip(outs[1:], side_weights)]


def _mla_proj_kernel(h_ref, pos_ref, g_ref, wqa_ref, qnorm_ref, wqb_ref, wkva_ref, kvnorm_ref, wkvb_ref,
                     invf_ref, sgn_ref, qt_o, k_o, vt_o, *, heads, lora):
    hn = heads * NOPE
    rows = h_ref.shape[0]
    u = _rms(h_ref[...], g_ref[...]).astype(BF16)
    qa = _rms(_dot(u, wqa_ref[...]), qnorm_ref[...]).astype(BF16)
    q = _dot(qa, wqb_ref[...])
    kva = _dot(u, wkva_ref[...])
    ckv = _rms(kva[:, :lora], kvnorm_ref[...]).astype(BF16)
    kv = _dot(ckv, wkvb_ref[...])
    ang = pos_ref[...].astype(F32) * invf_ref[...]
    c = jnp.cos(ang)
    s = jnp.sin(ang) * sgn_ref[...]
    lane = lax.broadcasted_iota(jnp.int32, (rows, LANES), 1)

    def rope(x):
        swapped = jnp.where(lane < ROPE // 2, pltpu.roll(x, LANES - ROPE // 2, 1), pltpu.roll(x, ROPE // 2, 1))
        return x * c + swapped * s

    kr = rope(kva[:, lora:lora + LANES]).astype(BF16)
    ones = jnp.ones((V_ROWS - NOPE, rows), BF16)
    for hd in range(heads):
        ns = slice(hd * NOPE, (hd + 1) * NOPE)
        qr = rope(q[:, hn + hd * LANES:hn + (hd + 1) * LANES])
        qt_o[2 * hd * NOPE:(2 * hd + 1) * NOPE, :] = q[:, ns].T.astype(BF16)
        qt_o[(2 * hd + 1) * NOPE:(2 * hd + 2) * NOPE, :] = qr.T.astype(BF16)
        k_o[:, 2 * hd * NOPE:(2 * hd + 1) * NOPE] = kv[:, ns].astype(BF16)
        k_o[:, (2 * hd + 1) * NOPE:(2 * hd + 2) * NOPE] = kr
        vt_o[hd * V_ROWS:hd * V_ROWS + NOPE, :] = kv[:, hn + hd * NOPE:hn + (hd + 1) * NOPE].T.astype(BF16)
        vt_o[hd * V_ROWS + NOPE:(hd + 1) * V_ROWS, :] = ones


def _attn_kernel(qi_ref, kj_ref, qt_ref, k_ref, vt_ref, h_ref, wot_ref, *rest, heads, tq, tk, scale, n_side):
    side_in, o_ref, side_out, (m_sc, acc_sc) = rest[:n_side], rest[n_side], rest[n_side + 1:-2], rest[-2:]
    _side_cast(side_in, side_out)
    i = qi_ref[pl.program_id(1)]
    j = kj_ref[pl.program_id(1)]
    c = scale * LOG2E

    @pl.when(j == 0)
    def _():
        m_sc[...] = jnp.full(m_sc.shape, -jnp.inf, F32)
        acc_sc[...] = jnp.zeros(acc_sc.shape, F32)

    def step(diagonal):
        if diagonal:
            causal = (lax.broadcasted_iota(jnp.int32, (tk, tq), 0)
                      <= lax.broadcasted_iota(jnp.int32, (tk, tq), 1))
        es = [slice(2 * hd * NOPE, (2 * hd + 2) * NOPE) for hd in range(heads)]
        vs = [slice(hd * V_ROWS, (hd + 1) * V_ROWS) for hd in range(heads)]
        ss = [_dot(k_ref[:, e], qt_ref[e, :]) for e in es]
        if diagonal:
            ss = [jnp.where(causal, s, -jnp.inf) for s in ss]
        m_prev = [m_sc[hd] for hd in range(heads)]
        m_new = [jnp.maximum(mp, jnp.max(s, axis=0, keepdims=True)) for mp, s in zip(m_prev, ss)]
        ps = [jnp.exp2((s - mn) * c).astype(BF16) for s, mn in zip(ss, m_new)]
        pv = [_dot(vt_ref[v, :], p) for p, v in zip(ps, vs)]
        for hd in range(heads):
            alpha = jnp.exp2((m_prev[hd] - m_new[hd]) * c)
            acc_sc[vs[hd], :] = alpha * acc_sc[vs[hd], :] + pv[hd]
            m_sc[hd] = m_new[hd]

    @pl.when(j < i)
    def _():
        step(False)

    @pl.when(j == i)
    def _():
        step(True)
        parts = [acc_sc[hd * V_ROWS:hd * V_ROWS + NOPE, :] / acc_sc[hd * V_ROWS + NOPE:hd * V_ROWS + NOPE + 1, :]
                 for hd in range(heads)]
        ot = jnp.concatenate(parts, axis=0).astype(BF16)
        o_ref[...] = h_ref[...] + _dot(wot_ref[...], ot).T


def _mla_layer(h, positions, g, wqa, qnorm, wqb, wkva, kvnorm, wkvb, wot, invf, sgn, side_weights, side_layer,
               *, batch, seq, heads, lora):
    n, d = h.shape
    hn = heads * NOPE
    tp = min(WIDE_TILE, n)
    c2 = lambda i: (0, 0)
    row = lambda i: (i, 0)
    col = lambda i: (0, i)
    qt, k, vt = pl.pallas_call(
        functools.partial(_mla_proj_kernel, heads=heads, lora=lora),
        grid=(n // tp,),
        in_specs=[pl.BlockSpec((tp, d), row),
                  pl.BlockSpec((tp, 1), row),
                  pl.BlockSpec((1, d), c2),
                  pl.BlockSpec(wqa.shape, c2),
                  pl.BlockSpec(qnorm.shape, c2),
                  pl.BlockSpec(wqb.shape, c2),
                  pl.BlockSpec(wkva.shape, c2),
                  pl.BlockSpec(kvnorm.shape, c2),
                  pl.BlockSpec(wkvb.shape, c2),
                  pl.BlockSpec((1, LANES), c2),
                  pl.BlockSpec((1, LANES), c2)],
        out_specs=[pl.BlockSpec((2 * hn, tp), col), pl.BlockSpec((tp, 2 * hn), row),
                   pl.BlockSpec((heads * V_ROWS, tp), col)],
        out_shape=[jax.ShapeDtypeStruct((2 * hn, n), BF16), jax.ShapeDtypeStruct((n, 2 * hn), BF16),
                   jax.ShapeDtypeStruct((heads * V_ROWS, n), BF16)],
        compiler_params=_params("arbitrary"),
        name="mla_proj",
    )(h, positions, g, wqa, qnorm, wqb, wkva, kvnorm, wkvb, invf, sgn)

    tq = tk = min(TOKEN_TILE, seq)
    nq = seq // tq
    scale = float(NOPE + ROPE) ** -0.5
    pairs = [(i, j) for i in range(nq) for j in range(i + 1)]
    qi = jnp.asarray([p[0] for p in pairs], jnp.int32)
    kj = jnp.asarray([p[1] for p in pairs], jnp.int32)
    npairs = len(pairs)
    qrow = lambda b, p, qi, kj: (b * nq + qi[p], 0)
    qcol = lambda b, p, qi, kj: (0, b * nq + qi[p])
    krow = lambda b, p, qi, kj: (b * nq + kj[p], 0)
    kcol = lambda b, p, qi, kj: (0, b * nq + kj[p])
    side, side_in, side_out, side_shapes = _side_cast_plan(side_weights, side_layer, batch * npairs,
                                                           lambda b, p, qi, kj: b * npairs + p)
    grid_spec = pltpu.PrefetchScalarGridSpec(
        num_scalar_prefetch=2,
        grid=(batch, npairs),
        in_specs=[pl.BlockSpec((2 * hn, tq), qcol),
                  pl.BlockSpec((tk, 2 * hn), krow),
                  pl.BlockSpec((heads * V_ROWS, tk), kcol),
                  pl.BlockSpec((tq, d), qrow),
                  pl.BlockSpec(wot.shape, lambda b, p, qi, kj: (0, 0))] + side_in,
        out_specs=[pl.BlockSpec((tq, d), qrow)] + side_out,
        scratch_shapes=[pltpu.VMEM((heads, 1, tq), F32), pltpu.VMEM((heads * V_ROWS, tq), F32)],
    )
    outs = pl.pallas_call(
        functools.partial(_attn_kernel, heads=heads, tq=tq, tk=tk, scale=scale, n_side=len(side)),
        grid_spec=grid_spec,
        out_shape=[jax.ShapeDtypeStruct((n, d), F32)] + side_shapes,
        compiler_params=_params("arbitrary", "arbitrary"),
        name="mla_attention",
    )(qi, kj, qt, k, vt, h, wot, *side)
    return outs[0], [o.reshape(w.shape[1:]) for o, w in zip(outs[1:], side_weights)]


def _store_row_tiles(ref, value, rows):
    slabs = value.shape[1] // LANES
    for s in range(slabs):
        ref[pl.ds(s, rows, stride=slabs), :] = value[:, s * LANES:(s + 1) * LANES]


def _load_row_tile_slab(ref, s, rows, slabs):
    return ref[pl.ds(s, rows, stride=slabs), :]


def _route_kernel(h_ref, g_ref, r_ref, u_ref, info_ref, info_t_ref, cnt_ref, carry, *, tr, n_exp):
    i = pl.program_id(0)

    @pl.when(i == 0)
    def _():
        carry[...] = jnp.zeros(carry.shape, F32)

    u = _rms(h_ref[...], g_ref[...])
    u_ref[...] = u.astype(BF16)
    logits = _dot_3pass(u, r_ref[...])
    lane = lax.broadcasted_iota(jnp.int32, (tr, LANES), 1)
    lg = jnp.where(lane < n_exp, logits, -jnp.inf)
    m1 = jnp.max(lg, axis=1, keepdims=True)
    i1 = jnp.min(jnp.where(lg == m1, lane, LANES), axis=1, keepdims=True)
    lg2 = jnp.where(lane == i1, -jnp.inf, lg)
    m2 = jnp.max(lg2, axis=1, keepdims=True)
    i2 = jnp.min(jnp.where(lg2 == m2, lane, LANES), axis=1, keepdims=True)
    e = jnp.exp(m2 - m1)
    g1 = 1.0 / (1.0 + e)
    g2 = e / (1.0 + e)
    sel1 = lane == i1
    sel2 = lane == i2
    onehot = jnp.where(sel1 | sel2, 1.0, 0.0)
    rr = lax.broadcasted_iota(jnp.int32, (tr, tr), 0)
    cc = lax.broadcasted_iota(jnp.int32, (tr, tr), 1)
    tri = jnp.where(cc < rr, 1.0, 0.0).astype(BF16)
    before = _dot(tri, onehot.astype(BF16)) + carry[0:1, :]
    rank1 = jnp.sum(jnp.where(sel1, before, 0.0), axis=1, keepdims=True)
    rank2 = jnp.sum(jnp.where(sel2, before, 0.0), axis=1, keepdims=True)
    carry[...] = carry[...] + jnp.sum(onehot, axis=0, keepdims=True)
    cols = (i1.astype(F32), i2.astype(F32), g1, g2, rank1, rank2)
    info = jnp.zeros((tr, LANES), F32)
    for k, col in enumerate(cols):
        info = jnp.where(lane == k, col, info)
    info_ref[...] = info
    info_t_ref[...] = info.T[0:SUBLANES, :]
    cnt_ref[...] = carry[...]


def _route(h, g, router_padded, n_exp):
    n, d = h.shape
    tr = min(TOKEN_TILE, n)
    slabs = d // LANES
    return pl.pallas_call(
        functools.partial(_route_kernel, tr=tr, n_exp=n_exp),
        grid=(n // tr,),
        in_specs=[pl.BlockSpec((tr, d), lambda i: (i, 0)),
                  pl.BlockSpec((1, d), lambda i: (0, 0)),
                  pl.BlockSpec((d, LANES), lambda i: (0, 0))],
        out_specs=[pl.BlockSpec((tr, d), lambda i: (i, 0)),
                   pl.BlockSpec((tr, LANES), lambda i: (i, 0)),
                   pl.BlockSpec((SUBLANES, tr), lambda i: (0, i)),
                   pl.BlockSpec((8, LANES), lambda i: (i, 0))],
        out_shape=[jax.ShapeDtypeStruct((n, d), BF16),
                   jax.ShapeDtypeStruct((n, LANES), F32),
                   jax.ShapeDtypeStruct((SUBLANES, n), F32),
                   jax.ShapeDtypeStruct((n // tr * 8, LANES), F32)],
        scratch_shapes=[pltpu.VMEM((8, LANES), F32)],
        compiler_params=_params("arbitrary"),
        name="moe_route",
    )(h, g, router_padded)


def _run_pieces(length, piece):
    rem = length & (RUN_CHUNK - 1)
    full = length - rem

    def body(c, carry):
        piece(c * RUN_CHUNK, RUN_CHUNK)
        return carry

    lax.fori_loop(0, lax.shift_right_logical(full, RUN_CHUNK.bit_length() - 1), body, 0)
    bit = RUN_CHUNK // 2
    while bit:
        done = rem & (RUN_CHUNK - 2 * bit)
        pl.when((rem & bit) != 0)(functools.partial(piece, full + done, bit))
        bit //= 2


def _dispatch_kernel(pad_start_ref, pad_cnt_ref, plan_ref, info_t_ref, u_ref, x_ref, zero_sc, stage, sem, wsem,
                     *, tile, slabs, n_exp):
    i = pl.program_id(0)
    slot = i % 2

    def each_pad_copy(action):
        for e in range(n_exp):
            def piece(off, nrows):
                d = pl.multiple_of((pad_start_ref[e] + off) * slabs, slabs)
                action(pltpu.make_async_copy(zero_sc.at[pl.ds(0, nrows * slabs)],
                                             x_ref.at[pl.ds(d, nrows * slabs)], sem))

            _run_pieces(pad_cnt_ref[e], piece)

    @pl.when(pl.program_id(0) == 0)
    def _():
        zero_sc[...] = jnp.zeros(zero_sc.shape, F32)
        each_pad_copy(lambda copy: copy.start())
        each_pad_copy(lambda copy: copy.wait())

        def chunk_copy(i):
            dst = pl.multiple_of((pad_start_ref[n_exp] + i * ZERO_CHUNK) * slabs, slabs)
            return pltpu.make_async_copy(zero_sc, x_ref.at[pl.ds(dst, ZERO_CHUNK * slabs)], sem)

        def fill_tail(i, c):
            chunk_copy(i).start()
            return c

        def drain_tail(i, c):
            chunk_copy(i).wait()
            return c

        lax.fori_loop(0, pad_cnt_ref[n_exp], fill_tail, 0)
        lax.fori_loop(0, pad_cnt_ref[n_exp], drain_tail, 0)

    def each_run_copy(block, buf, action):
        for e in range(n_exp):
            base = (block * PLAN_ROWS) * n_exp + e
            dst0, length, src0 = plan_ref[base], plan_ref[base + n_exp], plan_ref[base + 2 * n_exp]

            def piece(off, nrows):
                s = pl.multiple_of((src0 + off) * slabs, slabs)
                d = pl.multiple_of((dst0 + off) * slabs, slabs)
                action(pltpu.make_async_copy(stage.at[buf, pl.ds(s, nrows * slabs)],
                                             x_ref.at[pl.ds(d, nrows * slabs)], wsem.at[buf]))

            _run_pieces(length, piece)

    info_t = info_t_ref[...]
    shift_base = (i * PLAN_ROWS + 3) * n_exp
    r_idx = lax.broadcasted_iota(jnp.int32, (TOP_K * tile, tile), 0)
    onehot = jnp.zeros(r_idx.shape, F32)
    for k in range(TOP_K):
        ls = info_t[4 + k:5 + k, :] + _select_by_expert(info_t[k:k + 1, :], plan_ref, shift_base, n_exp)
        onehot = onehot + jnp.where(r_idx == ls.astype(jnp.int32), 1.0, 0.0)
    x = _dot(onehot.astype(BF16), u_ref[...])
    _store_row_tiles(stage.at[slot], x, TOP_K * tile)
    each_run_copy(i, slot, lambda copy: copy.start())

    @pl.when(i > 0)
    def _():
        each_run_copy(i - 1, 1 - slot, lambda copy: copy.wait())

    @pl.when(i == pl.num_programs(0) - 1)
    def _():
        each_run_copy(i, slot, lambda copy: copy.wait())


def _dispatch(u, info_t, plan, pad_start, pad_cnt, sorted_rows, tile, slabs):
    n, d = u.shape
    grid_spec = pltpu.PrefetchScalarGridSpec(
        num_scalar_prefetch=3,
        grid=(n // tile,),
        in_specs=[pl.BlockSpec((SUBLANES, tile), lambda i, ps, pc, pn: (0, i)),
                  pl.BlockSpec((tile, d), lambda i, ps, pc, pn: (i, 0))],
        out_specs=pl.BlockSpec(memory_space=pl.ANY),
        scratch_shapes=[pltpu.VMEM((ZERO_CHUNK * slabs, LANES), F32),
                        pltpu.VMEM((2, TOP_K * tile * slabs, LANES), F32),
                        pltpu.SemaphoreType.DMA(()), pltpu.SemaphoreType.DMA((2,))],
    )
    return pl.pallas_call(
        functools.partial(_dispatch_kernel, tile=tile, slabs=slabs, n_exp=pad_start.shape[0] - 1),
        grid_spec=grid_spec,
        out_shape=jax.ShapeDtypeStruct((sorted_rows * slabs, LANES), F32),
        compiler_params=_params("arbitrary"),
        name="moe_dispatch",
    )(pad_start, pad_cnt, plan, info_t, u)


def _moe_kernel(te_ref, nu_ref, x_ref, wg_ref, wu_ref, wd_ref, o_ref, x_sc, acc_sc, *, tm, slabs, nj):
    t = pl.program_id(0)
    j = pl.program_id(1)

    @pl.when(t < nu_ref[0])
    def _():
        @pl.when(j == 0)
        def _():
            for s in range(slabs):
                x_sc[:, s * LANES:(s + 1) * LANES] = _load_row_tile_slab(x_ref, s, tm, slabs).astype(BF16)

        def finish(r):
            o_ref[...] = r.astype(BF16)

        _accumulate_steps(j, nj, lambda: _swiglu_step(x_sc[...], wg_ref[0], wu_ref[0], wd_ref[0]), acc_sc, finish)

    @pl.when((t >= nu_ref[0]) & (j == 0))
    def _():
        o_ref[...] = jnp.zeros(o_ref.shape, BF16)


def _moe_experts(x_sorted, tile_expert, n_used, wg, wu, wd, tm, slabs):
    d = slabs * LANES
    rows = x_sorted.shape[0] // slabs
    f = wg.shape[2]
    tf = _ff_chunk(f)
    nj = f // tf
    tmap = lambda t, j, te, nu: (jnp.minimum(t, nu[0] - 1), 0)
    jsel = lambda t, j, nu: jnp.where(t < nu[0], j, nj - 1)
    grid_spec = pltpu.PrefetchScalarGridSpec(
        num_scalar_prefetch=2,
        grid=(rows // tm, nj),
        in_specs=[pl.BlockSpec((tm * slabs, LANES), tmap),
                  pl.BlockSpec((1, d, tf), lambda t, j, te, nu: (te[t], 0, jsel(t, j, nu))),
                  pl.BlockSpec((1, d, tf), lambda t, j, te, nu: (te[t], 0, jsel(t, j, nu))),
                  pl.BlockSpec((1, tf, d), lambda t, j, te, nu: (te[t], jsel(t, j, nu), 0))],
        out_specs=pl.BlockSpec((tm, d), lambda t, j, te, nu: (t, 0)),
        scratch_shapes=[pltpu.VMEM((tm, d), BF16), pltpu.VMEM((tm, d), F32)],
    )
    return pl.pallas_call(
        functools.partial(_moe_kernel, tm=tm, slabs=slabs, nj=nj),
        grid_spec=grid_spec,
        out_shape=jax.ShapeDtypeStruct((rows, d), BF16),
        compiler_params=_params("arbitrary", "arbitrary"),
        name="moe_experts",
    )(tile_expert, n_used, x_sorted, wg, wu, wd)


def _select_by_expert(expert, plan_ref, base, n_exp):
    out = jnp.zeros(expert.shape, F32)
    for e in range(n_exp):
        out = jnp.where(expert == float(e), plan_ref[base + e].astype(F32), out)
    return out


def _combine_stage_rows(tile, n_exp):
    max_chunks = (TOP_K * tile + n_exp * (BF16_ROWS - 1 + COMBINE_CHUNK - 1)) // COMBINE_CHUNK
    return -(-max_chunks * COMBINE_CHUNK // 256) * 256


def _combine_kernel(plan_ref, h_ref, info_ref, gf_ref, y_ref, o_ref, stage, sem, *, tile, n_exp, final_norm):
    i = pl.program_id(0)
    slot = i % 2

    def each_chunk(block, buf, action):
        for e in range(n_exp):
            base = (block * PLAN_ROWS) * n_exp + e

            def body(c, carry):
                src = pl.multiple_of(plan_ref[base] + c * COMBINE_CHUNK, BF16_ROWS)
                dst = pl.multiple_of(plan_ref[base + 2 * n_exp] + c * COMBINE_CHUNK, COMBINE_CHUNK)
                action(pltpu.make_async_copy(y_ref.at[pl.ds(src, COMBINE_CHUNK)],
                                             stage.at[buf, pl.ds(dst, COMBINE_CHUNK)], sem.at[buf]))
                return carry

            lax.fori_loop(0, plan_ref[base + n_exp], body, 0)

    @pl.when(i == 0)
    def _():
        stage[...] = jnp.zeros(stage.shape, BF16)
        each_chunk(i, slot, lambda copy: copy.start())

    @pl.when(i + 1 < pl.num_programs(0))
    def _():
        each_chunk(i + 1, 1 - slot, lambda copy: copy.start())

    each_chunk(i, slot, lambda copy: copy.wait())

    info = info_ref[...]
    shift_base = (i * PLAN_ROWS + 3) * n_exp
    col = lax.broadcasted_iota(jnp.int32, (tile, stage.shape[1]), 1)
    p = jnp.zeros(col.shape, F32)
    for k in range(TOP_K):
        loc = info[:, 4 + k:5 + k] + _select_by_expert(info[:, k:k + 1], plan_ref, shift_base, n_exp)
        p = p + jnp.where(col == loc.astype(jnp.int32), info[:, 2 + k:3 + k], 0.0)
    out = h_ref[...] + _dot(p.astype(BF16), stage[slot])
    if final_norm:
        out = _rms(out, gf_ref[...])
    o_ref[...] = out


def _combine(h, info, plan, y_sorted, g_final, final_norm, n_exp):
    n, d = h.shape
    tile = n // (plan.shape[0] // (PLAN_ROWS * n_exp))
    grid_spec = pltpu.PrefetchScalarGridSpec(
        num_scalar_prefetch=1,
        grid=(n // tile,),
        in_specs=[pl.BlockSpec((tile, d), lambda i, plan: (i, 0)),
                  pl.BlockSpec((tile, LANES), lambda i, plan: (i, 0)),
                  pl.BlockSpec((1, d), lambda i, plan: (0, 0)),
                  pl.BlockSpec(memory_space=pl.ANY)],
        out_specs=pl.BlockSpec((tile, d), lambda i, plan: (i, 0)),
        scratch_shapes=[pltpu.VMEM((2, _combine_stage_rows(tile, n_exp), d), BF16), pltpu.SemaphoreType.DMA((2,))],
    )
    return pl.pallas_call(
        functools.partial(_combine_kernel, tile=tile, n_exp=n_exp, final_norm=final_norm),
        grid_spec=grid_spec,
        out_shape=jax.ShapeDtypeStruct((n, d), F32),
        compiler_params=_params("arbitrary"),
        name="moe_combine",
    )(plan, h, info, g_final, y_sorted)


def _moe_layer(h, g, router, wg, wu, wd, g_final, final_norm):
    n, d = h.shape
    n_exp = router.shape[1]
    tm = min(TOKEN_TILE, n)
    router_padded = jnp.pad(router, ((0, 0), (0, LANES - n_exp)))
    u, info, info_t, cnt = _route(h, g, router_padded, n_exp)

    tile = min(TOKEN_TILE, n)
    n_blocks = n // tile
    after = cnt.reshape(n_blocks, 8, LANES)[:, 0, :n_exp].astype(jnp.int32)
    before = jnp.concatenate([jnp.zeros((1, n_exp), jnp.int32), after[:-1]])
    counts = after[-1]
    padded = (counts + tm - 1) // tm * tm
    ends = jnp.cumsum(padded)
    starts = ends - padded
    rows = TOP_K * n + n_exp * tm
    n_used = (ends[-1] // tm).astype(jnp.int32)
    tile_ids = jnp.minimum(jnp.arange(rows // tm, dtype=jnp.int32), n_used - 1)
    tile_expert = jnp.sum(tile_ids[:, None] * tm >= ends[None, :], axis=1).astype(jnp.int32)

    run_start = starts[None, :] + before
    run_len = after - before
    fetch_start = run_start // BF16_ROWS * BF16_ROWS
    n_chunks = jnp.where(run_len > 0, (run_start - fetch_start + run_len + COMBINE_CHUNK - 1) // COMBINE_CHUNK, 0)
    stage_off = COMBINE_CHUNK * (jnp.cumsum(n_chunks, axis=1) - n_chunks)
    plan = jnp.stack([fetch_start, n_chunks, stage_off, starts[None, :] + stage_off - fetch_start],
                     axis=1).reshape(-1).astype(jnp.int32)

    slabs = d // LANES
    assert slabs == 8
    local_off = jnp.cumsum(run_len, axis=1) - run_len
    dispatch_plan = jnp.stack([run_start, run_len, local_off, local_off - before],
                              axis=1).reshape(-1).astype(jnp.int32)
    assert tm % ZERO_CHUNK == 0
    pad_start = jnp.concatenate([starts + counts, ends[-1:]]).astype(jnp.int32)
    pad_cnt = jnp.concatenate([padded - counts, (rows - ends[-1:]) // ZERO_CHUNK]).astype(jnp.int32)
    x_sorted = _dispatch(u, info_t, dispatch_plan, pad_start, pad_cnt, rows, tile, slabs)
    y_sorted = _moe_experts(x_sorted, tile_expert, n_used.reshape(1), wg, wu, wd, tm, slabs)
    return _combine(h, info, plan, y_sorted, g_final, final_norm, n_exp)


def _mla_weights(wq_b, wkv_a, wkv_b, heads, lora):
    def padded(w):
        return jnp.concatenate([w, jnp.zeros(w.shape[:-1] + (LANES - ROPE,), w.dtype)], axis=-1)

    rq = wq_b.shape[0]
    wq = wq_b.reshape(rq, heads, NOPE + ROPE)
    wqb = jnp.concatenate([wq[..., :NOPE].reshape(rq, -1), padded(wq[..., NOPE:]).reshape(rq, -1)], axis=1)
    wkva = jnp.concatenate([wkv_a[:, :lora], padded(wkv_a[:, lora:])], axis=1)
    wkv = wkv_b.reshape(lora, heads, 2 * NOPE)
    wkvb = jnp.concatenate([wkv[..., :NOPE].reshape(lora, -1), wkv[..., NOPE:].reshape(lora, -1)], axis=1)
    return wqb.astype(BF16), wkva.astype(BF16), wkvb.astype(BF16)


def _cast_kernel(x_ref, o_ref):
    o_ref[...] = x_ref[0].astype(BF16)


def _to_bf16(w, layer):
    cols = w.shape[-1]
    w3 = w.reshape(w.shape[0], -1, cols)
    rows = w3.shape[1]
    cap = 256 if cols > 2048 else 1024
    tr = next(t for t in (1024, 512, 256, 128, 64, 32, 16, 8) if t <= cap and rows % t == 0)
    out = pl.pallas_call(
        _cast_kernel,
        grid=(rows // tr,),
        in_specs=[pl.BlockSpec((1, tr, cols), lambda i: (layer, i, 0))],
        out_specs=pl.BlockSpec((tr, cols), lambda i: (i, 0)),
        out_shape=jax.ShapeDtypeStruct((rows, cols), BF16),
        compiler_params=_params("arbitrary"),
        name="weights_to_bf16",
    )(w3)
    return out.reshape(w.shape[1:])


def _final_norm_kernel(h_ref, g_ref, o_ref):
    o_ref[...] = _rms(h_ref[...], g_ref[...])


def _final_norm(h, g):
    n, d = h.shape
    tm = min(TOKEN_TILE, n)
    return pl.pallas_call(
        _final_norm_kernel,
        grid=(n // tm,),
        in_specs=[pl.BlockSpec((tm, d), lambda i: (i, 0)), pl.BlockSpec((1, d), lambda i: (0, 0))],
        out_specs=pl.BlockSpec((tm, d), lambda i: (i, 0)),
        out_shape=jax.ShapeDtypeStruct((n, d), F32),
        compiler_params=_params("arbitrary"),
        name="final_norm",
    )(h, g)


@jax.jit
def _trunk(x, positions, norm_mix, norm_ffn, norm_final, pool_w, pool_scale,
           conv_w1, conv_b1, conv_dw, conv_bdw, conv_ln_g, conv_ln_b, conv_w2, conv_b2,
           mla_wq_a, mla_q_norm, mla_wq_b, mla_wkv_a, mla_kv_norm, mla_wkv_b, mla_wo,
           ffn_wg, ffn_wu, ffn_wd, moe_router, moe_wg, moe_wu, moe_wd):
    batch, seq, d = x.shape
    depth = norm_mix.shape[0]
    lora = mla_kv_norm.shape[1]
    heads = mla_wo.shape[1] // NOPE
    assert mla_wq_b.shape[2] == heads * (NOPE + ROPE) and mla_wkv_a.shape[2] == lora + ROPE
    n = batch * seq
    h = x.reshape(n, d)
    pos = positions.reshape(n, 1)
    row = lambda v: v.reshape(1, -1)

    half = jnp.arange(0, ROPE, 2, dtype=F32) / ROPE
    inv_freq = ROPE_THETA ** (-half)
    invf = row(jnp.concatenate([inv_freq, inv_freq, jnp.zeros((LANES - ROPE,), F32)]))
    sgn = row(jnp.concatenate([-jnp.ones((ROPE // 2,), F32), jnp.ones((ROPE // 2,), F32),
                               jnp.zeros((LANES - ROPE,), F32)]))

    moe_w = (moe_wg, moe_wu, moe_wd)
    moe_bf16 = {}
    ffn_bf16 = {}

    def moe_cast_jobs(i):
        first = (i if i % 2 == 1 else i + 1) // 2
        jobs = [(a, f) for f in range(first, moe_wg.shape[0]) for a in range(len(moe_w)) if (a, f) not in moe_bf16]
        return jobs[:SIDE_JOBS_PER_HOST]

    for i in range(depth):
        kind, j = i % 3, i // 3
        g = row(norm_mix[i])
        if kind == 0:
            jobs = [(w, f) for f in range(ffn_wg.shape[0]) for w in (ffn_wg, ffn_wu, ffn_wd)] if not ffn_bf16 else []
            h, cast = _pool_layer(h, g, pool_w[j].astype(BF16), row(pool_scale[j]), [w for w, _ in jobs],
                                  [f for _, f in jobs], batch=batch, seq=seq)
            for k in range(0, len(cast), 3):
                ffn_bf16[jobs[k][1]] = cast[k:k + 3]
        elif kind == 1:
            jobs = moe_cast_jobs(i)
            h, cast = _conv_layer(h, g, conv_w1[j].astype(BF16), row(conv_b1[j]), conv_dw[j], row(conv_bdw[j]),
                                  row(conv_ln_g[j]), row(conv_ln_b[j]), conv_w2[j].astype(BF16), row(conv_b2[j]),
                                  [moe_w[a] for a, _ in jobs], [f for _, f in jobs], batch=batch, seq=seq)
            moe_bf16.update(zip(jobs, cast))
        else:
            wqb, wkva, wkvb = _mla_weights(mla_wq_b[j], mla_wkv_a[j], mla_wkv_b[j], heads, lora)
            jobs = moe_cast_jobs(i)
            h, cast = _mla_layer(h, pos, g, mla_wq_a[j].astype(BF16), row(mla_q_norm[j]), wqb, wkva,
                                 row(mla_kv_norm[j]), wkvb, mla_wo[j].T.astype(BF16), invf, sgn,
                                 [moe_w[a] for a, _ in jobs], [f for _, f in jobs],
                                 batch=batch, seq=seq, heads=heads, lora=lora)
            moe_bf16.update(zip(jobs, cast))
        g = row(norm_ffn[i])
        f_idx = i // 2
        last = i == depth - 1
        if i % 2 == 0:
            wg, wu, wd = ffn_bf16.get(f_idx) or [_to_bf16(w, f_idx) for w in (ffn_wg, ffn_wu, ffn_wd)]
            h = _ffn_layer(h, g, wg, wu, wd)
            if last:
                h = _final_norm(h, row(norm_final))
        else:
            wg, wu, wd = [moe_bf16[(a, f_idx)] if (a, f_idx) in moe_bf16 else _to_bf16(w, f_idx)
                          for a, w in enumerate(moe_w)]
            h = _moe_layer(h, g, moe_router[f_idx], wg, wu, wd, row(norm_final), last)
    return h.reshape(batch, seq, d)


def kernel(x, positions, norm_mix, norm_ffn, norm_final, pool_w, pool_scale, conv_w1, conv_b1, conv_dw, conv_bdw, conv_ln_g, conv_ln_b, conv_w2, conv_b2, mla_wq_a, mla_q_norm, mla_wq_b, mla_wkv_a, mla_kv_norm, mla_wkv_b, mla_wo, ffn_wg, ffn_wu, ffn_wd, moe_router, moe_wg, moe_wu, moe_wd):
    return _trunk(x, positions, norm_mix, norm_ffn, norm_final, pool_w, pool_scale,
                  conv_w1, conv_b1, conv_dw, conv_bdw, conv_ln_g, conv_ln_b, conv_w2, conv_b2,
                  mla_wq_a, mla_q_norm, mla_wq_b, mla_wkv_a, mla_kv_norm, mla_wkv_b, mla_wo,
                  ffn_wg, ffn_wu, ffn_wd, moe_router, moe_wg, moe_wu, moe_wd)
```
